```python
import jax, jax.numpy as jnp
from jax import lax
import numpy as np

D_MODEL = 1024
BATCH = 4
SEQ = 8192
DEPTH = 1
DEC_BATCH = 16
DEC_SEQ = 32
PAST_LEN = 4096

CHUNK = 64
HEAD_DIM = 64
N_ATT_HEADS = 8
N_MLSTM_HEADS = 8
ATT_WIDTH = N_ATT_HEADS * HEAD_DIM
MLSTM_WIDTH = N_MLSTM_HEADS * HEAD_DIM
MIX_WIDTH = ATT_WIDTH + MLSTM_WIDTH
IN_WIDTH = 3 * ATT_WIDTH + 4 * MLSTM_WIDTH + 2 * N_MLSTM_HEADS
N_PAST_CHUNKS = 8
PAST_BAND = N_PAST_CHUNKS * CHUNK
BAND = PAST_BAND + CHUNK
REL_CLIP = 256
N_EXPERTS = 32
TOP_K = 4
D_FF = D_MODEL
SWIGLU_ALPHA = 1.702
SWIGLU_LIMIT = 7.0
EXPERT_BLOCK = 128
RMS_EPS = 1e-6

kernel_name = 'hybrid_mlstm_chunkband_moe_step'


def rmsnorm(x, g):
    xf = x.astype(jnp.float32)
    y = xf * lax.rsqrt(jnp.mean(xf * xf, axis=-1, keepdims=True) + RMS_EPS)
    return (y * g.astype(jnp.float32)).astype(x.dtype)


def _mlstm_chunkwise(q, k, v, log_i, log_f, c0, n0, m0):
    b, t, h, d = q.shape
    blk = min(CHUNK, t)
    nc = t // blk

    def to_blocks(a):
        a = a.reshape((b, nc, blk) + a.shape[2:])
        return a.transpose((1, 0, 3, 2) + tuple(range(4, a.ndim)))

    causal = jnp.tril(jnp.ones((blk, blk), dtype=bool))

    def step(carry, inp):
        c, n, m = carry
        qb, kb, vb, ib, fb = inp
        cum_f = jnp.cumsum(fb, axis=-1)
        log_d = jnp.where(causal, cum_f[..., :, None] - cum_f[..., None, :] + ib[..., None, :], -jnp.inf)
        log_inter = cum_f + m[..., None]
        m_t = jnp.maximum(log_inter, jnp.max(log_d, axis=-1))
        w_intra = jnp.exp(log_d - m_t[..., None])
        w_inter = jnp.exp(log_inter - m_t)
        s = jnp.einsum('bhtd,bhsd->bhts', qb, kb) * w_intra
        num = jnp.einsum('bhts,bhsv->bhtv', s, vb) + w_inter[..., None] * jnp.einsum('bhtk,bhkv->bhtv', qb, c)
        den = jnp.sum(s, axis=-1) + w_inter * jnp.einsum('bhtk,bhk->bht', qb, n)
        hb = num / jnp.maximum(jnp.abs(den), jnp.exp(-m_t))[..., None]
        m_new = m_t[..., -1]
        w_state = jnp.exp(cum_f[..., -1:] - cum_f + ib - m_new[..., None])
        decay = jnp.exp(cum_f[..., -1] + m - m_new)
        c_new = decay[..., None, None] * c + jnp.einsum('bhs,bhsk,bhsv->bhkv', w_state, kb, vb)
        n_new = decay[..., None] * n + jnp.einsum('bhs,bhsk->bhk', w_state, kb)
        return (c_new, n_new, m_new), hb

    xs = (to_blocks(q), to_blocks(k), to_blocks(v), to_blocks(log_i), to_blocks(log_f))
    (c, n, m), hs = lax.scan(step, (c0, n0, m0), xs)
    hs = hs.transpose(1, 0, 3, 2, 4).reshape(b, t, h, d)
    return hs, c, n, m


def _band_attention_prompt(q, k, v, rel_bias):
    b, t, h, d = q.shape
    nc = t // CHUNK
    qc = q.reshape(b, nc, CHUNK, h, d)
    pad = ((0, 0), (PAST_BAND, 0), (0, 0), (0, 0))
    kp = jnp.pad(k, pad).reshape(b, nc + N_PAST_CHUNKS, CHUNK, h, d)
    vp = jnp.pad(v, pad).reshape(b, nc + N_PAST_CHUNKS, CHUNK, h, d)
    kb = jnp.concatenate([kp[:, j:j + nc] for j in range(N_PAST_CHUNKS + 1)], axis=2)
    vb = jnp.concatenate([vp[:, j:j + nc] for j in range(N_PAST_CHUNKS + 1)], axis=2)
    s = jnp.einsum('bcqhd,bckhd->bchqk', qc, kb).astype(jnp.float32) * (HEAD_DIM ** -0.5)
    rel = np.clip(PAST_BAND + np.arange(CHUNK)[:, None] - np.arange(BAND)[None, :], -REL_CLIP, REL_CLIP) + REL_CLIP
    s = s + rel_bias[:, rel].astype(jnp.float32)[None, None]
    valid = (np.arange(nc)[:, None] + np.arange(BAND)[None, :] // CHUNK) >= N_PAST_CHUNKS
    s = jnp.where(valid[None, :, None, None, :], s, -jnp.inf)
    p = jax.nn.softmax(s, axis=-1)
    o = jnp.einsum('bchqk,bckhd->bcqhd', p.astype(v.dtype), vb)
    return o.reshape(b, t, h, d)


def _band_attention_sample(q, k, v, k_cache, v_cache, rel_bias):
    t = q.shape[1]
    lc = k_cache.shape[1]
    kc = jnp.concatenate([k_cache.astype(k.dtype), k], axis=1)
    vc = jnp.concatenate([v_cache.astype(v.dtype), v], axis=1)
    s = jnp.einsum('bqhd,bkhd->bhqk', q, kc).astype(jnp.float32) * (HEAD_DIM ** -0.5)
    rel = np.clip(lc + np.arange(t)[:, None] - np.arange(lc + t)[None, :], -REL_CLIP, REL_CLIP) + REL_CLIP
    s = s + rel_bias[:, rel].astype(jnp.float32)[None]
    p = jax.nn.softmax(s, axis=-1)
    return jnp.einsum('bhqk,bkhd->bqhd', p.astype(v.dtype), vc)


def _token_mixer(x, g_mix, w_in, g_q, g_k, rel_bias, b_igate, b_fgate, g_mlstm, w_out,
                 k_cache, v_cache, c0, n0, m0):
    b, t, _ = x.shape
    f32 = jnp.float32
    z = rmsnorm(x, g_mix) @ w_in
    splits = np.cumsum([ATT_WIDTH] * 3 + [MLSTM_WIDTH] * 4 + [N_MLSTM_HEADS])
    aq, ak, av, mq, mk, mv, mo, mi, mf = jnp.split(z, splits, axis=-1)
    heads = lambda a, nh: a.reshape(b, t, nh, HEAD_DIM)
    aq = rmsnorm(heads(aq, N_ATT_HEADS), g_q)
    ak = rmsnorm(heads(ak, N_ATT_HEADS), g_k)
    av = heads(av, N_ATT_HEADS)
    if k_cache is None:
        att = _band_attention_prompt(aq, ak, av, rel_bias)
        keep = min(PAST_BAND, t)
        k_new, v_new = ak[:, t - keep:], av[:, t - keep:]
    else:
        att = _band_attention_sample(aq, ak, av, k_cache, v_cache, rel_bias)
        k_new, v_new = ak, av
    h_m, c, n, m = _mlstm_chunkwise(
        heads(mq, N_MLSTM_HEADS).astype(f32) * (HEAD_DIM ** -0.5),
        heads(mk, N_MLSTM_HEADS).astype(f32),
        heads(mv, N_MLSTM_HEADS).astype(f32),
        mi.astype(f32) + b_igate.astype(f32),
        jax.nn.log_sigmoid(mf.astype(f32) + b_fgate.astype(f32)),
        c0.astype(f32), n0.astype(f32), m0.astype(f32))
    h_m = jax.nn.sigmoid(heads(mo, N_MLSTM_HEADS).astype(f32)) * rmsnorm(h_m, g_mlstm)
    mixed = jnp.concatenate([att.reshape(b, t, ATT_WIDTH), h_m.reshape(b, t, MLSTM_WIDTH).astype(x.dtype)], axis=-1)
    y = x + mixed @ w_out
    return y, (k_new, v_new, c.astype(x.dtype), n.astype(x.dtype), m.astype(x.dtype))


def _moe_ffn(x, g_ffn, w_router, b_router, w_up, b_up, w_down, b_down):
    b, t, dm = x.shape
    n_tok = b * t
    xf = rmsnorm(x, g_ffn).reshape(n_tok, dm)
    logits = xf.astype(jnp.float32) @ w_router.astype(jnp.float32) + b_router.astype(jnp.float32)
    top_val, top_idx = lax.top_k(logits, TOP_K)
    gates = jax.nn.softmax(top_val, axis=-1)
    n_asg = n_tok * TOP_K
    flat_e = top_idx.reshape(n_asg)
    order = jnp.argsort(flat_e)
    sorted_e = flat_e[order]
    token_of = order // TOP_K
    gate_sorted = gates.reshape(n_asg)[order].astype(x.dtype)
    counts = jnp.bincount(flat_e, length=N_EXPERTS)
    padded = (counts + EXPERT_BLOCK - 1) // EXPERT_BLOCK * EXPERT_BLOCK
    start = jnp.cumsum(counts) - counts
    pad_end = jnp.cumsum(padded)
    pad_start = pad_end - padded
    row = pad_start[sorted_e] + jnp.arange(n_asg) - start[sorted_e]
    n_blocks = -(-(n_asg + N_EXPERTS * (EXPERT_BLOCK - 1)) // EXPERT_BLOCK)
    row_token = jnp.full((n_blocks * EXPERT_BLOCK,), n_tok, dtype=jnp.int32).at[row].set(token_of.astype(jnp.int32))
    block_e = jnp.minimum(jnp.searchsorted(pad_end, jnp.arange(n_blocks) * EXPERT_BLOCK, side='right'), N_EXPERTS - 1)
    x_pad = jnp.concatenate([xf, jnp.zeros((1, dm), xf.dtype)], axis=0)
    xb = x_pad[row_token].reshape(n_blocks, EXPERT_BLOCK, dm)

    def expert_block(args):
        xg, e = args
        hg = xg @ w_up[e] + b_up[e]
        glu = jnp.minimum(hg[:, 0::2], SWIGLU_LIMIT)
        lin = jnp.clip(hg[:, 1::2], -SWIGLU_LIMIT, SWIGLU_LIMIT)
        act = glu * jax.nn.sigmoid(SWIGLU_ALPHA * glu) * (lin + 1.0)
        return act @ w_down[e] + b_down[e]

    y_rows = lax.map(expert_block, (xb, block_e)).reshape(n_blocks * EXPERT_BLOCK, dm)
    y = jnp.zeros((n_tok, dm), x.dtype).at[token_of].add(gate_sorted[:, None] * y_rows[row])
    return x + y.reshape(b, t, dm)


def setup_inputs(seed: int = 0) -> dict:
    key = jax.random.key(seed)
    ks = jax.random.split(key, 24)
    nl = DEPTH
    att_cache = min(PAST_BAND, PAST_LEN)
    f32 = jnp.float32
    nrm = lambda k, shape, scale: jax.random.normal(k, shape, f32) * scale
    return {
        'x_prompt': nrm(ks[0], (BATCH, SEQ, D_MODEL), 1.0),
        'x_sample': nrm(ks[1], (DEC_BATCH, DEC_SEQ, D_MODEL), 1.0),
        'cache_k': nrm(ks[2], (nl, DEC_BATCH, att_cache, N_ATT_HEADS, HEAD_DIM), 1.0),
        'cache_v': nrm(ks[3], (nl, DEC_BATCH, att_cache, N_ATT_HEADS, HEAD_DIM), 1.0),
        'state_C': nrm(ks[4], (nl, DEC_BATCH, N_MLSTM_HEADS, HEAD_DIM, HEAD_DIM), 0.3),
        'state_n': nrm(ks[5], (nl, DEC_BATCH, N_MLSTM_HEADS, HEAD_DIM), 0.3),
        'state_m': nrm(ks[6], (nl, DEC_BATCH, N_MLSTM_HEADS), 1.0),
        'g_mix': 1.0 + nrm(ks[7], (nl, D_MODEL), 0.02),
        'w_in': nrm(ks[8], (nl, D_MODEL, IN_WIDTH), D_MODEL ** -0.5),
        'g_q': 1.0 + nrm(ks[9], (nl, HEAD_DIM), 0.02),
        'g_k': 1.0 + nrm(ks[10], (nl, HEAD_DIM), 0.02),
        'rel_bias': nrm(ks[11], (nl, N_ATT_HEADS, 2 * REL_CLIP + 1), 0.2),
        'b_igate': -1.0 + nrm(ks[12], (nl, N_MLSTM_HEADS), 0.1),
        'b_fgate': jnp.linspace(3.0, 6.0, N_MLSTM_HEADS, dtype=f32)[None] + nrm(ks[13], (nl, N_MLSTM_HEADS), 0.1),
        'g_mlstm': 1.0 + nrm(ks[14], (nl, N_MLSTM_HEADS, HEAD_DIM), 0.02),
        'w_out': nrm(ks[15], (nl, MIX_WIDTH, D_MODEL), MIX_WIDTH ** -0.5),
        'g_ffn': 1.0 + nrm(ks[16], (nl, D_MODEL), 0.02),
        'w_router': nrm(ks[17], (nl, D_MODEL, N_EXPERTS), D_MODEL ** -0.5),
        'b_router': nrm(ks[18], (nl, N_EXPERTS), 0.01),
        'w_up': nrm(ks[19], (nl, N_EXPERTS, D_MODEL, 2 * D_FF), D_MODEL ** -0.5),
        'b_up': nrm(ks[20], (nl, N_EXPERTS, 2 * D_FF), 0.01),
        'w_down': nrm(ks[21], (nl, N_EXPERTS, D_FF, D_MODEL), D_FF ** -0.5),
        'b_down': nrm(ks[22], (nl, N_EXPERTS, D_MODEL), 0.01),
    }


def reference(x_prompt, x_sample, cache_k, cache_v, state_C, state_n, state_m,
              g_mix, w_in, g_q, g_k, rel_bias, b_igate, b_fgate, g_mlstm, w_out,
              g_ffn, w_router, b_router, w_up, b_up, w_down, b_down):
    yp, ys = x_prompt, x_sample
    bp = x_prompt.shape[0]
    st_prompt, st_sample = [], []
    for l in range(DEPTH):
        mix_w = (g_mix[l], w_in[l], g_q[l], g_k[l], rel_bias[l], b_igate[l], b_fgate[l], g_mlstm[l], w_out[l])
        ffn_w = (g_ffn[l], w_router[l], b_router[l], w_up[l], b_up[l], w_down[l], b_down[l])
        c0 = jnp.zeros((bp, N_MLSTM_HEADS, HEAD_DIM, HEAD_DIM), jnp.float32)
        n0 = jnp.zeros((bp, N_MLSTM_HEADS, HEAD_DIM), jnp.float32)
        m0 = jnp.zeros((bp, N_MLSTM_HEADS), jnp.float32)
        yp, sp = _token_mixer(yp, *mix_w, None, None, c0, n0, m0)
        ys, ss = _token_mixer(ys, *mix_w, cache_k[l], cache_v[l], state_C[l], state_n[l], state_m[l])
        yp = _moe_ffn(yp, *ffn_w)
        ys = _moe_ffn(ys, *ffn_w)
        st_prompt.append(sp)
        st_sample.append(ss)
    k_prompt, v_prompt, C_prompt, n_prompt, m_prompt = [jnp.stack(a) for a in zip(*st_prompt)]
    k_sample, v_sample, C_sample, n_sample, m_sample = [jnp.stack(a) for a in zip(*st_sample)]
    return (yp, ys, k_prompt, v_prompt, C_prompt, n_prompt, m_prompt,
            k_sample, v_sample, C_sample, n_sample, m_sample)
```

```python
import functools

import numpy as np
import jax
import jax.numpy as jnp
from jax import lax
from jax.experimental import pallas as pl
from jax.experimental.pallas import tpu as pltpu

F32 = jnp.float32
BF16 = jnp.bfloat16
HIGHEST = lax.Precision.HIGHEST

D_MODEL = 1024
N_HEADS = 8
HEAD_DIM = 64
GROUP_W = N_HEADS * HEAD_DIM
N_PROJ = 7
GATE_W = 128
CHUNK = 64
PAST_BAND = 512
KEY_WIN = 640
REL_CLIP = 256
N_EXPERTS = 32
TOP_K = 4
D_FF = 1024
SWIGLU_ALPHA = 1.702
SWIGLU_LIMIT = 7.0
RMS_EPS = 1e-6
NEG_BIG = -1e30
VMEM_LIMIT_BYTES = 56 * 1024 * 1024


def _params(*sem):
    return pltpu.CompilerParams(dimension_semantics=sem, vmem_limit_bytes=VMEM_LIMIT_BYTES)


def _head_block_diag():
    h = np.arange(GROUP_W) // HEAD_DIM
    return (h[:, None] == h[None, :]).astype(np.float32)


def _full(shape):
    return pl.BlockSpec(shape, lambda *_: (0,) * len(shape))


def _in_proj_body(x_ref, gmix_ref, w_ref, wg_ref, gq_ref, gk_ref, gmat_ref,
                  q_ref, k_ref, v_ref, mq_ref, mk_ref, mv_ref, mo_ref, gate_ref):
    x = x_ref[...]
    xn = x * lax.rsqrt(jnp.mean(x * x, axis=-1, keepdims=True) + RMS_EPS) * gmix_ref[...]
    xb = xn.astype(BF16)

    def proj(j):
        return jnp.dot(xb, w_ref[:, j * GROUP_W:(j + 1) * GROUP_W], preferred_element_type=F32)

    def head_norm(a, g_ref):
        msq = jnp.dot((a * a).astype(BF16), gmat_ref[...], preferred_element_type=F32)
        return a * lax.rsqrt(msq + RMS_EPS) * g_ref[...]

    q_ref[...] = head_norm(proj(0), gq_ref)
    k_ref[...] = head_norm(proj(1), gk_ref)
    v_ref[...] = proj(2)
    mq_ref[...] = proj(3)
    mk_ref[...] = proj(4)
    mv_ref[...] = proj(5)
    mo_ref[...] = proj(6)
    gate_ref[...] = jnp.dot(xn, wg_ref[...], preferred_element_type=F32, precision=HIGHEST)


def _in_proj(x2d, g_mix, w_main, w_gate, gq_row, gk_row, gmat, tm):
    n = x2d.shape[0]
    row = lambda w: pl.BlockSpec((tm, w), lambda i: (i, 0))
    outs = [jax.ShapeDtypeStruct((n, GROUP_W), F32)] * N_PROJ + [jax.ShapeDtypeStruct((n, GATE_W), F32)]
    return pl.pallas_call(
        _in_proj_body,
        grid=(n // tm,),
        in_specs=[row(D_MODEL), _full((1, D_MODEL)), _full((D_MODEL, N_PROJ * GROUP_W)),
                  _full((D_MODEL, GATE_W)), _full((1, GROUP_W)), _full((1, GROUP_W)),
                  _full((GROUP_W, GROUP_W))],
        out_specs=[row(GROUP_W)] * N_PROJ + [row(GATE_W)],
        out_shape=outs,
        compiler_params=_params("arbitrary"),
        name="in_proj",
    )(x2d, g_mix, w_main, w_gate, gq_row, gk_row, gmat)


def _attn_body(q_ref, kp_ref, kc_ref, vp_ref, vc_ref, bias_ref, hmask_ref, o_ref, kwin, vwin,
               *, cq, nq, mask_first):
    tc = cq * nq
    i = pl.program_id(1)
    kwin[0:PAST_BAND, :] = kp_ref[...].astype(BF16)
    kwin[PAST_BAND:PAST_BAND + tc, :] = kc_ref[...].astype(BF16)
    vwin[0:PAST_BAND, :] = vp_ref[...].astype(BF16)
    vwin[PAST_BAND:PAST_BAND + tc, :] = vc_ref[...].astype(BF16)
    pad_rows = kwin.shape[0] - PAST_BAND - tc
    kwin[PAST_BAND + tc:, :] = jnp.zeros((pad_rows, GROUP_W), BF16)
    vwin[PAST_BAND + tc:, :] = jnp.zeros((pad_rows, GROUP_W), BF16)

    hm = hmask_ref[...]
    bias = bias_ref[...]
    kk = lax.broadcasted_iota(jnp.int32, (1, KEY_WIN), 1)

    def chunk(j, carry):
        r0 = pl.multiple_of(j * cq, cq)
        q = q_ref[pl.ds(r0, cq), :] * (HEAD_DIM ** -0.5)
        qm = (jnp.concatenate([q] * N_HEADS, axis=0) * hm).astype(BF16)
        kw = kwin[pl.ds(r0, KEY_WIN), :]
        s = lax.dot_general(qm, kw, (((1,), (1,)), ((), ())), preferred_element_type=F32) + bias
        if mask_first:
            first_valid = jnp.where(i == 0, PAST_BAND - r0, 0)
            s = jnp.where(kk >= first_valid, s, NEG_BIG)
        m = jnp.max(s, axis=-1, keepdims=True)
        p = jnp.exp(s - m)
        l = jnp.sum(p, axis=-1, keepdims=True)
        vw = vwin[pl.ds(r0, KEY_WIN), :]
        o_all = jnp.dot(p.astype(BF16), vw, preferred_element_type=F32) / l * hm
        o = o_all[0:cq]
        for h in range(1, N_HEADS):
            o = o + o_all[h * cq:(h + 1) * cq]
        o_ref[pl.ds(r0, cq), :] = o
        return carry

    lax.fori_loop(0, nq, chunk, 0)


def _attention(q, k_prev_src, k_cur_src, v_prev_src, v_cur_src, bias, hmask, *, batch, tiles, cq, nq,
               prev_index, mask_first):
    tc = cq * nq
    cur = pl.BlockSpec((tc, GROUP_W), lambda b, i: (b * tiles + i, 0))
    prev = pl.BlockSpec((PAST_BAND, GROUP_W), prev_index)
    win_rows = (nq - 1) * cq + KEY_WIN
    return pl.pallas_call(
        functools.partial(_attn_body, cq=cq, nq=nq, mask_first=mask_first),
        grid=(batch, tiles),
        in_specs=[cur, prev, cur, prev, cur, _full((N_HEADS * cq, KEY_WIN)), _full((N_HEADS * cq, GROUP_W))],
        out_specs=cur,
        out_shape=jax.ShapeDtypeStruct(q.shape, F32),
        scratch_shapes=[pltpu.VMEM((win_rows, GROUP_W), BF16), pltpu.VMEM((win_rows, GROUP_W), BF16)],
        compiler_params=_params("arbitrary", "arbitrary"),
        name="band_attention",
    )(q, k_prev_src, k_cur_src, v_prev_src, v_cur_src, bias, hmask)


def _rel_table(rel_bias_l, cq):
    nk = PAST_BAND + cq
    rel = np.clip(PAST_BAND + np.arange(cq)[:, None] - np.arange(nk)[None, :], -REL_CLIP, REL_CLIP) + REL_CLIP
    tab = rel_bias_l[:, rel].astype(F32)
    tab = jnp.pad(tab, ((0, 0), (0, 0), (0, KEY_WIN - nk)), constant_values=NEG_BIG)
    return tab.reshape(N_HEADS * cq, KEY_WIN)


def _head_row_mask(cq):
    h_row = np.repeat(np.arange(N_HEADS), cq)
    h_col = np.arange(GROUP_W) // HEAD_DIM
    return jnp.asarray((h_row[:, None] == h_col[None, :]).astype(np.float32))


def _log_sigmoid(x):
    return jnp.minimum(x, 0.0) - jnp.log(1.0 + jnp.exp(-jnp.abs(x)))


def _mlstm_body(q_ref, k_ref, v_ref, o_ref, g_ref, c0_ref, n0_ref, m0_ref,
                expand_ref, gbias_ref, bd_ref, gmat_ref, ltri_ref, eye_ref, causal_ref, gml_ref,
                h_ref, ct_ref, n_ref, m_ref, ct_s, n_s, m_s, *, valid):
    c = pl.program_id(1)
    bd = bd_ref[...]

    @pl.when(c == 0)
    def _():
        ct_s[...] = jnp.concatenate([c0_ref[0]] * N_HEADS, axis=0) * bd
        n_s[...] = n0_ref[0]
        m_s[...] = m0_ref[0]

    gp = jnp.dot(g_ref[...], expand_ref[...], preferred_element_type=F32, precision=HIGHEST) + gbias_ref[...]
    log_i = gp[:, :GROUP_W]
    log_f = _log_sigmoid(gp[:, GROUP_W:])
    if valid < CHUNK:
        live = lax.broadcasted_iota(jnp.int32, (CHUNK, GROUP_W), 0) < valid
        log_i = jnp.where(live, log_i, -jnp.inf)
        log_f = jnp.where(live, log_f, 0.0)
    cum_f = jnp.dot(ltri_ref[...], log_f, preferred_element_type=F32, precision=HIGHEST)

    eye = eye_ref[...] > 0.5
    b_row = jnp.sum(jnp.where(eye, log_i - cum_f, 0.0), axis=0, keepdims=True)
    m_prev = m_s[...]
    log_inter = cum_f + m_prev
    log_d = jnp.where(causal_ref[...] > 0.5, cum_f + b_row, -jnp.inf)
    max_d = jnp.concatenate(
        [jnp.broadcast_to(jnp.max(log_d[:, h * HEAD_DIM:(h + 1) * HEAD_DIM], axis=-1, keepdims=True),
                          (CHUNK, HEAD_DIM)) for h in range(N_HEADS)], axis=1)
    m_t = jnp.maximum(log_inter, max_d)
    w_intra = jnp.exp(log_d - m_t)
    w_inter = jnp.exp(log_inter - m_t)

    q = q_ref[...] * (HEAD_DIM ** -0.5)
    k = k_ref[...]
    v = v_ref[...]
    qb = q.astype(BF16)
    kbd = (jnp.concatenate([k] * N_HEADS, axis=0) * bd).astype(BF16)
    vbd = (jnp.concatenate([v] * N_HEADS, axis=0) * bd).astype(BF16)
    s = lax.dot_general(qb, kbd, (((1,), (1,)), ((), ())), preferred_element_type=F32) * w_intra
    ct = ct_s[...]
    n_prev = n_s[...]
    num = (jnp.dot(s.astype(BF16), vbd, preferred_element_type=F32)
           + w_inter * lax.dot_general(qb, ct.astype(BF16), (((1,), (1,)), ((), ())),
                                       preferred_element_type=F32))
    gmat = gmat_ref[...]
    den_terms = (s + w_inter * q * n_prev) * float(HEAD_DIM)
    den_hi = den_terms.astype(BF16)
    den_lo = (den_terms - den_hi.astype(F32)).astype(BF16)
    den = (jnp.dot(den_hi, gmat, preferred_element_type=F32)
           + jnp.dot(den_lo, gmat, preferred_element_type=F32))
    hb = num / jnp.maximum(jnp.abs(den), jnp.exp(-m_t))

    m_new = m_t[CHUNK - 1:CHUNK, :]
    cum_last = cum_f[CHUNK - 1:CHUNK, :]
    w_state = jnp.exp(cum_last - cum_f + log_i - m_new)
    decay = jnp.exp(cum_last + m_prev - m_new)
    kw = k * w_state
    upd = lax.dot_general(v.astype(BF16), kw.astype(BF16), (((0,), (0,)), ((), ())),
                          preferred_element_type=F32)
    ct_new = decay * ct + upd * bd
    n_new = decay * n_prev + jnp.sum(kw, axis=0, keepdims=True)
    ct_s[...] = ct_new
    n_s[...] = n_new
    m_s[...] = m_new

    msq = jnp.dot((hb * hb).astype(BF16), gmat, preferred_element_type=F32)
    h_ref[...] = jax.nn.sigmoid(o_ref[...]) * (hb * lax.rsqrt(msq + RMS_EPS) * gml_ref[...])

    @pl.when(c == pl.num_programs(1) - 1)
    def _():
        acc = ct_new[0:HEAD_DIM]
        for h in range(1, N_HEADS):
            acc = acc + ct_new[h * HEAD_DIM:(h + 1) * HEAD_DIM]
        ct_ref[0] = acc
        n_ref[0] = n_new
        m_ref[0] = m_new


def _mlstm(mq, mk, mv, mo, gates, c0t, n0, m0, consts, *, batch, chunks, valid):
    row = lambda w: pl.BlockSpec((CHUNK, w), lambda b, c: (b * chunks + c, 0))
    per_b = lambda r: pl.BlockSpec((1, r, GROUP_W), lambda b, c: (b, 0, 0))
    expand, gbias, bd, gmat, ltri, eye, causal, gml = consts
    return pl.pallas_call(
        functools.partial(_mlstm_body, valid=valid),
        grid=(batch, chunks),
        in_specs=[row(GROUP_W)] * 4 + [row(GATE_W), per_b(HEAD_DIM), per_b(1), per_b(1),
                  _full((GATE_W, 2 * GROUP_W)), _full((1, 2 * GROUP_W)), _full((GROUP_W, GROUP_W)),
                  _full((GROUP_W, GROUP_W)), _full((CHUNK, CHUNK)), _full((CHUNK, GROUP_W)),
                  _full((CHUNK, GROUP_W)), _full((1, GROUP_W))],
        out_specs=[row(GROUP_W), per_b(HEAD_DIM), per_b(1), per_b(1)],
        out_shape=[jax.ShapeDtypeStruct(mq.shape, F32),
                   jax.ShapeDtypeStruct((batch, HEAD_DIM, GROUP_W), F32),
                   jax.ShapeDtypeStruct((batch, 1, GROUP_W), F32),
                   jax.ShapeDtypeStruct((batch, 1, GROUP_W), F32)],
        scratch_shapes=[pltpu.VMEM((GROUP_W, GROUP_W), F32), pltpu.VMEM((1, GROUP_W), F32),
                        pltpu.VMEM((1, GROUP_W), F32)],
        compiler_params=_params("arbitrary", "arbitrary"),
        name="mlstm",
    )(mq, mk, mv, mo, gates, c0t, n0, m0, expand, gbias, bd, gmat, ltri, eye, causal, gml)


def _mlstm_consts(b_igate_l, b_fgate_l, g_mlstm_l):
    expand = np.zeros((GATE_W, 2 * GROUP_W), np.float32)
    for h in range(N_HEADS):
        expand[h, h * HEAD_DIM:(h + 1) * HEAD_DIM] = 1.0
        expand[N_HEADS + h, GROUP_W + h * HEAD_DIM:GROUP_W + (h + 1) * HEAD_DIM] = 1.0
    gbias = jnp.concatenate([jnp.repeat(b_igate_l.astype(F32), HEAD_DIM),
                             jnp.repeat(b_fgate_l.astype(F32), HEAD_DIM)])[None, :]
    bd = _head_block_diag()
    ltri = np.tril(np.ones((CHUNK, CHUNK), np.float32))
    s_of_lane = np.arange(GROUP_W) % HEAD_DIM
    t = np.arange(CHUNK)
    eye = (t[:, None] == s_of_lane[None, :]).astype(np.float32)
    causal = (s_of_lane[None, :] <= t[:, None]).astype(np.float32)
    return (jnp.asarray(expand), gbias, jnp.asarray(bd), jnp.asarray(bd / HEAD_DIM, BF16), jnp.asarray(ltri),
            jnp.asarray(eye), jnp.asarray(causal), g_mlstm_l.astype(F32).reshape(1, GROUP_W))


def _out_proj_body(x_ref, att_ref, hm_ref, wa_ref, wm_ref, gffn_ref, wr_ref, br_ref,
                   y_ref, xf_ref, logit_ref):
    y = (x_ref[...]
         + jnp.dot(att_ref[...].astype(BF16), wa_ref[...], preferred_element_type=F32)
         + jnp.dot(hm_ref[...].astype(BF16), wm_ref[...], preferred_element_type=F32))
    y_ref[...] = y
    xf = y * lax.rsqrt(jnp.mean(y * y, axis=-1, keepdims=True) + RMS_EPS) * gffn_ref[...]
    xf_ref[...] = xf
    logit_ref[...] = jnp.dot(xf, wr_ref[...], preferred_element_type=F32, precision=HIGHEST) + br_ref[...]


def _out_proj(x2d, att, hm, wa, wm, g_ffn, w_router, b_router, tm):
    n = x2d.shape[0]
    row = lambda w: pl.BlockSpec((tm, w), lambda i: (i, 0))
    return pl.pallas_call(
        _out_proj_body,
        grid=(n // tm,),
        in_specs=[row(D_MODEL), row(GROUP_W), row(GROUP_W), _full((GROUP_W, D_MODEL)), _full((GROUP_W, D_MODEL)),
                  _full((1, D_MODEL)), _full((D_MODEL, GATE_W)), _full((1, GATE_W))],
        out_specs=[row(D_MODEL), row(D_MODEL), row(GATE_W)],
        out_shape=[jax.ShapeDtypeStruct((n, D_MODEL), F32), jax.ShapeDtypeStruct((n, D_MODEL), F32),
                   jax.ShapeDtypeStruct((n, GATE_W), F32)],
        compiler_params=_params("arbitrary"),
        name="out_proj_router",
    )(x2d, att, hm, wa, wm, g_ffn, w_router, b_router)


def _row_copy(src_hbm, row, dst, slot, sem):
    return pltpu.make_async_copy(src_hbm.at[pl.ds(row, 1)], dst.at[pl.ds(slot, 1)], sem)


def _expert_body(te_ref, nused_ref, rt_ref, xf_hbm, wg_ref, wl_ref, wd_ref, bg_ref, bl_ref, bd_ref,
                 y_ref, xbuf, sem, *, tm):
    i = pl.program_id(0)

    @pl.when(i >= nused_ref[0])
    def _():
        y_ref[...] = jnp.zeros((tm, D_MODEL), F32)

    @pl.when(i < nused_ref[0])
    def _():
        def issue(r, carry):
            _row_copy(xf_hbm, rt_ref[0, 0, r], xbuf, r, sem).start()
            return carry

        lax.fori_loop(0, tm, issue, 0, unroll=8)

        def drain(r, carry):
            _row_copy(xf_hbm, 0, xbuf, 0, sem).wait()
            return carry

        lax.fori_loop(0, tm, drain, 0, unroll=8)

        x = xbuf[...].astype(BF16)
        glu = jnp.minimum(jnp.dot(x, wg_ref[0], preferred_element_type=F32) + bg_ref[0], SWIGLU_LIMIT)
        lin = jnp.clip(jnp.dot(x, wl_ref[0], preferred_element_type=F32) + bl_ref[0], -SWIGLU_LIMIT, SWIGLU_LIMIT)
        act = glu * jax.nn.sigmoid(SWIGLU_ALPHA * glu) * (lin + 1.0)
        y_ref[...] = jnp.dot(act.astype(BF16), wd_ref[0], preferred_element_type=F32) + bd_ref[0]


def _experts(tile_expert, n_used, row_token, xf, w_glu, w_lin, w_down, b_glu, b_lin, b_down, *, tm):
    n_tiles = row_token.shape[0]
    wspec = lambda k, n: pl.BlockSpec((1, k, n), lambda i, te, nu: (te[i], 0, 0))
    grid_spec = pltpu.PrefetchScalarGridSpec(
        num_scalar_prefetch=2,
        grid=(n_tiles,),
        in_specs=[pl.BlockSpec((1, 1, tm), lambda i, te, nu: (i, 0, 0), memory_space=pltpu.SMEM),
                  pl.BlockSpec(memory_space=pl.ANY),
                  wspec(D_MODEL, D_FF), wspec(D_MODEL, D_FF), wspec(D_FF, D_MODEL),
                  wspec(1, D_FF), wspec(1, D_FF), wspec(1, D_MODEL)],
        out_specs=pl.BlockSpec((tm, D_MODEL), lambda i, te, nu: (i, 0)),
        scratch_shapes=[pltpu.VMEM((tm, D_MODEL), F32), pltpu.SemaphoreType.DMA(())],
    )
    return pl.pallas_call(
        functools.partial(_expert_body, tm=tm),
        grid_spec=grid_spec,
        out_shape=jax.ShapeDtypeStruct((n_tiles * tm, D_MODEL), F32),
        compiler_params=_params("arbitrary"),
        name="expert_ffn",
    )(tile_expert, n_used, row_token, xf, w_glu, w_lin, w_down, b_glu, b_lin, b_down)


def _combine_body(dest_ref, y_ref, gate_ref, rows_hbm, out_ref, buf, sem, *, tt):
    for k in range(TOP_K):
        def issue(t, carry):
            _row_copy(rows_hbm, dest_ref[0, 0, t * TOP_K + k], buf.at[k], t, sem).start()
            return carry

        lax.fori_loop(0, tt, issue, 0, unroll=8)

    def drain(a, carry):
        _row_copy(rows_hbm, 0, buf.at[0], 0, sem).wait()
        return carry

    lax.fori_loop(0, tt * TOP_K, drain, 0, unroll=8)

    acc = y_ref[...]
    gate = gate_ref[...]
    for k in range(TOP_K):
        acc = acc + gate[:, k:k + 1] * buf[k]
    out_ref[...] = acc


def _combine(dest, y, gates, y_rows, *, tt):
    n = y.shape[0]
    return pl.pallas_call(
        functools.partial(_combine_body, tt=tt),
        grid=(n // tt,),
        in_specs=[pl.BlockSpec((1, 1, tt * TOP_K), lambda i: (i, 0, 0), memory_space=pltpu.SMEM),
                  pl.BlockSpec((tt, D_MODEL), lambda i: (i, 0)),
                  pl.BlockSpec((tt, TOP_K), lambda i: (i, 0)),
                  pl.BlockSpec(memory_space=pl.ANY)],
        out_specs=pl.BlockSpec((tt, D_MODEL), lambda i: (i, 0)),
        out_shape=jax.ShapeDtypeStruct((n, D_MODEL), F32),
        scratch_shapes=[pltpu.VMEM((TOP_K, tt, D_MODEL), F32), pltpu.SemaphoreType.DMA(())],
        compiler_params=_params("arbitrary"),
        name="expert_combine",
    )(dest, y, gates, y_rows)


def _route(logits, tm):
    n = logits.shape[0]
    n_asg = n * TOP_K
    top_val, top_idx = lax.top_k(logits, TOP_K)
    gates = jax.nn.softmax(top_val, axis=-1)
    e = top_idx.reshape(n_asg)
    onehot = (e[:, None] == jnp.arange(N_EXPERTS)[None, :]).astype(jnp.int32)
    csum = jnp.cumsum(onehot, axis=0)
    rank = jnp.sum((csum - onehot) * onehot, axis=-1)
    counts = csum[-1]
    padded = (counts + tm - 1) // tm * tm
    pad_end = jnp.cumsum(padded)
    dest = (pad_end - padded)[e] + rank
    n_tiles = -(-(n_asg + N_EXPERTS * (tm - 1)) // tm)
    row_token = jnp.zeros((n_tiles * tm,), jnp.int32).at[dest].set(jnp.arange(n_asg, dtype=jnp.int32) // TOP_K)
    tile_expert = jnp.minimum(jnp.searchsorted(pad_end, jnp.arange(n_tiles) * tm, side='right'),
                              N_EXPERTS - 1).astype(jnp.int32)
    n_used = (pad_end[-1:] // tm).astype(jnp.int32)
    return gates, dest.astype(jnp.int32), row_token.reshape(n_tiles, 1, tm), tile_expert, n_used


def _moe(y, xf, logits, ffn_w, *, tm, tt):
    n = y.shape[0]
    gates, dest, row_token, tile_expert, n_used = _route(logits[:, :N_EXPERTS], tm)
    y_rows = _experts(tile_expert, n_used, row_token, xf, *ffn_w, tm=tm)
    return _combine(dest.reshape(n // tt, 1, tt * TOP_K), y, gates, y_rows, tt=tt)


def _layer(x, lw, ffn_w, cache, state, *, row_tile, moe_tile, comb_tile):
    b, t, _ = x.shape
    n = b * t
    x2d = x.reshape(n, D_MODEL)
    q, k, v, mq, mk, mv, mo, gates = _in_proj(x2d, lw["g_mix"], lw["w_main"], lw["w_gate"], lw["gq"], lw["gk"],
                                              lw["gmat"], row_tile)
    if cache is None:
        tiles = t // PAST_BAND
        nq = PAST_BAND // CHUNK
        att = _attention(q, k, k, v, v, lw["bias_prompt"], lw["hmask_prompt"], batch=b, tiles=tiles, cq=CHUNK,
                         nq=nq, prev_index=lambda bi, i: (bi * tiles + jnp.maximum(i - 1, 0), 0), mask_first=True)
        keep = min(PAST_BAND, t)
        k_new = k.reshape(b, t, N_HEADS, HEAD_DIM)[:, t - keep:]
        v_new = v.reshape(b, t, N_HEADS, HEAD_DIM)[:, t - keep:]
    else:
        ck, cv = cache
        att = _attention(q, ck.reshape(b * PAST_BAND, GROUP_W), k, cv.reshape(b * PAST_BAND, GROUP_W), v,
                         lw["bias_sample"], lw["hmask_sample"], batch=b, tiles=1, cq=t, nq=1,
                         prev_index=lambda bi, i: (bi, 0), mask_first=False)
        k_new = k.reshape(b, t, N_HEADS, HEAD_DIM)
        v_new = v.reshape(b, t, N_HEADS, HEAD_DIM)

    chunks = -(-t // CHUNK)
    valid = t if t < CHUNK else CHUNK
    tp = chunks * CHUNK

    def pad_t(a):
        if tp == t:
            return a
        return jnp.pad(a.reshape(b, t, -1), ((0, 0), (0, tp - t), (0, 0))).reshape(b * tp, -1)

    if state is None:
        c0t = jnp.zeros((b, HEAD_DIM, GROUP_W), F32)
        n0 = jnp.zeros((b, 1, GROUP_W), F32)
        m0 = jnp.zeros((b, 1, GROUP_W), F32)
    else:
        c_in, n_in, m_in = state
        c0t = c_in.astype(F32).transpose(0, 3, 1, 2).reshape(b, HEAD_DIM, GROUP_W)
        n0 = n_in.astype(F32).reshape(b, 1, GROUP_W)
        m0 = jnp.repeat(m_in.astype(F32), HEAD_DIM, axis=-1).reshape(b, 1, GROUP_W)
    hm, ct, n_out, m_out = _mlstm(pad_t(mq), pad_t(mk), pad_t(mv), pad_t(mo), pad_t(gates), c0t, n0, m0,
                                  lw["mlstm_consts"], batch=b, chunks=chunks, valid=valid)
    if tp != t:
        hm = hm.reshape(b, tp, GROUP_W)[:, :t].reshape(n, GROUP_W)
    c_new = ct.reshape(b, HEAD_DIM, N_HEADS, HEAD_DIM).transpose(0, 2, 3, 1)
    n_new = n_out.reshape(b, N_HEADS, HEAD_DIM)
    m_new = m_out.reshape(b, N_HEADS, HEAD_DIM)[:, :, 0]

    y, xf, logits = _out_proj(x2d, att, hm, lw["w_out_att"], lw["w_out_mlstm"], lw["g_ffn"], lw["w_router"],
                              lw["b_router"], row_tile)
    out = _moe(y, xf, logits, ffn_w, tm=moe_tile, tt=comb_tile)
    return out.reshape(b, t, D_MODEL), (k_new, v_new, c_new, n_new, m_new)


def kernel(x_prompt, x_sample, cache_k, cache_v, state_C, state_n, state_m, g_mix, w_in, g_q, g_k, rel_bias,
           b_igate, b_fgate, g_mlstm, w_out, g_ffn, w_router, b_router, w_up, b_up, w_down, b_down):
    depth = w_in.shape[0]
    yp, ys = x_prompt, x_sample
    bs, ts = x_sample.shape[0], x_sample.shape[1]
    st_prompt, st_sample = [], []
    n_main = N_PROJ * GROUP_W
    gmat = jnp.asarray(_head_block_diag() / HEAD_DIM, BF16)
    for l in range(depth):
        lw = dict(
            g_mix=g_mix[l].astype(F32)[None, :],
            w_main=w_in[l][:, :n_main].astype(BF16),
            w_gate=jnp.pad(w_in[l][:, n_main:].astype(F32), ((0, 0), (0, GATE_W - 2 * N_HEADS))),
            gq=jnp.tile(g_q[l].astype(F32), N_HEADS)[None, :],
            gk=jnp.tile(g_k[l].astype(F32), N_HEADS)[None, :],
            gmat=gmat,
            bias_prompt=_rel_table(rel_bias[l], CHUNK),
            hmask_prompt=_head_row_mask(CHUNK),
            bias_sample=_rel_table(rel_bias[l], ts),
            hmask_sample=_head_row_mask(ts),
            mlstm_consts=_mlstm_consts(b_igate[l], b_fgate[l], g_mlstm[l]),
            w_out_att=w_out[l][:GROUP_W].astype(BF16),
            w_out_mlstm=w_out[l][GROUP_W:].astype(BF16),
            g_ffn=g_ffn[l].astype(F32)[None, :],
            w_router=jnp.pad(w_router[l].astype(F32), ((0, 0), (0, GATE_W - N_EXPERTS))),
            b_router=jnp.pad(b_router[l].astype(F32), (0, GATE_W - N_EXPERTS))[None, :],
        )
        ffn_w = (w_up[l][:, :, 0::2].astype(BF16), w_up[l][:, :, 1::2].astype(BF16), w_down[l].astype(BF16),
                 b_up[l][:, None, 0::2].astype(F32), b_up[l][:, None, 1::2].astype(F32),
                 b_down[l][:, None, :].astype(F32))
        yp, sp = _layer(yp, lw, ffn_w, None, None, row_tile=512, moe_tile=512, comb_tile=256)
        cache = (cache_k[l].reshape(bs, PAST_BAND, GROUP_W), cache_v[l].reshape(bs, PAST_BAND, GROUP_W))
        ys, ss = _layer(ys, lw, ffn_w, cache, (state_C[l], state_n[l], state_m[l]),
                        row_tile=512, moe_tile=128, comb_tile=256)
        st_prompt.append(sp)
        st_sample.append(ss)
    k_p, v_p, c_p, n_p, m_p = [jnp.stack(a) for a in zip(*st_prompt)]
    k_s, v_s, c_s, n_s, m_s = [jnp.stack(a) for a in zip(*st_sample)]
    return (yp, ys, k_p, v_p, c_p, n_p, m_p, k_s, v_s, c_s, n_s, m_s)
```

```python
import functools

import numpy as np
import jax
import jax.numpy as jnp
from jax import lax
from jax.experimental import pallas as pl
from jax.experimental.pallas import tpu as pltpu

F32 = jnp.float32
BF16 = jnp.bfloat16
HIGHEST = lax.Precision.HIGHEST

D_MODEL = 1024
N_HEADS = 8
HEAD_DIM = 64
GROUP_W = N_HEADS * HEAD_DIM
N_PROJ = 7
GATE_W = 128
CHUNK = 64
PAST_BAND = 512
KEY_WIN = 640
REL_CLIP = 256
N_EXPERTS = 32
TOP_K = 4
D_FF = 1024
SWIGLU_ALPHA = 1.702
SWIGLU_LIMIT = 7.0
RMS_EPS = 1e-6
NEG_BIG = -1e30
VMEM_LIMIT_BYTES = 56 * 1024 * 1024


def _params(*sem):
    return pltpu.CompilerParams(dimension_semantics=sem, vmem_limit_bytes=VMEM_LIMIT_BYTES)


def _head_block_diag():
    h = np.arange(GROUP_W) // HEAD_DIM
    return (h[:, None] == h[None, :]).astype(np.float32)


def _full(shape):
    return pl.BlockSpec(shape, lambda *_: (0,) * len(shape))


def _in_proj_body(x_ref, gmix_ref, w_ref, wg_ref, gq_ref, gk_ref, gmat_ref,
                  q_ref, k_ref, v_ref, mq_ref, mk_ref, mv_ref, mo_ref, gate_ref):
    x = x_ref[...]
    xn = x * lax.rsqrt(jnp.mean(x * x, axis=-1, keepdims=True) + RMS_EPS) * gmix_ref[...]
    xb = xn.astype(BF16)

    def proj(j):
        return jnp.dot(xb, w_ref[:, j * GROUP_W:(j + 1) * GROUP_W], preferred_element_type=F32)

    def head_norm(a, g_ref):
        msq = jnp.dot((a * a).astype(BF16), gmat_ref[...], preferred_element_type=F32)
        return a * lax.rsqrt(msq + RMS_EPS) * g_ref[...]

    q_ref[...] = head_norm(proj(0), gq_ref)
    k_ref[...] = head_norm(proj(1), gk_ref)
    v_ref[...] = proj(2)
    mq_ref[...] = proj(3)
    mk_ref[...] = proj(4)
    mv_ref[...] = proj(5)
    mo_ref[...] = proj(6)
    gate_ref[...] = jnp.dot(xn, wg_ref[...], preferred_element_type=F32, precision=HIGHEST)


def _in_proj(x2d, g_mix, w_main, w_gate, gq_row, gk_row, gmat, tm):
    n = x2d.shape[0]
    row = lambda w: pl.BlockSpec((tm, w), lambda i: (i, 0))
    outs = [jax.ShapeDtypeStruct((n, GROUP_W), F32)] * N_PROJ + [jax.ShapeDtypeStruct((n, GATE_W), F32)]
    return pl.pallas_call(
        _in_proj_body,
        grid=(n // tm,),
        in_specs=[row(D_MODEL), _full((1, D_MODEL)), _full((D_MODEL, N_PROJ * GROUP_W)),
                  _full((D_MODEL, GATE_W)), _full((1, GROUP_W)), _full((1, GROUP_W)),
                  _full((GROUP_W, GROUP_W))],
        out_specs=[row(GROUP_W)] * N_PROJ + [row(GATE_W)],
        out_shape=outs,
        compiler_params=_params("arbitrary"),
        name="in_proj",
    )(x2d, g_mix, w_main, w_gate, gq_row, gk_row, gmat)


def _attn_body(q_ref, kp_ref, kc_ref, vp_ref, vc_ref, bias_ref, hmask_ref, o_ref, kwin, vwin,
               *, cq, nq, mask_first):
    tc = cq * nq
    i = pl.program_id(1)
    kwin[0:PAST_BAND, :] = kp_ref[...].astype(BF16)
    kwin[PAST_BAND:PAST_BAND + tc, :] = kc_ref[...].astype(BF16)
    vwin[0:PAST_BAND, :] = vp_ref[...].astype(BF16)
    vwin[PAST_BAND:PAST_BAND + tc, :] = vc_ref[...].astype(BF16)
    pad_rows = kwin.shape[0] - PAST_BAND - tc
    kwin[PAST_BAND + tc:, :] = jnp.zeros((pad_rows, GROUP_W), BF16)
    vwin[PAST_BAND + tc:, :] = jnp.zeros((pad_rows, GROUP_W), BF16)

    hm = hmask_ref[...]
    bias = bias_ref[...]
    kk = lax.broadcasted_iota(jnp.int32, (1, KEY_WIN), 1)

    def chunk(j, carry):
        r0 = pl.multiple_of(j * cq, cq)
        q = q_ref[pl.ds(r0, cq), :] * (HEAD_DIM ** -0.5)
        qm = (jnp.concatenate([q] * N_HEADS, axis=0) * hm).astype(BF16)
        kw = kwin[pl.ds(r0, KEY_WIN), :]
        s = lax.dot_general(qm, kw, (((1,), (1,)), ((), ())), preferred_element_type=F32) + bias
        if mask_first:
            first_valid = jnp.where(i == 0, PAST_BAND - r0, 0)
            s = jnp.where(kk >= first_valid, s, NEG_BIG)
        m = jnp.max(s, axis=-1, keepdims=True)
        p = jnp.exp(s - m)
        l = jnp.sum(p, axis=-1, keepdims=True)
        vw = vwin[pl.ds(r0, KEY_WIN), :]
        o_all = jnp.dot(p.astype(BF16), vw, preferred_element_type=F32) / l * hm
        o = o_all[0:cq]
        for h in range(1, N_HEADS):
            o = o + o_all[h * cq:(h + 1) * cq]
        o_ref[pl.ds(r0, cq), :] = o
        return carry

    lax.fori_loop(0, nq, chunk, 0)


def _attention(q, k_prev_src, k_cur_src, v_prev_src, v_cur_src, bias, hmask, *, batch, tiles, cq, nq,
               prev_index, mask_first):
    tc = cq * nq
    cur = pl.BlockSpec((tc, GROUP_W), lambda b, i: (b * tiles + i, 0))
    prev = pl.BlockSpec((PAST_BAND, GROUP_W), prev_index)
    win_rows = (nq - 1) * cq + KEY_WIN
    return pl.pallas_call(
        functools.partial(_attn_body, cq=cq, nq=nq, mask_first=mask_first),
        grid=(batch, tiles),
        in_specs=[cur, prev, cur, prev, cur, _full((N_HEADS * cq, KEY_WIN)), _full((N_HEADS * cq, GROUP_W))],
        out_specs=cur,
        out_shape=jax.ShapeDtypeStruct(q.shape, F32),
        scratch_shapes=[pltpu.VMEM((win_rows, GROUP_W), BF16), pltpu.VMEM((win_rows, GROUP_W), BF16)],
        compiler_params=_params("arbitrary", "arbitrary"),
        name="band_attention",
    )(q, k_prev_src, k_cur_src, v_prev_src, v_cur_src, bias, hmask)


def _rel_table(rel_bias_l, cq):
    nk = PAST_BAND + cq
    rel = np.clip(PAST_BAND + np.arange(cq)[:, None] - np.arange(nk)[None, :], -REL_CLIP, REL_CLIP) + REL_CLIP
    tab = rel_bias_l[:, rel].astype(F32)
    tab = jnp.pad(tab, ((0, 0), (0, 0), (0, KEY_WIN - nk)), constant_values=NEG_BIG)
    return tab.reshape(N_HEADS * cq, KEY_WIN)


def _head_row_mask(cq):
    h_row = np.repeat(np.arange(N_HEADS), cq)
    h_col = np.arange(GROUP_W) // HEAD_DIM
    return jnp.asarray((h_row[:, None] == h_col[None, :]).astype(np.float32))


def _log_sigmoid(x):
    return jnp.minimum(x, 0.0) - jnp.log(1.0 + jnp.exp(-jnp.abs(x)))


def _mlstm_body(q_ref, k_ref, v_ref, o_ref, g_ref, c0_ref, n0_ref, m0_ref,
                expand_ref, gbias_ref, bd_ref, gmat_ref, ltri_ref, eye_ref, causal_ref, gml_ref,
                h_ref, ct_ref, n_ref, m_ref, ct_s, n_s, m_s, *, valid):
    c = pl.program_id(1)
    bd = bd_ref[...]

    @pl.when(c == 0)
    def _():
        ct_s[...] = jnp.concatenate([c0_ref[0]] * N_HEADS, axis=0) * bd
        n_s[...] = n0_ref[0]
        m_s[...] = m0_ref[0]

    gp = jnp.dot(g_ref[...], expand_ref[...], preferred_element_type=F32, precision=HIGHEST) + gbias_ref[...]
    log_i = gp[:, :GROUP_W]
    log_f = _log_sigmoid(gp[:, GROUP_W:])
    if valid < CHUNK:
        live = lax.broadcasted_iota(jnp.int32, (CHUNK, GROUP_W), 0) < valid
        log_i = jnp.where(live, log_i, -jnp.inf)
        log_f = jnp.where(live, log_f, 0.0)
    cum_f = jnp.dot(ltri_ref[...], log_f, preferred_element_type=F32, precision=HIGHEST)

    eye = eye_ref[...] > 0.5
    b_row = jnp.sum(jnp.where(eye, log_i - cum_f, 0.0), axis=0, keepdims=True)
    m_prev = m_s[...]
    log_inter = cum_f + m_prev
    log_d = jnp.where(causal_ref[...] > 0.5, cum_f + b_row, -jnp.inf)
    max_d = jnp.concatenate(
        [jnp.broadcast_to(jnp.max(log_d[:, h * HEAD_DIM:(h + 1) * HEAD_DIM], axis=-1, keepdims=True),
                          (CHUNK, HEAD_DIM)) for h in range(N_HEADS)], axis=1)
    m_t = jnp.maximum(log_inter, max_d)
    w_intra = jnp.exp(log_d - m_t)
    w_inter = jnp.exp(log_inter - m_t)

    q = q_ref[...] * (HEAD_DIM ** -0.5)
    k = k_ref[...]
    v = v_ref[...]
    qb = q.astype(BF16)
    kbd = (jnp.concatenate([k] * N_HEADS, axis=0) * bd).astype(BF16)
    vbd = (jnp.concatenate([v] * N_HEADS, axis=0) * bd).astype(BF16)
    s = lax.dot_general(qb, kbd, (((1,), (1,)), ((), ())), preferred_element_type=F32) * w_intra
    ct = ct_s[...]
    n_prev = n_s[...]
    num = (jnp.dot(s.astype(BF16), vbd, preferred_element_type=F32)
           + w_inter * lax.dot_general(qb, ct.astype(BF16), (((1,), (1,)), ((), ())),
                                       preferred_element_type=F32))
    gmat = gmat_ref[...]
    den_terms = (s + w_inter * q * n_prev) * float(HEAD_DIM)
    den_hi = den_terms.astype(BF16)
    den_lo = (den_terms - den_hi.astype(F32)).astype(BF16)
    den = (jnp.dot(den_hi, gmat, preferred_element_type=F32)
           + jnp.dot(den_lo, gmat, preferred_element_type=F32))
    hb = num / jnp.maximum(jnp.abs(den), jnp.exp(-m_t))

    m_new = m_t[CHUNK - 1:CHUNK, :]
    cum_last = cum_f[CHUNK - 1:CHUNK, :]
    w_state = jnp.exp(cum_last - cum_f + log_i - m_new)
    decay = jnp.exp(cum_last + m_prev - m_new)
    kw = k * w_state
    upd = lax.dot_general(v.astype(BF16), kw.astype(BF16), (((0,), (0,)), ((), ())),
                          preferred_element_type=F32)
    ct_new = decay * ct + upd * bd
    n_new = decay * n_prev + jnp.sum(kw, axis=0, keepdims=True)
    ct_s[...] = ct_new
    n_s[...] = n_new
    m_s[...] = m_new

    msq = jnp.dot((hb * hb).astype(BF16), gmat, preferred_element_type=F32)
    h_ref[...] = jax.nn.sigmoid(o_ref[...]) * (hb * lax.rsqrt(msq + RMS_EPS) * gml_ref[...])

    @pl.when(c == pl.num_programs(1) - 1)
    def _():
        acc = ct_new[0:HEAD_DIM]
        for h in range(1, N_HEADS):
            acc = acc + ct_new[h * HEAD_DIM:(h + 1) * HEAD_DIM]
        ct_ref[0] = acc
        n_ref[0] = n_new
        m_ref[0] = m_new


def _mlstm(mq, mk, mv, mo, gates, c0t, n0, m0, consts, *, batch, chunks, valid):
    row = lambda w: pl.BlockSpec((CHUNK, w), lambda b, c: (b * chunks + c, 0))
    per_b = lambda r: pl.BlockSpec((1, r, GROUP_W), lambda b, c: (b, 0, 0))
    expand, gbias, bd, gmat, ltri, eye, causal, gml = consts
    return pl.pallas_call(
        functools.partial(_mlstm_body, valid=valid),
        grid=(batch, chunks),
        in_specs=[row(GROUP_W)] * 4 + [row(GATE_W), per_b(HEAD_DIM), per_b(1), per_b(1),
                  _full((GATE_W, 2 * GROUP_W)), _full((1, 2 * GROUP_W)), _full((GROUP_W, GROUP_W)),
                  _full((GROUP_W, GROUP_W)), _full((CHUNK, CHUNK)), _full((CHUNK, GROUP_W)),
                  _full((CHUNK, GROUP_W)), _full((1, GROUP_W))],
        out_specs=[row(GROUP_W), per_b(HEAD_DIM), per_b(1), per_b(1)],
        out_shape=[jax.ShapeDtypeStruct(mq.shape, F32),
                   jax.ShapeDtypeStruct((batch, HEAD_DIM, GROUP_W), F32),
                   jax.ShapeDtypeStruct((batch, 1, GROUP_W), F32),
                   jax.ShapeDtypeStruct((batch, 1, GROUP_W), F32)],
        scratch_shapes=[pltpu.VMEM((GROUP_W, GROUP_W), F32), pltpu.VMEM((1, GROUP_W), F32),
                        pltpu.VMEM((1, GROUP_W), F32)],
        compiler_params=_params("arbitrary", "arbitrary"),
        name="mlstm",
    )(mq, mk, mv, mo, gates, c0t, n0, m0, expand, gbias, bd, gmat, ltri, eye, causal, gml)


def _mlstm_consts(b_igate_l, b_fgate_l, g_mlstm_l):
    expand = np.zeros((GATE_W, 2 * GROUP_W), np.float32)
    for h in range(N_HEADS):
        expand[h, h * HEAD_DIM:(h + 1) * HEAD_DIM] = 1.0
        expand[N_HEADS + h, GROUP_W + h * HEAD_DIM:GROUP_W + (h + 1) * HEAD_DIM] = 1.0
    gbias = jnp.concatenate([jnp.repeat(b_igate_l.astype(F32), HEAD_DIM),
                             jnp.repeat(b_fgate_l.astype(F32), HEAD_DIM)])[None, :]
    bd = _head_block_diag()
    ltri = np.tril(np.ones((CHUNK, CHUNK), np.float32))
    s_of_lane = np.arange(GROUP_W) % HEAD_DIM
    t = np.arange(CHUNK)
    eye = (t[:, None] == s_of_lane[None, :]).astype(np.float32)
    causal = (s_of_lane[None, :] <= t[:, None]).astype(np.float32)
    return (jnp.asarray(expand), gbias, jnp.asarray(bd), jnp.asarray(bd / HEAD_DIM, BF16), jnp.asarray(ltri),
            jnp.asarray(eye), jnp.asarray(causal), g_mlstm_l.astype(F32).reshape(1, GROUP_W))


def _out_proj_body(x_ref, att_ref, hm_ref, wa_ref, wm_ref, gffn_ref, wr_ref, br_ref,
                   y_ref, xf_ref, logit_ref):
    y = (x_ref[...]
         + jnp.dot(att_ref[...].astype(BF16), wa_ref[...], preferred_element_type=F32)
         + jnp.dot(hm_ref[...].astype(BF16), wm_ref[...], preferred_element_type=F32))
    y_ref[...] = y
    xf = y * lax.rsqrt(jnp.mean(y * y, axis=-1, keepdims=True) + RMS_EPS) * gffn_ref[...]
    xf_ref[...] = xf
    logit_ref[...] = jnp.dot(xf, wr_ref[...], preferred_element_type=F32, precision=HIGHEST) + br_ref[...]


def _out_proj(x2d, att, hm, wa, wm, g_ffn, w_router, b_router, tm):
    n = x2d.shape[0]
    row = lambda w: pl.BlockSpec((tm, w), lambda i: (i, 0))
    return pl.pallas_call(
        _out_proj_body,
        grid=(n // tm,),
        in_specs=[row(D_MODEL), row(GROUP_W), row(GROUP_W), _full((GROUP_W, D_MODEL)), _full((GROUP_W, D_MODEL)),
                  _full((1, D_MODEL)), _full((D_MODEL, GATE_W)), _full((1, GATE_W))],
        out_specs=[row(D_MODEL), row(D_MODEL), row(GATE_W)],
        out_shape=[jax.ShapeDtypeStruct((n, D_MODEL), F32), jax.ShapeDtypeStruct((n, D_MODEL), F32),
                   jax.ShapeDtypeStruct((n, GATE_W), F32)],
        compiler_params=_params("arbitrary"),
        name="out_proj_router",
    )(x2d, att, hm, wa, wm, g_ffn, w_router, b_router)


def _row_copy(src_hbm, row, dst, slot, sem):
    return pltpu.make_async_copy(src_hbm.at[pl.ds(row, 1)], dst.at[pl.ds(slot, 1)], sem)


def _expert_body(te_ref, nused_ref, rt_cur_ref, rt_nxt_ref, xf_hbm, wup_ref, wdn_ref, perm_ref,
                 bg_ref, bl_ref, bd_ref, y_ref, xbuf, wg_s, wl_s, wd_s, sem, *, tm):
    i = pl.program_id(0)
    n_used = nused_ref[0]
    slot = lax.rem(i, 2)

    def gather(rt_ref, dst_slot):
        def issue(r, carry):
            _row_copy(xf_hbm, rt_ref[0, 0, r], xbuf.at[dst_slot], r, sem.at[dst_slot]).start()
            return carry

        lax.fori_loop(0, tm, issue, 0, unroll=8)

    @pl.when(i >= n_used)
    def _():
        y_ref[...] = jnp.zeros((tm, D_MODEL), F32)

    @pl.when(i == 0)
    def _():
        gather(rt_cur_ref, 0)

    @pl.when(i + 1 < n_used)
    def _():
        gather(rt_nxt_ref, 1 - slot)

    @pl.when((i == 0) | (te_ref[i] != te_ref[jnp.maximum(i - 1, 0)]))
    def _():
        perm = perm_ref[...]
        for c in range(2 * D_FF // 256):
            blk = wup_ref[0, :, c * 256:(c + 1) * 256].astype(BF16)
            sep = jnp.dot(blk, perm, preferred_element_type=F32).astype(BF16)
            wg_s[:, c * 128:(c + 1) * 128] = sep[:, :128]
            wl_s[:, c * 128:(c + 1) * 128] = sep[:, 128:]
        wd_s[...] = wdn_ref[0].astype(BF16)

    @pl.when(i < n_used)
    def _():
        def drain(r, carry):
            _row_copy(xf_hbm, 0, xbuf.at[slot], 0, sem.at[slot]).wait()
            return carry

        lax.fori_loop(0, tm, drain, 0, unroll=8)

        x = xbuf[slot].astype(BF16)
        glu = jnp.minimum(jnp.dot(x, wg_s[...], preferred_element_type=F32) + bg_ref[0], SWIGLU_LIMIT)
        lin = jnp.clip(jnp.dot(x, wl_s[...], preferred_element_type=F32) + bl_ref[0], -SWIGLU_LIMIT, SWIGLU_LIMIT)
        act = glu * jax.nn.sigmoid(SWIGLU_ALPHA * glu) * (lin + 1.0)
        y_ref[...] = jnp.dot(act.astype(BF16), wd_s[...], preferred_element_type=F32) + bd_ref[0]


def _deinterleave_perm():
    p = np.zeros((256, 256), np.float32)
    j = np.arange(128)
    p[2 * j, j] = 1.0
    p[2 * j + 1, 128 + j] = 1.0
    return jnp.asarray(p, BF16)


def _experts(tile_expert, n_used, row_token, xf, w_up, w_down, b_glu, b_lin, b_down, *, tm):
    n_tiles = row_token.shape[0]
    wspec = lambda k, n: pl.BlockSpec((1, k, n), lambda i, te, nu: (te[i], 0, 0))
    rows = lambda f: pl.BlockSpec((1, 1, tm), lambda i, te, nu: (f(i), 0, 0), memory_space=pltpu.SMEM)
    grid_spec = pltpu.PrefetchScalarGridSpec(
        num_scalar_prefetch=2,
        grid=(n_tiles,),
        in_specs=[rows(lambda i: i), rows(lambda i: jnp.minimum(i + 1, n_tiles - 1)),
                  pl.BlockSpec(memory_space=pl.ANY),
                  wspec(D_MODEL, 2 * D_FF), wspec(D_FF, D_MODEL),
                  pl.BlockSpec((256, 256), lambda i, te, nu: (0, 0)),
                  wspec(1, D_FF), wspec(1, D_FF), wspec(1, D_MODEL)],
        out_specs=pl.BlockSpec((tm, D_MODEL), lambda i, te, nu: (i, 0)),
        scratch_shapes=[pltpu.VMEM((2, tm, D_MODEL), F32),
                        pltpu.VMEM((D_MODEL, D_FF), BF16), pltpu.VMEM((D_MODEL, D_FF), BF16),
                        pltpu.VMEM((D_FF, D_MODEL), BF16), pltpu.SemaphoreType.DMA((2,))],
    )
    return pl.pallas_call(
        functools.partial(_expert_body, tm=tm),
        grid_spec=grid_spec,
        out_shape=jax.ShapeDtypeStruct((n_tiles * tm, D_MODEL), F32),
        compiler_params=_params("arbitrary"),
        name="expert_ffn",
    )(tile_expert, n_used, row_token, row_token, xf, w_up, w_down, _deinterleave_perm(), b_glu, b_lin, b_down)


def _combine_body(dest_cur_ref, dest_nxt_ref, y_ref, gate_ref, rows_hbm, out_ref, buf, sem, *, tt):
    i = pl.program_id(0)
    slot = lax.rem(i, 2)

    def gather(dest_ref, dst_slot):
        for k in range(TOP_K):
            def issue(t, carry):
                _row_copy(rows_hbm, dest_ref[0, 0, t * TOP_K + k], buf.at[dst_slot, k], t, sem.at[dst_slot]).start()
                return carry

            lax.fori_loop(0, tt, issue, 0, unroll=8)

    @pl.when(i == 0)
    def _():
        gather(dest_cur_ref, 0)

    @pl.when(i + 1 < pl.num_programs(0))
    def _():
        gather(dest_nxt_ref, 1 - slot)

    def drain(a, carry):
        _row_copy(rows_hbm, 0, buf.at[slot, 0], 0, sem.at[slot]).wait()
        return carry

    lax.fori_loop(0, tt * TOP_K, drain, 0, unroll=8)

    acc = y_ref[...]
    gate = gate_ref[...]
    for k in range(TOP_K):
        acc = acc + gate[:, k:k + 1] * buf[slot, k]
    out_ref[...] = acc


def _combine(dest, y, gates, y_rows, *, tt):
    n = y.shape[0]
    nt = n // tt
    idx = lambda f: pl.BlockSpec((1, 1, tt * TOP_K), lambda i: (f(i), 0, 0), memory_space=pltpu.SMEM)
    return pl.pallas_call(
        functools.partial(_combine_body, tt=tt),
        grid=(nt,),
        in_specs=[idx(lambda i: i), idx(lambda i: jnp.minimum(i + 1, nt - 1)),
                  pl.BlockSpec((tt, D_MODEL), lambda i: (i, 0)),
                  pl.BlockSpec((tt, TOP_K), lambda i: (i, 0)),
                  pl.BlockSpec(memory_space=pl.ANY)],
        out_specs=pl.BlockSpec((tt, D_MODEL), lambda i: (i, 0)),
        out_shape=jax.ShapeDtypeStruct((n, D_MODEL), F32),
        scratch_shapes=[pltpu.VMEM((2, TOP_K, tt, D_MODEL), F32), pltpu.SemaphoreType.DMA((2,))],
        compiler_params=_params("arbitrary"),
        name="expert_combine",
    )(dest, dest, y, gates, y_rows)


def _route(logits, tm):
    n = logits.shape[0]
    n_asg = n * TOP_K
    top_val, top_idx = lax.top_k(logits, TOP_K)
    gates = jax.nn.softmax(top_val, axis=-1)
    e = top_idx.reshape(n_asg)
    onehot = (e[:, None] == jnp.arange(N_EXPERTS)[None, :]).astype(jnp.int32)
    csum = jnp.cumsum(onehot, axis=0)
    rank = jnp.sum((csum - onehot) * onehot, axis=-1)
    counts = csum[-1]
    padded = (counts + tm - 1) // tm * tm
    pad_end = jnp.cumsum(padded)
    dest = (pad_end - padded)[e] + rank
    n_tiles = -(-(n_asg + N_EXPERTS * (tm - 1)) // tm)
    row_token = jnp.zeros((n_tiles * tm,), jnp.int32).at[dest].set(jnp.arange(n_asg, dtype=jnp.int32) // TOP_K)
    tile_expert = jnp.minimum(jnp.searchsorted(pad_end, jnp.arange(n_tiles) * tm, side='right'),
                              N_EXPERTS - 1).astype(jnp.int32)
    n_used = (pad_end[-1:] // tm).astype(jnp.int32)
    return gates, dest.astype(jnp.int32), row_token.reshape(n_tiles, 1, tm), tile_expert, n_used


def _moe(y, xf, logits, ffn_w, *, tm, tt):
    n = y.shape[0]
    gates, dest, row_token, tile_expert, n_used = _route(logits[:, :N_EXPERTS], tm)
    y_rows = _experts(tile_expert, n_used, row_token, xf, *ffn_w, tm=tm)
    return _combine(dest.reshape(n // tt, 1, tt * TOP_K), y, gates, y_rows, tt=tt)


def _layer(x, lw, ffn_w, cache, state, *, row_tile, moe_tile, comb_tile):
    b, t, _ = x.shape
    n = b * t
    x2d = x.reshape(n, D_MODEL)
    q, k, v, mq, mk, mv, mo, gates = _in_proj(x2d, lw["g_mix"], lw["w_main"], lw["w_gate"], lw["gq"], lw["gk"],
                                              lw["gmat"], row_tile)
    if cache is None:
        tiles = t // PAST_BAND
        nq = PAST_BAND // CHUNK
        att = _attention(q, k, k, v, v, lw["bias_prompt"], lw["hmask_prompt"], batch=b, tiles=tiles, cq=CHUNK,
                         nq=nq, prev_index=lambda bi, i: (bi * tiles + jnp.maximum(i - 1, 0), 0), mask_first=True)
        keep = min(PAST_BAND, t)
        k_new = k.reshape(b, t, N_HEADS, HEAD_DIM)[:, t - keep:]
        v_new = v.reshape(b, t, N_HEADS, HEAD_DIM)[:, t - keep:]
    else:
        ck, cv = cache
        att = _attention(q, ck.reshape(b * PAST_BAND, GROUP_W), k, cv.reshape(b * PAST_BAND, GROUP_W), v,
                         lw["bias_sample"], lw["hmask_sample"], batch=b, tiles=1, cq=t, nq=1,
                         prev_index=lambda bi, i: (bi, 0), mask_first=False)
        k_new = k.reshape(b, t, N_HEADS, HEAD_DIM)
        v_new = v.reshape(b, t, N_HEADS, HEAD_DIM)

    chunks = -(-t // CHUNK)
    valid = t if t < CHUNK else CHUNK
    tp = chunks * CHUNK

    def pad_t(a):
        if tp == t:
            return a
        return jnp.pad(a.reshape(b, t, -1), ((0, 0), (0, tp - t), (0, 0))).reshape(b * tp, -1)

    if state is None:
        c0t = jnp.zeros((b, HEAD_DIM, GROUP_W), F32)
        n0 = jnp.zeros((b, 1, GROUP_W), F32)
        m0 = jnp.zeros((b, 1, GROUP_W), F32)
    else:
        c_in, n_in, m_in = state
        c0t = c_in.astype(F32).transpose(0, 3, 1, 2).reshape(b, HEAD_DIM, GROUP_W)
        n0 = n_in.astype(F32).reshape(b, 1, GROUP_W)
        m0 = jnp.repeat(m_in.astype(F32), HEAD_DIM, axis=-1).reshape(b, 1, GROUP_W)
    hm, ct, n_out, m_out = _mlstm(pad_t(mq), pad_t(mk), pad_t(mv), pad_t(mo), pad_t(gates), c0t, n0, m0,
                                  lw["mlstm_consts"], batch=b, chunks=chunks, valid=valid)
    if tp != t:
        hm = hm.reshape(b, tp, GROUP_W)[:, :t].reshape(n, GROUP_W)
    c_new = ct.reshape(b, HEAD_DIM, N_HEADS, HEAD_DIM).transpose(0, 2, 3, 1)
    n_new = n_out.reshape(b, N_HEADS, HEAD_DIM)
    m_new = m_out.reshape(b, N_HEADS, HEAD_DIM)[:, :, 0]

    y, xf, logits = _out_proj(x2d, att, hm, lw["w_out_att"], lw["w_out_mlstm"], lw["g_ffn"], lw["w_router"],
                              lw["b_router"], row_tile)
    out = _moe(y, xf, logits, ffn_w, tm=moe_tile, tt=comb_tile)
    return out.reshape(b, t, D_MODEL), (k_new, v_new, c_new, n_new, m_new)


def kernel(x_prompt, x_sample, cache_k, cache_v, state_C, state_n, state_m, g_mix, w_in, g_q, g_k, rel_bias,
           b_igate, b_fgate, g_mlstm, w_out, g_ffn, w_router, b_router, w_up, b_up, w_down, b_down):
    depth = w_in.shape[0]
    yp, ys = x_prompt, x_sample
    bs, ts = x_sample.shape[0], x_sample.shape[1]
    st_prompt, st_sample = [], []
    n_main = N_PROJ * GROUP_W
    gmat = jnp.asarray(_head_block_diag() / HEAD_DIM, BF16)
    for l in range(depth):
        lw = dict(
            g_mix=g_mix[l].astype(F32)[None, :],
            w_main=w_in[l][:, :n_main].astype(BF16),
            w_gate=jnp.pad(w_in[l][:, n_main:].astype(F32), ((0, 0), (0, GATE_W - 2 * N_HEADS))),
            gq=jnp.tile(g_q[l].astype(F32), N_HEADS)[None, :],
            gk=jnp.tile(g_k[l].astype(F32), N_HEADS)[None, :],
            gmat=gmat,
            bias_prompt=_rel_table(rel_bias[l], CHUNK),
            hmask_prompt=_head_row_mask(CHUNK),
            bias_sample=_rel_table(rel_bias[l], ts),
            hmask_sample=_head_row_mask(ts),
            mlstm_consts=_mlstm_consts(b_igate[l], b_fgate[l], g_mlstm[l]),
            w_out_att=w_out[l][:GROUP_W].astype(BF16),
            w_out_mlstm=w_out[l][GROUP_W:].astype(BF16),
            g_ffn=g_ffn[l].astype(F32)[None, :],
            w_router=jnp.pad(w_router[l].astype(F32), ((0, 0), (0, GATE_W - N_EXPERTS))),
            b_router=jnp.pad(b_router[l].astype(F32), (0, GATE_W - N_EXPERTS))[None, :],
        )
        ffn_w = (w_up[l].astype(F32), w_down[l].astype(F32),
                 b_up[l][:, None, 0::2].astype(F32), b_up[l][:, None, 1::2].astype(F32),
                 b_down[l][:, None, :].astype(F32))
        yp, sp = _layer(yp, lw, ffn_w, None, None, row_tile=512, moe_tile=512, comb_tile=256)
        cache = (cache_k[l].reshape(bs, PAST_BAND, GROUP_W), cache_v[l].reshape(bs, PAST_BAND, GROUP_W))
        ys, ss = _layer(ys, lw, ffn_w, cache, (state_C[l], state_n[l], state_m[l]),
                        row_tile=512, moe_tile=128, comb_tile=256)
        st_prompt.append(sp)
        st_sample.append(ss)
    k_p, v_p, c_p, n_p, m_p = [jnp.stack(a) for a in zip(*st_prompt)]
    k_s, v_s, c_s, n_s, m_s = [jnp.stack(a) for a in zip(*st_sample)]
    return (yp, ys, k_p, v_p, c_p, n_p, m_p, k_s, v_s, c_s, n_s, m_s)
```

```python
import functools

import numpy as np
import jax
import jax.numpy as jnp
from jax import lax
from jax.experimental import pallas as pl
from jax.experimental.pallas import tpu as pltpu

F32 = jnp.float32
BF16 = jnp.bfloat16
I32 = jnp.int32
HIGHEST = lax.Precision.HIGHEST

D_MODEL = 1024
N_HEADS = 8
HEAD_DIM = 64
GROUP_W = N_HEADS * HEAD_DIM
N_PROJ = 7
LANES = 128
CHUNK = 64
PAST_BAND = 512
KEY_WIN = 640
REL_CLIP = 256
N_EXPERTS = 32
TOP_K = 4
D_FF = 1024
SWIGLU_ALPHA = 1.702
SWIGLU_LIMIT = 7.0
RMS_EPS = 1e-6
NEG_BIG = -1e30
ROW_TILE = 512
EXPERT_TILE = 512
TOKEN_TILE = 256
SUBLANES = 8
MLSTM_STREAMS = 4
VMEM_LIMIT_BYTES = 56 * 1024 * 1024


def _params(*sem):
    return pltpu.CompilerParams(dimension_semantics=sem, vmem_limit_bytes=VMEM_LIMIT_BYTES)


def _head_block_diag():
    h = np.arange(GROUP_W) // HEAD_DIM
    return (h[:, None] == h[None, :]).astype(np.float32)


def _full(shape):
    return pl.BlockSpec(shape, lambda *_: (0,) * len(shape))


def _in_proj_body(x_ref, gmix_ref, w_ref, wg_ref, gq_ref, gk_ref, gmat_ref,
                  q_ref, k_ref, v_ref, mq_ref, mk_ref, mv_ref, mo_ref, gate_ref):
    x = x_ref[...]
    xn = x * lax.rsqrt(jnp.mean(x * x, axis=-1, keepdims=True) + RMS_EPS) * gmix_ref[...]
    xb = xn.astype(BF16)

    def proj(j):
        return jnp.dot(xb, w_ref[:, j * GROUP_W:(j + 1) * GROUP_W], preferred_element_type=F32)

    def head_norm(a, g_ref):
        msq = jnp.dot((a * a).astype(BF16), gmat_ref[...], preferred_element_type=F32)
        return a * lax.rsqrt(msq + RMS_EPS) * g_ref[...]

    q_ref[...] = head_norm(proj(0), gq_ref)
    k_ref[...] = head_norm(proj(1), gk_ref)
    v_ref[...] = proj(2)
    mq_ref[...] = proj(3)
    mk_ref[...] = proj(4)
    mv_ref[...] = proj(5)
    mo_ref[...] = proj(6)
    gate_ref[...] = jnp.dot(xn, wg_ref[...], preferred_element_type=F32, precision=HIGHEST)


def _in_proj(x2d, g_mix, w_main, w_gate, gq_row, gk_row, gmat):
    n = x2d.shape[0]
    tm = ROW_TILE
    row = lambda w: pl.BlockSpec((tm, w), lambda i: (i, 0))
    outs = [jax.ShapeDtypeStruct((n, GROUP_W), F32)] * N_PROJ + [jax.ShapeDtypeStruct((n, LANES), F32)]
    return pl.pallas_call(
        _in_proj_body,
        grid=(n // tm,),
        in_specs=[row(D_MODEL), _full((1, D_MODEL)), _full((D_MODEL, N_PROJ * GROUP_W)),
                  _full((D_MODEL, LANES)), _full((1, GROUP_W)), _full((1, GROUP_W)),
                  _full((GROUP_W, GROUP_W))],
        out_specs=[row(GROUP_W)] * N_PROJ + [row(LANES)],
        out_shape=outs,
        compiler_params=_params("arbitrary"),
        name="in_proj",
    )(x2d, g_mix, w_main, w_gate, gq_row, gk_row, gmat)


def _attn_body(q_ref, kp_ref, kc_ref, vp_ref, vc_ref, bias_ref, hmask_ref, o_ref, kwin, vwin,
               *, cq, nq, mask_first):
    tc = cq * nq
    i = pl.program_id(1)
    kwin[0:PAST_BAND, :] = kp_ref[...].astype(BF16)
    kwin[PAST_BAND:PAST_BAND + tc, :] = kc_ref[...].astype(BF16)
    vwin[0:PAST_BAND, :] = vp_ref[...].astype(BF16)
    vwin[PAST_BAND:PAST_BAND + tc, :] = vc_ref[...].astype(BF16)
    pad_rows = kwin.shape[0] - PAST_BAND - tc
    kwin[PAST_BAND + tc:, :] = jnp.zeros((pad_rows, GROUP_W), BF16)
    vwin[PAST_BAND + tc:, :] = jnp.zeros((pad_rows, GROUP_W), BF16)

    hm = hmask_ref[...]
    bias = bias_ref[...]
    kk = lax.broadcasted_iota(I32, (1, KEY_WIN), 1)

    def chunk(j, carry):
        r0 = pl.multiple_of(j * cq, cq)
        q = q_ref[pl.ds(r0, cq), :] * (HEAD_DIM ** -0.5)
        qm = (jnp.concatenate([q] * N_HEADS, axis=0) * hm).astype(BF16)
        kw = kwin[pl.ds(r0, KEY_WIN), :]
        s = lax.dot_general(qm, kw, (((1,), (1,)), ((), ())), preferred_element_type=F32) + bias
        if mask_first:
            first_valid = jnp.where(i == 0, PAST_BAND - r0, 0)
            s = jnp.where(kk >= first_valid, s, NEG_BIG)
        m = jnp.max(s, axis=-1, keepdims=True)
        p = jnp.exp(s - m)
        l = jnp.sum(p, axis=-1, keepdims=True)
        vw = vwin[pl.ds(r0, KEY_WIN), :]
        o_all = jnp.dot(p.astype(BF16), vw, preferred_element_type=F32) / l * hm
        o = o_all[0:cq]
        for h in range(1, N_HEADS):
            o = o + o_all[h * cq:(h + 1) * cq]
        o_ref[pl.ds(r0, cq), :] = o
        return carry

    lax.fori_loop(0, nq, chunk, 0)


def _attention(q, k_prev_src, k_cur_src, v_prev_src, v_cur_src, bias, hmask, *, batch, tiles, cq, nq,
               prev_index, mask_first):
    tc = cq * nq
    cur = pl.BlockSpec((tc, GROUP_W), lambda b, i: (b * tiles + i, 0))
    prev = pl.BlockSpec((PAST_BAND, GROUP_W), prev_index)
    win_rows = (nq - 1) * cq + KEY_WIN
    return pl.pallas_call(
        functools.partial(_attn_body, cq=cq, nq=nq, mask_first=mask_first),
        grid=(batch, tiles),
        in_specs=[cur, prev, cur, prev, cur, _full((N_HEADS * cq, KEY_WIN)), _full((N_HEADS * cq, GROUP_W))],
        out_specs=cur,
        out_shape=jax.ShapeDtypeStruct(q.shape, F32),
        scratch_shapes=[pltpu.VMEM((win_rows, GROUP_W), BF16), pltpu.VMEM((win_rows, GROUP_W), BF16)],
        compiler_params=_params("arbitrary", "arbitrary"),
        name="band_attention",
    )(q, k_prev_src, k_cur_src, v_prev_src, v_cur_src, bias, hmask)


def _rel_table(rel_bias_l, cq):
    nk = PAST_BAND + cq
    rel = np.clip(PAST_BAND + np.arange(cq)[:, None] - np.arange(nk)[None, :], -REL_CLIP, REL_CLIP) + REL_CLIP
    tab = rel_bias_l[:, rel].astype(F32)
    tab = jnp.pad(tab, ((0, 0), (0, 0), (0, KEY_WIN - nk)), constant_values=NEG_BIG)
    return tab.reshape(N_HEADS * cq, KEY_WIN)


def _head_row_mask(cq):
    h_row = np.repeat(np.arange(N_HEADS), cq)
    h_col = np.arange(GROUP_W) // HEAD_DIM
    return jnp.asarray((h_row[:, None] == h_col[None, :]).astype(np.float32))


def _log_sigmoid(x):
    return jnp.minimum(x, 0.0) - jnp.log(1.0 + jnp.exp(-jnp.abs(x)))


def _mlstm_body(q_ref, k_ref, v_ref, o_ref, g_ref, c0_ref, n0_ref, m0_ref,
                expand_ref, gbias_ref, bd_ref, bdb_ref, gmat_ref, ltri_ref, eye_ref, causal_ref, gml_ref,
                h_ref, ct_ref, n_ref, m_ref, ct_s, n_s, m_s, *, valid, nb):
    c = pl.program_id(1)
    last = c == pl.num_programs(1) - 1
    bd = bd_ref[...]
    bdb = bdb_ref[...]
    gmat = gmat_ref[...]
    eye = eye_ref[...] > 0.5
    causal = causal_ref[...] > 0.5

    @pl.when(c == 0)
    def _():
        for b in range(nb):
            ct_s[b] = jnp.concatenate([c0_ref[b]] * N_HEADS, axis=0) * bd
            n_s[b] = n0_ref[b]
            m_s[b] = m0_ref[b]

    for b in range(nb):
        gp = jnp.dot(g_ref[b], expand_ref[...], preferred_element_type=F32, precision=HIGHEST) + gbias_ref[...]
        log_i = gp[:, :GROUP_W]
        log_f = _log_sigmoid(gp[:, GROUP_W:])
        if valid < CHUNK:
            live = lax.broadcasted_iota(I32, (CHUNK, GROUP_W), 0) < valid
            log_i = jnp.where(live, log_i, -jnp.inf)
            log_f = jnp.where(live, log_f, 0.0)
        cum_f = jnp.dot(ltri_ref[...], log_f, preferred_element_type=F32, precision=HIGHEST)

        b_row = jnp.sum(jnp.where(eye, log_i - cum_f, 0.0), axis=0, keepdims=True)
        m_prev = m_s[b]
        log_inter = cum_f + m_prev
        log_d = jnp.where(causal, cum_f + b_row, -jnp.inf)
        max_d = jnp.concatenate(
            [jnp.broadcast_to(jnp.max(log_d[:, h * HEAD_DIM:(h + 1) * HEAD_DIM], axis=-1, keepdims=True),
                              (CHUNK, HEAD_DIM)) for h in range(N_HEADS)], axis=1)
        m_t = jnp.maximum(log_inter, max_d)
        w_intra = jnp.exp(log_d - m_t)
        w_inter = jnp.exp(log_inter - m_t)

        q = q_ref[b] * (HEAD_DIM ** -0.5)
        k = k_ref[b]
        v = v_ref[b]
        qb = q.astype(BF16)
        vb = v.astype(BF16)
        kbd = jnp.concatenate([k.astype(BF16)] * N_HEADS, axis=0) * bdb
        vbd = jnp.concatenate([vb] * N_HEADS, axis=0) * bdb
        s = lax.dot_general(qb, kbd, (((1,), (1,)), ((), ())), preferred_element_type=F32) * w_intra
        ct = ct_s[b]
        n_prev = n_s[b]
        num = (jnp.dot(s.astype(BF16), vbd, preferred_element_type=F32)
               + w_inter * lax.dot_general(qb, ct.astype(BF16), (((1,), (1,)), ((), ())),
                                           preferred_element_type=F32))
        den_terms = (s + w_inter * q * n_prev) * float(HEAD_DIM)
        den_hi = den_terms.astype(BF16)
        den_lo = (den_terms - den_hi.astype(F32)).astype(BF16)
        den = (jnp.dot(den_hi, gmat, preferred_element_type=F32)
               + jnp.dot(den_lo, gmat, preferred_element_type=F32))
        hb = num / jnp.maximum(jnp.abs(den), jnp.exp(-m_t))

        m_new = m_t[CHUNK - 1:CHUNK, :]
        cum_last = cum_f[CHUNK - 1:CHUNK, :]
        w_state = jnp.exp(cum_last - cum_f + log_i - m_new)
        decay = jnp.exp(cum_last + m_prev - m_new)
        kw = k * w_state
        upd = lax.dot_general(vb, kw.astype(BF16), (((0,), (0,)), ((), ())),
                              preferred_element_type=F32)
        ct_new = decay * ct + upd * bd
        n_new = decay * n_prev + jnp.sum(kw, axis=0, keepdims=True)
        ct_s[b] = ct_new
        n_s[b] = n_new
        m_s[b] = m_new

        msq = jnp.dot((hb * hb).astype(BF16), gmat, preferred_element_type=F32)
        h_ref[b] = jax.nn.sigmoid(o_ref[b]) * (hb * lax.rsqrt(msq + RMS_EPS) * gml_ref[...])

    @pl.when(last)
    def _():
        for b in range(nb):
            ct_new = ct_s[b]
            acc = ct_new[0:HEAD_DIM]
            for h in range(1, N_HEADS):
                acc = acc + ct_new[h * HEAD_DIM:(h + 1) * HEAD_DIM]
            ct_ref[b] = acc
            n_ref[b] = n_s[b]
            m_ref[b] = m_s[b]


def _mlstm(mq, mk, mv, mo, gates, c0t, n0, m0, consts, *, valid):
    batch, t, _ = mq.shape
    chunks = t // CHUNK
    nb = MLSTM_STREAMS
    row = lambda w: pl.BlockSpec((nb, CHUNK, w), lambda g, c: (g, c, 0))
    per_b = lambda r: pl.BlockSpec((nb, r, GROUP_W), lambda g, c: (g, 0, 0))
    expand, gbias, bd, bdb, gmat, ltri, eye, causal, gml = consts
    return pl.pallas_call(
        functools.partial(_mlstm_body, valid=valid, nb=nb),
        grid=(batch // nb, chunks),
        in_specs=[row(GROUP_W)] * 4 + [row(LANES), per_b(HEAD_DIM), per_b(1), per_b(1),
                  _full((LANES, 2 * GROUP_W)), _full((1, 2 * GROUP_W)), _full((GROUP_W, GROUP_W)),
                  _full((GROUP_W, GROUP_W)), _full((GROUP_W, GROUP_W)), _full((CHUNK, CHUNK)),
                  _full((CHUNK, GROUP_W)), _full((CHUNK, GROUP_W)), _full((1, GROUP_W))],
        out_specs=[row(GROUP_W), per_b(HEAD_DIM), per_b(1), per_b(1)],
        out_shape=[jax.ShapeDtypeStruct(mq.shape, F32),
                   jax.ShapeDtypeStruct((batch, HEAD_DIM, GROUP_W), F32),
                   jax.ShapeDtypeStruct((batch, 1, GROUP_W), F32),
                   jax.ShapeDtypeStruct((batch, 1, GROUP_W), F32)],
        scratch_shapes=[pltpu.VMEM((nb, GROUP_W, GROUP_W), F32), pltpu.VMEM((nb, 1, GROUP_W), F32),
                        pltpu.VMEM((nb, 1, GROUP_W), F32)],
        compiler_params=_params("arbitrary", "arbitrary"),
        name="mlstm",
    )(mq, mk, mv, mo, gates, c0t, n0, m0, expand, gbias, bd, bdb, gmat, ltri, eye, causal, gml)


def _mlstm_consts(b_igate_l, b_fgate_l, g_mlstm_l):
    expand = np.zeros((LANES, 2 * GROUP_W), np.float32)
    for h in range(N_HEADS):
        expand[h, h * HEAD_DIM:(h + 1) * HEAD_DIM] = 1.0
        expand[N_HEADS + h, GROUP_W + h * HEAD_DIM:GROUP_W + (h + 1) * HEAD_DIM] = 1.0
    gbias = jnp.concatenate([jnp.repeat(b_igate_l.astype(F32), HEAD_DIM),
                             jnp.repeat(b_fgate_l.astype(F32), HEAD_DIM)])[None, :]
    bd = _head_block_diag()
    ltri = np.tril(np.ones((CHUNK, CHUNK), np.float32))
    s_of_lane = np.arange(GROUP_W) % HEAD_DIM
    t = np.arange(CHUNK)
    eye = (t[:, None] == s_of_lane[None, :]).astype(np.float32)
    causal = (s_of_lane[None, :] <= t[:, None]).astype(np.float32)
    return (jnp.asarray(expand), gbias, jnp.asarray(bd), jnp.asarray(bd, BF16), jnp.asarray(bd / HEAD_DIM, BF16),
            jnp.asarray(ltri), jnp.asarray(eye), jnp.asarray(causal), g_mlstm_l.astype(F32).reshape(1, GROUP_W))


def _out_proj_body(xp_ref, xs_ref, ap_ref, as_ref, hp_ref, hs_ref, wa_ref, wm_ref, gffn_ref, wr_ref, br_ref,
                   y_ref, xf_ref, logit_ref, *, prompt_tiles):
    is_prompt = pl.program_id(0) < prompt_tiles
    x = jnp.where(is_prompt, xp_ref[...], xs_ref[...])
    att = jnp.where(is_prompt, ap_ref[...], as_ref[...])
    hm = jnp.where(is_prompt, hp_ref[...], hs_ref[...])
    y = (x + jnp.dot(att.astype(BF16), wa_ref[...], preferred_element_type=F32)
         + jnp.dot(hm.astype(BF16), wm_ref[...], preferred_element_type=F32))
    y_ref[...] = y
    xf = y * lax.rsqrt(jnp.mean(y * y, axis=-1, keepdims=True) + RMS_EPS) * gffn_ref[...]
    xf_ref[...] = xf
    logit_ref[...] = jnp.dot(xf, wr_ref[...], preferred_element_type=F32, precision=HIGHEST) + br_ref[...]


def _out_proj(xp, xs, att_p, att_s, hm_p, hm_s, wa, wm, g_ffn, w_router, b_router):
    tm = ROW_TILE
    pt, st = xp.shape[0] // tm, xs.shape[0] // tm
    n = xp.shape[0] + xs.shape[0]
    p_row = lambda w: pl.BlockSpec((tm, w), lambda i: (jnp.minimum(i, pt - 1), 0))
    s_row = lambda w: pl.BlockSpec((tm, w), lambda i: (jnp.maximum(i - pt, 0), 0))
    row = lambda w: pl.BlockSpec((tm, w), lambda i: (i, 0))
    return pl.pallas_call(
        functools.partial(_out_proj_body, prompt_tiles=pt),
        grid=(pt + st,),
        in_specs=[p_row(D_MODEL), s_row(D_MODEL), p_row(GROUP_W), s_row(GROUP_W), p_row(GROUP_W), s_row(GROUP_W),
                  _full((GROUP_W, D_MODEL)), _full((GROUP_W, D_MODEL)),
                  _full((1, D_MODEL)), _full((D_MODEL, LANES)), _full((1, LANES))],
        out_specs=[row(D_MODEL), row(D_MODEL), row(LANES)],
        out_shape=[jax.ShapeDtypeStruct((n, D_MODEL), F32), jax.ShapeDtypeStruct((n, D_MODEL), F32),
                   jax.ShapeDtypeStruct((n, LANES), F32)],
        compiler_params=_params("arbitrary"),
        name="out_proj_router",
    )(xp, xs, att_p, att_s, hm_p, hm_s, wa, wm, g_ffn, w_router, b_router)


def _route_body(logit_ref, lstrict_ref, gate_ref, info_ref, count_ref, run_s):
    i = pl.program_id(0)

    @pl.when(i == 0)
    def _():
        run_s[...] = jnp.zeros_like(run_s)

    tt = logit_ref.shape[0]
    lane = lax.broadcasted_iota(I32, (tt, LANES), 1)
    work = jnp.where(lane < N_EXPERTS, logit_ref[...], -jnp.inf)
    vals, idxs = [], []
    for _ in range(TOP_K):
        m = jnp.max(work, axis=-1, keepdims=True)
        idx = jnp.min(jnp.where(work == m, lane, LANES), axis=-1, keepdims=True)
        vals.append(m)
        idxs.append(idx)
        work = jnp.where(lane == idx, -jnp.inf, work)
    exps = [jnp.exp(v - vals[0]) for v in vals]
    total = exps[0] + exps[1] + exps[2] + exps[3]

    chosen = jnp.zeros((tt, LANES), F32)
    for idx in idxs:
        chosen = chosen + (lane == idx).astype(F32)
    before = jnp.dot(lstrict_ref[...], chosen.astype(BF16), preferred_element_type=F32) + run_s[0:1, :]
    run_new = run_s[0:1, :] + jnp.sum(chosen, axis=0, keepdims=True)
    run_s[...] = jnp.broadcast_to(run_new, run_s.shape)

    gate_out = jnp.zeros((tt, LANES), F32)
    info_out = jnp.zeros((tt, LANES), I32)
    for k in range(TOP_K):
        rank = jnp.sum(jnp.where(lane == idxs[k], before, 0.0), axis=-1, keepdims=True).astype(I32)
        gate_out = jnp.where(lane == k, exps[k] / total, gate_out)
        info_out = jnp.where(lane == k, idxs[k], info_out)
        info_out = jnp.where(lane == TOP_K + k, rank, info_out)
    gate_ref[...] = gate_out
    info_ref[...] = info_out
    count_ref[...] = run_s[...].astype(I32)


def _route(logits):
    n = logits.shape[0]
    tt = ROW_TILE
    lstrict = jnp.asarray(np.tril(np.ones((tt, tt), np.float32), -1), BF16)
    row = lambda: pl.BlockSpec((tt, LANES), lambda i: (i, 0))
    return pl.pallas_call(
        _route_body,
        grid=(n // tt,),
        in_specs=[row(), _full((tt, tt))],
        out_specs=[row(), row(), _full((SUBLANES, LANES))],
        out_shape=[jax.ShapeDtypeStruct((n, LANES), F32), jax.ShapeDtypeStruct((n, LANES), I32),
                   jax.ShapeDtypeStruct((SUBLANES, LANES), I32)],
        scratch_shapes=[pltpu.VMEM((SUBLANES, LANES), F32)],
        compiler_params=_params("arbitrary"),
        name="route_topk",
    )(logits, lstrict)


def _row_copy(src, src_row, dst, dst_row, sem):
    return pltpu.make_async_copy(src.at[pl.ds(src_row, 1)], dst.at[pl.ds(dst_row, 1)], sem)


def _dispatch_body(fill_ref, nused_ref, dest_ref, xf_hbm, xs_hbm, zbuf, sem, zsem, *, tt, tm, n_tiles):
    i = pl.program_id(0)
    fill_rows = zbuf.shape[0]

    @pl.when(i == 0)
    def _():
        zbuf[...] = jnp.zeros_like(zbuf)

        def fill(e):
            start = pl.multiple_of(fill_ref[e], SUBLANES)
            return pltpu.make_async_copy(zbuf, xs_hbm.at[pl.ds(start, fill_rows)], zsem)

        for e in range(N_EXPERTS):
            fill(e).start()
        for e in range(N_EXPERTS):
            fill(e).wait()

        def tail(j, carry):
            cp = pltpu.make_async_copy(zbuf.at[pl.ds(0, tm)], xs_hbm.at[pl.ds(pl.multiple_of(j * tm, tm), tm)], zsem)
            cp.start()
            cp.wait()
            return carry

        lax.fori_loop(nused_ref[0], n_tiles, tail, 0)

    def issue(t, carry):
        for k in range(TOP_K):
            _row_copy(xf_hbm, i * tt + t, xs_hbm, dest_ref[0, 0, t * TOP_K + k], sem).start()
        return carry

    lax.fori_loop(0, tt, issue, 0, unroll=2)

    def drain(a, carry):
        _row_copy(xf_hbm, 0, xs_hbm, 0, sem).wait()
        return carry

    lax.fori_loop(0, tt * TOP_K, drain, 0, unroll=8)


def _dispatch(fill_start, n_used, dest, xf, n_tiles):
    n = xf.shape[0]
    tt, tm = TOKEN_TILE, EXPERT_TILE
    fill_rows = tm + SUBLANES
    grid_spec = pltpu.PrefetchScalarGridSpec(
        num_scalar_prefetch=2,
        grid=(n // tt,),
        in_specs=[pl.BlockSpec((1, 1, tt * TOP_K), lambda i, fs, nu: (i, 0, 0), memory_space=pltpu.SMEM),
                  pl.BlockSpec(memory_space=pl.ANY)],
        out_specs=pl.BlockSpec(memory_space=pl.ANY),
        scratch_shapes=[pltpu.VMEM((fill_rows, D_MODEL), F32), pltpu.SemaphoreType.DMA(()),
                        pltpu.SemaphoreType.DMA(())],
    )
    return pl.pallas_call(
        functools.partial(_dispatch_body, tt=tt, tm=tm, n_tiles=n_tiles + 2),
        grid_spec=grid_spec,
        out_shape=jax.ShapeDtypeStruct(((n_tiles + 2) * tm, D_MODEL), F32),
        compiler_params=_params("arbitrary"),
        name="expert_dispatch",
    )(fill_start, n_used, dest, xf)


def _expert_body(te_ref, nused_ref, x_ref, wup_ref, wdn_ref, perm_ref, bg_ref, bl_ref, bd_ref,
                 y_ref, wg_s, wl_s, wd_s):
    i = pl.program_id(0)
    n_used = nused_ref[0]

    @pl.when(i >= n_used)
    def _():
        y_ref[...] = jnp.zeros_like(y_ref)

    @pl.when((i == 0) | (te_ref[i] != te_ref[jnp.maximum(i - 1, 0)]))
    def _():
        perm = perm_ref[...]
        for c in range(2 * D_FF // 256):
            blk = wup_ref[0, :, c * 256:(c + 1) * 256].astype(BF16)
            sep = jnp.dot(blk, perm, preferred_element_type=F32).astype(BF16)
            wg_s[:, c * 128:(c + 1) * 128] = sep[:, :128]
            wl_s[:, c * 128:(c + 1) * 128] = sep[:, 128:]
        wd_s[...] = wdn_ref[0].astype(BF16)

    @pl.when(i < n_used)
    def _():
        x = x_ref[...].astype(BF16)
        glu = jnp.minimum(jnp.dot(x, wg_s[...], preferred_element_type=F32) + bg_ref[0], SWIGLU_LIMIT)
        lin = jnp.clip(jnp.dot(x, wl_s[...], preferred_element_type=F32) + bl_ref[0], -SWIGLU_LIMIT, SWIGLU_LIMIT)
        act = glu * jax.nn.sigmoid(SWIGLU_ALPHA * glu) * (lin + 1.0)
        y_ref[...] = jnp.dot(act.astype(BF16), wd_s[...], preferred_element_type=F32) + bd_ref[0]


def _deinterleave_perm():
    p = np.zeros((256, 256), np.float32)
    j = np.arange(128)
    p[2 * j, j] = 1.0
    p[2 * j + 1, 128 + j] = 1.0
    return jnp.asarray(p, BF16)


def _experts(tile_expert, n_used, x_sorted, w_up, w_down, b_glu, b_lin, b_down):
    tm = EXPERT_TILE
    n_tiles = tile_expert.shape[0]
    wspec = lambda k, n: pl.BlockSpec((1, k, n), lambda i, te, nu: (te[i], 0, 0))
    grid_spec = pltpu.PrefetchScalarGridSpec(
        num_scalar_prefetch=2,
        grid=(n_tiles,),
        in_specs=[pl.BlockSpec((tm, D_MODEL), lambda i, te, nu: (jnp.minimum(i, nu[0] - 1), 0)),
                  wspec(D_MODEL, 2 * D_FF), wspec(D_FF, D_MODEL),
                  pl.BlockSpec((256, 256), lambda i, te, nu: (0, 0)),
                  wspec(1, D_FF), wspec(1, D_FF), wspec(1, D_MODEL)],
        out_specs=pl.BlockSpec((tm, D_MODEL), lambda i, te, nu: (i, 0)),
        scratch_shapes=[pltpu.VMEM((D_MODEL, D_FF), BF16), pltpu.VMEM((D_MODEL, D_FF), BF16),
                        pltpu.VMEM((D_FF, D_MODEL), BF16)],
    )
    return pl.pallas_call(
        _expert_body,
        grid_spec=grid_spec,
        out_shape=jax.ShapeDtypeStruct((n_tiles * tm, D_MODEL), F32),
        compiler_params=_params("arbitrary"),
        name="expert_ffn",
    )(tile_expert, n_used, x_sorted, w_up, w_down, _deinterleave_perm(), b_glu, b_lin, b_down)


def _combine_body(dest_cur_ref, dest_nxt_ref, y_ref, gate_ref, rows_hbm, outp_ref, outs_ref, buf, sem,
                  *, tt, prompt_tiles):
    i = pl.program_id(0)
    slot = lax.rem(i, 2)

    def gather(dest_ref, dst_slot):
        for k in range(TOP_K):
            def issue(t, carry):
                _row_copy(rows_hbm, dest_ref[0, 0, t * TOP_K + k], buf.at[dst_slot, k], t, sem.at[dst_slot]).start()
                return carry

            lax.fori_loop(0, tt, issue, 0, unroll=8)

    @pl.when(i == 0)
    def _():
        gather(dest_cur_ref, 0)

    @pl.when(i + 1 < pl.num_programs(0))
    def _():
        gather(dest_nxt_ref, 1 - slot)

    def drain(a, carry):
        _row_copy(rows_hbm, 0, buf.at[slot, 0], 0, sem.at[slot]).wait()
        return carry

    lax.fori_loop(0, tt * TOP_K, drain, 0, unroll=8)

    acc = y_ref[...]
    gate = gate_ref[...]
    for k in range(TOP_K):
        acc = acc + gate[:, k:k + 1] * buf[slot, k]

    @pl.when(i < prompt_tiles)
    def _():
        outp_ref[...] = acc

    @pl.when(i >= prompt_tiles)
    def _():
        outs_ref[...] = acc


def _combine(dest, y, gates, y_rows, n_prompt):
    n = y.shape[0]
    tt = TOKEN_TILE
    nt, pt = n // tt, n_prompt // tt
    idx = lambda f: pl.BlockSpec((1, 1, tt * TOP_K), lambda i: (f(i), 0, 0), memory_space=pltpu.SMEM)
    return pl.pallas_call(
        functools.partial(_combine_body, tt=tt, prompt_tiles=pt),
        grid=(nt,),
        in_specs=[idx(lambda i: i), idx(lambda i: jnp.minimum(i + 1, nt - 1)),
                  pl.BlockSpec((tt, D_MODEL), lambda i: (i, 0)),
                  pl.BlockSpec((tt, LANES), lambda i: (i, 0)),
                  pl.BlockSpec(memory_space=pl.ANY)],
        out_specs=[pl.BlockSpec((tt, D_MODEL), lambda i: (jnp.minimum(i, pt - 1), 0)),
                   pl.BlockSpec((tt, D_MODEL), lambda i: (jnp.maximum(i - pt, 0), 0))],
        out_shape=[jax.ShapeDtypeStruct((n_prompt, D_MODEL), F32),
                   jax.ShapeDtypeStruct((n - n_prompt, D_MODEL), F32)],
        scratch_shapes=[pltpu.VMEM((2, TOP_K, tt, D_MODEL), F32), pltpu.SemaphoreType.DMA((2,))],
        compiler_params=_params("arbitrary"),
        name="expert_combine",
    )(dest, dest, y, gates, y_rows)


def _moe(y, xf, logits, ffn_w, n_prompt):
    n = y.shape[0]
    tm, tt = EXPERT_TILE, TOKEN_TILE
    n_asg = n * TOP_K
    gates, info, counts = _route(logits)
    counts = counts[0, :N_EXPERTS]
    padded = (counts + tm - 1) // tm * tm
    pad_end = jnp.cumsum(padded)
    pad_start = pad_end - padded
    n_tiles = -(-(n_asg + N_EXPERTS * (tm - 1)) // tm)
    tile_expert = jnp.minimum(jnp.searchsorted(pad_end, jnp.arange(n_tiles) * tm, side='right'),
                              N_EXPERTS - 1).astype(I32)
    n_used = (pad_end[-1:] // tm).astype(I32)
    fill_start = ((pad_start + counts) // SUBLANES * SUBLANES).astype(I32)
    expert = info[:, :TOP_K]
    start_of = jnp.sum(jnp.where(expert[:, :, None] == jnp.arange(N_EXPERTS)[None, None, :],
                                 pad_start[None, None, :], 0), axis=-1)
    dest = (start_of + info[:, TOP_K:2 * TOP_K]).astype(I32).reshape(n // tt, 1, tt * TOP_K)

    x_sorted = _dispatch(fill_start, n_used, dest, xf, n_tiles)
    y_rows = _experts(tile_expert, n_used, x_sorted, *ffn_w)
    return _combine(dest, y, gates, y_rows, n_prompt)


def _mixer(x, lw, cache, state):
    b, t, _ = x.shape
    n = b * t
    q, k, v, mq, mk, mv, mo, gates = _in_proj(x.reshape(n, D_MODEL), lw["g_mix"], lw["w_main"], lw["w_gate"],
                                              lw["gq"], lw["gk"], lw["gmat"])
    heads = lambda a, rows: a.reshape(b, rows, N_HEADS, HEAD_DIM)
    if cache is None:
        tiles = t // PAST_BAND
        att = _attention(q, k, k, v, v, lw["bias_prompt"], lw["hmask_prompt"], batch=b, tiles=tiles, cq=CHUNK,
                         nq=PAST_BAND // CHUNK,
                         prev_index=lambda bi, i: (bi * tiles + jnp.maximum(i - 1, 0), 0), mask_first=True)
        keep = min(PAST_BAND, t)
        k_new = heads(k.reshape(b, t, GROUP_W)[:, t - keep:], keep)
        v_new = heads(v.reshape(b, t, GROUP_W)[:, t - keep:], keep)
    else:
        ck, cv = cache
        att = _attention(q, ck.reshape(b * PAST_BAND, GROUP_W), k, cv.reshape(b * PAST_BAND, GROUP_W), v,
                         lw["bias_sample"], lw["hmask_sample"], batch=b, tiles=1, cq=t, nq=1,
                         prev_index=lambda bi, i: (bi, 0), mask_first=False)
        k_new, v_new = heads(k, t), heads(v, t)

    tp = -(-t // CHUNK) * CHUNK
    valid = t if t < CHUNK else CHUNK

    def streams(a):
        a = a.reshape(b, t, -1)
        return a if tp == t else jnp.pad(a, ((0, 0), (0, tp - t), (0, 0)))

    if state is None:
        c0t = jnp.zeros((b, HEAD_DIM, GROUP_W), F32)
        n0 = jnp.zeros((b, 1, GROUP_W), F32)
        m0 = jnp.zeros((b, 1, GROUP_W), F32)
    else:
        c_in, n_in, m_in = state
        c0t = c_in.astype(F32).transpose(0, 3, 1, 2).reshape(b, HEAD_DIM, GROUP_W)
        n0 = n_in.astype(F32).reshape(b, 1, GROUP_W)
        m0 = jnp.repeat(m_in.astype(F32), HEAD_DIM, axis=-1).reshape(b, 1, GROUP_W)
    hm, ct, n_out, m_out = _mlstm(streams(mq), streams(mk), streams(mv), streams(mo), streams(gates),
                                  c0t, n0, m0, lw["mlstm_consts"], valid=valid)
    hm = hm[:, :t].reshape(n, GROUP_W)
    c_new = ct.reshape(b, HEAD_DIM, N_HEADS, HEAD_DIM).transpose(0, 2, 3, 1)
    n_new = n_out.reshape(b, N_HEADS, HEAD_DIM)
    m_new = m_out.reshape(b, N_HEADS, HEAD_DIM)[:, :, 0]
    return att, hm, (k_new, v_new, c_new, n_new, m_new)


def kernel(x_prompt, x_sample, cache_k, cache_v, state_C, state_n, state_m, g_mix, w_in, g_q, g_k, rel_bias,
           b_igate, b_fgate, g_mlstm, w_out, g_ffn, w_router, b_router, w_up, b_up, w_down, b_down):
    depth = w_in.shape[0]
    yp, ys = x_prompt, x_sample
    bs, ts = x_sample.shape[0], x_sample.shape[1]
    n_prompt = x_prompt.shape[0] * x_prompt.shape[1]
    st_prompt, st_sample = [], []
    n_main = N_PROJ * GROUP_W
    gmat = jnp.asarray(_head_block_diag() / HEAD_DIM, BF16)
    for l in range(depth):
        lw = dict(
            g_mix=g_mix[l].astype(F32)[None, :],
            w_main=w_in[l][:, :n_main].astype(BF16),
            w_gate=jnp.pad(w_in[l][:, n_main:].astype(F32), ((0, 0), (0, LANES - 2 * N_HEADS))),
            gq=jnp.tile(g_q[l].astype(F32), N_HEADS)[None, :],
            gk=jnp.tile(g_k[l].astype(F32), N_HEADS)[None, :],
            gmat=gmat,
            bias_prompt=_rel_table(rel_bias[l], CHUNK),
            hmask_prompt=_head_row_mask(CHUNK),
            bias_sample=_rel_table(rel_bias[l], ts),
            hmask_sample=_head_row_mask(ts),
            mlstm_consts=_mlstm_consts(b_igate[l], b_fgate[l], g_mlstm[l]),
        )
        ffn_w = (w_up[l].astype(F32), w_down[l].astype(F32),
                 b_up[l][:, None, 0::2].astype(F32), b_up[l][:, None, 1::2].astype(F32),
                 b_down[l][:, None, :].astype(F32))
        att_p, hm_p, sp = _mixer(yp, lw, None, None)
        cache = (cache_k[l].reshape(bs, PAST_BAND, GROUP_W), cache_v[l].reshape(bs, PAST_BAND, GROUP_W))
        att_s, hm_s, ss = _mixer(ys, lw, cache, (state_C[l], state_n[l], state_m[l]))
        y, xf, logits = _out_proj(
            yp.reshape(-1, D_MODEL), ys.reshape(-1, D_MODEL), att_p, att_s, hm_p, hm_s,
            w_out[l][:GROUP_W].astype(BF16), w_out[l][GROUP_W:].astype(BF16), g_ffn[l].astype(F32)[None, :],
            jnp.pad(w_router[l].astype(F32), ((0, 0), (0, LANES - N_EXPERTS))),
            jnp.pad(b_router[l].astype(F32), (0, LANES - N_EXPERTS))[None, :])
        out_p, out_s = _moe(y, xf, logits, ffn_w, n_prompt)
        yp, ys = out_p.reshape(x_prompt.shape), out_s.reshape(x_sample.shape)
        st_prompt.append(sp)
        st_sample.append(ss)
    k_p, v_p, c_p, n_p, m_p = [jnp.stack(a) for a in zip(*st_prompt)]
    k_s, v_s, c_s, n_s, m_s = [jnp.stack(a) for a in zip(*st_sample)]
    return (yp, ys, k_p, v_p, c_p, n_p, m_p, k_s, v_s, c_s, n_s, m_s)
```

```python
import functools

import numpy as np
import jax
import jax.numpy as jnp
from jax import lax
from jax.experimental import pallas as pl
from jax.experimental.pallas import tpu as pltpu

F32 = jnp.float32
BF16 = jnp.bfloat16
I32 = jnp.int32
HIGHEST = lax.Precision.HIGHEST

D_MODEL = 1024
N_HEADS = 8
HEAD_DIM = 64
GROUP_W = N_HEADS * HEAD_DIM
N_PROJ = 7
LANES = 128
CHUNK = 64
PAST_BAND = 512
KEY_WIN = 640
REL_CLIP = 256
N_EXPERTS = 32
TOP_K = 4
D_FF = 1024
SWIGLU_ALPHA = 1.702
SWIGLU_LIMIT = 7.0
RMS_EPS = 1e-6
NEG_BIG = -1e30
ROW_TILE = 512
EXPERT_TILE = 512
TOKEN_TILE = 256
SUBLANES = 8
MLSTM_STREAMS = 4
VMEM_LIMIT_BYTES = 56 * 1024 * 1024


def _params(*sem):
    return pltpu.CompilerParams(dimension_semantics=sem, vmem_limit_bytes=VMEM_LIMIT_BYTES)


def _head_block_diag():
    h = np.arange(GROUP_W) // HEAD_DIM
    return (h[:, None] == h[None, :]).astype(np.float32)


def _full(shape):
    return pl.BlockSpec(shape, lambda *_: (0,) * len(shape))


def _in_proj_body(x_ref, gmix_ref, w_ref, wg_ref, gq_ref, gk_ref, gmat_ref,
                  q_ref, k_ref, v_ref, mq_ref, mk_ref, mv_ref, mo_ref, gate_ref):
    x = x_ref[...]
    xn = x * lax.rsqrt(jnp.mean(x * x, axis=-1, keepdims=True) + RMS_EPS) * gmix_ref[...]
    xb = xn.astype(BF16)

    def proj(j):
        return jnp.dot(xb, w_ref[:, j * GROUP_W:(j + 1) * GROUP_W], preferred_element_type=F32)

    def head_norm(a, g_ref):
        msq = jnp.dot((a * a).astype(BF16), gmat_ref[...], preferred_element_type=F32)
        return a * lax.rsqrt(msq + RMS_EPS) * g_ref[...]

    q_ref[...] = head_norm(proj(0), gq_ref)
    k_ref[...] = head_norm(proj(1), gk_ref)
    v_ref[...] = proj(2)
    mq_ref[...] = proj(3)
    mk_ref[...] = proj(4)
    mv_ref[...] = proj(5)
    mo_ref[...] = proj(6)
    gate_ref[...] = jnp.dot(xn, wg_ref[...], preferred_element_type=F32, precision=HIGHEST)


def _in_proj(x2d, g_mix, w_main, w_gate, gq_row, gk_row, gmat):
    n = x2d.shape[0]
    tm = ROW_TILE
    row = lambda w: pl.BlockSpec((tm, w), lambda i: (i, 0))
    outs = [jax.ShapeDtypeStruct((n, GROUP_W), F32)] * N_PROJ + [jax.ShapeDtypeStruct((n, LANES), F32)]
    return pl.pallas_call(
        _in_proj_body,
        grid=(n // tm,),
        in_specs=[row(D_MODEL), _full((1, D_MODEL)), _full((D_MODEL, N_PROJ * GROUP_W)),
                  _full((D_MODEL, LANES)), _full((1, GROUP_W)), _full((1, GROUP_W)),
                  _full((GROUP_W, GROUP_W))],
        out_specs=[row(GROUP_W)] * N_PROJ + [row(LANES)],
        out_shape=outs,
        compiler_params=_params("arbitrary"),
        name="in_proj",
    )(x2d, g_mix, w_main, w_gate, gq_row, gk_row, gmat)


def _attn_body(q_ref, kp_ref, kc_ref, vp_ref, vc_ref, bias_ref, hmask_ref, o_ref, kwin, vwin,
               *, cq, nq, mask_first):
    tc = cq * nq
    i = pl.program_id(1)
    kwin[0:PAST_BAND, :] = kp_ref[...].astype(BF16)
    kwin[PAST_BAND:PAST_BAND + tc, :] = kc_ref[...].astype(BF16)
    vwin[0:PAST_BAND, :] = vp_ref[...].astype(BF16)
    vwin[PAST_BAND:PAST_BAND + tc, :] = vc_ref[...].astype(BF16)
    pad_rows = kwin.shape[0] - PAST_BAND - tc
    kwin[PAST_BAND + tc:, :] = jnp.zeros((pad_rows, GROUP_W), BF16)
    vwin[PAST_BAND + tc:, :] = jnp.zeros((pad_rows, GROUP_W), BF16)

    hm = hmask_ref[...]
    bias = bias_ref[...]
    kk = lax.broadcasted_iota(I32, (1, KEY_WIN), 1)

    def chunk(j, carry):
        r0 = pl.multiple_of(j * cq, cq)
        q = q_ref[pl.ds(r0, cq), :] * (HEAD_DIM ** -0.5)
        qm = (jnp.concatenate([q] * N_HEADS, axis=0) * hm).astype(BF16)
        kw = kwin[pl.ds(r0, KEY_WIN), :]
        s = lax.dot_general(qm, kw, (((1,), (1,)), ((), ())), preferred_element_type=F32) + bias
        if mask_first:
            first_valid = jnp.where(i == 0, PAST_BAND - r0, 0)
            s = jnp.where(kk >= first_valid, s, NEG_BIG)
        m = jnp.max(s, axis=-1, keepdims=True)
        p = jnp.exp(s - m)
        l = jnp.sum(p, axis=-1, keepdims=True)
        vw = vwin[pl.ds(r0, KEY_WIN), :]
        o_all = jnp.dot(p.astype(BF16), vw, preferred_element_type=F32) / l * hm
        o = o_all[0:cq]
        for h in range(1, N_HEADS):
            o = o + o_all[h * cq:(h + 1) * cq]
        o_ref[pl.ds(r0, cq), :] = o
        return carry

    lax.fori_loop(0, nq, chunk, 0)


def _attention(q, k_prev_src, k_cur_src, v_prev_src, v_cur_src, bias, hmask, *, batch, tiles, cq, nq,
               prev_index, mask_first):
    tc = cq * nq
    cur = pl.BlockSpec((tc, GROUP_W), lambda b, i: (b * tiles + i, 0))
    prev = pl.BlockSpec((PAST_BAND, GROUP_W), prev_index)
    win_rows = (nq - 1) * cq + KEY_WIN
    return pl.pallas_call(
        functools.partial(_attn_body, cq=cq, nq=nq, mask_first=mask_first),
        grid=(batch, tiles),
        in_specs=[cur, prev, cur, prev, cur, _full((N_HEADS * cq, KEY_WIN)), _full((N_HEADS * cq, GROUP_W))],
        out_specs=cur,
        out_shape=jax.ShapeDtypeStruct(q.shape, F32),
        scratch_shapes=[pltpu.VMEM((win_rows, GROUP_W), BF16), pltpu.VMEM((win_rows, GROUP_W), BF16)],
        compiler_params=_params("arbitrary", "arbitrary"),
        name="band_attention",
    )(q, k_prev_src, k_cur_src, v_prev_src, v_cur_src, bias, hmask)


def _rel_table(rel_bias_l, cq):
    nk = PAST_BAND + cq
    dist = PAST_BAND + (cq - 1) - np.arange(nk + cq - 1)
    rev = rel_bias_l[:, np.clip(dist, -REL_CLIP, REL_CLIP) + REL_CLIP].astype(F32)
    tab = jnp.stack([rev[:, cq - 1 - q:cq - 1 - q + nk] for q in range(cq)], axis=1)
    tab = jnp.pad(tab, ((0, 0), (0, 0), (0, KEY_WIN - nk)), constant_values=NEG_BIG)
    return tab.reshape(N_HEADS * cq, KEY_WIN)


def _head_row_mask(cq):
    h_row = np.repeat(np.arange(N_HEADS), cq)
    h_col = np.arange(GROUP_W) // HEAD_DIM
    return jnp.asarray((h_row[:, None] == h_col[None, :]).astype(np.float32))


def _log_sigmoid(x):
    return jnp.minimum(x, 0.0) - jnp.log(1.0 + jnp.exp(-jnp.abs(x)))


def _mlstm_body(q_ref, k_ref, v_ref, o_ref, g_ref, c0_ref, n0_ref, m0_ref,
                expand_ref, gbias_ref, bd_ref, bdb_ref, gmat_ref, ltri_ref, eye_ref, causal_ref, gml_ref,
                h_ref, ct_ref, n_ref, m_ref, ct_s, n_s, m_s, *, valid, nb):
    c = pl.program_id(1)
    last = c == pl.num_programs(1) - 1
    bd = bd_ref[...]
    bdb = bdb_ref[...]
    gmat = gmat_ref[...]
    eye = eye_ref[...] > 0.5
    causal = causal_ref[...] > 0.5

    @pl.when(c == 0)
    def _():
        for b in range(nb):
            ct_s[b] = jnp.concatenate([c0_ref[b]] * N_HEADS, axis=0) * bd
            n_s[b] = n0_ref[b]
            m_s[b] = m0_ref[b]

    for b in range(nb):
        gp = jnp.dot(g_ref[b], expand_ref[...], preferred_element_type=F32, precision=HIGHEST) + gbias_ref[...]
        log_i = gp[:, :GROUP_W]
        log_f = _log_sigmoid(gp[:, GROUP_W:])
        if valid < CHUNK:
            live = lax.broadcasted_iota(I32, (CHUNK, GROUP_W), 0) < valid
            log_i = jnp.where(live, log_i, -jnp.inf)
            log_f = jnp.where(live, log_f, 0.0)
        cum_f = jnp.dot(ltri_ref[...], log_f, preferred_element_type=F32, precision=HIGHEST)

        b_row = jnp.sum(jnp.where(eye, log_i - cum_f, 0.0), axis=0, keepdims=True)
        m_prev = m_s[b]
        log_inter = cum_f + m_prev
        log_d = jnp.where(causal, cum_f + b_row, -jnp.inf)
        max_d = jnp.concatenate(
            [jnp.broadcast_to(jnp.max(log_d[:, h * HEAD_DIM:(h + 1) * HEAD_DIM], axis=-1, keepdims=True),
                              (CHUNK, HEAD_DIM)) for h in range(N_HEADS)], axis=1)
        m_t = jnp.maximum(log_inter, max_d)
        w_intra = jnp.exp(log_d - m_t)
        w_inter = jnp.exp(log_inter - m_t)

        q = q_ref[b] * (HEAD_DIM ** -0.5)
        k = k_ref[b]
        v = v_ref[b]
        qb = q.astype(BF16)
        vb = v.astype(BF16)
        kbd = jnp.concatenate([k.astype(BF16)] * N_HEADS, axis=0) * bdb
        vbd = jnp.concatenate([vb] * N_HEADS, axis=0) * bdb
        s = lax.dot_general(qb, kbd, (((1,), (1,)), ((), ())), preferred_element_type=F32) * w_intra
        ct = ct_s[b]
        n_prev = n_s[b]
        num = (jnp.dot(s.astype(BF16), vbd, preferred_element_type=F32)
               + w_inter * lax.dot_general(qb, ct.astype(BF16), (((1,), (1,)), ((), ())),
                                           preferred_element_type=F32))
        den_terms = (s + w_inter * q * n_prev) * float(HEAD_DIM)
        den_hi = den_terms.astype(BF16)
        den_lo = (den_terms - den_hi.astype(F32)).astype(BF16)
        den = (jnp.dot(den_hi, gmat, preferred_element_type=F32)
               + jnp.dot(den_lo, gmat, preferred_element_type=F32))
        hb = num / jnp.maximum(jnp.abs(den), jnp.exp(-m_t))

        m_new = m_t[CHUNK - 1:CHUNK, :]
        cum_last = cum_f[CHUNK - 1:CHUNK, :]
        w_state = jnp.exp(cum_last - cum_f + log_i - m_new)
        decay = jnp.exp(cum_last + m_prev - m_new)
        kw = k * w_state
        upd = lax.dot_general(vb, kw.astype(BF16), (((0,), (0,)), ((), ())),
                              preferred_element_type=F32)
        ct_new = decay * ct + upd * bd
        n_new = decay * n_prev + jnp.sum(kw, axis=0, keepdims=True)
        ct_s[b] = ct_new
        n_s[b] = n_new
        m_s[b] = m_new

        msq = jnp.dot((hb * hb).astype(BF16), gmat, preferred_element_type=F32)
        h_ref[b] = jax.nn.sigmoid(o_ref[b]) * (hb * lax.rsqrt(msq + RMS_EPS) * gml_ref[...])

    @pl.when(last)
    def _():
        for b in range(nb):
            ct_new = ct_s[b]
            acc = ct_new[0:HEAD_DIM]
            for h in range(1, N_HEADS):
                acc = acc + ct_new[h * HEAD_DIM:(h + 1) * HEAD_DIM]
            ct_ref[b] = acc
            n_ref[b] = n_s[b]
            m_ref[b] = m_s[b]


def _mlstm(mq, mk, mv, mo, gates, c0t, n0, m0, consts, *, valid):
    batch, t, _ = mq.shape
    chunks = t // CHUNK
    nb = MLSTM_STREAMS
    row = lambda w: pl.BlockSpec((nb, CHUNK, w), lambda g, c: (g, c, 0))
    per_b = lambda r: pl.BlockSpec((nb, r, GROUP_W), lambda g, c: (g, 0, 0))
    expand, gbias, bd, bdb, gmat, ltri, eye, causal, gml = consts
    return pl.pallas_call(
        functools.partial(_mlstm_body, valid=valid, nb=nb),
        grid=(batch // nb, chunks),
        in_specs=[row(GROUP_W)] * 4 + [row(LANES), per_b(HEAD_DIM), per_b(1), per_b(1),
                  _full((LANES, 2 * GROUP_W)), _full((1, 2 * GROUP_W)), _full((GROUP_W, GROUP_W)),
                  _full((GROUP_W, GROUP_W)), _full((GROUP_W, GROUP_W)), _full((CHUNK, CHUNK)),
                  _full((CHUNK, GROUP_W)), _full((CHUNK, GROUP_W)), _full((1, GROUP_W))],
        out_specs=[row(GROUP_W), per_b(HEAD_DIM), per_b(1), per_b(1)],
        out_shape=[jax.ShapeDtypeStruct(mq.shape, F32),
                   jax.ShapeDtypeStruct((batch, HEAD_DIM, GROUP_W), F32),
                   jax.ShapeDtypeStruct((batch, 1, GROUP_W), F32),
                   jax.ShapeDtypeStruct((batch, 1, GROUP_W), F32)],
        scratch_shapes=[pltpu.VMEM((nb, GROUP_W, GROUP_W), F32), pltpu.VMEM((nb, 1, GROUP_W), F32),
                        pltpu.VMEM((nb, 1, GROUP_W), F32)],
        compiler_params=_params("arbitrary", "arbitrary"),
        name="mlstm",
    )(mq, mk, mv, mo, gates, c0t, n0, m0, expand, gbias, bd, bdb, gmat, ltri, eye, causal, gml)


def _mlstm_consts(b_igate_l, b_fgate_l, g_mlstm_l):
    expand = np.zeros((LANES, 2 * GROUP_W), np.float32)
    for h in range(N_HEADS):
        expand[h, h * HEAD_DIM:(h + 1) * HEAD_DIM] = 1.0
        expand[N_HEADS + h, GROUP_W + h * HEAD_DIM:GROUP_W + (h + 1) * HEAD_DIM] = 1.0
    gbias = jnp.concatenate([jnp.repeat(b_igate_l.astype(F32), HEAD_DIM),
                             jnp.repeat(b_fgate_l.astype(F32), HEAD_DIM)])[None, :]
    bd = _head_block_diag()
    ltri = np.tril(np.ones((CHUNK, CHUNK), np.float32))
    s_of_lane = np.arange(GROUP_W) % HEAD_DIM
    t = np.arange(CHUNK)
    eye = (t[:, None] == s_of_lane[None, :]).astype(np.float32)
    causal = (s_of_lane[None, :] <= t[:, None]).astype(np.float32)
    return (jnp.asarray(expand), gbias, jnp.asarray(bd), jnp.asarray(bd, BF16), jnp.asarray(bd / HEAD_DIM, BF16),
            jnp.asarray(ltri), jnp.asarray(eye), jnp.asarray(causal), g_mlstm_l.astype(F32).reshape(1, GROUP_W))


def _out_proj_body(xp_ref, xs_ref, ap_ref, as_ref, hp_ref, hs_ref, wa_ref, wm_ref, gffn_ref, wr_ref, br_ref,
                   y_ref, xf_ref, logit_ref, *, prompt_tiles):
    is_prompt = pl.program_id(0) < prompt_tiles
    x = jnp.where(is_prompt, xp_ref[...], xs_ref[...])
    att = jnp.where(is_prompt, ap_ref[...], as_ref[...])
    hm = jnp.where(is_prompt, hp_ref[...], hs_ref[...])
    y = (x + jnp.dot(att.astype(BF16), wa_ref[...], preferred_element_type=F32)
         + jnp.dot(hm.astype(BF16), wm_ref[...], preferred_element_type=F32))
    y_ref[...] = y
    xf = y * lax.rsqrt(jnp.mean(y * y, axis=-1, keepdims=True) + RMS_EPS) * gffn_ref[...]
    xf_ref[...] = xf
    logit_ref[...] = jnp.dot(xf, wr_ref[...], preferred_element_type=F32, precision=HIGHEST) + br_ref[...]


def _out_proj(xp, xs, att_p, att_s, hm_p, hm_s, wa, wm, g_ffn, w_router, b_router):
    tm = ROW_TILE
    pt, st = xp.shape[0] // tm, xs.shape[0] // tm
    n = xp.shape[0] + xs.shape[0]
    p_row = lambda w: pl.BlockSpec((tm, w), lambda i: (jnp.minimum(i, pt - 1), 0))
    s_row = lambda w: pl.BlockSpec((tm, w), lambda i: (jnp.maximum(i - pt, 0), 0))
    row = lambda w: pl.BlockSpec((tm, w), lambda i: (i, 0))
    return pl.pallas_call(
        functools.partial(_out_proj_body, prompt_tiles=pt),
        grid=(pt + st,),
        in_specs=[p_row(D_MODEL), s_row(D_MODEL), p_row(GROUP_W), s_row(GROUP_W), p_row(GROUP_W), s_row(GROUP_W),
                  _full((GROUP_W, D_MODEL)), _full((GROUP_W, D_MODEL)),
                  _full((1, D_MODEL)), _full((D_MODEL, LANES)), _full((1, LANES))],
        out_specs=[row(D_MODEL), row(D_MODEL), row(LANES)],
        out_shape=[jax.ShapeDtypeStruct((n, D_MODEL), F32), jax.ShapeDtypeStruct((n, D_MODEL), F32),
                   jax.ShapeDtypeStruct((n, LANES), F32)],
        compiler_params=_params("arbitrary"),
        name="out_proj_router",
    )(xp, xs, att_p, att_s, hm_p, hm_s, wa, wm, g_ffn, w_router, b_router)


def _route_body(logit_ref, lstrict_ref, gate_ref, info_ref, count_ref, run_s):
    i = pl.program_id(0)

    @pl.when(i == 0)
    def _():
        run_s[...] = jnp.zeros_like(run_s)

    tt = logit_ref.shape[0]
    lane = lax.broadcasted_iota(I32, (tt, LANES), 1)
    work = jnp.where(lane < N_EXPERTS, logit_ref[...], -jnp.inf)
    vals, idxs = [], []
    for _ in range(TOP_K):
        m = jnp.max(work, axis=-1, keepdims=True)
        idx = jnp.min(jnp.where(work == m, lane, LANES), axis=-1, keepdims=True)
        vals.append(m)
        idxs.append(idx)
        work = jnp.where(lane == idx, -jnp.inf, work)
    exps = [jnp.exp(v - vals[0]) for v in vals]
    total = exps[0] + exps[1] + exps[2] + exps[3]

    chosen = jnp.zeros((tt, LANES), F32)
    for idx in idxs:
        chosen = chosen + (lane == idx).astype(F32)
    before = jnp.dot(lstrict_ref[...], chosen.astype(BF16), preferred_element_type=F32) + run_s[0:1, :]
    run_new = run_s[0:1, :] + jnp.sum(chosen, axis=0, keepdims=True)
    run_s[...] = jnp.broadcast_to(run_new, run_s.shape)

    gate_out = jnp.zeros((tt, LANES), F32)
    info_out = jnp.zeros((tt, LANES), I32)
    for k in range(TOP_K):
        rank = jnp.sum(jnp.where(lane == idxs[k], before, 0.0), axis=-1, keepdims=True).astype(I32)
        gate_out = jnp.where(lane == k, exps[k] / total, gate_out)
        info_out = jnp.where(lane == k, idxs[k], info_out)
        info_out = jnp.where(lane == TOP_K + k, rank, info_out)
    gate_ref[...] = gate_out
    info_ref[...] = info_out
    count_ref[...] = run_s[...].astype(I32)


def _route(logits):
    n = logits.shape[0]
    tt = ROW_TILE
    lstrict = jnp.asarray(np.tril(np.ones((tt, tt), np.float32), -1), BF16)
    row = lambda: pl.BlockSpec((tt, LANES), lambda i: (i, 0))
    return pl.pallas_call(
        _route_body,
        grid=(n // tt,),
        in_specs=[row(), _full((tt, tt))],
        out_specs=[row(), row(), _full((SUBLANES, LANES))],
        out_shape=[jax.ShapeDtypeStruct((n, LANES), F32), jax.ShapeDtypeStruct((n, LANES), I32),
                   jax.ShapeDtypeStruct((SUBLANES, LANES), I32)],
        scratch_shapes=[pltpu.VMEM((SUBLANES, LANES), F32)],
        compiler_params=_params("arbitrary"),
        name="route_topk",
    )(logits, lstrict)


def _row_copy(src, src_row, dst, dst_row, sem):
    return pltpu.make_async_copy(src.at[pl.ds(src_row, 1)], dst.at[pl.ds(dst_row, 1)], sem)


def _dispatch_body(fill_ref, nused_ref, dest_ref, xf_ref, xs_hbm, zbuf, sem, zsem, *, tt, tm, n_tiles):
    i = pl.program_id(0)
    fill_rows = zbuf.shape[0]

    @pl.when(i == 0)
    def _():
        zbuf[...] = jnp.zeros_like(zbuf)

        def fill(e):
            start = pl.multiple_of(fill_ref[e], SUBLANES)
            return pltpu.make_async_copy(zbuf, xs_hbm.at[pl.ds(start, fill_rows)], zsem)

        for e in range(N_EXPERTS):
            fill(e).start()
        for e in range(N_EXPERTS):
            fill(e).wait()

        def tail(j, carry):
            cp = pltpu.make_async_copy(zbuf.at[pl.ds(0, tm)], xs_hbm.at[pl.ds(pl.multiple_of(j * tm, tm), tm)], zsem)
            cp.start()
            cp.wait()
            return carry

        lax.fori_loop(nused_ref[0], n_tiles, tail, 0)

    def issue(t, carry):
        for k in range(TOP_K):
            _row_copy(xf_ref, t, xs_hbm, dest_ref[0, 0, t * TOP_K + k], sem).start()
        return carry

    lax.fori_loop(0, tt, issue, 0, unroll=2)

    def drain(a, carry):
        _row_copy(xf_ref, 0, xs_hbm, 0, sem).wait()
        return carry

    lax.fori_loop(0, tt * TOP_K, drain, 0, unroll=8)


def _dispatch(fill_start, n_used, dest, xf, n_tiles):
    n = xf.shape[0]
    tt, tm = TOKEN_TILE, EXPERT_TILE
    fill_rows = tm + SUBLANES
    grid_spec = pltpu.PrefetchScalarGridSpec(
        num_scalar_prefetch=2,
        grid=(n // tt,),
        in_specs=[pl.BlockSpec((1, 1, tt * TOP_K), lambda i, fs, nu: (i, 0, 0), memory_space=pltpu.SMEM),
                  pl.BlockSpec((tt, D_MODEL), lambda i, fs, nu: (i, 0))],
        out_specs=pl.BlockSpec(memory_space=pl.ANY),
        scratch_shapes=[pltpu.VMEM((fill_rows, D_MODEL), F32), pltpu.SemaphoreType.DMA(()),
                        pltpu.SemaphoreType.DMA(())],
    )
    return pl.pallas_call(
        functools.partial(_dispatch_body, tt=tt, tm=tm, n_tiles=n_tiles + 2),
        grid_spec=grid_spec,
        out_shape=jax.ShapeDtypeStruct(((n_tiles + 2) * tm, D_MODEL), F32),
        compiler_params=_params("arbitrary"),
        name="expert_dispatch",
    )(fill_start, n_used, dest, xf)


def _expert_body(te_ref, nused_ref, x_ref, wup_ref, wdn_ref, perm_ref, bg_ref, bl_ref, bd_ref,
                 y_ref, wg_s, wl_s, wd_s):
    i = pl.program_id(0)
    n_used = nused_ref[0]

    @pl.when(i >= n_used)
    def _():
        y_ref[...] = jnp.zeros_like(y_ref)

    @pl.when((i == 0) | (te_ref[i] != te_ref[jnp.maximum(i - 1, 0)]))
    def _():
        perm = perm_ref[...]
        for c in range(2 * D_FF // 256):
            blk = wup_ref[0, :, c * 256:(c + 1) * 256].astype(BF16)
            sep = jnp.dot(blk, perm, preferred_element_type=F32).astype(BF16)
            wg_s[:, c * 128:(c + 1) * 128] = sep[:, :128]
            wl_s[:, c * 128:(c + 1) * 128] = sep[:, 128:]
        wd_s[...] = wdn_ref[0].astype(BF16)

    @pl.when(i < n_used)
    def _():
        x = x_ref[...].astype(BF16)
        glu = jnp.minimum(jnp.dot(x, wg_s[...], preferred_element_type=F32) + bg_ref[0], SWIGLU_LIMIT)
        lin = jnp.clip(jnp.dot(x, wl_s[...], preferred_element_type=F32) + bl_ref[0], -SWIGLU_LIMIT, SWIGLU_LIMIT)
        act = glu * jax.nn.sigmoid(SWIGLU_ALPHA * glu) * (lin + 1.0)
        y_ref[...] = jnp.dot(act.astype(BF16), wd_s[...], preferred_element_type=F32) + bd_ref[0]


def _deinterleave_perm():
    p = np.zeros((256, 256), np.float32)
    j = np.arange(128)
    p[2 * j, j] = 1.0
    p[2 * j + 1, 128 + j] = 1.0
    return jnp.asarray(p, BF16)


def _experts(tile_expert, n_used, x_sorted, w_up, w_down, b_glu, b_lin, b_down):
    tm = EXPERT_TILE
    n_tiles = tile_expert.shape[0]
    wspec = lambda k, n: pl.BlockSpec((1, k, n), lambda i, te, nu: (te[i], 0, 0))
    grid_spec = pltpu.PrefetchScalarGridSpec(
        num_scalar_prefetch=2,
        grid=(n_tiles,),
        in_specs=[pl.BlockSpec((tm, D_MODEL), lambda i, te, nu: (jnp.minimum(i, nu[0] - 1), 0)),
                  wspec(D_MODEL, 2 * D_FF), wspec(D_FF, D_MODEL),
                  pl.BlockSpec((256, 256), lambda i, te, nu: (0, 0)),
                  wspec(1, D_FF), wspec(1, D_FF), wspec(1, D_MODEL)],
        out_specs=pl.BlockSpec((tm, D_MODEL), lambda i, te, nu: (i, 0)),
        scratch_shapes=[pltpu.VMEM((D_MODEL, D_FF), BF16), pltpu.VMEM((D_MODEL, D_FF), BF16),
                        pltpu.VMEM((D_FF, D_MODEL), BF16)],
    )
    return pl.pallas_call(
        _expert_body,
        grid_spec=grid_spec,
        out_shape=jax.ShapeDtypeStruct((n_tiles * tm, D_MODEL), F32),
        compiler_params=_params("arbitrary"),
        name="expert_ffn",
    )(tile_expert, n_used, x_sorted, w_up, w_down, _deinterleave_perm(), b_glu, b_lin, b_down)


def _combine_body(dest_cur_ref, dest_nxt_ref, y_ref, gate_ref, rows_hbm, outp_ref, outs_ref, buf, sem,
                  *, tt, prompt_tiles):
    i = pl.program_id(0)
    slot = lax.rem(i, 2)

    def gather(dest_ref, dst_slot):
        for k in range(TOP_K):
            def issue(t, carry):
                _row_copy(rows_hbm, dest_ref[0, 0, t * TOP_K + k], buf.at[dst_slot, k], t, sem.at[dst_slot]).start()
                return carry

            lax.fori_loop(0, tt, issue, 0, unroll=8)

    @pl.when(i == 0)
    def _():
        gather(dest_cur_ref, 0)

    @pl.when(i + 1 < pl.num_programs(0))
    def _():
        gather(dest_nxt_ref, 1 - slot)

    def drain(a, carry):
        _row_copy(rows_hbm, 0, buf.at[slot, 0], 0, sem.at[slot]).wait()
        return carry

    lax.fori_loop(0, tt * TOP_K, drain, 0, unroll=8)

    acc = y_ref[...]
    gate = gate_ref[...]
    for k in range(TOP_K):
        acc = acc + gate[:, k:k + 1] * buf[slot, k]

    @pl.when(i < prompt_tiles)
    def _():
        outp_ref[...] = acc

    @pl.when(i >= prompt_tiles)
    def _():
        outs_ref[...] = acc


def _combine(dest, y, gates, y_rows, n_prompt):
    n = y.shape[0]
    tt = TOKEN_TILE
    nt, pt = n // tt, n_prompt // tt
    idx = lambda f: pl.BlockSpec((1, 1, tt * TOP_K), lambda i: (f(i), 0, 0), memory_space=pltpu.SMEM)
    return pl.pallas_call(
        functools.partial(_combine_body, tt=tt, prompt_tiles=pt),
        grid=(nt,),
        in_specs=[idx(lambda i: i), idx(lambda i: jnp.minimum(i + 1, nt - 1)),
                  pl.BlockSpec((tt, D_MODEL), lambda i: (i, 0)),
                  pl.BlockSpec((tt, LANES), lambda i: (i, 0)),
                  pl.BlockSpec(memory_space=pl.ANY)],
        out_specs=[pl.BlockSpec((tt, D_MODEL), lambda i: (jnp.minimum(i, pt - 1), 0)),
                   pl.BlockSpec((tt, D_MODEL), lambda i: (jnp.maximum(i - pt, 0), 0))],
        out_shape=[jax.ShapeDtypeStruct((n_prompt, D_MODEL), F32),
                   jax.ShapeDtypeStruct((n - n_prompt, D_MODEL), F32)],
        scratch_shapes=[pltpu.VMEM((2, TOP_K, tt, D_MODEL), F32), pltpu.SemaphoreType.DMA((2,))],
        compiler_params=_params("arbitrary"),
        name="expert_combine",
    )(dest, dest, y, gates, y_rows)


def _moe(y, xf, logits, ffn_w, n_prompt):
    n = y.shape[0]
    tm, tt = EXPERT_TILE, TOKEN_TILE
    n_asg = n * TOP_K
    gates, info, counts = _route(logits)
    counts = counts[0, :N_EXPERTS]
    padded = (counts + tm - 1) // tm * tm
    pad_end = jnp.cumsum(padded)
    pad_start = pad_end - padded
    n_tiles = -(-(n_asg + N_EXPERTS * (tm - 1)) // tm)
    tile_expert = jnp.minimum(jnp.sum(pad_end[None, :] <= (jnp.arange(n_tiles) * tm)[:, None], axis=1),
                              N_EXPERTS - 1).astype(I32)
    n_used = (pad_end[-1:] // tm).astype(I32)
    fill_start = ((pad_start + counts) // SUBLANES * SUBLANES).astype(I32)
    expert = info[:, :TOP_K]
    start_of = jnp.sum(jnp.where(expert[:, :, None] == jnp.arange(N_EXPERTS)[None, None, :],
                                 pad_start[None, None, :], 0), axis=-1)
    dest = (start_of + info[:, TOP_K:2 * TOP_K]).astype(I32).reshape(n // tt, 1, tt * TOP_K)

    x_sorted = _dispatch(fill_start, n_used, dest, xf, n_tiles)
    y_rows = _experts(tile_expert, n_used, x_sorted, *ffn_w)
    return _combine(dest, y, gates, y_rows, n_prompt)


def _mixer(x, lw, cache, state):
    b, t, _ = x.shape
    n = b * t
    q, k, v, mq, mk, mv, mo, gates = _in_proj(x.reshape(n, D_MODEL), lw["g_mix"], lw["w_main"], lw["w_gate"],
                                              lw["gq"], lw["gk"], lw["gmat"])
    heads = lambda a, rows: a.reshape(b, rows, N_HEADS, HEAD_DIM)
    if cache is None:
        tiles = t // PAST_BAND
        att = _attention(q, k, k, v, v, lw["bias_prompt"], lw["hmask_prompt"], batch=b, tiles=tiles, cq=CHUNK,
                         nq=PAST_BAND // CHUNK,
                         prev_index=lambda bi, i: (bi * tiles + jnp.maximum(i - 1, 0), 0), mask_first=True)
        keep = min(PAST_BAND, t)
        k_new = heads(k.reshape(b, t, GROUP_W)[:, t - keep:], keep)
        v_new = heads(v.reshape(b, t, GROUP_W)[:, t - keep:], keep)
    else:
        ck, cv = cache
        att = _attention(q, ck.reshape(b * PAST_BAND, GROUP_W), k, cv.reshape(b * PAST_BAND, GROUP_W), v,
                         lw["bias_sample"], lw["hmask_sample"], batch=b, tiles=1, cq=t, nq=1,
                         prev_index=lambda bi, i: (bi, 0), mask_first=False)
        k_new, v_new = heads(k, t), heads(v, t)

    tp = -(-t // CHUNK) * CHUNK
    valid = t if t < CHUNK else CHUNK

    def streams(a):
        a = a.reshape(b, t, -1)
        return a if tp == t else jnp.pad(a, ((0, 0), (0, tp - t), (0, 0)))

    if state is None:
        c0t = jnp.zeros((b, HEAD_DIM, GROUP_W), F32)
        n0 = jnp.zeros((b, 1, GROUP_W), F32)
        m0 = jnp.zeros((b, 1, GROUP_W), F32)
    else:
        c_in, n_in, m_in = state
        c0t = c_in.astype(F32).transpose(0, 3, 1, 2).reshape(b, HEAD_DIM, GROUP_W)
        n0 = n_in.astype(F32).reshape(b, 1, GROUP_W)
        m0 = jnp.repeat(m_in.astype(F32), HEAD_DIM, axis=-1).reshape(b, 1, GROUP_W)
    hm, ct, n_out, m_out = _mlstm(streams(mq), streams(mk), streams(mv), streams(mo), streams(gates),
                                  c0t, n0, m0, lw["mlstm_consts"], valid=valid)
    hm = hm[:, :t].reshape(n, GROUP_W)
    c_new = ct.reshape(b, HEAD_DIM, N_HEADS, HEAD_DIM).transpose(0, 2, 3, 1)
    n_new = n_out.reshape(b, N_HEADS, HEAD_DIM)
    m_new = m_out.reshape(b, N_HEADS, HEAD_DIM)[:, :, 0]
    return att, hm, (k_new, v_new, c_new, n_new, m_new)


def kernel(x_prompt, x_sample, cache_k, cache_v, state_C, state_n, state_m, g_mix, w_in, g_q, g_k, rel_bias,
           b_igate, b_fgate, g_mlstm, w_out, g_ffn, w_router, b_router, w_up, b_up, w_down, b_down):
    depth = w_in.shape[0]
    yp, ys = x_prompt, x_sample
    bs, ts = x_sample.shape[0], x_sample.shape[1]
    n_prompt = x_prompt.shape[0] * x_prompt.shape[1]
    st_prompt, st_sample = [], []
    n_main = N_PROJ * GROUP_W
    gmat = jnp.asarray(_head_block_diag() / HEAD_DIM, BF16)
    for l in range(depth):
        lw = dict(
            g_mix=g_mix[l].astype(F32)[None, :],
            w_main=w_in[l][:, :n_main].astype(BF16),
            w_gate=jnp.pad(w_in[l][:, n_main:].astype(F32), ((0, 0), (0, LANES - 2 * N_HEADS))),
            gq=jnp.tile(g_q[l].astype(F32), N_HEADS)[None, :],
            gk=jnp.tile(g_k[l].astype(F32), N_HEADS)[None, :],
            gmat=gmat,
            bias_prompt=_rel_table(rel_bias[l], CHUNK),
            hmask_prompt=_head_row_mask(CHUNK),
            bias_sample=_rel_table(rel_bias[l], ts),
            hmask_sample=_head_row_mask(ts),
            mlstm_consts=_mlstm_consts(b_igate[l], b_fgate[l], g_mlstm[l]),
        )
        ffn_w = (w_up[l].astype(F32), w_down[l].astype(F32),
                 b_up[l][:, None, 0::2].astype(F32), b_up[l][:, None, 1::2].astype(F32),
                 b_down[l][:, None, :].astype(F32))
        att_p, hm_p, sp = _mixer(yp, lw, None, None)
        cache = (cache_k[l].reshape(bs, PAST_BAND, GROUP_W), cache_v[l].reshape(bs, PAST_BAND, GROUP_W))
        att_s, hm_s, ss = _mixer(ys, lw, cache, (state_C[l], state_n[l], state_m[l]))
        y, xf, logits = _out_proj(
            yp.reshape(-1, D_MODEL), ys.reshape(-1, D_MODEL), att_p, att_s, hm_p, hm_s,
            w_out[l][:GROUP_W].astype(BF16), w_out[l][GROUP_W:].astype(BF16), g_ffn[l].astype(F32)[None, :],
            jnp.pad(w_router[l].astype(F32), ((0, 0), (0, LANES - N_EXPERTS))),
            jnp.pad(b_router[l].astype(F32), (0, LANES - N_EXPERTS))[None, :])
        out_p, out_s = _moe(y, xf, logits, ffn_w, n_prompt)
        yp, ys = out_p.reshape(x_prompt.shape), out_s.reshape(x_sample.shape)
        st_prompt.append(sp)
        st_sample.append(ss)
    k_p, v_p, c_p, n_p, m_p = [jnp.stack(a) for a in zip(*st_prompt)]
    k_s, v_s, c_s, n_s, m_s = [jnp.stack(a) for a in zip(*st_sample)]
    return (yp, ys, k_p, v_p, c_p, n_p, m_p, k_s, v_s, c_s, n_s, m_s)
```

```python
import functools

import numpy as np
import jax
import jax.numpy as jnp
from jax import lax
from jax.experimental import pallas as pl
from jax.experimental.pallas import tpu as pltpu

F32 = jnp.float32
BF16 = jnp.bfloat16
I32 = jnp.int32
HIGHEST = lax.Precision.HIGHEST

D_MODEL = 1024
N_HEADS = 8
HEAD_DIM = 64
GROUP_W = N_HEADS * HEAD_DIM
N_PROJ = 7
LANES = 128
CHUNK = 64
PAST_BAND = 512
KEY_WIN = 640
REL_CLIP = 256
N_EXPERTS = 32
TOP_K = 4
D_FF = 1024
SWIGLU_ALPHA = 1.702
SWIGLU_LIMIT = 7.0
RMS_EPS = 1e-6
NEG_BIG = -1e30
ROW_TILE = 512
EXPERT_TILE = 512
SUBLANES = 8
SORT_CHUNK = 256
LOCAL_ROWS = -(-(ROW_TILE * TOP_K + N_EXPERTS * (SUBLANES - 1)) // SORT_CHUNK) * SORT_CHUNK
SEG_SIZES = (512, 256, 128, 64, 32, 16, 8)
MLSTM_STREAMS = 4
VMEM_LIMIT_BYTES = 56 * 1024 * 1024


def _params(*sem):
    return pltpu.CompilerParams(dimension_semantics=sem, vmem_limit_bytes=VMEM_LIMIT_BYTES)


def _head_block_diag():
    h = np.arange(GROUP_W) // HEAD_DIM
    return (h[:, None] == h[None, :]).astype(np.float32)


def _full(shape):
    return pl.BlockSpec(shape, lambda *_: (0,) * len(shape))


def _in_proj_body(x_ref, gmix_ref, w_ref, wg_ref, gq_ref, gk_ref, gmat_ref,
                  q_ref, k_ref, v_ref, mq_ref, mk_ref, mv_ref, mo_ref, gate_ref):
    x = x_ref[...]
    xn = x * lax.rsqrt(jnp.mean(x * x, axis=-1, keepdims=True) + RMS_EPS) * gmix_ref[...]
    xb = xn.astype(BF16)

    def proj(j):
        return jnp.dot(xb, w_ref[:, j * GROUP_W:(j + 1) * GROUP_W], preferred_element_type=F32)

    def head_norm(a, g_ref):
        msq = jnp.dot((a * a).astype(BF16), gmat_ref[...], preferred_element_type=F32)
        return a * lax.rsqrt(msq + RMS_EPS) * g_ref[...]

    q_ref[...] = head_norm(proj(0), gq_ref)
    k_ref[...] = head_norm(proj(1), gk_ref)
    v_ref[...] = proj(2)
    mq_ref[...] = proj(3)
    mk_ref[...] = proj(4)
    mv_ref[...] = proj(5)
    mo_ref[...] = proj(6)
    gate_ref[...] = jnp.dot(xn, wg_ref[...], preferred_element_type=F32, precision=HIGHEST)


def _in_proj(x2d, g_mix, w_main, w_gate, gq_row, gk_row, gmat):
    n = x2d.shape[0]
    tm = ROW_TILE
    row = lambda w: pl.BlockSpec((tm, w), lambda i: (i, 0))
    outs = [jax.ShapeDtypeStruct((n, GROUP_W), F32)] * N_PROJ + [jax.ShapeDtypeStruct((n, LANES), F32)]
    return pl.pallas_call(
        _in_proj_body,
        grid=(n // tm,),
        in_specs=[row(D_MODEL), _full((1, D_MODEL)), _full((D_MODEL, N_PROJ * GROUP_W)),
                  _full((D_MODEL, LANES)), _full((1, GROUP_W)), _full((1, GROUP_W)),
                  _full((GROUP_W, GROUP_W))],
        out_specs=[row(GROUP_W)] * N_PROJ + [row(LANES)],
        out_shape=outs,
        compiler_params=_params("arbitrary"),
        name="in_proj",
    )(x2d, g_mix, w_main, w_gate, gq_row, gk_row, gmat)


def _attn_body(q_ref, kp_ref, kc_ref, vp_ref, vc_ref, bias_ref, hmask_ref, o_ref, kwin, vwin,
               *, cq, nq, mask_first):
    tc = cq * nq
    i = pl.program_id(1)
    kwin[0:PAST_BAND, :] = kp_ref[...].astype(BF16)
    kwin[PAST_BAND:PAST_BAND + tc, :] = kc_ref[...].astype(BF16)
    vwin[0:PAST_BAND, :] = vp_ref[...].astype(BF16)
    vwin[PAST_BAND:PAST_BAND + tc, :] = vc_ref[...].astype(BF16)
    pad_rows = kwin.shape[0] - PAST_BAND - tc
    kwin[PAST_BAND + tc:, :] = jnp.zeros((pad_rows, GROUP_W), BF16)
    vwin[PAST_BAND + tc:, :] = jnp.zeros((pad_rows, GROUP_W), BF16)

    hm = hmask_ref[...]
    bias = bias_ref[...]
    kk = lax.broadcasted_iota(I32, (1, KEY_WIN), 1)

    def chunk(j, carry):
        r0 = pl.multiple_of(j * cq, cq)
        q = q_ref[pl.ds(r0, cq), :] * (HEAD_DIM ** -0.5)
        qm = (jnp.concatenate([q] * N_HEADS, axis=0) * hm).astype(BF16)
        kw = kwin[pl.ds(r0, KEY_WIN), :]
        s = lax.dot_general(qm, kw, (((1,), (1,)), ((), ())), preferred_element_type=F32) + bias
        if mask_first:
            first_valid = jnp.where(i == 0, PAST_BAND - r0, 0)
            s = jnp.where(kk >= first_valid, s, NEG_BIG)
        m = jnp.max(s, axis=-1, keepdims=True)
        p = jnp.exp(s - m)
        l = jnp.sum(p, axis=-1, keepdims=True)
        vw = vwin[pl.ds(r0, KEY_WIN), :]
        o_all = jnp.dot(p.astype(BF16), vw, preferred_element_type=F32) / l * hm
        o = o_all[0:cq]
        for h in range(1, N_HEADS):
            o = o + o_all[h * cq:(h + 1) * cq]
        o_ref[pl.ds(r0, cq), :] = o
        return carry

    lax.fori_loop(0, nq, chunk, 0)


def _attention(q, k_prev_src, k_cur_src, v_prev_src, v_cur_src, bias, hmask, *, batch, tiles, cq, nq,
               prev_index, mask_first):
    tc = cq * nq
    cur = pl.BlockSpec((tc, GROUP_W), lambda b, i: (b * tiles + i, 0))
    prev = pl.BlockSpec((PAST_BAND, GROUP_W), prev_index)
    win_rows = (nq - 1) * cq + KEY_WIN
    return pl.pallas_call(
        functools.partial(_attn_body, cq=cq, nq=nq, mask_first=mask_first),
        grid=(batch, tiles),
        in_specs=[cur, prev, cur, prev, cur, _full((N_HEADS * cq, KEY_WIN)), _full((N_HEADS * cq, GROUP_W))],
        out_specs=cur,
        out_shape=jax.ShapeDtypeStruct(q.shape, F32),
        scratch_shapes=[pltpu.VMEM((win_rows, GROUP_W), BF16), pltpu.VMEM((win_rows, GROUP_W), BF16)],
        compiler_params=_params("arbitrary", "arbitrary"),
        name="band_attention",
    )(q, k_prev_src, k_cur_src, v_prev_src, v_cur_src, bias, hmask)


def _rel_table(rel_bias_l, cq):
    nk = PAST_BAND + cq
    dist = PAST_BAND + (cq - 1) - np.arange(nk + cq - 1)
    rev = rel_bias_l[:, np.clip(dist, -REL_CLIP, REL_CLIP) + REL_CLIP].astype(F32)
    tab = jnp.stack([rev[:, cq - 1 - q:cq - 1 - q + nk] for q in range(cq)], axis=1)
    tab = jnp.pad(tab, ((0, 0), (0, 0), (0, KEY_WIN - nk)), constant_values=NEG_BIG)
    return tab.reshape(N_HEADS * cq, KEY_WIN)


def _head_row_mask(cq):
    h_row = np.repeat(np.arange(N_HEADS), cq)
    h_col = np.arange(GROUP_W) // HEAD_DIM
    return jnp.asarray((h_row[:, None] == h_col[None, :]).astype(np.float32))


def _log_sigmoid(x):
    return jnp.minimum(x, 0.0) - jnp.log(1.0 + jnp.exp(-jnp.abs(x)))


def _mlstm_body(q_ref, k_ref, v_ref, o_ref, g_ref, c0_ref, n0_ref, m0_ref,
                expand_ref, gbias_ref, bd_ref, bdb_ref, gmat_ref, ltri_ref, eye_ref, causal_ref, gml_ref,
                h_ref, ct_ref, n_ref, m_ref, ct_s, n_s, m_s, *, valid, nb):
    c = pl.program_id(1)
    last = c == pl.num_programs(1) - 1
    bd = bd_ref[...]
    bdb = bdb_ref[...]
    gmat = gmat_ref[...]
    eye = eye_ref[...] > 0.5
    causal = causal_ref[...] > 0.5

    @pl.when(c == 0)
    def _():
        for b in range(nb):
            ct_s[b] = jnp.concatenate([c0_ref[b]] * N_HEADS, axis=0) * bd
            n_s[b] = n0_ref[b]
            m_s[b] = m0_ref[b]

    for b in range(nb):
        gp = jnp.dot(g_ref[b], expand_ref[...], preferred_element_type=F32, precision=HIGHEST) + gbias_ref[...]
        log_i = gp[:, :GROUP_W]
        log_f = _log_sigmoid(gp[:, GROUP_W:])
        if valid < CHUNK:
            live = lax.broadcasted_iota(I32, (CHUNK, GROUP_W), 0) < valid
            log_i = jnp.where(live, log_i, -jnp.inf)
            log_f = jnp.where(live, log_f, 0.0)
        cum_f = jnp.dot(ltri_ref[...], log_f, preferred_element_type=F32, precision=HIGHEST)

        b_row = jnp.sum(jnp.where(eye, log_i - cum_f, 0.0), axis=0, keepdims=True)
        m_prev = m_s[b]
        log_inter = cum_f + m_prev
        log_d = jnp.where(causal, cum_f + b_row, -jnp.inf)
        max_d = jnp.concatenate(
            [jnp.broadcast_to(jnp.max(log_d[:, h * HEAD_DIM:(h + 1) * HEAD_DIM], axis=-1, keepdims=True),
                              (CHUNK, HEAD_DIM)) for h in range(N_HEADS)], axis=1)
        m_t = jnp.maximum(log_inter, max_d)
        w_intra = jnp.exp(log_d - m_t)
        w_inter = jnp.exp(log_inter - m_t)

        q = q_ref[b] * (HEAD_DIM ** -0.5)
        k = k_ref[b]
        v = v_ref[b]
        qb = q.astype(BF16)
        vb = v.astype(BF16)
        kbd = jnp.concatenate([k.astype(BF16)] * N_HEADS, axis=0) * bdb
        vbd = jnp.concatenate([vb] * N_HEADS, axis=0) * bdb
        s = lax.dot_general(qb, kbd, (((1,), (1,)), ((), ())), preferred_element_type=F32) * w_intra
        ct = ct_s[b]
        n_prev = n_s[b]
        num = (jnp.dot(s.astype(BF16), vbd, preferred_element_type=F32)
               + w_inter * lax.dot_general(qb, ct.astype(BF16), (((1,), (1,)), ((), ())),
                                           preferred_element_type=F32))
        den_terms = (s + w_inter * q * n_prev) * float(HEAD_DIM)
        den_hi = den_terms.astype(BF16)
        den_lo = (den_terms - den_hi.astype(F32)).astype(BF16)
        den = (jnp.dot(den_hi, gmat, preferred_element_type=F32)
               + jnp.dot(den_lo, gmat, preferred_element_type=F32))
        hb = num / jnp.maximum(jnp.abs(den), jnp.exp(-m_t))

        m_new = m_t[CHUNK - 1:CHUNK, :]
        cum_last = cum_f[CHUNK - 1:CHUNK, :]
        w_state = jnp.exp(cum_last - cum_f + log_i - m_new)
        decay = jnp.exp(cum_last + m_prev - m_new)
        kw = k * w_state
        upd = lax.dot_general(vb, kw.astype(BF16), (((0,), (0,)), ((), ())),
                              preferred_element_type=F32)
        ct_new = decay * ct + upd * bd
        n_new = decay * n_prev + jnp.sum(kw, axis=0, keepdims=True)
        ct_s[b] = ct_new
        n_s[b] = n_new
        m_s[b] = m_new

        msq = jnp.dot((hb * hb).astype(BF16), gmat, preferred_element_type=F32)
        h_ref[b] = jax.nn.sigmoid(o_ref[b]) * (hb * lax.rsqrt(msq + RMS_EPS) * gml_ref[...])

    @pl.when(last)
    def _():
        for b in range(nb):
            ct_new = ct_s[b]
            acc = ct_new[0:HEAD_DIM]
            for h in range(1, N_HEADS):
                acc = acc + ct_new[h * HEAD_DIM:(h + 1) * HEAD_DIM]
            ct_ref[b] = acc
            n_ref[b] = n_s[b]
            m_ref[b] = m_s[b]


def _mlstm(mq, mk, mv, mo, gates, c0t, n0, m0, consts, *, valid):
    batch, t, _ = mq.shape
    chunks = t // CHUNK
    nb = MLSTM_STREAMS
    row = lambda w: pl.BlockSpec((nb, CHUNK, w), lambda g, c: (g, c, 0))
    per_b = lambda r: pl.BlockSpec((nb, r, GROUP_W), lambda g, c: (g, 0, 0))
    expand, gbias, bd, bdb, gmat, ltri, eye, causal, gml = consts
    return pl.pallas_call(
        functools.partial(_mlstm_body, valid=valid, nb=nb),
        grid=(batch // nb, chunks),
        in_specs=[row(GROUP_W)] * 4 + [row(LANES), per_b(HEAD_DIM), per_b(1), per_b(1),
                  _full((LANES, 2 * GROUP_W)), _full((1, 2 * GROUP_W)), _full((GROUP_W, GROUP_W)),
                  _full((GROUP_W, GROUP_W)), _full((GROUP_W, GROUP_W)), _full((CHUNK, CHUNK)),
                  _full((CHUNK, GROUP_W)), _full((CHUNK, GROUP_W)), _full((1, GROUP_W))],
        out_specs=[row(GROUP_W), per_b(HEAD_DIM), per_b(1), per_b(1)],
        out_shape=[jax.ShapeDtypeStruct(mq.shape, F32),
                   jax.ShapeDtypeStruct((batch, HEAD_DIM, GROUP_W), F32),
                   jax.ShapeDtypeStruct((batch, 1, GROUP_W), F32),
                   jax.ShapeDtypeStruct((batch, 1, GROUP_W), F32)],
        scratch_shapes=[pltpu.VMEM((nb, GROUP_W, GROUP_W), F32), pltpu.VMEM((nb, 1, GROUP_W), F32),
                        pltpu.VMEM((nb, 1, GROUP_W), F32)],
        compiler_params=_params("arbitrary", "arbitrary"),
        name="mlstm",
    )(mq, mk, mv, mo, gates, c0t, n0, m0, expand, gbias, bd, bdb, gmat, ltri, eye, causal, gml)


def _mlstm_consts(b_igate_l, b_fgate_l, g_mlstm_l):
    expand = np.zeros((LANES, 2 * GROUP_W), np.float32)
    for h in range(N_HEADS):
        expand[h, h * HEAD_DIM:(h + 1) * HEAD_DIM] = 1.0
        expand[N_HEADS + h, GROUP_W + h * HEAD_DIM:GROUP_W + (h + 1) * HEAD_DIM] = 1.0
    gbias = jnp.concatenate([jnp.repeat(b_igate_l.astype(F32), HEAD_DIM),
                             jnp.repeat(b_fgate_l.astype(F32), HEAD_DIM)])[None, :]
    bd = _head_block_diag()
    ltri = np.tril(np.ones((CHUNK, CHUNK), np.float32))
    s_of_lane = np.arange(GROUP_W) % HEAD_DIM
    t = np.arange(CHUNK)
    eye = (t[:, None] == s_of_lane[None, :]).astype(np.float32)
    causal = (s_of_lane[None, :] <= t[:, None]).astype(np.float32)
    return (jnp.asarray(expand), gbias, jnp.asarray(bd), jnp.asarray(bd, BF16), jnp.asarray(bd / HEAD_DIM, BF16),
            jnp.asarray(ltri), jnp.asarray(eye), jnp.asarray(causal), g_mlstm_l.astype(F32).reshape(1, GROUP_W))


def _out_proj_body(xp_ref, xs_ref, ap_ref, as_ref, hp_ref, hs_ref, wa_ref, wm_ref, gffn_ref, wr_ref, br_ref,
                   y_ref, xf_ref, logit_ref, *, prompt_tiles):
    is_prompt = pl.program_id(0) < prompt_tiles
    x = jnp.where(is_prompt, xp_ref[...], xs_ref[...])
    att = jnp.where(is_prompt, ap_ref[...], as_ref[...])
    hm = jnp.where(is_prompt, hp_ref[...], hs_ref[...])
    y = (x + jnp.dot(att.astype(BF16), wa_ref[...], preferred_element_type=F32)
         + jnp.dot(hm.astype(BF16), wm_ref[...], preferred_element_type=F32))
    y_ref[...] = y
    xf = y * lax.rsqrt(jnp.mean(y * y, axis=-1, keepdims=True) + RMS_EPS) * gffn_ref[...]
    xf_ref[...] = xf
    logit_ref[...] = jnp.dot(xf, wr_ref[...], preferred_element_type=F32, precision=HIGHEST) + br_ref[...]


def _out_proj(xp, xs, att_p, att_s, hm_p, hm_s, wa, wm, g_ffn, w_router, b_router):
    tm = ROW_TILE
    pt, st = xp.shape[0] // tm, xs.shape[0] // tm
    n = xp.shape[0] + xs.shape[0]
    p_row = lambda w: pl.BlockSpec((tm, w), lambda i: (jnp.minimum(i, pt - 1), 0))
    s_row = lambda w: pl.BlockSpec((tm, w), lambda i: (jnp.maximum(i - pt, 0), 0))
    row = lambda w: pl.BlockSpec((tm, w), lambda i: (i, 0))
    return pl.pallas_call(
        functools.partial(_out_proj_body, prompt_tiles=pt),
        grid=(pt + st,),
        in_specs=[p_row(D_MODEL), s_row(D_MODEL), p_row(GROUP_W), s_row(GROUP_W), p_row(GROUP_W), s_row(GROUP_W),
                  _full((GROUP_W, D_MODEL)), _full((GROUP_W, D_MODEL)),
                  _full((1, D_MODEL)), _full((D_MODEL, LANES)), _full((1, LANES))],
        out_specs=[row(D_MODEL), row(D_MODEL), row(LANES)],
        out_shape=[jax.ShapeDtypeStruct((n, D_MODEL), F32), jax.ShapeDtypeStruct((n, D_MODEL), F32),
                   jax.ShapeDtypeStruct((n, LANES), F32)],
        compiler_params=_params("arbitrary"),
        name="out_proj_router",
    )(xp, xs, att_p, att_s, hm_p, hm_s, wa, wm, g_ffn, w_router, b_router)


def _route_body(logit_ref, lstrict_ref, ustrict_ref, gate_ref, pos_ref, seg_ref):
    tt = logit_ref.shape[0]
    lane = lax.broadcasted_iota(I32, (tt, LANES), 1)
    work = jnp.where(lane < N_EXPERTS, logit_ref[...], -jnp.inf)
    vals, idxs = [], []
    for _ in range(TOP_K):
        m = jnp.max(work, axis=-1, keepdims=True)
        idx = jnp.min(jnp.where(work == m, lane, LANES), axis=-1, keepdims=True)
        vals.append(m)
        idxs.append(idx)
        work = jnp.where(lane == idx, -jnp.inf, work)
    exps = [jnp.exp(v - vals[0]) for v in vals]
    total = exps[0] + exps[1] + exps[2] + exps[3]

    chosen = jnp.zeros((tt, LANES), F32)
    for idx in idxs:
        chosen = chosen + (lane == idx).astype(F32)
    before = jnp.dot(lstrict_ref[...], chosen.astype(BF16), preferred_element_type=F32)
    count = jnp.sum(chosen, axis=0, keepdims=True)
    groups = jnp.floor((count + (SUBLANES - 1)) * (1.0 / SUBLANES))
    groups8 = jnp.broadcast_to(groups, (SUBLANES, LANES)).astype(BF16)
    start = jnp.dot(groups8, ustrict_ref[...], preferred_element_type=F32) * float(SUBLANES)
    local = before + start[0:1, :]

    gate_out = jnp.zeros((tt, LANES), F32)
    pos_out = jnp.zeros((tt, LANES), F32)
    for k in range(TOP_K):
        pos = jnp.sum(jnp.where(lane == idxs[k], local, 0.0), axis=-1, keepdims=True)
        gate_out = jnp.where(lane == k, exps[k] / total, gate_out)
        pos_out = jnp.where(lane == k, pos, pos_out)
    gate_ref[...] = gate_out
    pos_ref[...] = pos_out
    row = lax.broadcasted_iota(I32, (SUBLANES, LANES), 0)
    seg = jnp.where(row == 0, groups * float(SUBLANES), jnp.where(row == 1, start, 0.0))
    seg_ref[...] = seg.astype(I32)


def _route(logits):
    n = logits.shape[0]
    tt = ROW_TILE
    lstrict = jnp.asarray(np.tril(np.ones((tt, tt), np.float32), -1), BF16)
    ustrict = jnp.asarray(np.triu(np.ones((LANES, LANES), np.float32), 1), BF16)
    row = lambda: pl.BlockSpec((tt, LANES), lambda i: (i, 0))
    return pl.pallas_call(
        _route_body,
        grid=(n // tt,),
        in_specs=[row(), _full((tt, tt)), _full((LANES, LANES))],
        out_specs=[row(), row(), pl.BlockSpec((SUBLANES, LANES), lambda i: (i, 0))],
        out_shape=[jax.ShapeDtypeStruct((n, LANES), F32), jax.ShapeDtypeStruct((n, LANES), F32),
                   jax.ShapeDtypeStruct((n // tt * SUBLANES, LANES), I32)],
        compiler_params=_params("arbitrary"),
        name="route_topk",
    )(logits, lstrict, ustrict)


def _segment_copies(i, len_ref, lst_ref, off_ref, make_copy, act):
    for e in range(N_EXPERTS):
        seg = i * N_EXPERTS + e
        length = len_ref[seg]
        local = lst_ref[seg]
        glob = off_ref[seg]
        for size in SEG_SIZES:
            take = length & size

            @pl.when(take != 0)
            def _(local=local, glob=glob, size=size):
                act(make_copy(pl.multiple_of(local, SUBLANES), pl.multiple_of(glob, SUBLANES), size))

            local = local + take
            glob = glob + take


def _local_onehot(pos_rows, base, rows, values=None):
    r = (lax.broadcasted_iota(I32, (rows, pos_rows[0].shape[1]), 0) + base).astype(F32)
    out = jnp.zeros(r.shape, F32)
    for k in range(TOP_K):
        out = jnp.where(r == pos_rows[k], 1.0 if values is None else values[k], out)
    return out


def _dispatch_body(len_ref, lst_ref, off_ref, fill_ref, nused_ref, xf_ref, pos_ref, xs_hbm,
                   xloc, zbuf, sem, zsem, *, tm, n_tiles):
    i = pl.program_id(0)
    tt = xf_ref.shape[0]
    fill_rows = zbuf.shape[0]

    @pl.when(i == 0)
    def _():
        zbuf[...] = jnp.zeros_like(zbuf)

        def fill(e):
            start = pl.multiple_of(fill_ref[e], SUBLANES)
            return pltpu.make_async_copy(zbuf, xs_hbm.at[pl.ds(start, fill_rows)], zsem)

        for e in range(N_EXPERTS):
            fill(e).start()
        for e in range(N_EXPERTS):
            fill(e).wait()

        def tail(j, carry):
            cp = pltpu.make_async_copy(zbuf.at[pl.ds(0, tm)], xs_hbm.at[pl.ds(pl.multiple_of(j * tm, tm), tm)], zsem)
            cp.start()
            cp.wait()
            return carry

        lax.fori_loop(nused_ref[0], n_tiles, tail, 0)

    pos_t = jnp.transpose(pos_ref[...])
    pos_rows = [pos_t[k:k + 1, :] for k in range(TOP_K)]
    xb = xf_ref[...].astype(BF16)

    def sort_rows(c, carry):
        r0 = pl.multiple_of(c * SORT_CHUNK, SORT_CHUNK)
        sel = _local_onehot(pos_rows, r0, SORT_CHUNK).astype(BF16)
        xloc[pl.ds(r0, SORT_CHUNK), :] = jnp.dot(sel, xb, preferred_element_type=F32)
        return carry

    lax.fori_loop(0, LOCAL_ROWS // SORT_CHUNK, sort_rows, 0)

    def make_copy(local, glob, size):
        return pltpu.make_async_copy(xloc.at[pl.ds(local, size)], xs_hbm.at[pl.ds(glob, size)], sem)

    _segment_copies(i, len_ref, lst_ref, off_ref, make_copy, lambda cp: cp.start())
    _segment_copies(i, len_ref, lst_ref, off_ref, make_copy, lambda cp: cp.wait())


def _dispatch(seg_len, seg_local, seg_off, fill_start, n_used, xf, pos, n_tiles):
    n = xf.shape[0]
    tt, tm = ROW_TILE, EXPERT_TILE
    fill_rows = tm + SUBLANES
    grid_spec = pltpu.PrefetchScalarGridSpec(
        num_scalar_prefetch=5,
        grid=(n // tt,),
        in_specs=[pl.BlockSpec((tt, D_MODEL), lambda i, *_: (i, 0)),
                  pl.BlockSpec((tt, LANES), lambda i, *_: (i, 0))],
        out_specs=pl.BlockSpec(memory_space=pl.ANY),
        scratch_shapes=[pltpu.VMEM((LOCAL_ROWS, D_MODEL), F32), pltpu.VMEM((fill_rows, D_MODEL), F32),
                        pltpu.SemaphoreType.DMA(()), pltpu.SemaphoreType.DMA(())],
    )
    return pl.pallas_call(
        functools.partial(_dispatch_body, tm=tm, n_tiles=n_tiles + 2),
        grid_spec=grid_spec,
        out_shape=jax.ShapeDtypeStruct(((n_tiles + 2) * tm, D_MODEL), F32),
        compiler_params=_params("arbitrary"),
        name="expert_dispatch",
    )(seg_len, seg_local, seg_off, fill_start, n_used, xf, pos)


def _expert_body(te_ref, nused_ref, x_ref, wup_ref, wdn_ref, perm_ref, bg_ref, bl_ref, bd_ref,
                 y_ref, wg_s, wl_s, wd_s):
    i = pl.program_id(0)
    n_used = nused_ref[0]

    @pl.when(i >= n_used)
    def _():
        y_ref[...] = jnp.zeros_like(y_ref)

    @pl.when((i == 0) | (te_ref[i] != te_ref[jnp.maximum(i - 1, 0)]))
    def _():
        perm = perm_ref[...]
        for c in range(2 * D_FF // 256):
            blk = wup_ref[0, :, c * 256:(c + 1) * 256].astype(BF16)
            sep = jnp.dot(blk, perm, preferred_element_type=F32).astype(BF16)
            wg_s[:, c * 128:(c + 1) * 128] = sep[:, :128]
            wl_s[:, c * 128:(c + 1) * 128] = sep[:, 128:]
        wd_s[...] = wdn_ref[0].astype(BF16)

    @pl.when(i < n_used)
    def _():
        x = x_ref[...].astype(BF16)
        glu = jnp.minimum(jnp.dot(x, wg_s[...], preferred_element_type=F32) + bg_ref[0], SWIGLU_LIMIT)
        lin = jnp.clip(jnp.dot(x, wl_s[...], preferred_element_type=F32) + bl_ref[0], -SWIGLU_LIMIT, SWIGLU_LIMIT)
        act = glu * jax.nn.sigmoid(SWIGLU_ALPHA * glu) * (lin + 1.0)
        y_ref[...] = jnp.dot(act.astype(BF16), wd_s[...], preferred_element_type=F32) + bd_ref[0]


def _deinterleave_perm():
    p = np.zeros((256, 256), np.float32)
    j = np.arange(128)
    p[2 * j, j] = 1.0
    p[2 * j + 1, 128 + j] = 1.0
    return jnp.asarray(p, BF16)


def _experts(tile_expert, n_used, x_sorted, w_up, w_down, b_glu, b_lin, b_down):
    tm = EXPERT_TILE
    n_tiles = tile_expert.shape[0]
    wspec = lambda k, n: pl.BlockSpec((1, k, n), lambda i, te, nu: (te[i], 0, 0))
    grid_spec = pltpu.PrefetchScalarGridSpec(
        num_scalar_prefetch=2,
        grid=(n_tiles,),
        in_specs=[pl.BlockSpec((tm, D_MODEL), lambda i, te, nu: (jnp.minimum(i, nu[0] - 1), 0)),
                  wspec(D_MODEL, 2 * D_FF), wspec(D_FF, D_MODEL),
                  pl.BlockSpec((256, 256), lambda i, te, nu: (0, 0)),
                  wspec(1, D_FF), wspec(1, D_FF), wspec(1, D_MODEL)],
        out_specs=pl.BlockSpec((tm, D_MODEL), lambda i, te, nu: (i, 0)),
        scratch_shapes=[pltpu.VMEM((D_MODEL, D_FF), BF16), pltpu.VMEM((D_MODEL, D_FF), BF16),
                        pltpu.VMEM((D_FF, D_MODEL), BF16)],
    )
    return pl.pallas_call(
        _expert_body,
        grid_spec=grid_spec,
        out_shape=jax.ShapeDtypeStruct((n_tiles * tm, D_MODEL), F32),
        compiler_params=_params("arbitrary"),
        name="expert_ffn",
    )(tile_expert, n_used, x_sorted, w_up, w_down, _deinterleave_perm(), b_glu, b_lin, b_down)


def _combine_body(len_ref, lst_ref, off_ref, y_ref, gate_ref, pos_ref, rows_hbm, outp_ref, outs_ref,
                  yloc, acc_s, sem, *, prompt_tiles):
    i = pl.program_id(0)

    @pl.when(i == 0)
    def _():
        yloc[...] = jnp.zeros_like(yloc)

    def make_copy(local, glob, size):
        return pltpu.make_async_copy(rows_hbm.at[pl.ds(glob, size)], yloc.at[pl.ds(local, size)], sem)

    _segment_copies(i, len_ref, lst_ref, off_ref, make_copy, lambda cp: cp.start())
    _segment_copies(i, len_ref, lst_ref, off_ref, make_copy, lambda cp: cp.wait())

    pos = pos_ref[...]
    gate = gate_ref[...]
    acc_s[...] = y_ref[...]

    def weigh(c, carry):
        r0 = pl.multiple_of(c * SORT_CHUNK, SORT_CHUNK)
        r = (lax.broadcasted_iota(I32, (pos.shape[0], SORT_CHUNK), 1) + r0).astype(F32)
        w = jnp.zeros(r.shape, F32)
        for k in range(TOP_K):
            w = jnp.where(r == pos[:, k:k + 1], gate[:, k:k + 1], w)
        acc_s[...] += jnp.dot(w.astype(BF16), yloc[pl.ds(r0, SORT_CHUNK), :].astype(BF16),
                              preferred_element_type=F32)
        return carry

    lax.fori_loop(0, LOCAL_ROWS // SORT_CHUNK, weigh, 0)

    @pl.when(i < prompt_tiles)
    def _():
        outp_ref[...] = acc_s[...]

    @pl.when(i >= prompt_tiles)
    def _():
        outs_ref[...] = acc_s[...]


def _combine(seg_len, seg_local, seg_off, y, gates, pos, y_rows, n_prompt):
    n = y.shape[0]
    tt = ROW_TILE
    nt, pt = n // tt, n_prompt // tt
    grid_spec = pltpu.PrefetchScalarGridSpec(
        num_scalar_prefetch=3,
        grid=(nt,),
        in_specs=[pl.BlockSpec((tt, D_MODEL), lambda i, *_: (i, 0)),
                  pl.BlockSpec((tt, LANES), lambda i, *_: (i, 0)),
                  pl.BlockSpec((tt, LANES), lambda i, *_: (i, 0)),
                  pl.BlockSpec(memory_space=pl.ANY)],
        out_specs=[pl.BlockSpec((tt, D_MODEL), lambda i, *_: (jnp.minimum(i, pt - 1), 0)),
                   pl.BlockSpec((tt, D_MODEL), lambda i, *_: (jnp.maximum(i - pt, 0), 0))],
        scratch_shapes=[pltpu.VMEM((LOCAL_ROWS, D_MODEL), F32), pltpu.VMEM((tt, D_MODEL), F32),
                        pltpu.SemaphoreType.DMA(())],
    )
    return pl.pallas_call(
        functools.partial(_combine_body, prompt_tiles=pt),
        grid_spec=grid_spec,
        out_shape=[jax.ShapeDtypeStruct((n_prompt, D_MODEL), F32),
                   jax.ShapeDtypeStruct((n - n_prompt, D_MODEL), F32)],
        compiler_params=_params("arbitrary"),
        name="expert_combine",
    )(seg_len, seg_local, seg_off, y, gates, pos, y_rows)


def _moe(y, xf, logits, ffn_w, n_prompt):
    n = y.shape[0]
    tm, tt = EXPERT_TILE, ROW_TILE
    n_tt = n // tt
    gates, pos, seg = _route(logits)
    seg = seg.reshape(n_tt, SUBLANES, LANES)
    seg_len, seg_local = seg[:, 0, :N_EXPERTS], seg[:, 1, :N_EXPERTS]
    rows = jnp.sum(seg_len, axis=0)
    padded = (rows + tm - 1) // tm * tm
    pad_end = jnp.cumsum(padded)
    pad_start = pad_end - padded
    seg_off = pad_start[None, :] + jnp.cumsum(seg_len, axis=0) - seg_len
    n_tiles = -(-(n * TOP_K + n_tt * N_EXPERTS * (SUBLANES - 1) + N_EXPERTS * (tm - 1)) // tm)
    tile_expert = jnp.minimum(jnp.sum(pad_end[None, :] <= (jnp.arange(n_tiles) * tm)[:, None], axis=1),
                              N_EXPERTS - 1).astype(I32)
    n_used = (pad_end[-1:] // tm).astype(I32)
    fill_start = (pad_start + rows).astype(I32)
    flat = lambda a: a.astype(I32).reshape(n_tt * N_EXPERTS)

    x_sorted = _dispatch(flat(seg_len), flat(seg_local), flat(seg_off), fill_start, n_used, xf, pos, n_tiles)
    y_rows = _experts(tile_expert, n_used, x_sorted, *ffn_w)
    return _combine(flat(seg_len), flat(seg_local), flat(seg_off), y, gates, pos, y_rows, n_prompt)


def _mixer(x, lw, cache, state):
    b, t, _ = x.shape
    n = b * t
    q, k, v, mq, mk, mv, mo, gates = _in_proj(x.reshape(n, D_MODEL), lw["g_mix"], lw["w_main"], lw["w_gate"],
                                              lw["gq"], lw["gk"], lw["gmat"])
    heads = lambda a, rows: a.reshape(b, rows, N_HEADS, HEAD_DIM)
    if cache is None:
        tiles = t // PAST_BAND
        att = _attention(q, k, k, v, v, lw["bias_prompt"], lw["hmask_prompt"], batch=b, tiles=tiles, cq=CHUNK,
                         nq=PAST_BAND // CHUNK,
                         prev_index=lambda bi, i: (bi * tiles + jnp.maximum(i - 1, 0), 0), mask_first=True)
        keep = min(PAST_BAND, t)
        k_new = heads(k.reshape(b, t, GROUP_W)[:, t - keep:], keep)
        v_new = heads(v.reshape(b, t, GROUP_W)[:, t - keep:], keep)
    else:
        ck, cv = cache
        att = _attention(q, ck.reshape(b * PAST_BAND, GROUP_W), k, cv.reshape(b * PAST_BAND, GROUP_W), v,
                         lw["bias_sample"], lw["hmask_sample"], batch=b, tiles=1, cq=t, nq=1,
                         prev_index=lambda bi, i: (bi, 0), mask_first=False)
        k_new, v_new = heads(k, t), heads(v, t)

    tp = -(-t // CHUNK) * CHUNK
    valid = t if t < CHUNK else CHUNK

    def streams(a):
        a = a.reshape(b, t, -1)
        return a if tp == t else jnp.pad(a, ((0, 0), (0, tp - t), (0, 0)))

    if state is None:
        c0t = jnp.zeros((b, HEAD_DIM, GROUP_W), F32)
        n0 = jnp.zeros((b, 1, GROUP_W), F32)
        m0 = jnp.zeros((b, 1, GROUP_W), F32)
    else:
        c_in, n_in, m_in = state
        c0t = c_in.astype(F32).transpose(0, 3, 1, 2).reshape(b, HEAD_DIM, GROUP_W)
        n0 = n_in.astype(F32).reshape(b, 1, GROUP_W)
        m0 = jnp.repeat(m_in.astype(F32), HEAD_DIM, axis=-1).reshape(b, 1, GROUP_W)
    hm, ct, n_out, m_out = _mlstm(streams(mq), streams(mk), streams(mv), streams(mo), streams(gates),
                                  c0t, n0, m0, lw["mlstm_consts"], valid=valid)
    hm = hm[:, :t].reshape(n, GROUP_W)
    c_new = ct.reshape(b, HEAD_DIM, N_HEADS, HEAD_DIM).transpose(0, 2, 3, 1)
    n_new = n_out.reshape(b, N_HEADS, HEAD_DIM)
    m_new = m_out.reshape(b, N_HEADS, HEAD_DIM)[:, :, 0]
    return att, hm, (k_new, v_new, c_new, n_new, m_new)


def kernel(x_prompt, x_sample, cache_k, cache_v, state_C, state_n, state_m, g_mix, w_in, g_q, g_k, rel_bias,
           b_igate, b_fgate, g_mlstm, w_out, g_ffn, w_router, b_router, w_up, b_up, w_down, b_down):
    depth = w_in.shape[0]
    yp, ys = x_prompt, x_sample
    bs, ts = x_sample.shape[0], x_sample.shape[1]
    n_prompt = x_prompt.shape[0] * x_prompt.shape[1]
    st_prompt, st_sample = [], []
    n_main = N_PROJ * GROUP_W
    gmat = jnp.asarray(_head_block_diag() / HEAD_DIM, BF16)
    for l in range(depth):
        lw = dict(
            g_mix=g_mix[l].astype(F32)[None, :],
            w_main=w_in[l][:, :n_main].astype(BF16),
            w_gate=jnp.pad(w_in[l][:, n_main:].astype(F32), ((0, 0), (0, LANES - 2 * N_HEADS))),
            gq=jnp.tile(g_q[l].astype(F32), N_HEADS)[None, :],
            gk=jnp.tile(g_k[l].astype(F32), N_HEADS)[None, :],
            gmat=gmat,
            bias_prompt=_rel_table(rel_bias[l], CHUNK),
            hmask_prompt=_head_row_mask(CHUNK),
            bias_sample=_rel_table(rel_bias[l], ts),
            hmask_sample=_head_row_mask(ts),
            mlstm_consts=_mlstm_consts(b_igate[l], b_fgate[l], g_mlstm[l]),
        )
        ffn_w = (w_up[l].astype(F32), w_down[l].astype(F32),
                 b_up[l][:, None, 0::2].astype(F32), b_up[l][:, None, 1::2].astype(F32),
                 b_down[l][:, None, :].astype(F32))
        att_p, hm_p, sp = _mixer(yp, lw, None, None)
        cache = (cache_k[l].reshape(bs, PAST_BAND, GROUP_W), cache_v[l].reshape(bs, PAST_BAND, GROUP_W))
        att_s, hm_s, ss = _mixer(ys, lw, cache, (state_C[l], state_n[l], state_m[l]))
        y, xf, logits = _out_proj(
            yp.reshape(-1, D_MODEL), ys.reshape(-1, D_MODEL), att_p, att_s, hm_p, hm_s,
            w_out[l][:GROUP_W].astype(BF16), w_out[l][GROUP_W:].astype(BF16), g_ffn[l].astype(F32)[None, :],
            jnp.pad(w_router[l].astype(F32), ((0, 0), (0, LANES - N_EXPERTS))),
            jnp.pad(b_router[l].astype(F32), (0, LANES - N_EXPERTS))[None, :])
        out_p, out_s = _moe(y, xf, logits, ffn_w, n_prompt)
        yp, ys = out_p.reshape(x_prompt.shape), out_s.reshape(x_sample.shape)
        st_prompt.append(sp)
        st_sample.append(ss)
    k_p, v_p, c_p, n_p, m_p = [jnp.stack(a) for a in zip(*st_prompt)]
    k_s, v_s, c_s, n_s, m_s = [jnp.stack(a) for a in zip(*st_sample)]
    return (yp, ys, k_p, v_p, c_p, n_p, m_p, k_s, v_s, c_s, n_s, m_s)
```

```python
import functools

import numpy as np
import jax
import jax.numpy as jnp
from jax import lax
from jax.experimental import pallas as pl
from jax.experimental.pallas import tpu as pltpu

F32 = jnp.float32
BF16 = jnp.bfloat16
I32 = jnp.int32
HIGHEST = lax.Precision.HIGHEST

D_MODEL = 1024
N_HEADS = 8
HEAD_DIM = 64
GROUP_W = N_HEADS * HEAD_DIM
N_PROJ = 7
LANES = 128
CHUNK = 64
PAST_BAND = 512
KEY_WIN = 640
REL_CLIP = 256
N_EXPERTS = 32
TOP_K = 4
D_FF = 1024
SWIGLU_ALPHA = 1.702
SWIGLU_LIMIT = 7.0
RMS_EPS = 1e-6
NEG_BIG = -1e30
ROW_TILE = 512
EXPERT_TILE = 512
SUBLANES = 8
SORT_CHUNK = 256
LOCAL_ROWS = -(-(ROW_TILE * TOP_K + N_EXPERTS * (SUBLANES - 1)) // SORT_CHUNK) * SORT_CHUNK
SEG_SIZES = (512, 256, 128, 64, 32, 16, 8)
MLSTM_STREAMS = 4
VMEM_LIMIT_BYTES = 56 * 1024 * 1024


def _params(*sem):
    return pltpu.CompilerParams(dimension_semantics=sem, vmem_limit_bytes=VMEM_LIMIT_BYTES)


def _head_block_diag():
    h = np.arange(GROUP_W) // HEAD_DIM
    return (h[:, None] == h[None, :]).astype(np.float32)


def _full(shape):
    return pl.BlockSpec(shape, lambda *_: (0,) * len(shape))


def _in_proj_body(x_ref, gmix_ref, w_ref, wg_ref, gq_ref, gk_ref, gmat_ref,
                  q_ref, k_ref, v_ref, mq_ref, mk_ref, mv_ref, mo_ref, gate_ref):
    x = x_ref[...]
    xn = x * lax.rsqrt(jnp.mean(x * x, axis=-1, keepdims=True) + RMS_EPS) * gmix_ref[...]
    xb = xn.astype(BF16)

    def proj(j):
        return jnp.dot(xb, w_ref[:, j * GROUP_W:(j + 1) * GROUP_W], preferred_element_type=F32)

    def head_norm(a, g_ref):
        msq = jnp.dot((a * a).astype(BF16), gmat_ref[...], preferred_element_type=F32)
        return a * lax.rsqrt(msq + RMS_EPS) * g_ref[...]

    q_ref[...] = head_norm(proj(0), gq_ref)
    k_ref[...] = head_norm(proj(1), gk_ref)
    v_ref[...] = proj(2)
    mq_ref[...] = proj(3)
    mk_ref[...] = proj(4)
    mv_ref[...] = proj(5)
    mo_ref[...] = proj(6)
    gate_ref[...] = jnp.dot(xn, wg_ref[...], preferred_element_type=F32, precision=HIGHEST)


def _in_proj(x2d, g_mix, w_main, w_gate, gq_row, gk_row, gmat):
    n = x2d.shape[0]
    tm = ROW_TILE
    row = lambda w: pl.BlockSpec((tm, w), lambda i: (i, 0))
    outs = [jax.ShapeDtypeStruct((n, GROUP_W), F32)] * N_PROJ + [jax.ShapeDtypeStruct((n, LANES), F32)]
    return pl.pallas_call(
        _in_proj_body,
        grid=(n // tm,),
        in_specs=[row(D_MODEL), _full((1, D_MODEL)), _full((D_MODEL, N_PROJ * GROUP_W)),
                  _full((D_MODEL, LANES)), _full((1, GROUP_W)), _full((1, GROUP_W)),
                  _full((GROUP_W, GROUP_W))],
        out_specs=[row(GROUP_W)] * N_PROJ + [row(LANES)],
        out_shape=outs,
        compiler_params=_params("arbitrary"),
        name="in_proj",
    )(x2d, g_mix, w_main, w_gate, gq_row, gk_row, gmat)


def _attn_body(q_ref, kp_ref, kc_ref, vp_ref, vc_ref, bias_ref, hmask_ref, o_ref, kwin, vwin,
               *, cq, nq, mask_first):
    tc = cq * nq
    i = pl.program_id(1)
    kwin[0:PAST_BAND, :] = kp_ref[...].astype(BF16)
    kwin[PAST_BAND:PAST_BAND + tc, :] = kc_ref[...].astype(BF16)
    vwin[0:PAST_BAND, :] = vp_ref[...].astype(BF16)
    vwin[PAST_BAND:PAST_BAND + tc, :] = vc_ref[...].astype(BF16)
    pad_rows = kwin.shape[0] - PAST_BAND - tc
    kwin[PAST_BAND + tc:, :] = jnp.zeros((pad_rows, GROUP_W), BF16)
    vwin[PAST_BAND + tc:, :] = jnp.zeros((pad_rows, GROUP_W), BF16)

    hm = hmask_ref[...]
    bias = bias_ref[...]
    kk = lax.broadcasted_iota(I32, (1, KEY_WIN), 1)

    def chunk(j, carry):
        r0 = pl.multiple_of(j * cq, cq)
        q = q_ref[pl.ds(r0, cq), :] * (HEAD_DIM ** -0.5)
        qm = (jnp.concatenate([q] * N_HEADS, axis=0) * hm).astype(BF16)
        kw = kwin[pl.ds(r0, KEY_WIN), :]
        s = lax.dot_general(qm, kw, (((1,), (1,)), ((), ())), preferred_element_type=F32) + bias
        if mask_first:
            first_valid = jnp.where(i == 0, PAST_BAND - r0, 0)
            s = jnp.where(kk >= first_valid, s, NEG_BIG)
        m = jnp.max(s, axis=-1, keepdims=True)
        p = jnp.exp(s - m)
        l = jnp.sum(p, axis=-1, keepdims=True)
        vw = vwin[pl.ds(r0, KEY_WIN), :]
        o_all = jnp.dot(p.astype(BF16), vw, preferred_element_type=F32) / l * hm
        o = o_all[0:cq]
        for h in range(1, N_HEADS):
            o = o + o_all[h * cq:(h + 1) * cq]
        o_ref[pl.ds(r0, cq), :] = o
        return carry

    lax.fori_loop(0, nq, chunk, 0)


def _attention(q, k_prev_src, k_cur_src, v_prev_src, v_cur_src, bias, hmask, *, batch, tiles, cq, nq,
               prev_index, mask_first):
    tc = cq * nq
    cur = pl.BlockSpec((tc, GROUP_W), lambda b, i: (b * tiles + i, 0))
    prev = pl.BlockSpec((PAST_BAND, GROUP_W), prev_index)
    win_rows = (nq - 1) * cq + KEY_WIN
    return pl.pallas_call(
        functools.partial(_attn_body, cq=cq, nq=nq, mask_first=mask_first),
        grid=(batch, tiles),
        in_specs=[cur, prev, cur, prev, cur, _full((N_HEADS * cq, KEY_WIN)), _full((N_HEADS * cq, GROUP_W))],
        out_specs=cur,
        out_shape=jax.ShapeDtypeStruct(q.shape, F32),
        scratch_shapes=[pltpu.VMEM((win_rows, GROUP_W), BF16), pltpu.VMEM((win_rows, GROUP_W), BF16)],
        compiler_params=_params("arbitrary", "arbitrary"),
        name="band_attention",
    )(q, k_prev_src, k_cur_src, v_prev_src, v_cur_src, bias, hmask)


def _rel_table(rel_bias_l, cq):
    nk = PAST_BAND + cq
    dist = PAST_BAND + (cq - 1) - np.arange(nk + cq - 1)
    rev = rel_bias_l[:, np.clip(dist, -REL_CLIP, REL_CLIP) + REL_CLIP].astype(F32)
    tab = jnp.stack([rev[:, cq - 1 - q:cq - 1 - q + nk] for q in range(cq)], axis=1)
    tab = jnp.pad(tab, ((0, 0), (0, 0), (0, KEY_WIN - nk)), constant_values=NEG_BIG)
    return tab.reshape(N_HEADS * cq, KEY_WIN)


def _head_row_mask(cq):
    h_row = np.repeat(np.arange(N_HEADS), cq)
    h_col = np.arange(GROUP_W) // HEAD_DIM
    return jnp.asarray((h_row[:, None] == h_col[None, :]).astype(np.float32))


def _log_sigmoid(x):
    return jnp.minimum(x, 0.0) - jnp.log(1.0 + jnp.exp(-jnp.abs(x)))


def _mlstm_body(q_ref, k_ref, v_ref, o_ref, g_ref, c0_ref, n0_ref, m0_ref,
                expand_ref, gbias_ref, bd_ref, bdb_ref, gmat_ref, ltri_ref, eye_ref, causal_ref, gml_ref,
                h_ref, ct_ref, n_ref, m_ref, ct_s, n_s, m_s, *, valid, nb):
    c = pl.program_id(1)
    last = c == pl.num_programs(1) - 1
    bd = bd_ref[...]
    bdb = bdb_ref[...]
    gmat = gmat_ref[...]
    eye = eye_ref[...] > 0.5
    causal = causal_ref[...] > 0.5

    @pl.when(c == 0)
    def _():
        for b in range(nb):
            ct_s[b] = jnp.concatenate([c0_ref[b]] * N_HEADS, axis=0) * bd
            n_s[b] = n0_ref[b]
            m_s[b] = m0_ref[b]

    for b in range(nb):
        gp = jnp.dot(g_ref[b], expand_ref[...], preferred_element_type=F32, precision=HIGHEST) + gbias_ref[...]
        log_i = gp[:, :GROUP_W]
        log_f = _log_sigmoid(gp[:, GROUP_W:])
        if valid < CHUNK:
            live = lax.broadcasted_iota(I32, (CHUNK, GROUP_W), 0) < valid
            log_i = jnp.where(live, log_i, -jnp.inf)
            log_f = jnp.where(live, log_f, 0.0)
        cum_f = jnp.dot(ltri_ref[...], log_f, preferred_element_type=F32, precision=HIGHEST)

        b_row = jnp.sum(jnp.where(eye, log_i - cum_f, 0.0), axis=0, keepdims=True)
        m_prev = m_s[b]
        log_inter = cum_f + m_prev
        log_d = jnp.where(causal, cum_f + b_row, -jnp.inf)
        max_d = jnp.concatenate(
            [jnp.broadcast_to(jnp.max(log_d[:, h * HEAD_DIM:(h + 1) * HEAD_DIM], axis=-1, keepdims=True),
                              (CHUNK, HEAD_DIM)) for h in range(N_HEADS)], axis=1)
        m_t = jnp.maximum(log_inter, max_d)
        w_intra = jnp.exp(log_d - m_t)
        w_inter = jnp.exp(log_inter - m_t)

        q = q_ref[b] * (HEAD_DIM ** -0.5)
        k = k_ref[b]
        v = v_ref[b]
        qb = q.astype(BF16)
        vb = v.astype(BF16)
        kbd = jnp.concatenate([k.astype(BF16)] * N_HEADS, axis=0) * bdb
        vbd = jnp.concatenate([vb] * N_HEADS, axis=0) * bdb
        s = lax.dot_general(qb, kbd, (((1,), (1,)), ((), ())), preferred_element_type=F32) * w_intra
        ct = ct_s[b]
        n_prev = n_s[b]
        num = (jnp.dot(s.astype(BF16), vbd, preferred_element_type=F32)
               + w_inter * lax.dot_general(qb, ct.astype(BF16), (((1,), (1,)), ((), ())),
                                           preferred_element_type=F32))
        den_terms = (s + w_inter * q * n_prev) * float(HEAD_DIM)
        den_hi = den_terms.astype(BF16)
        den_lo = (den_terms - den_hi.astype(F32)).astype(BF16)
        den = (jnp.dot(den_hi, gmat, preferred_element_type=F32)
               + jnp.dot(den_lo, gmat, preferred_element_type=F32))
        hb = num / jnp.maximum(jnp.abs(den), jnp.exp(-m_t))

        m_new = m_t[CHUNK - 1:CHUNK, :]
        cum_last = cum_f[CHUNK - 1:CHUNK, :]
        w_state = jnp.exp(cum_last - cum_f + log_i - m_new)
        decay = jnp.exp(cum_last + m_prev - m_new)
        kw = k * w_state
        upd = lax.dot_general(vb, kw.astype(BF16), (((0,), (0,)), ((), ())),
                              preferred_element_type=F32)
        ct_new = decay * ct + upd * bd
        n_new = decay * n_prev + jnp.sum(kw, axis=0, keepdims=True)
        ct_s[b] = ct_new
        n_s[b] = n_new
        m_s[b] = m_new

        msq = jnp.dot((hb * hb).astype(BF16), gmat, preferred_element_type=F32)
        h_ref[b] = jax.nn.sigmoid(o_ref[b]) * (hb * lax.rsqrt(msq + RMS_EPS) * gml_ref[...])

    @pl.when(last)
    def _():
        for b in range(nb):
            ct_new = ct_s[b]
            acc = ct_new[0:HEAD_DIM]
            for h in range(1, N_HEADS):
                acc = acc + ct_new[h * HEAD_DIM:(h + 1) * HEAD_DIM]
            ct_ref[b] = acc
            n_ref[b] = n_s[b]
            m_ref[b] = m_s[b]


def _mlstm(mq, mk, mv, mo, gates, c0t, n0, m0, consts, *, valid):
    batch, t, _ = mq.shape
    chunks = t // CHUNK
    nb = MLSTM_STREAMS
    row = lambda w: pl.BlockSpec((nb, CHUNK, w), lambda g, c: (g, c, 0))
    per_b = lambda r: pl.BlockSpec((nb, r, GROUP_W), lambda g, c: (g, 0, 0))
    expand, gbias, bd, bdb, gmat, ltri, eye, causal, gml = consts
    return pl.pallas_call(
        functools.partial(_mlstm_body, valid=valid, nb=nb),
        grid=(batch // nb, chunks),
        in_specs=[row(GROUP_W)] * 4 + [row(LANES), per_b(HEAD_DIM), per_b(1), per_b(1),
                  _full((LANES, 2 * GROUP_W)), _full((1, 2 * GROUP_W)), _full((GROUP_W, GROUP_W)),
                  _full((GROUP_W, GROUP_W)), _full((GROUP_W, GROUP_W)), _full((CHUNK, CHUNK)),
                  _full((CHUNK, GROUP_W)), _full((CHUNK, GROUP_W)), _full((1, GROUP_W))],
        out_specs=[row(GROUP_W), per_b(HEAD_DIM), per_b(1), per_b(1)],
        out_shape=[jax.ShapeDtypeStruct(mq.shape, F32),
                   jax.ShapeDtypeStruct((batch, HEAD_DIM, GROUP_W), F32),
                   jax.ShapeDtypeStruct((batch, 1, GROUP_W), F32),
                   jax.ShapeDtypeStruct((batch, 1, GROUP_W), F32)],
        scratch_shapes=[pltpu.VMEM((nb, GROUP_W, GROUP_W), F32), pltpu.VMEM((nb, 1, GROUP_W), F32),
                        pltpu.VMEM((nb, 1, GROUP_W), F32)],
        compiler_params=_params("arbitrary", "arbitrary"),
        name="mlstm",
    )(mq, mk, mv, mo, gates, c0t, n0, m0, expand, gbias, bd, bdb, gmat, ltri, eye, causal, gml)


def _mlstm_consts(b_igate_l, b_fgate_l, g_mlstm_l):
    expand = np.zeros((LANES, 2 * GROUP_W), np.float32)
    for h in range(N_HEADS):
        expand[h, h * HEAD_DIM:(h + 1) * HEAD_DIM] = 1.0
        expand[N_HEADS + h, GROUP_W + h * HEAD_DIM:GROUP_W + (h + 1) * HEAD_DIM] = 1.0
    gbias = jnp.concatenate([jnp.repeat(b_igate_l.astype(F32), HEAD_DIM),
                             jnp.repeat(b_fgate_l.astype(F32), HEAD_DIM)])[None, :]
    bd = _head_block_diag()
    ltri = np.tril(np.ones((CHUNK, CHUNK), np.float32))
    s_of_lane = np.arange(GROUP_W) % HEAD_DIM
    t = np.arange(CHUNK)
    eye = (t[:, None] == s_of_lane[None, :]).astype(np.float32)
    causal = (s_of_lane[None, :] <= t[:, None]).astype(np.float32)
    return (jnp.asarray(expand), gbias, jnp.asarray(bd), jnp.asarray(bd, BF16), jnp.asarray(bd / HEAD_DIM, BF16),
            jnp.asarray(ltri), jnp.asarray(eye), jnp.asarray(causal), g_mlstm_l.astype(F32).reshape(1, GROUP_W))


def _out_proj_body(xp_ref, xs_ref, ap_ref, as_ref, hp_ref, hs_ref, wa_ref, wm_ref, gffn_ref, wr_ref, br_ref,
                   y_ref, xf_ref, logit_ref, *, prompt_tiles):
    is_prompt = pl.program_id(0) < prompt_tiles
    x = jnp.where(is_prompt, xp_ref[...], xs_ref[...])
    att = jnp.where(is_prompt, ap_ref[...], as_ref[...])
    hm = jnp.where(is_prompt, hp_ref[...], hs_ref[...])
    y = (x + jnp.dot(att.astype(BF16), wa_ref[...], preferred_element_type=F32)
         + jnp.dot(hm.astype(BF16), wm_ref[...], preferred_element_type=F32))
    y_ref[...] = y
    xf = y * lax.rsqrt(jnp.mean(y * y, axis=-1, keepdims=True) + RMS_EPS) * gffn_ref[...]
    xf_ref[...] = xf
    logit_ref[...] = jnp.dot(xf, wr_ref[...], preferred_element_type=F32, precision=HIGHEST) + br_ref[...]


def _out_proj(xp, xs, att_p, att_s, hm_p, hm_s, wa, wm, g_ffn, w_router, b_router):
    tm = ROW_TILE
    pt, st = xp.shape[0] // tm, xs.shape[0] // tm
    n = xp.shape[0] + xs.shape[0]
    p_row = lambda w: pl.BlockSpec((tm, w), lambda i: (jnp.minimum(i, pt - 1), 0))
    s_row = lambda w: pl.BlockSpec((tm, w), lambda i: (jnp.maximum(i - pt, 0), 0))
    row = lambda w: pl.BlockSpec((tm, w), lambda i: (i, 0))
    return pl.pallas_call(
        functools.partial(_out_proj_body, prompt_tiles=pt),
        grid=(pt + st,),
        in_specs=[p_row(D_MODEL), s_row(D_MODEL), p_row(GROUP_W), s_row(GROUP_W), p_row(GROUP_W), s_row(GROUP_W),
                  _full((GROUP_W, D_MODEL)), _full((GROUP_W, D_MODEL)),
                  _full((1, D_MODEL)), _full((D_MODEL, LANES)), _full((1, LANES))],
        out_specs=[row(D_MODEL), row(D_MODEL), row(LANES)],
        out_shape=[jax.ShapeDtypeStruct((n, D_MODEL), F32), jax.ShapeDtypeStruct((n, D_MODEL), F32),
                   jax.ShapeDtypeStruct((n, LANES), F32)],
        compiler_params=_params("arbitrary"),
        name="out_proj_router",
    )(xp, xs, att_p, att_s, hm_p, hm_s, wa, wm, g_ffn, w_router, b_router)


def _route_body(logit_ref, lstrict_ref, ustrict_ref, gate_ref, pos_ref, seg_ref):
    tt = logit_ref.shape[0]
    lane = lax.broadcasted_iota(I32, (tt, LANES), 1)
    work = jnp.where(lane < N_EXPERTS, logit_ref[...], -jnp.inf)
    vals, idxs = [], []
    for _ in range(TOP_K):
        m = jnp.max(work, axis=-1, keepdims=True)
        idx = jnp.min(jnp.where(work == m, lane, LANES), axis=-1, keepdims=True)
        vals.append(m)
        idxs.append(idx)
        work = jnp.where(lane == idx, -jnp.inf, work)
    exps = [jnp.exp(v - vals[0]) for v in vals]
    total = exps[0] + exps[1] + exps[2] + exps[3]

    chosen = jnp.zeros((tt, LANES), F32)
    for idx in idxs:
        chosen = chosen + (lane == idx).astype(F32)
    before = jnp.dot(lstrict_ref[...], chosen.astype(BF16), preferred_element_type=F32)
    count = jnp.sum(chosen, axis=0, keepdims=True)
    groups = jnp.floor((count + (SUBLANES - 1)) * (1.0 / SUBLANES))
    groups8 = jnp.broadcast_to(groups, (SUBLANES, LANES)).astype(BF16)
    start = jnp.dot(groups8, ustrict_ref[...], preferred_element_type=F32) * float(SUBLANES)
    local = before + start[0:1, :]

    gate_out = jnp.zeros((tt, LANES), F32)
    pos_out = jnp.zeros((tt, LANES), F32)
    for k in range(TOP_K):
        pos = jnp.sum(jnp.where(lane == idxs[k], local, 0.0), axis=-1, keepdims=True)
        gate_out = jnp.where(lane == k, exps[k] / total, gate_out)
        pos_out = jnp.where(lane == k, pos, pos_out)
    gate_ref[...] = gate_out
    pos_ref[...] = pos_out
    row = lax.broadcasted_iota(I32, (SUBLANES, LANES), 0)
    seg = jnp.where(row == 0, groups * float(SUBLANES), jnp.where(row == 1, start, 0.0))
    seg_ref[...] = seg.astype(I32)


def _route(logits):
    n = logits.shape[0]
    tt = ROW_TILE
    lstrict = jnp.asarray(np.tril(np.ones((tt, tt), np.float32), -1), BF16)
    ustrict = jnp.asarray(np.triu(np.ones((LANES, LANES), np.float32), 1), BF16)
    row = lambda: pl.BlockSpec((tt, LANES), lambda i: (i, 0))
    return pl.pallas_call(
        _route_body,
        grid=(n // tt,),
        in_specs=[row(), _full((tt, tt)), _full((LANES, LANES))],
        out_specs=[row(), row(), pl.BlockSpec((SUBLANES, LANES), lambda i: (i, 0))],
        out_shape=[jax.ShapeDtypeStruct((n, LANES), F32), jax.ShapeDtypeStruct((n, LANES), F32),
                   jax.ShapeDtypeStruct((n // tt * SUBLANES, LANES), I32)],
        compiler_params=_params("arbitrary"),
        name="route_topk",
    )(logits, lstrict, ustrict)


def _segment_copies(i, len_ref, lst_ref, off_ref, make_copy, act):
    for e in range(N_EXPERTS):
        seg = i * N_EXPERTS + e
        length = len_ref[seg]
        local = lst_ref[seg]
        glob = off_ref[seg]
        for size in SEG_SIZES:
            take = length & size

            @pl.when(take != 0)
            def _(local=local, glob=glob, size=size):
                act(make_copy(pl.multiple_of(local, SUBLANES), pl.multiple_of(glob, SUBLANES), size))

            local = local + take
            glob = glob + take


def _local_onehot(pos_rows, base, rows, values=None):
    r = (lax.broadcasted_iota(I32, (rows, pos_rows[0].shape[1]), 0) + base).astype(F32)
    out = jnp.zeros(r.shape, F32)
    for k in range(TOP_K):
        out = jnp.where(r == pos_rows[k], 1.0 if values is None else values[k], out)
    return out


def _dispatch_body(len_ref, lst_ref, off_ref, fill_ref, nused_ref, xf_ref, pos_ref, xs_hbm,
                   xloc, zbuf, sem, zsem, *, tm, n_tiles):
    i = pl.program_id(0)
    tt = xf_ref.shape[0]
    fill_rows = zbuf.shape[0]

    @pl.when(i == 0)
    def _():
        zbuf[...] = jnp.zeros_like(zbuf)

        def fill(e):
            start = pl.multiple_of(fill_ref[e], SUBLANES)
            return pltpu.make_async_copy(zbuf, xs_hbm.at[pl.ds(start, fill_rows)], zsem)

        for e in range(N_EXPERTS):
            fill(e).start()
        for e in range(N_EXPERTS):
            fill(e).wait()

        def tail(j, carry):
            cp = pltpu.make_async_copy(zbuf.at[pl.ds(0, tm)], xs_hbm.at[pl.ds(pl.multiple_of(j * tm, tm), tm)], zsem)
            cp.start()
            cp.wait()
            return carry

        lax.fori_loop(nused_ref[0], n_tiles, tail, 0)

    pos_t = jnp.transpose(pos_ref[...])
    pos_rows = [pos_t[k:k + 1, :] for k in range(TOP_K)]
    xb = xf_ref[...].astype(BF16)

    slot = lax.rem(i, 2)

    def sort_rows(c, carry):
        r0 = pl.multiple_of(c * SORT_CHUNK, SORT_CHUNK)
        sel = _local_onehot(pos_rows, r0, SORT_CHUNK).astype(BF16)
        xloc[slot, pl.ds(r0, SORT_CHUNK), :] = jnp.dot(sel, xb, preferred_element_type=F32)
        return carry

    lax.fori_loop(0, LOCAL_ROWS // SORT_CHUNK, sort_rows, 0)

    def copies(step, which):
        def make_copy(local, glob, size):
            return pltpu.make_async_copy(xloc.at[which, pl.ds(local, size)], xs_hbm.at[pl.ds(glob, size)],
                                         sem.at[which])
        return functools.partial(_segment_copies, step, len_ref, lst_ref, off_ref, make_copy)

    copies(i, slot)(lambda cp: cp.start())

    @pl.when(i > 0)
    def _():
        copies(i - 1, 1 - slot)(lambda cp: cp.wait())

    @pl.when(i == pl.num_programs(0) - 1)
    def _():
        copies(i, slot)(lambda cp: cp.wait())


def _dispatch(seg_len, seg_local, seg_off, fill_start, n_used, xf, pos, n_tiles):
    n = xf.shape[0]
    tt, tm = ROW_TILE, EXPERT_TILE
    fill_rows = tm + SUBLANES
    grid_spec = pltpu.PrefetchScalarGridSpec(
        num_scalar_prefetch=5,
        grid=(n // tt,),
        in_specs=[pl.BlockSpec((tt, D_MODEL), lambda i, *_: (i, 0)),
                  pl.BlockSpec((tt, LANES), lambda i, *_: (i, 0))],
        out_specs=pl.BlockSpec(memory_space=pl.ANY),
        scratch_shapes=[pltpu.VMEM((2, LOCAL_ROWS, D_MODEL), F32), pltpu.VMEM((fill_rows, D_MODEL), F32),
                        pltpu.SemaphoreType.DMA((2,)), pltpu.SemaphoreType.DMA(())],
    )
    return pl.pallas_call(
        functools.partial(_dispatch_body, tm=tm, n_tiles=n_tiles + 2),
        grid_spec=grid_spec,
        out_shape=jax.ShapeDtypeStruct(((n_tiles + 2) * tm, D_MODEL), F32),
        compiler_params=_params("arbitrary"),
        name="expert_dispatch",
    )(seg_len, seg_local, seg_off, fill_start, n_used, xf, pos)


def _expert_body(te_ref, nused_ref, x_ref, wup_ref, wdn_ref, perm_ref, bg_ref, bl_ref, bd_ref,
                 y_ref, wg_s, wl_s, wd_s):
    i = pl.program_id(0)
    n_used = nused_ref[0]

    @pl.when(i >= n_used)
    def _():
        y_ref[...] = jnp.zeros_like(y_ref)

    @pl.when((i == 0) | (te_ref[i] != te_ref[jnp.maximum(i - 1, 0)]))
    def _():
        perm = perm_ref[...]
        for c in range(2 * D_FF // 256):
            blk = wup_ref[0, :, c * 256:(c + 1) * 256].astype(BF16)
            sep = jnp.dot(blk, perm, preferred_element_type=F32).astype(BF16)
            wg_s[:, c * 128:(c + 1) * 128] = sep[:, :128]
            wl_s[:, c * 128:(c + 1) * 128] = sep[:, 128:]
        wd_s[...] = wdn_ref[0].astype(BF16)

    @pl.when(i < n_used)
    def _():
        x = x_ref[...].astype(BF16)
        glu = jnp.minimum(jnp.dot(x, wg_s[...], preferred_element_type=F32) + bg_ref[0], SWIGLU_LIMIT)
        lin = jnp.clip(jnp.dot(x, wl_s[...], preferred_element_type=F32) + bl_ref[0], -SWIGLU_LIMIT, SWIGLU_LIMIT)
        act = glu * jax.nn.sigmoid(SWIGLU_ALPHA * glu) * (lin + 1.0)
        y_ref[...] = jnp.dot(act.astype(BF16), wd_s[...], preferred_element_type=F32) + bd_ref[0]


def _deinterleave_perm():
    p = np.zeros((256, 256), np.float32)
    j = np.arange(128)
    p[2 * j, j] = 1.0
    p[2 * j + 1, 128 + j] = 1.0
    return jnp.asarray(p, BF16)


def _experts(tile_expert, n_used, x_sorted, w_up, w_down, b_glu, b_lin, b_down):
    tm = EXPERT_TILE
    n_tiles = tile_expert.shape[0]
    wspec = lambda k, n: pl.BlockSpec((1, k, n), lambda i, te, nu: (te[i], 0, 0))
    grid_spec = pltpu.PrefetchScalarGridSpec(
        num_scalar_prefetch=2,
        grid=(n_tiles,),
        in_specs=[pl.BlockSpec((tm, D_MODEL), lambda i, te, nu: (jnp.minimum(i, nu[0] - 1), 0)),
                  wspec(D_MODEL, 2 * D_FF), wspec(D_FF, D_MODEL),
                  pl.BlockSpec((256, 256), lambda i, te, nu: (0, 0)),
                  wspec(1, D_FF), wspec(1, D_FF), wspec(1, D_MODEL)],
        out_specs=pl.BlockSpec((tm, D_MODEL), lambda i, te, nu: (i, 0)),
        scratch_shapes=[pltpu.VMEM((D_MODEL, D_FF), BF16), pltpu.VMEM((D_MODEL, D_FF), BF16),
                        pltpu.VMEM((D_FF, D_MODEL), BF16)],
    )
    return pl.pallas_call(
        _expert_body,
        grid_spec=grid_spec,
        out_shape=jax.ShapeDtypeStruct((n_tiles * tm, D_MODEL), F32),
        compiler_params=_params("arbitrary"),
        name="expert_ffn",
    )(tile_expert, n_used, x_sorted, w_up, w_down, _deinterleave_perm(), b_glu, b_lin, b_down)


def _combine_body(len_ref, lst_ref, off_ref, y_ref, gate_ref, pos_ref, rows_hbm, outp_ref, outs_ref,
                  yloc, acc_s, wide_s, sem, *, prompt_tiles):
    i = pl.program_id(0)
    slot = lax.rem(i, 2)
    tt = y_ref.shape[0]

    def copies(step, which):
        def make_copy(local, glob, size):
            return pltpu.make_async_copy(rows_hbm.at[pl.ds(glob, size)], yloc.at[which, pl.ds(local, size)],
                                         sem.at[which])
        return functools.partial(_segment_copies, step, len_ref, lst_ref, off_ref, make_copy)

    @pl.when(i == 0)
    def _():
        yloc[...] = jnp.zeros_like(yloc)
        copies(i, slot)(lambda cp: cp.start())

    @pl.when(i + 1 < pl.num_programs(0))
    def _():
        copies(i + 1, 1 - slot)(lambda cp: cp.start())

    pos = pos_ref[...]
    gate = gate_ref[...]
    for k in range(TOP_K):
        wide_s[k] = jnp.broadcast_to(pos[:, k:k + 1], (tt, LANES))
        wide_s[TOP_K + k] = jnp.broadcast_to(gate[:, k:k + 1], (tt, LANES))
    acc_s[...] = y_ref[...]
    copies(i, slot)(lambda cp: cp.wait())

    lane = lax.broadcasted_iota(I32, (tt, LANES), 1).astype(F32)

    def weigh(c, carry):
        r0 = pl.multiple_of(c * SORT_CHUNK, SORT_CHUNK)
        halves = []
        for half in range(SORT_CHUNK // LANES):
            r = lane + (r0 + half * LANES).astype(F32)
            w = jnp.zeros((tt, LANES), F32)
            for k in range(TOP_K):
                w = jnp.where(r == wide_s[k], wide_s[TOP_K + k], w)
            halves.append(w.astype(BF16))
        w = jnp.concatenate(halves, axis=1)
        acc_s[...] += jnp.dot(w, yloc[slot, pl.ds(r0, SORT_CHUNK), :].astype(BF16), preferred_element_type=F32)
        return carry

    lax.fori_loop(0, LOCAL_ROWS // SORT_CHUNK, weigh, 0)

    @pl.when(i < prompt_tiles)
    def _():
        outp_ref[...] = acc_s[...]

    @pl.when(i >= prompt_tiles)
    def _():
        outs_ref[...] = acc_s[...]


def _combine(seg_len, seg_local, seg_off, y, gates, pos, y_rows, n_prompt):
    n = y.shape[0]
    tt = ROW_TILE
    nt, pt = n // tt, n_prompt // tt
    grid_spec = pltpu.PrefetchScalarGridSpec(
        num_scalar_prefetch=3,
        grid=(nt,),
        in_specs=[pl.BlockSpec((tt, D_MODEL), lambda i, *_: (i, 0)),
                  pl.BlockSpec((tt, LANES), lambda i, *_: (i, 0)),
                  pl.BlockSpec((tt, LANES), lambda i, *_: (i, 0)),
                  pl.BlockSpec(memory_space=pl.ANY)],
        out_specs=[pl.BlockSpec((tt, D_MODEL), lambda i, *_: (jnp.minimum(i, pt - 1), 0)),
                   pl.BlockSpec((tt, D_MODEL), lambda i, *_: (jnp.maximum(i - pt, 0), 0))],
        scratch_shapes=[pltpu.VMEM((2, LOCAL_ROWS, D_MODEL), F32), pltpu.VMEM((tt, D_MODEL), F32),
                        pltpu.VMEM((2 * TOP_K, tt, LANES), F32), pltpu.SemaphoreType.DMA((2,))],
    )
    return pl.pallas_call(
        functools.partial(_combine_body, prompt_tiles=pt),
        grid_spec=grid_spec,
        out_shape=[jax.ShapeDtypeStruct((n_prompt, D_MODEL), F32),
                   jax.ShapeDtypeStruct((n - n_prompt, D_MODEL), F32)],
        compiler_params=_params("arbitrary"),
        name="expert_combine",
    )(seg_len, seg_local, seg_off, y, gates, pos, y_rows)


def _moe(y, xf, logits, ffn_w, n_prompt):
    n = y.shape[0]
    tm, tt = EXPERT_TILE, ROW_TILE
    n_tt = n // tt
    gates, pos, seg = _route(logits)
    seg = seg.reshape(n_tt, SUBLANES, LANES)
    seg_len, seg_local = seg[:, 0, :N_EXPERTS], seg[:, 1, :N_EXPERTS]
    rows = jnp.sum(seg_len, axis=0)
    padded = (rows + tm - 1) // tm * tm
    pad_end = jnp.cumsum(padded)
    pad_start = pad_end - padded
    seg_off = pad_start[None, :] + jnp.cumsum(seg_len, axis=0) - seg_len
    n_tiles = -(-(n * TOP_K + n_tt * N_EXPERTS * (SUBLANES - 1) + N_EXPERTS * (tm - 1)) // tm)
    tile_expert = jnp.minimum(jnp.sum(pad_end[None, :] <= (jnp.arange(n_tiles) * tm)[:, None], axis=1),
                              N_EXPERTS - 1).astype(I32)
    n_used = (pad_end[-1:] // tm).astype(I32)
    fill_start = (pad_start + rows).astype(I32)
    flat = lambda a: a.astype(I32).reshape(n_tt * N_EXPERTS)

    x_sorted = _dispatch(flat(seg_len), flat(seg_local), flat(seg_off), fill_start, n_used, xf, pos, n_tiles)
    y_rows = _experts(tile_expert, n_used, x_sorted, *ffn_w)
    return _combine(flat(seg_len), flat(seg_local), flat(seg_off), y, gates, pos, y_rows, n_prompt)


def _mixer(x, lw, cache, state):
    b, t, _ = x.shape
    n = b * t
    q, k, v, mq, mk, mv, mo, gates = _in_proj(x.reshape(n, D_MODEL), lw["g_mix"], lw["w_main"], lw["w_gate"],
                                              lw["gq"], lw["gk"], lw["gmat"])
    heads = lambda a, rows: a.reshape(b, rows, N_HEADS, HEAD_DIM)
    if cache is None:
        tiles = t // PAST_BAND
        att = _attention(q, k, k, v, v, lw["bias_prompt"], lw["hmask_prompt"], batch=b, tiles=tiles, cq=CHUNK,
                         nq=PAST_BAND // CHUNK,
                         prev_index=lambda bi, i: (bi * tiles + jnp.maximum(i - 1, 0), 0), mask_first=True)
        keep = min(PAST_BAND, t)
        k_new = heads(k.reshape(b, t, GROUP_W)[:, t - keep:], keep)
        v_new = heads(v.reshape(b, t, GROUP_W)[:, t - keep:], keep)
    else:
        ck, cv = cache
        att = _attention(q, ck.reshape(b * PAST_BAND, GROUP_W), k, cv.reshape(b * PAST_BAND, GROUP_W), v,
                         lw["bias_sample"], lw["hmask_sample"], batch=b, tiles=1, cq=t, nq=1,
                         prev_index=lambda bi, i: (bi, 0), mask_first=False)
        k_new, v_new = heads(k, t), heads(v, t)

    tp = -(-t // CHUNK) * CHUNK
    valid = t if t < CHUNK else CHUNK

    def streams(a):
        a = a.reshape(b, t, -1)
        return a if tp == t else jnp.pad(a, ((0, 0), (0, tp - t), (0, 0)))

    if state is None:
        c0t = jnp.zeros((b, HEAD_DIM, GROUP_W), F32)
        n0 = jnp.zeros((b, 1, GROUP_W), F32)
        m0 = jnp.zeros((b, 1, GROUP_W), F32)
    else:
        c_in, n_in, m_in = state
        c0t = c_in.astype(F32).transpose(0, 3, 1, 2).reshape(b, HEAD_DIM, GROUP_W)
        n0 = n_in.astype(F32).reshape(b, 1, GROUP_W)
        m0 = jnp.repeat(m_in.astype(F32), HEAD_DIM, axis=-1).reshape(b, 1, GROUP_W)
    hm, ct, n_out, m_out = _mlstm(streams(mq), streams(mk), streams(mv), streams(mo), streams(gates),
                                  c0t, n0, m0, lw["mlstm_consts"], valid=valid)
    hm = hm[:, :t].reshape(n, GROUP_W)
    c_new = ct.reshape(b, HEAD_DIM, N_HEADS, HEAD_DIM).transpose(0, 2, 3, 1)
    n_new = n_out.reshape(b, N_HEADS, HEAD_DIM)
    m_new = m_out.reshape(b, N_HEADS, HEAD_DIM)[:, :, 0]
    return att, hm, (k_new, v_new, c_new, n_new, m_new)


def kernel(x_prompt, x_sample, cache_k, cache_v, state_C, state_n, state_m, g_mix, w_in, g_q, g_k, rel_bias,
           b_igate, b_fgate, g_mlstm, w_out, g_ffn, w_router, b_router, w_up, b_up, w_down, b_down):
    depth = w_in.shape[0]
    yp, ys = x_prompt, x_sample
    bs, ts = x_sample.shape[0], x_sample.shape[1]
    n_prompt = x_prompt.shape[0] * x_prompt.shape[1]
    st_prompt, st_sample = [], []
    n_main = N_PROJ * GROUP_W
    gmat = jnp.asarray(_head_block_diag() / HEAD_DIM, BF16)
    for l in range(depth):
        lw = dict(
            g_mix=g_mix[l].astype(F32)[None, :],
            w_main=w_in[l][:, :n_main].astype(BF16),
            w_gate=jnp.pad(w_in[l][:, n_main:].astype(F32), ((0, 0), (0, LANES - 2 * N_HEADS))),
            gq=jnp.tile(g_q[l].astype(F32), N_HEADS)[None, :],
            gk=jnp.tile(g_k[l].astype(F32), N_HEADS)[None, :],
            gmat=gmat,
            bias_prompt=_rel_table(rel_bias[l], CHUNK),
            hmask_prompt=_head_row_mask(CHUNK),
            bias_sample=_rel_table(rel_bias[l], ts),
            hmask_sample=_head_row_mask(ts),
            mlstm_consts=_mlstm_consts(b_igate[l], b_fgate[l], g_mlstm[l]),
        )
        ffn_w = (w_up[l].astype(F32), w_down[l].astype(F32),
                 b_up[l][:, None, 0::2].astype(F32), b_up[l][:, None, 1::2].astype(F32),
                 b_down[l][:, None, :].astype(F32))
        att_p, hm_p, sp = _mixer(yp, lw, None, None)
        cache = (cache_k[l].reshape(bs, PAST_BAND, GROUP_W), cache_v[l].reshape(bs, PAST_BAND, GROUP_W))
        att_s, hm_s, ss = _mixer(ys, lw, cache, (state_C[l], state_n[l], state_m[l]))
        y, xf, logits = _out_proj(
            yp.reshape(-1, D_MODEL), ys.reshape(-1, D_MODEL), att_p, att_s, hm_p, hm_s,
            w_out[l][:GROUP_W].astype(BF16), w_out[l][GROUP_W:].astype(BF16), g_ffn[l].astype(F32)[None, :],
            jnp.pad(w_router[l].astype(F32), ((0, 0), (0, LANES - N_EXPERTS))),
            jnp.pad(b_router[l].astype(F32), (0, LANES - N_EXPERTS))[None, :])
        out_p, out_s = _moe(y, xf, logits, ffn_w, n_prompt)
        yp, ys = out_p.reshape(x_prompt.shape), out_s.reshape(x_sample.shape)
        st_prompt.append(sp)
        st_sample.append(ss)
    k_p, v_p, c_p, n_p, m_p = [jnp.stack(a) for a in zip(*st_prompt)]
    k_s, v_s, c_s, n_s, m_s = [jnp.stack(a) for a in zip(*st_sample)]
    return (yp, ys, k_p, v_p, c_p, n_p, m_p, k_s, v_s, c_s, n_s, m_s)
```

```python
import functools

import numpy as np
import jax
import jax.numpy as jnp
from jax import lax
from jax.experimental import pallas as pl
from jax.experimental.pallas import tpu as pltpu

F32 = jnp.float32
BF16 = jnp.bfloat16
I32 = jnp.int32
HIGHEST = lax.Precision.HIGHEST

D_MODEL = 1024
N_HEADS = 8
HEAD_DIM = 64
GROUP_W = N_HEADS * HEAD_DIM
N_PROJ = 7
LANES = 128
CHUNK = 64
PAST_BAND = 512
KEY_WIN = 640
REL_CLIP = 256
N_EXPERTS = 32
TOP_K = 4
D_FF = 1024
SWIGLU_ALPHA = 1.702
SWIGLU_LIMIT = 7.0
RMS_EPS = 1e-6
NEG_BIG = -1e30
ROW_TILE = 512
EXPERT_TILE = 512
SUBLANES = 8
SORT_CHUNK = 256
LOCAL_ROWS = -(-(ROW_TILE * TOP_K + N_EXPERTS * (SUBLANES - 1)) // SORT_CHUNK) * SORT_CHUNK
SEG_SIZES = (512, 256, 128, 64, 32, 16, 8)
MLSTM_STREAMS = 4
VMEM_LIMIT_BYTES = 56 * 1024 * 1024


def _params(*sem):
    return pltpu.CompilerParams(dimension_semantics=sem, vmem_limit_bytes=VMEM_LIMIT_BYTES)


def _head_block_diag():
    h = np.arange(GROUP_W) // HEAD_DIM
    return (h[:, None] == h[None, :]).astype(np.float32)


def _full(shape):
    return pl.BlockSpec(shape, lambda *_: (0,) * len(shape))


def _split3(x):
    hi = x.astype(BF16)
    r = x - hi.astype(F32)
    mid = r.astype(BF16)
    lo = (r - mid.astype(F32)).astype(BF16)
    return hi, mid, lo


def _dot_f32ish(x, w_stack):
    hi = x.astype(BF16)
    lo = (x - hi.astype(F32)).astype(BF16)
    return jnp.dot(jnp.concatenate([hi, lo, hi], axis=1), w_stack, preferred_element_type=F32)


def _stack_hi_lo(w):
    hi = w.astype(BF16)
    lo = (w - hi.astype(F32)).astype(BF16)
    return jnp.concatenate([hi, hi, lo], axis=0)


def _in_proj_body(x_ref, gmix_ref, w_ref, wg_ref, gq_ref, gk_ref, gmat_ref,
                  q_ref, k_ref, v_ref, mq_ref, mk_ref, mv_ref, mo_ref, gate_ref):
    x = x_ref[...]
    xn = x * lax.rsqrt(jnp.mean(x * x, axis=-1, keepdims=True) + RMS_EPS) * gmix_ref[...]
    xb = xn.astype(BF16)

    def proj(j):
        return jnp.dot(xb, w_ref[:, j * GROUP_W:(j + 1) * GROUP_W], preferred_element_type=F32)

    def head_norm(a, g_ref):
        msq = jnp.dot((a * a).astype(BF16), gmat_ref[...], preferred_element_type=F32)
        return a * lax.rsqrt(msq + RMS_EPS) * g_ref[...]

    q_ref[...] = head_norm(proj(0), gq_ref)
    k_ref[...] = head_norm(proj(1), gk_ref)
    v_ref[...] = proj(2)
    mq_ref[...] = proj(3)
    mk_ref[...] = proj(4)
    mv_ref[...] = proj(5)
    mo_ref[...] = proj(6)
    gate_ref[...] = _dot_f32ish(xn, wg_ref[...])


def _in_proj(x2d, g_mix, w_main, w_gate, gq_row, gk_row, gmat):
    n = x2d.shape[0]
    tm = ROW_TILE
    row = lambda w: pl.BlockSpec((tm, w), lambda i: (i, 0))
    outs = [jax.ShapeDtypeStruct((n, GROUP_W), F32)] * N_PROJ + [jax.ShapeDtypeStruct((n, LANES), F32)]
    return pl.pallas_call(
        _in_proj_body,
        grid=(n // tm,),
        in_specs=[row(D_MODEL), _full((1, D_MODEL)), _full((D_MODEL, N_PROJ * GROUP_W)),
                  _full((3 * D_MODEL, LANES)), _full((1, GROUP_W)), _full((1, GROUP_W)),
                  _full((GROUP_W, GROUP_W))],
        out_specs=[row(GROUP_W)] * N_PROJ + [row(LANES)],
        out_shape=outs,
        compiler_params=_params("arbitrary"),
        name="in_proj",
    )(x2d, g_mix, w_main, w_gate, gq_row, gk_row, gmat)


def _attn_body(q_ref, kp_ref, kc_ref, vp_ref, vc_ref, bias_ref, hmask_ref, o_ref, kwin, vwin,
               *, cq, nq, mask_first):
    tc = cq * nq
    i = pl.program_id(1)
    kwin[0:PAST_BAND, :] = kp_ref[...].astype(BF16)
    kwin[PAST_BAND:PAST_BAND + tc, :] = kc_ref[...].astype(BF16)
    vwin[0:PAST_BAND, :] = vp_ref[...].astype(BF16)
    vwin[PAST_BAND:PAST_BAND + tc, :] = vc_ref[...].astype(BF16)
    pad_rows = kwin.shape[0] - PAST_BAND - tc
    kwin[PAST_BAND + tc:, :] = jnp.zeros((pad_rows, GROUP_W), BF16)
    vwin[PAST_BAND + tc:, :] = jnp.zeros((pad_rows, GROUP_W), BF16)

    hm = hmask_ref[...]
    bias = bias_ref[...]
    kk = lax.broadcasted_iota(I32, (1, KEY_WIN), 1)

    def chunk(j, carry):
        r0 = pl.multiple_of(j * cq, cq)
        q = q_ref[pl.ds(r0, cq), :] * (HEAD_DIM ** -0.5)
        qm = (jnp.concatenate([q] * N_HEADS, axis=0) * hm).astype(BF16)
        kw = kwin[pl.ds(r0, KEY_WIN), :]
        s = lax.dot_general(qm, kw, (((1,), (1,)), ((), ())), preferred_element_type=F32) + bias
        if mask_first:
            first_valid = jnp.where(i == 0, PAST_BAND - r0, 0)
            s = jnp.where(kk >= first_valid, s, NEG_BIG)
        m = jnp.max(s, axis=-1, keepdims=True)
        p = jnp.exp(s - m)
        l = jnp.sum(p, axis=-1, keepdims=True)
        vw = vwin[pl.ds(r0, KEY_WIN), :]
        o_all = jnp.dot(p.astype(BF16), vw, preferred_element_type=F32) / l * hm
        o = o_all[0:cq]
        for h in range(1, N_HEADS):
            o = o + o_all[h * cq:(h + 1) * cq]
        o_ref[pl.ds(r0, cq), :] = o
        return carry

    lax.fori_loop(0, nq, chunk, 0, unroll=2 if nq % 2 == 0 else 1)


def _attention(q, k_prev_src, k_cur_src, v_prev_src, v_cur_src, bias, hmask, *, batch, tiles, cq, nq,
               prev_index, mask_first):
    tc = cq * nq
    cur = pl.BlockSpec((tc, GROUP_W), lambda b, i: (b * tiles + i, 0))
    prev = pl.BlockSpec((PAST_BAND, GROUP_W), prev_index)
    win_rows = (nq - 1) * cq + KEY_WIN
    return pl.pallas_call(
        functools.partial(_attn_body, cq=cq, nq=nq, mask_first=mask_first),
        grid=(batch, tiles),
        in_specs=[cur, prev, cur, prev, cur, _full((N_HEADS * cq, KEY_WIN)), _full((N_HEADS * cq, GROUP_W))],
        out_specs=cur,
        out_shape=jax.ShapeDtypeStruct(q.shape, F32),
        scratch_shapes=[pltpu.VMEM((win_rows, GROUP_W), BF16), pltpu.VMEM((win_rows, GROUP_W), BF16)],
        compiler_params=_params("arbitrary", "arbitrary"),
        name="band_attention",
    )(q, k_prev_src, k_cur_src, v_prev_src, v_cur_src, bias, hmask)


def _rel_table(rel_bias_l, cq):
    nk = PAST_BAND + cq
    dist = PAST_BAND + (cq - 1) - np.arange(nk + cq - 1)
    rev = rel_bias_l[:, np.clip(dist, -REL_CLIP, REL_CLIP) + REL_CLIP].astype(F32)
    tab = jnp.stack([rev[:, cq - 1 - q:cq - 1 - q + nk] for q in range(cq)], axis=1)
    tab = jnp.pad(tab, ((0, 0), (0, 0), (0, KEY_WIN - nk)), constant_values=NEG_BIG)
    return tab.reshape(N_HEADS * cq, KEY_WIN)


def _head_row_mask(cq):
    h_row = np.repeat(np.arange(N_HEADS), cq)
    h_col = np.arange(GROUP_W) // HEAD_DIM
    return jnp.asarray((h_row[:, None] == h_col[None, :]).astype(np.float32))


def _log_sigmoid(x):
    return jnp.minimum(x, 0.0) - jnp.log(1.0 + jnp.exp(-jnp.abs(x)))


def _mlstm_body(q_ref, k_ref, v_ref, o_ref, g_ref, c0_ref, n0_ref, m0_ref,
                expand_ref, gbias_ref, bd_ref, bdb_ref, gmat_ref, ltri_ref, eye_ref, causal_ref, gml_ref,
                h_ref, ct_ref, n_ref, m_ref, ct_s, n_s, m_s, *, valid, nb):
    c = pl.program_id(1)
    last = c == pl.num_programs(1) - 1
    bd = bd_ref[...]
    bdb = bdb_ref[...]
    gmat = gmat_ref[...]
    eye = eye_ref[...] > 0.5
    causal = causal_ref[...] > 0.5

    @pl.when(c == 0)
    def _():
        for b in range(nb):
            ct_s[b] = jnp.concatenate([c0_ref[b]] * N_HEADS, axis=0) * bd
            n_s[b] = n0_ref[b]
            m_s[b] = m0_ref[b]

    for b in range(nb):
        gp = jnp.dot(jnp.concatenate(_split3(g_ref[b]), axis=1), expand_ref[...],
                     preferred_element_type=F32) + gbias_ref[...]
        log_i = gp[:, :GROUP_W]
        log_f = _log_sigmoid(gp[:, GROUP_W:])
        if valid < CHUNK:
            live = lax.broadcasted_iota(I32, (CHUNK, GROUP_W), 0) < valid
            log_i = jnp.where(live, log_i, -jnp.inf)
            log_f = jnp.where(live, log_f, 0.0)
        cum_f = jnp.dot(ltri_ref[...], jnp.concatenate(_split3(log_f), axis=0), preferred_element_type=F32)

        b_row = jnp.sum(jnp.where(eye, log_i - cum_f, 0.0), axis=0, keepdims=True)
        m_prev = m_s[b]
        log_inter = cum_f + m_prev
        log_d = jnp.where(causal, cum_f + b_row, -jnp.inf)
        max_d = jnp.concatenate(
            [jnp.broadcast_to(jnp.max(log_d[:, h * HEAD_DIM:(h + 1) * HEAD_DIM], axis=-1, keepdims=True),
                              (CHUNK, HEAD_DIM)) for h in range(N_HEADS)], axis=1)
        m_t = jnp.maximum(log_inter, max_d)
        w_intra = jnp.exp(log_d - m_t)
        w_inter = jnp.exp(log_inter - m_t)

        q = q_ref[b] * (HEAD_DIM ** -0.5)
        k = k_ref[b]
        v = v_ref[b]
        qb = q.astype(BF16)
        vb = v.astype(BF16)
        kbd = jnp.concatenate([k.astype(BF16)] * N_HEADS, axis=0) * bdb
        vbd = jnp.concatenate([vb] * N_HEADS, axis=0) * bdb
        s = lax.dot_general(qb, kbd, (((1,), (1,)), ((), ())), preferred_element_type=F32) * w_intra
        ct = ct_s[b]
        n_prev = n_s[b]
        num = (jnp.dot(s.astype(BF16), vbd, preferred_element_type=F32)
               + w_inter * lax.dot_general(qb, ct.astype(BF16), (((1,), (1,)), ((), ())),
                                           preferred_element_type=F32))
        den_terms = (s + w_inter * q * n_prev) * float(HEAD_DIM)
        den_hi = den_terms.astype(BF16)
        den_lo = (den_terms - den_hi.astype(F32)).astype(BF16)
        den = (jnp.dot(den_hi, gmat, preferred_element_type=F32)
               + jnp.dot(den_lo, gmat, preferred_element_type=F32))
        hb = num / jnp.maximum(jnp.abs(den), jnp.exp(-m_t))

        m_new = m_t[CHUNK - 1:CHUNK, :]
        cum_last = cum_f[CHUNK - 1:CHUNK, :]
        w_state = jnp.exp(cum_last - cum_f + log_i - m_new)
        decay = jnp.exp(cum_last + m_prev - m_new)
        kw = k * w_state
        upd = lax.dot_general(vb, kw.astype(BF16), (((0,), (0,)), ((), ())),
                              preferred_element_type=F32)
        ct_new = decay * ct + upd * bd
        n_new = decay * n_prev + jnp.sum(kw, axis=0, keepdims=True)
        ct_s[b] = ct_new
        n_s[b] = n_new
        m_s[b] = m_new

        msq = jnp.dot((hb * hb).astype(BF16), gmat, preferred_element_type=F32)
        h_ref[b] = jax.nn.sigmoid(o_ref[b]) * (hb * lax.rsqrt(msq + RMS_EPS) * gml_ref[...])

    @pl.when(last)
    def _():
        for b in range(nb):
            ct_new = ct_s[b]
            acc = ct_new[0:HEAD_DIM]
            for h in range(1, N_HEADS):
                acc = acc + ct_new[h * HEAD_DIM:(h + 1) * HEAD_DIM]
            ct_ref[b] = acc
            n_ref[b] = n_s[b]
            m_ref[b] = m_s[b]


def _mlstm(mq, mk, mv, mo, gates, c0t, n0, m0, consts, *, valid):
    batch, t, _ = mq.shape
    chunks = t // CHUNK
    nb = MLSTM_STREAMS
    row = lambda w: pl.BlockSpec((nb, CHUNK, w), lambda g, c: (g, c, 0))
    per_b = lambda r: pl.BlockSpec((nb, r, GROUP_W), lambda g, c: (g, 0, 0))
    expand, gbias, bd, bdb, gmat, ltri, eye, causal, gml = consts
    return pl.pallas_call(
        functools.partial(_mlstm_body, valid=valid, nb=nb),
        grid=(batch // nb, chunks),
        in_specs=[row(GROUP_W)] * 4 + [row(LANES), per_b(HEAD_DIM), per_b(1), per_b(1),
                  _full((3 * LANES, 2 * GROUP_W)), _full((1, 2 * GROUP_W)), _full((GROUP_W, GROUP_W)),
                  _full((GROUP_W, GROUP_W)), _full((GROUP_W, GROUP_W)), _full((CHUNK, 3 * CHUNK)),
                  _full((CHUNK, GROUP_W)), _full((CHUNK, GROUP_W)), _full((1, GROUP_W))],
        out_specs=[row(GROUP_W), per_b(HEAD_DIM), per_b(1), per_b(1)],
        out_shape=[jax.ShapeDtypeStruct(mq.shape, F32),
                   jax.ShapeDtypeStruct((batch, HEAD_DIM, GROUP_W), F32),
                   jax.ShapeDtypeStruct((batch, 1, GROUP_W), F32),
                   jax.ShapeDtypeStruct((batch, 1, GROUP_W), F32)],
        scratch_shapes=[pltpu.VMEM((nb, GROUP_W, GROUP_W), F32), pltpu.VMEM((nb, 1, GROUP_W), F32),
                        pltpu.VMEM((nb, 1, GROUP_W), F32)],
        compiler_params=_params("arbitrary", "arbitrary"),
        name="mlstm",
    )(mq, mk, mv, mo, gates, c0t, n0, m0, expand, gbias, bd, bdb, gmat, ltri, eye, causal, gml)


def _mlstm_consts(b_igate_l, b_fgate_l, g_mlstm_l):
    expand = np.zeros((LANES, 2 * GROUP_W), np.float32)
    for h in range(N_HEADS):
        expand[h, h * HEAD_DIM:(h + 1) * HEAD_DIM] = 1.0
        expand[N_HEADS + h, GROUP_W + h * HEAD_DIM:GROUP_W + (h + 1) * HEAD_DIM] = 1.0
    gbias = jnp.concatenate([jnp.repeat(b_igate_l.astype(F32), HEAD_DIM),
                             jnp.repeat(b_fgate_l.astype(F32), HEAD_DIM)])[None, :]
    bd = _head_block_diag()
    ltri = np.tril(np.ones((CHUNK, CHUNK), np.float32))
    s_of_lane = np.arange(GROUP_W) % HEAD_DIM
    t = np.arange(CHUNK)
    eye = (t[:, None] == s_of_lane[None, :]).astype(np.float32)
    causal = (s_of_lane[None, :] <= t[:, None]).astype(np.float32)
    return (jnp.asarray(np.concatenate([expand] * 3, axis=0), BF16), gbias, jnp.asarray(bd), jnp.asarray(bd, BF16),
            jnp.asarray(bd / HEAD_DIM, BF16), jnp.asarray(np.concatenate([ltri] * 3, axis=1), BF16), jnp.asarray(eye), jnp.asarray(causal), g_mlstm_l.astype(F32).reshape(1, GROUP_W))


def _out_proj_body(xp_ref, xs_ref, ap_ref, as_ref, hp_ref, hs_ref, wa_ref, wm_ref, gffn_ref, wr_ref, br_ref,
                   y_ref, xf_ref, logit_ref, *, prompt_tiles):
    is_prompt = pl.program_id(0) < prompt_tiles
    x = jnp.where(is_prompt, xp_ref[...], xs_ref[...])
    att = jnp.where(is_prompt, ap_ref[...], as_ref[...])
    hm = jnp.where(is_prompt, hp_ref[...], hs_ref[...])
    y = (x + jnp.dot(att.astype(BF16), wa_ref[...], preferred_element_type=F32)
         + jnp.dot(hm.astype(BF16), wm_ref[...], preferred_element_type=F32))
    y_ref[...] = y
    xf = y * lax.rsqrt(jnp.mean(y * y, axis=-1, keepdims=True) + RMS_EPS) * gffn_ref[...]
    xf_ref[...] = xf
    logit_ref[...] = _dot_f32ish(xf, wr_ref[...]) + br_ref[...]


def _out_proj(xp, xs, att_p, att_s, hm_p, hm_s, wa, wm, g_ffn, w_router, b_router):
    tm = ROW_TILE
    pt, st = xp.shape[0] // tm, xs.shape[0] // tm
    n = xp.shape[0] + xs.shape[0]
    p_row = lambda w: pl.BlockSpec((tm, w), lambda i: (jnp.minimum(i, pt - 1), 0))
    s_row = lambda w: pl.BlockSpec((tm, w), lambda i: (jnp.maximum(i - pt, 0), 0))
    row = lambda w: pl.BlockSpec((tm, w), lambda i: (i, 0))
    return pl.pallas_call(
        functools.partial(_out_proj_body, prompt_tiles=pt),
        grid=(pt + st,),
        in_specs=[p_row(D_MODEL), s_row(D_MODEL), p_row(GROUP_W), s_row(GROUP_W), p_row(GROUP_W), s_row(GROUP_W),
                  _full((GROUP_W, D_MODEL)), _full((GROUP_W, D_MODEL)),
                  _full((1, D_MODEL)), _full((3 * D_MODEL, LANES)), _full((1, LANES))],
        out_specs=[row(D_MODEL), row(D_MODEL), row(LANES)],
        out_shape=[jax.ShapeDtypeStruct((n, D_MODEL), F32), jax.ShapeDtypeStruct((n, D_MODEL), F32),
                   jax.ShapeDtypeStruct((n, LANES), F32)],
        compiler_params=_params("arbitrary"),
        name="out_proj_router",
    )(xp, xs, att_p, att_s, hm_p, hm_s, wa, wm, g_ffn, w_router, b_router)


def _route_body(logit_ref, lstrict_ref, ustrict_ref, gate_ref, pos_ref, seg_ref):
    tt = logit_ref.shape[0]
    lane = lax.broadcasted_iota(I32, (tt, LANES), 1)
    work = jnp.where(lane < N_EXPERTS, logit_ref[...], -jnp.inf)
    vals, idxs = [], []
    for _ in range(TOP_K):
        m = jnp.max(work, axis=-1, keepdims=True)
        idx = jnp.min(jnp.where(work == m, lane, LANES), axis=-1, keepdims=True)
        vals.append(m)
        idxs.append(idx)
        work = jnp.where(lane == idx, -jnp.inf, work)
    exps = [jnp.exp(v - vals[0]) for v in vals]
    total = exps[0] + exps[1] + exps[2] + exps[3]

    chosen = jnp.zeros((tt, LANES), F32)
    for idx in idxs:
        chosen = chosen + (lane == idx).astype(F32)
    before = jnp.dot(lstrict_ref[...], chosen.astype(BF16), preferred_element_type=F32)
    count = jnp.sum(chosen, axis=0, keepdims=True)
    groups = jnp.floor((count + (SUBLANES - 1)) * (1.0 / SUBLANES))
    groups8 = jnp.broadcast_to(groups, (SUBLANES, LANES)).astype(BF16)
    start = jnp.dot(groups8, ustrict_ref[...], preferred_element_type=F32) * float(SUBLANES)
    local = before + start[0:1, :]

    gate_out = jnp.zeros((tt, LANES), F32)
    pos_out = jnp.zeros((tt, LANES), F32)
    for k in range(TOP_K):
        pos = jnp.sum(jnp.where(lane == idxs[k], local, 0.0), axis=-1, keepdims=True)
        gate_out = jnp.where(lane == k, exps[k] / total, gate_out)
        pos_out = jnp.where(lane == k, pos, pos_out)
    gate_ref[...] = gate_out
    pos_ref[...] = pos_out
    row = lax.broadcasted_iota(I32, (SUBLANES, LANES), 0)
    seg = jnp.where(row == 0, groups * float(SUBLANES), jnp.where(row == 1, start, 0.0))
    seg_ref[...] = seg.astype(I32)


def _route(logits):
    n = logits.shape[0]
    tt = ROW_TILE
    lstrict = jnp.asarray(np.tril(np.ones((tt, tt), np.float32), -1), BF16)
    ustrict = jnp.asarray(np.triu(np.ones((LANES, LANES), np.float32), 1), BF16)
    row = lambda: pl.BlockSpec((tt, LANES), lambda i: (i, 0))
    return pl.pallas_call(
        _route_body,
        grid=(n // tt,),
        in_specs=[row(), _full((tt, tt)), _full((LANES, LANES))],
        out_specs=[row(), row(), pl.BlockSpec((SUBLANES, LANES), lambda i: (i, 0))],
        out_shape=[jax.ShapeDtypeStruct((n, LANES), F32), jax.ShapeDtypeStruct((n, LANES), F32),
                   jax.ShapeDtypeStruct((n // tt * SUBLANES, LANES), I32)],
        compiler_params=_params("arbitrary"),
        name="route_topk",
    )(logits, lstrict, ustrict)


def _segment_copies(i, len_ref, lst_ref, off_ref, make_copy, act):
    for e in range(N_EXPERTS):
        seg = i * N_EXPERTS + e
        length = len_ref[seg]
        local = lst_ref[seg]
        glob = off_ref[seg]
        for size in SEG_SIZES:
            take = length & size

            @pl.when(take != 0)
            def _(local=local, glob=glob, size=size):
                act(make_copy(pl.multiple_of(local, SUBLANES), pl.multiple_of(glob, SUBLANES), size))

            local = local + take
            glob = glob + take


def _local_onehot(pos_rows, base, rows, values=None):
    r = (lax.broadcasted_iota(I32, (rows, pos_rows[0].shape[1]), 0) + base).astype(F32)
    out = jnp.zeros(r.shape, F32)
    for k in range(TOP_K):
        out = jnp.where(r == pos_rows[k], 1.0 if values is None else values[k], out)
    return out


def _dispatch_body(len_ref, lst_ref, off_ref, fill_ref, nused_ref, xf_ref, pos_ref, xs_hbm,
                   xloc, zbuf, sem, zsem, *, tm, n_tiles):
    i = pl.program_id(0)
    tt = xf_ref.shape[0]
    fill_rows = zbuf.shape[0]

    @pl.when(i == 0)
    def _():
        zbuf[...] = jnp.zeros_like(zbuf)

        def fill(e):
            start = pl.multiple_of(fill_ref[e], SUBLANES)
            return pltpu.make_async_copy(zbuf, xs_hbm.at[pl.ds(start, fill_rows)], zsem)

        for e in range(N_EXPERTS):
            fill(e).start()
        for e in range(N_EXPERTS):
            fill(e).wait()

        def tail(j, carry):
            cp = pltpu.make_async_copy(zbuf.at[pl.ds(0, tm)], xs_hbm.at[pl.ds(pl.multiple_of(j * tm, tm), tm)], zsem)
            cp.start()
            cp.wait()
            return carry

        lax.fori_loop(nused_ref[0], n_tiles, tail, 0)

    pos_t = jnp.transpose(pos_ref[...])
    pos_rows = [pos_t[k:k + 1, :] for k in range(TOP_K)]
    xb = xf_ref[...].astype(BF16)

    slot = lax.rem(i, 2)

    def sort_rows(c, carry):
        r0 = pl.multiple_of(c * SORT_CHUNK, SORT_CHUNK)
        sel = _local_onehot(pos_rows, r0, SORT_CHUNK).astype(BF16)
        xloc[slot, pl.ds(r0, SORT_CHUNK), :] = jnp.dot(sel, xb, preferred_element_type=F32)
        return carry

    lax.fori_loop(0, LOCAL_ROWS // SORT_CHUNK, sort_rows, 0)

    def copies(step, which):
        def make_copy(local, glob, size):
            return pltpu.make_async_copy(xloc.at[which, pl.ds(local, size)], xs_hbm.at[pl.ds(glob, size)],
                                         sem.at[which])
        return functools.partial(_segment_copies, step, len_ref, lst_ref, off_ref, make_copy)

    copies(i, slot)(lambda cp: cp.start())

    @pl.when(i > 0)
    def _():
        copies(i - 1, 1 - slot)(lambda cp: cp.wait())

    @pl.when(i == pl.num_programs(0) - 1)
    def _():
        copies(i, slot)(lambda cp: cp.wait())


def _dispatch(seg_len, seg_local, seg_off, fill_start, n_used, xf, pos, n_tiles):
    n = xf.shape[0]
    tt, tm = ROW_TILE, EXPERT_TILE
    fill_rows = tm + SUBLANES
    grid_spec = pltpu.PrefetchScalarGridSpec(
        num_scalar_prefetch=5,
        grid=(n // tt,),
        in_specs=[pl.BlockSpec((tt, D_MODEL), lambda i, *_: (i, 0)),
                  pl.BlockSpec((tt, LANES), lambda i, *_: (i, 0))],
        out_specs=pl.BlockSpec(memory_space=pl.ANY),
        scratch_shapes=[pltpu.VMEM((2, LOCAL_ROWS, D_MODEL), F32), pltpu.VMEM((fill_rows, D_MODEL), F32),
                        pltpu.SemaphoreType.DMA((2,)), pltpu.SemaphoreType.DMA(())],
    )
    return pl.pallas_call(
        functools.partial(_dispatch_body, tm=tm, n_tiles=n_tiles + 2),
        grid_spec=grid_spec,
        out_shape=jax.ShapeDtypeStruct(((n_tiles + 2) * tm, D_MODEL), F32),
        compiler_params=_params("arbitrary"),
        name="expert_dispatch",
    )(seg_len, seg_local, seg_off, fill_start, n_used, xf, pos)


def _expert_body(te_ref, nused_ref, x_ref, wup_ref, wdn_ref, perm_ref, bg_ref, bl_ref, bd_ref,
                 y_ref, wg_s, wl_s, wd_s):
    i = pl.program_id(0)
    n_used = nused_ref[0]

    @pl.when(i >= n_used)
    def _():
        y_ref[...] = jnp.zeros_like(y_ref)

    @pl.when((i == 0) | (te_ref[i] != te_ref[jnp.maximum(i - 1, 0)]))
    def _():
        perm = perm_ref[...]
        for c in range(2 * D_FF // 256):
            blk = wup_ref[0, :, c * 256:(c + 1) * 256].astype(BF16)
            sep = jnp.dot(blk, perm, preferred_element_type=F32).astype(BF16)
            wg_s[:, c * 128:(c + 1) * 128] = sep[:, :128]
            wl_s[:, c * 128:(c + 1) * 128] = sep[:, 128:]
        wd_s[...] = wdn_ref[0].astype(BF16)

    @pl.when(i < n_used)
    def _():
        x = x_ref[...].astype(BF16)
        glu = jnp.minimum(jnp.dot(x, wg_s[...], preferred_element_type=F32) + bg_ref[0], SWIGLU_LIMIT)
        lin = jnp.clip(jnp.dot(x, wl_s[...], preferred_element_type=F32) + bl_ref[0], -SWIGLU_LIMIT, SWIGLU_LIMIT)
        act = glu * jax.nn.sigmoid(SWIGLU_ALPHA * glu) * (lin + 1.0)
        y_ref[...] = jnp.dot(act.astype(BF16), wd_s[...], preferred_element_type=F32) + bd_ref[0]


def _deinterleave_perm():
    p = np.zeros((256, 256), np.float32)
    j = np.arange(128)
    p[2 * j, j] = 1.0
    p[2 * j + 1, 128 + j] = 1.0
    return jnp.asarray(p, BF16)


def _experts(tile_expert, n_used, x_sorted, w_up, w_down, b_glu, b_lin, b_down):
    tm = EXPERT_TILE
    n_tiles = tile_expert.shape[0]
    wspec = lambda k, n: pl.BlockSpec((1, k, n), lambda i, te, nu: (te[i], 0, 0))
    grid_spec = pltpu.PrefetchScalarGridSpec(
        num_scalar_prefetch=2,
        grid=(n_tiles,),
        in_specs=[pl.BlockSpec((tm, D_MODEL), lambda i, te, nu: (jnp.minimum(i, nu[0] - 1), 0)),
                  wspec(D_MODEL, 2 * D_FF), wspec(D_FF, D_MODEL),
                  pl.BlockSpec((256, 256), lambda i, te, nu: (0, 0)),
                  wspec(1, D_FF), wspec(1, D_FF), wspec(1, D_MODEL)],
        out_specs=pl.BlockSpec((tm, D_MODEL), lambda i, te, nu: (i, 0)),
        scratch_shapes=[pltpu.VMEM((D_MODEL, D_FF), BF16), pltpu.VMEM((D_MODEL, D_FF), BF16),
                        pltpu.VMEM((D_FF, D_MODEL), BF16)],
    )
    return pl.pallas_call(
        _expert_body,
        grid_spec=grid_spec,
        out_shape=jax.ShapeDtypeStruct((n_tiles * tm, D_MODEL), F32),
        compiler_params=_params("arbitrary"),
        name="expert_ffn",
    )(tile_expert, n_used, x_sorted, w_up, w_down, _deinterleave_perm(), b_glu, b_lin, b_down)


def _combine_body(len_ref, lst_ref, off_ref, y_ref, gate_ref, pos_ref, rows_hbm, outp_ref, outs_ref,
                  yloc, acc_s, wide_s, sem, *, prompt_tiles):
    i = pl.program_id(0)
    slot = lax.rem(i, 2)
    tt = y_ref.shape[0]

    def copies(step, which):
        def make_copy(local, glob, size):
            return pltpu.make_async_copy(rows_hbm.at[pl.ds(glob, size)], yloc.at[which, pl.ds(local, size)],
                                         sem.at[which])
        return functools.partial(_segment_copies, step, len_ref, lst_ref, off_ref, make_copy)

    @pl.when(i == 0)
    def _():
        yloc[...] = jnp.zeros_like(yloc)
        copies(i, slot)(lambda cp: cp.start())

    @pl.when(i + 1 < pl.num_programs(0))
    def _():
        copies(i + 1, 1 - slot)(lambda cp: cp.start())

    pos = pos_ref[...]
    gate = gate_ref[...]
    for k in range(TOP_K):
        wide_s[k] = jnp.broadcast_to(pos[:, k:k + 1], (tt, LANES))
        wide_s[TOP_K + k] = jnp.broadcast_to(gate[:, k:k + 1], (tt, LANES))
    acc_s[...] = y_ref[...]
    copies(i, slot)(lambda cp: cp.wait())

    lane = lax.broadcasted_iota(I32, (tt, LANES), 1).astype(F32)

    def weigh(c, carry):
        r0 = pl.multiple_of(c * SORT_CHUNK, SORT_CHUNK)
        halves = []
        for half in range(SORT_CHUNK // LANES):
            r = lane + (r0 + half * LANES).astype(F32)
            w = jnp.zeros((tt, LANES), F32)
            for k in range(TOP_K):
                w = jnp.where(r == wide_s[k], wide_s[TOP_K + k], w)
            halves.append(w.astype(BF16))
        w = jnp.concatenate(halves, axis=1)
        acc_s[...] += jnp.dot(w, yloc[slot, pl.ds(r0, SORT_CHUNK), :].astype(BF16), preferred_element_type=F32)
        return carry

    lax.fori_loop(0, LOCAL_ROWS // SORT_CHUNK, weigh, 0)

    @pl.when(i < prompt_tiles)
    def _():
        outp_ref[...] = acc_s[...]

    @pl.when(i >= prompt_tiles)
    def _():
        outs_ref[...] = acc_s[...]


def _combine(seg_len, seg_local, seg_off, y, gates, pos, y_rows, n_prompt):
    n = y.shape[0]
    tt = ROW_TILE
    nt, pt = n // tt, n_prompt // tt
    grid_spec = pltpu.PrefetchScalarGridSpec(
        num_scalar_prefetch=3,
        grid=(nt,),
        in_specs=[pl.BlockSpec((tt, D_MODEL), lambda i, *_: (i, 0)),
                  pl.BlockSpec((tt, LANES), lambda i, *_: (i, 0)),
                  pl.BlockSpec((tt, LANES), lambda i, *_: (i, 0)),
                  pl.BlockSpec(memory_space=pl.ANY)],
        out_specs=[pl.BlockSpec((tt, D_MODEL), lambda i, *_: (jnp.minimum(i, pt - 1), 0)),
                   pl.BlockSpec((tt, D_MODEL), lambda i, *_: (jnp.maximum(i - pt, 0), 0))],
        scratch_shapes=[pltpu.VMEM((2, LOCAL_ROWS, D_MODEL), F32), pltpu.VMEM((tt, D_MODEL), F32),
                        pltpu.VMEM((2 * TOP_K, tt, LANES), F32), pltpu.SemaphoreType.DMA((2,))],
    )
    return pl.pallas_call(
        functools.partial(_combine_body, prompt_tiles=pt),
        grid_spec=grid_spec,
        out_shape=[jax.ShapeDtypeStruct((n_prompt, D_MODEL), F32),
                   jax.ShapeDtypeStruct((n - n_prompt, D_MODEL), F32)],
        compiler_params=_params("arbitrary"),
        name="expert_combine",
    )(seg_len, seg_local, seg_off, y, gates, pos, y_rows)


def _moe(y, xf, logits, ffn_w, n_prompt):
    n = y.shape[0]
    tm, tt = EXPERT_TILE, ROW_TILE
    n_tt = n // tt
    gates, pos, seg = _route(logits)
    seg = seg.reshape(n_tt, SUBLANES, LANES)
    seg_len, seg_local = seg[:, 0, :N_EXPERTS], seg[:, 1, :N_EXPERTS]
    rows = jnp.sum(seg_len, axis=0)
    padded = (rows + tm - 1) // tm * tm
    pad_end = jnp.cumsum(padded)
    pad_start = pad_end - padded
    seg_off = pad_start[None, :] + jnp.cumsum(seg_len, axis=0) - seg_len
    n_tiles = -(-(n * TOP_K + n_tt * N_EXPERTS * (SUBLANES - 1) + N_EXPERTS * (tm - 1)) // tm)
    tile_expert = jnp.minimum(jnp.sum(pad_end[None, :] <= (jnp.arange(n_tiles) * tm)[:, None], axis=1),
                              N_EXPERTS - 1).astype(I32)
    n_used = (pad_end[-1:] // tm).astype(I32)
    fill_start = (pad_start + rows).astype(I32)
    flat = lambda a: a.astype(I32).reshape(n_tt * N_EXPERTS)

    x_sorted = _dispatch(flat(seg_len), flat(seg_local), flat(seg_off), fill_start, n_used, xf, pos, n_tiles)
    y_rows = _experts(tile_expert, n_used, x_sorted, *ffn_w)
    return _combine(flat(seg_len), flat(seg_local), flat(seg_off), y, gates, pos, y_rows, n_prompt)


def _mixer(x, lw, cache, state):
    b, t, _ = x.shape
    n = b * t
    q, k, v, mq, mk, mv, mo, gates = _in_proj(x.reshape(n, D_MODEL), lw["g_mix"], lw["w_main"], lw["w_gate"],
                                              lw["gq"], lw["gk"], lw["gmat"])
    heads = lambda a, rows: a.reshape(b, rows, N_HEADS, HEAD_DIM)
    if cache is None:
        tiles = t // PAST_BAND
        att = _attention(q, k, k, v, v, lw["bias_prompt"], lw["hmask_prompt"], batch=b, tiles=tiles, cq=CHUNK,
                         nq=PAST_BAND // CHUNK,
                         prev_index=lambda bi, i: (bi * tiles + jnp.maximum(i - 1, 0), 0), mask_first=True)
        keep = min(PAST_BAND, t)
        k_new = heads(k.reshape(b, t, GROUP_W)[:, t - keep:], keep)
        v_new = heads(v.reshape(b, t, GROUP_W)[:, t - keep:], keep)
    else:
        ck, cv = cache
        att = _attention(q, ck.reshape(b * PAST_BAND, GROUP_W), k, cv.reshape(b * PAST_BAND, GROUP_W), v,
                         lw["bias_sample"], lw["hmask_sample"], batch=b, tiles=1, cq=t, nq=1,
                         prev_index=lambda bi, i: (bi, 0), mask_first=False)
        k_new, v_new = heads(k, t), heads(v, t)

    tp = -(-t // CHUNK) * CHUNK
    valid = t if t < CHUNK else CHUNK

    def streams(a):
        a = a.reshape(b, t, -1)
        return a if tp == t else jnp.pad(a, ((0, 0), (0, tp - t), (0, 0)))

    if state is None:
        c0t = jnp.zeros((b, HEAD_DIM, GROUP_W), F32)
        n0 = jnp.zeros((b, 1, GROUP_W), F32)
        m0 = jnp.zeros((b, 1, GROUP_W), F32)
    else:
        c_in, n_in, m_in = state
        c0t = c_in.astype(F32).transpose(0, 3, 1, 2).reshape(b, HEAD_DIM, GROUP_W)
        n0 = n_in.astype(F32).reshape(b, 1, GROUP_W)
        m0 = jnp.repeat(m_in.astype(F32), HEAD_DIM, axis=-1).reshape(b, 1, GROUP_W)
    hm, ct, n_out, m_out = _mlstm(streams(mq), streams(mk), streams(mv), streams(mo), streams(gates),
                                  c0t, n0, m0, lw["mlstm_consts"], valid=valid)
    hm = hm[:, :t].reshape(n, GROUP_W)
    c_new = ct.reshape(b, HEAD_DIM, N_HEADS, HEAD_DIM).transpose(0, 2, 3, 1)
    n_new = n_out.reshape(b, N_HEADS, HEAD_DIM)
    m_new = m_out.reshape(b, N_HEADS, HEAD_DIM)[:, :, 0]
    return att, hm, (k_new, v_new, c_new, n_new, m_new)


def kernel(x_prompt, x_sample, cache_k, cache_v, state_C, state_n, state_m, g_mix, w_in, g_q, g_k, rel_bias,
           b_igate, b_fgate, g_mlstm, w_out, g_ffn, w_router, b_router, w_up, b_up, w_down, b_down):
    depth = w_in.shape[0]
    yp, ys = x_prompt, x_sample
    bs, ts = x_sample.shape[0], x_sample.shape[1]
    n_prompt = x_prompt.shape[0] * x_prompt.shape[1]
    st_prompt, st_sample = [], []
    n_main = N_PROJ * GROUP_W
    gmat = jnp.asarray(_head_block_diag() / HEAD_DIM, BF16)
    for l in range(depth):
        lw = dict(
            g_mix=g_mix[l].astype(F32)[None, :],
            w_main=w_in[l][:, :n_main].astype(BF16),
            w_gate=_stack_hi_lo(jnp.pad(w_in[l][:, n_main:].astype(F32), ((0, 0), (0, LANES - 2 * N_HEADS)))),
            gq=jnp.tile(g_q[l].astype(F32), N_HEADS)[None, :],
            gk=jnp.tile(g_k[l].astype(F32), N_HEADS)[None, :],
            gmat=gmat,
            bias_prompt=_rel_table(rel_bias[l], CHUNK),
            hmask_prompt=_head_row_mask(CHUNK),
            bias_sample=_rel_table(rel_bias[l], ts),
            hmask_sample=_head_row_mask(ts),
            mlstm_consts=_mlstm_consts(b_igate[l], b_fgate[l], g_mlstm[l]),
        )
        ffn_w = (w_up[l].astype(F32), w_down[l].astype(F32),
                 b_up[l][:, None, 0::2].astype(F32), b_up[l][:, None, 1::2].astype(F32),
                 b_down[l][:, None, :].astype(F32))
        att_p, hm_p, sp = _mixer(yp, lw, None, None)
        cache = (cache_k[l].reshape(bs, PAST_BAND, GROUP_W), cache_v[l].reshape(bs, PAST_BAND, GROUP_W))
        att_s, hm_s, ss = _mixer(ys, lw, cache, (state_C[l], state_n[l], state_m[l]))
        y, xf, logits = _out_proj(
            yp.reshape(-1, D_MODEL), ys.reshape(-1, D_MODEL), att_p, att_s, hm_p, hm_s,
            w_out[l][:GROUP_W].astype(BF16), w_out[l][GROUP_W:].astype(BF16), g_ffn[l].astype(F32)[None, :],
            _stack_hi_lo(jnp.pad(w_router[l].astype(F32), ((0, 0), (0, LANES - N_EXPERTS)))),
            jnp.pad(b_router[l].astype(F32), (0, LANES - N_EXPERTS))[None, :])
        out_p, out_s = _moe(y, xf, logits, ffn_w, n_prompt)
        yp, ys = out_p.reshape(x_prompt.shape), out_s.reshape(x_sample.shape)
        st_prompt.append(sp)
        st_sample.append(ss)
    k_p, v_p, c_p, n_p, m_p = [jnp.stack(a) for a in zip(*st_prompt)]
    k_s, v_s, c_s, n_s, m_s = [jnp.stack(a) for a in zip(*st_sample)]
    return (yp, ys, k_p, v_p, c_p, n_p, m_p, k_s, v_s, c_s, n_s, m_s)
```

```python
import functools

import numpy as np
import jax
import jax.numpy as jnp
from jax import lax
from jax.experimental import pallas as pl
from jax.experimental.pallas import tpu as pltpu

F32 = jnp.float32
BF16 = jnp.bfloat16
I32 = jnp.int32
HIGHEST = lax.Precision.HIGHEST

D_MODEL = 1024
N_HEADS = 8
HEAD_DIM = 64
GROUP_W = N_HEADS * HEAD_DIM
N_PROJ = 7
LANES = 128
CHUNK = 64
PAST_BAND = 512
KEY_WIN = 640
REL_CLIP = 256
N_EXPERTS = 32
TOP_K = 4
D_FF = 1024
SWIGLU_ALPHA = 1.702
SWIGLU_LIMIT = 7.0
RMS_EPS = 1e-6
NEG_BIG = -1e30
ROW_TILE = 512
EXPERT_TILE = 512
SUBLANES = 8
SORT_CHUNK = 256
LOCAL_ROWS = -(-(ROW_TILE * TOP_K + N_EXPERTS * (SUBLANES - 1)) // SORT_CHUNK) * SORT_CHUNK
SEG_SIZES = (512, 256, 128, 64, 32, 16, 8)
MLSTM_STREAMS = 4
VMEM_LIMIT_BYTES = 56 * 1024 * 1024


def _params(*sem):
    return pltpu.CompilerParams(dimension_semantics=sem, vmem_limit_bytes=VMEM_LIMIT_BYTES)


def _head_block_diag():
    h = np.arange(GROUP_W) // HEAD_DIM
    return (h[:, None] == h[None, :]).astype(np.float32)


def _full(shape):
    return pl.BlockSpec(shape, lambda *_: (0,) * len(shape))


def _split3(x):
    hi = x.astype(BF16)
    r = x - hi.astype(F32)
    mid = r.astype(BF16)
    lo = (r - mid.astype(F32)).astype(BF16)
    return hi, mid, lo


def _dot_f32ish(x, w_stack):
    hi = x.astype(BF16)
    lo = (x - hi.astype(F32)).astype(BF16)
    return jnp.dot(jnp.concatenate([hi, lo, hi], axis=1), w_stack, preferred_element_type=F32)


def _stack_hi_lo(w):
    hi = w.astype(BF16)
    lo = (w - hi.astype(F32)).astype(BF16)
    return jnp.concatenate([hi, hi, lo], axis=0)


def _in_proj_body(x_ref, gmix_ref, w_ref, wg_ref, gq_ref, gk_ref, gmat_ref,
                  q_ref, k_ref, v_ref, mq_ref, mk_ref, mv_ref, mo_ref, gate_ref):
    x = x_ref[...]
    xn = x * lax.rsqrt(jnp.mean(x * x, axis=-1, keepdims=True) + RMS_EPS) * gmix_ref[...]
    xb = xn.astype(BF16)

    def proj(j):
        return jnp.dot(xb, w_ref[:, j * GROUP_W:(j + 1) * GROUP_W], preferred_element_type=F32)

    def head_norm(a, g_ref):
        msq = jnp.dot((a * a).astype(BF16), gmat_ref[...], preferred_element_type=F32)
        return a * lax.rsqrt(msq + RMS_EPS) * g_ref[...]

    q_ref[...] = head_norm(proj(0), gq_ref)
    k_ref[...] = head_norm(proj(1), gk_ref)
    v_ref[...] = proj(2)
    mq_ref[...] = proj(3)
    mk_ref[...] = proj(4)
    mv_ref[...] = proj(5)
    mo_ref[...] = proj(6)
    gate_ref[...] = jnp.dot(xb, wg_ref[...], preferred_element_type=F32)


def _in_proj(x2d, g_mix, w_main, w_gate, gq_row, gk_row, gmat):
    n = x2d.shape[0]
    tm = ROW_TILE
    row = lambda w: pl.BlockSpec((tm, w), lambda i: (i, 0))
    outs = [jax.ShapeDtypeStruct((n, GROUP_W), F32)] * N_PROJ + [jax.ShapeDtypeStruct((n, LANES), F32)]
    return pl.pallas_call(
        _in_proj_body,
        grid=(n // tm,),
        in_specs=[row(D_MODEL), _full((1, D_MODEL)), _full((D_MODEL, N_PROJ * GROUP_W)),
                  _full((D_MODEL, LANES)), _full((1, GROUP_W)), _full((1, GROUP_W)),
                  _full((GROUP_W, GROUP_W))],
        out_specs=[row(GROUP_W)] * N_PROJ + [row(LANES)],
        out_shape=outs,
        compiler_params=_params("arbitrary"),
        name="in_proj",
    )(x2d, g_mix, w_main, w_gate, gq_row, gk_row, gmat)


def _attn_body(q_ref, kp_ref, kc_ref, vp_ref, vc_ref, bias_ref, hmask_ref, o_ref, kwin, vwin,
               *, cq, nq, mask_first):
    tc = cq * nq
    i = pl.program_id(1)
    kwin[0:PAST_BAND, :] = kp_ref[...].astype(BF16)
    kwin[PAST_BAND:PAST_BAND + tc, :] = kc_ref[...].astype(BF16)
    vwin[0:PAST_BAND, :] = vp_ref[...].astype(BF16)
    vwin[PAST_BAND:PAST_BAND + tc, :] = vc_ref[...].astype(BF16)
    pad_rows = kwin.shape[0] - PAST_BAND - tc
    kwin[PAST_BAND + tc:, :] = jnp.zeros((pad_rows, GROUP_W), BF16)
    vwin[PAST_BAND + tc:, :] = jnp.zeros((pad_rows, GROUP_W), BF16)

    hm = hmask_ref[...]
    bias = bias_ref[...]
    kk = lax.broadcasted_iota(I32, (1, KEY_WIN), 1)

    def chunk(j, carry):
        r0 = pl.multiple_of(j * cq, cq)
        q = q_ref[pl.ds(r0, cq), :] * (HEAD_DIM ** -0.5)
        qm = (jnp.concatenate([q] * N_HEADS, axis=0) * hm).astype(BF16)
        kw = kwin[pl.ds(r0, KEY_WIN), :]
        s = lax.dot_general(qm, kw, (((1,), (1,)), ((), ())), preferred_element_type=F32) + bias
        if mask_first:
            first_valid = jnp.where(i == 0, PAST_BAND - r0, 0)
            s = jnp.where(kk >= first_valid, s, NEG_BIG)
        m = jnp.max(s, axis=-1, keepdims=True)
        p = jnp.exp(s - m)
        l = jnp.sum(p, axis=-1, keepdims=True)
        vw = vwin[pl.ds(r0, KEY_WIN), :]
        o_all = jnp.dot(p.astype(BF16), vw, preferred_element_type=F32) / l * hm
        o = o_all[0:cq]
        for h in range(1, N_HEADS):
            o = o + o_all[h * cq:(h + 1) * cq]
        o_ref[pl.ds(r0, cq), :] = o
        return carry

    lax.fori_loop(0, nq, chunk, 0)


def _attention(q, k_prev_src, k_cur_src, v_prev_src, v_cur_src, bias, hmask, *, batch, tiles, cq, nq,
               prev_index, mask_first):
    tc = cq * nq
    cur = pl.BlockSpec((tc, GROUP_W), lambda b, i: (b * tiles + i, 0))
    prev = pl.BlockSpec((PAST_BAND, GROUP_W), prev_index)
    win_rows = (nq - 1) * cq + KEY_WIN
    return pl.pallas_call(
        functools.partial(_attn_body, cq=cq, nq=nq, mask_first=mask_first),
        grid=(batch, tiles),
        in_specs=[cur, prev, cur, prev, cur, _full((N_HEADS * cq, KEY_WIN)), _full((N_HEADS * cq, GROUP_W))],
        out_specs=cur,
        out_shape=jax.ShapeDtypeStruct(q.shape, F32),
        scratch_shapes=[pltpu.VMEM((win_rows, GROUP_W), BF16), pltpu.VMEM((win_rows, GROUP_W), BF16)],
        compiler_params=_params("arbitrary", "arbitrary"),
        name="band_attention",
    )(q, k_prev_src, k_cur_src, v_prev_src, v_cur_src, bias, hmask)


def _rel_table(rel_bias_l, cq):
    nk = PAST_BAND + cq
    dist = PAST_BAND + (cq - 1) - np.arange(nk + cq - 1)
    rev = rel_bias_l[:, np.clip(dist, -REL_CLIP, REL_CLIP) + REL_CLIP].astype(F32)
    tab = jnp.stack([rev[:, cq - 1 - q:cq - 1 - q + nk] for q in range(cq)], axis=1)
    tab = jnp.pad(tab, ((0, 0), (0, 0), (0, KEY_WIN - nk)), constant_values=NEG_BIG)
    return tab.reshape(N_HEADS * cq, KEY_WIN)


def _head_row_mask(cq):
    h_row = np.repeat(np.arange(N_HEADS), cq)
    h_col = np.arange(GROUP_W) // HEAD_DIM
    return jnp.asarray((h_row[:, None] == h_col[None, :]).astype(np.float32))


def _log_sigmoid(x):
    return jnp.minimum(x, 0.0) - jnp.log(1.0 + jnp.exp(-jnp.abs(x)))


def _mlstm_body(q_ref, k_ref, v_ref, o_ref, g_ref, c0_ref, n0_ref, m0_ref,
                expand_ref, gbias_ref, bd_ref, bdb_ref, gmat_ref, ltri_ref, eye_ref, causal_ref, gml_ref,
                h_ref, ct_ref, n_ref, m_ref, ct_s, n_s, m_s, *, valid, nb):
    c = pl.program_id(1)
    last = c == pl.num_programs(1) - 1
    bd = bd_ref[...]
    bdb = bdb_ref[...]
    gmat = gmat_ref[...]
    eye = eye_ref[...] > 0.5
    causal = causal_ref[...] > 0.5

    @pl.when(c == 0)
    def _():
        for b in range(nb):
            ct_s[b] = jnp.concatenate([c0_ref[b]] * N_HEADS, axis=0) * bd
            n_s[b] = n0_ref[b]
            m_s[b] = m0_ref[b]

    for b in range(nb):
        gp = jnp.dot(jnp.concatenate(_split3(g_ref[b]), axis=1), expand_ref[...],
                     preferred_element_type=F32) + gbias_ref[...]
        log_i = gp[:, :GROUP_W]
        log_f = _log_sigmoid(gp[:, GROUP_W:])
        if valid < CHUNK:
            live = lax.broadcasted_iota(I32, (CHUNK, GROUP_W), 0) < valid
            log_i = jnp.where(live, log_i, -jnp.inf)
            log_f = jnp.where(live, log_f, 0.0)
        cum_f = jnp.dot(ltri_ref[...], jnp.concatenate(_split3(log_f), axis=0), preferred_element_type=F32)

        b_row = jnp.sum(jnp.where(eye, log_i - cum_f, 0.0), axis=0, keepdims=True)
        m_prev = m_s[b]
        log_inter = cum_f + m_prev
        log_d = jnp.where(causal, cum_f + b_row, -jnp.inf)
        max_d = jnp.concatenate(
            [jnp.broadcast_to(jnp.max(log_d[:, h * HEAD_DIM:(h + 1) * HEAD_DIM], axis=-1, keepdims=True),
                              (CHUNK, HEAD_DIM)) for h in range(N_HEADS)], axis=1)
        m_t = jnp.maximum(log_inter, max_d)
        w_intra = jnp.exp(log_d - m_t)
        w_inter = jnp.exp(log_inter - m_t)

        q = q_ref[b] * (HEAD_DIM ** -0.5)
        k = k_ref[b]
        v = v_ref[b]
        qb = q.astype(BF16)
        vb = v.astype(BF16)
        kbd = jnp.concatenate([k.astype(BF16)] * N_HEADS, axis=0) * bdb
        vbd = jnp.concatenate([vb] * N_HEADS, axis=0) * bdb
        s = lax.dot_general(qb, kbd, (((1,), (1,)), ((), ())), preferred_element_type=F32) * w_intra
        ct = ct_s[b]
        n_prev = n_s[b]
        num = (jnp.dot(s.astype(BF16), vbd, preferred_element_type=F32)
               + w_inter * lax.dot_general(qb, ct.astype(BF16), (((1,), (1,)), ((), ())),
                                           preferred_element_type=F32))
        den_terms = (s + w_inter * q * n_prev) * float(HEAD_DIM)
        den_hi = den_terms.astype(BF16)
        den_lo = (den_terms - den_hi.astype(F32)).astype(BF16)
        den = (jnp.dot(den_hi, gmat, preferred_element_type=F32)
               + jnp.dot(den_lo, gmat, preferred_element_type=F32))
        hb = num / jnp.maximum(jnp.abs(den), jnp.exp(-m_t))

        m_new = m_t[CHUNK - 1:CHUNK, :]
        cum_last = cum_f[CHUNK - 1:CHUNK, :]
        w_state = jnp.exp(cum_last - cum_f + log_i - m_new)
        decay = jnp.exp(cum_last + m_prev - m_new)
        kw = k * w_state
        upd = lax.dot_general(vb, kw.astype(BF16), (((0,), (0,)), ((), ())),
                              preferred_element_type=F32)
        ct_new = decay * ct + upd * bd
        n_new = decay * n_prev + jnp.sum(kw, axis=0, keepdims=True)
        ct_s[b] = ct_new
        n_s[b] = n_new
        m_s[b] = m_new

        msq = jnp.dot((hb * hb).astype(BF16), gmat, preferred_element_type=F32)
        h_ref[b] = jax.nn.sigmoid(o_ref[b]) * (hb * lax.rsqrt(msq + RMS_EPS) * gml_ref[...])

    @pl.when(last)
    def _():
        for b in range(nb):
            ct_new = ct_s[b]
            acc = ct_new[0:HEAD_DIM]
            for h in range(1, N_HEADS):
                acc = acc + ct_new[h * HEAD_DIM:(h + 1) * HEAD_DIM]
            ct_ref[b] = acc
            n_ref[b] = n_s[b]
            m_ref[b] = m_s[b]


def _mlstm(mq, mk, mv, mo, gates, c0t, n0, m0, consts, *, valid):
    batch, t, _ = mq.shape
    chunks = t // CHUNK
    nb = MLSTM_STREAMS
    row = lambda w: pl.BlockSpec((nb, CHUNK, w), lambda g, c: (g, c, 0))
    per_b = lambda r: pl.BlockSpec((nb, r, GROUP_W), lambda g, c: (g, 0, 0))
    expand, gbias, bd, bdb, gmat, ltri, eye, causal, gml = consts
    return pl.pallas_call(
        functools.partial(_mlstm_body, valid=valid, nb=nb),
        grid=(batch // nb, chunks),
        in_specs=[row(GROUP_W)] * 4 + [row(LANES), per_b(HEAD_DIM), per_b(1), per_b(1),
                  _full((3 * LANES, 2 * GROUP_W)), _full((1, 2 * GROUP_W)), _full((GROUP_W, GROUP_W)),
                  _full((GROUP_W, GROUP_W)), _full((GROUP_W, GROUP_W)), _full((CHUNK, 3 * CHUNK)),
                  _full((CHUNK, GROUP_W)), _full((CHUNK, GROUP_W)), _full((1, GROUP_W))],
        out_specs=[row(GROUP_W), per_b(HEAD_DIM), per_b(1), per_b(1)],
        out_shape=[jax.ShapeDtypeStruct(mq.shape, F32),
                   jax.ShapeDtypeStruct((batch, HEAD_DIM, GROUP_W), F32),
                   jax.ShapeDtypeStruct((batch, 1, GROUP_W), F32),
                   jax.ShapeDtypeStruct((batch, 1, GROUP_W), F32)],
        scratch_shapes=[pltpu.VMEM((nb, GROUP_W, GROUP_W), F32), pltpu.VMEM((nb, 1, GROUP_W), F32),
                        pltpu.VMEM((nb, 1, GROUP_W), F32)],
        compiler_params=_params("arbitrary", "arbitrary"),
        name="mlstm",
    )(mq, mk, mv, mo, gates, c0t, n0, m0, expand, gbias, bd, bdb, gmat, ltri, eye, causal, gml)


def _mlstm_consts(b_igate_l, b_fgate_l, g_mlstm_l):
    expand = np.zeros((LANES, 2 * GROUP_W), np.float32)
    for h in range(N_HEADS):
        expand[h, h * HEAD_DIM:(h + 1) * HEAD_DIM] = 1.0
        expand[N_HEADS + h, GROUP_W + h * HEAD_DIM:GROUP_W + (h + 1) * HEAD_DIM] = 1.0
    gbias = jnp.concatenate([jnp.repeat(b_igate_l.astype(F32), HEAD_DIM),
                             jnp.repeat(b_fgate_l.astype(F32), HEAD_DIM)])[None, :]
    bd = _head_block_diag()
    ltri = np.tril(np.ones((CHUNK, CHUNK), np.float32))
    s_of_lane = np.arange(GROUP_W) % HEAD_DIM
    t = np.arange(CHUNK)
    eye = (t[:, None] == s_of_lane[None, :]).astype(np.float32)
    causal = (s_of_lane[None, :] <= t[:, None]).astype(np.float32)
    return (jnp.asarray(np.concatenate([expand] * 3, axis=0), BF16), gbias, jnp.asarray(bd), jnp.asarray(bd, BF16),
            jnp.asarray(bd / HEAD_DIM, BF16), jnp.asarray(np.concatenate([ltri] * 3, axis=1), BF16), jnp.asarray(eye), jnp.asarray(causal), g_mlstm_l.astype(F32).reshape(1, GROUP_W))


def _out_proj_body(xp_ref, xs_ref, ap_ref, as_ref, hp_ref, hs_ref, wa_ref, wm_ref, gffn_ref, wr_ref, br_ref,
                   y_ref, xf_ref, logit_ref, *, prompt_tiles):
    is_prompt = pl.program_id(0) < prompt_tiles
    x = jnp.where(is_prompt, xp_ref[...], xs_ref[...])
    att = jnp.where(is_prompt, ap_ref[...], as_ref[...])
    hm = jnp.where(is_prompt, hp_ref[...], hs_ref[...])
    y = (x + jnp.dot(att.astype(BF16), wa_ref[...], preferred_element_type=F32)
         + jnp.dot(hm.astype(BF16), wm_ref[...], preferred_element_type=F32))
    y_ref[...] = y
    xf = y * lax.rsqrt(jnp.mean(y * y, axis=-1, keepdims=True) + RMS_EPS) * gffn_ref[...]
    xf_ref[...] = xf
    logit_ref[...] = _dot_f32ish(xf, wr_ref[...]) + br_ref[...]


def _out_proj(xp, xs, att_p, att_s, hm_p, hm_s, wa, wm, g_ffn, w_router, b_router):
    tm = ROW_TILE
    pt, st = xp.shape[0] // tm, xs.shape[0] // tm
    n = xp.shape[0] + xs.shape[0]
    p_row = lambda w: pl.BlockSpec((tm, w), lambda i: (jnp.minimum(i, pt - 1), 0))
    s_row = lambda w: pl.BlockSpec((tm, w), lambda i: (jnp.maximum(i - pt, 0), 0))
    row = lambda w: pl.BlockSpec((tm, w), lambda i: (i, 0))
    return pl.pallas_call(
        functools.partial(_out_proj_body, prompt_tiles=pt),
        grid=(pt + st,),
        in_specs=[p_row(D_MODEL), s_row(D_MODEL), p_row(GROUP_W), s_row(GROUP_W), p_row(GROUP_W), s_row(GROUP_W),
                  _full((GROUP_W, D_MODEL)), _full((GROUP_W, D_MODEL)),
                  _full((1, D_MODEL)), _full((3 * D_MODEL, LANES)), _full((1, LANES))],
        out_specs=[row(D_MODEL), row(D_MODEL), row(LANES)],
        out_shape=[jax.ShapeDtypeStruct((n, D_MODEL), F32), jax.ShapeDtypeStruct((n, D_MODEL), F32),
                   jax.ShapeDtypeStruct((n, LANES), F32)],
        compiler_params=_params("arbitrary"),
        name="out_proj_router",
    )(xp, xs, att_p, att_s, hm_p, hm_s, wa, wm, g_ffn, w_router, b_router)


def _route_body(logit_ref, lstrict_ref, ustrict_ref, gate_ref, pos_ref, seg_ref):
    tt = logit_ref.shape[0]
    lane = lax.broadcasted_iota(I32, (tt, LANES), 1)
    work = jnp.where(lane < N_EXPERTS, logit_ref[...], -jnp.inf)
    vals, idxs = [], []
    for _ in range(TOP_K):
        m = jnp.max(work, axis=-1, keepdims=True)
        idx = jnp.min(jnp.where(work == m, lane, LANES), axis=-1, keepdims=True)
        vals.append(m)
        idxs.append(idx)
        work = jnp.where(lane == idx, -jnp.inf, work)
    exps = [jnp.exp(v - vals[0]) for v in vals]
    total = exps[0] + exps[1] + exps[2] + exps[3]

    chosen = jnp.zeros((tt, LANES), F32)
    for idx in idxs:
        chosen = chosen + (lane == idx).astype(F32)
    before = jnp.dot(lstrict_ref[...], chosen.astype(BF16), preferred_element_type=F32)
    count = jnp.sum(chosen, axis=0, keepdims=True)
    groups = jnp.floor((count + (SUBLANES - 1)) * (1.0 / SUBLANES))
    groups8 = jnp.broadcast_to(groups, (SUBLANES, LANES)).astype(BF16)
    start = jnp.dot(groups8, ustrict_ref[...], preferred_element_type=F32) * float(SUBLANES)
    local = before + start[0:1, :]

    gate_out = jnp.zeros((tt, LANES), F32)
    pos_out = jnp.zeros((tt, LANES), F32)
    for k in range(TOP_K):
        pos = jnp.sum(jnp.where(lane == idxs[k], local, 0.0), axis=-1, keepdims=True)
        gate_out = jnp.where(lane == k, exps[k] / total, gate_out)
        pos_out = jnp.where(lane == k, pos, pos_out)
    gate_ref[...] = gate_out
    pos_ref[...] = pos_out
    row = lax.broadcasted_iota(I32, (SUBLANES, LANES), 0)
    seg = jnp.where(row == 0, groups * float(SUBLANES), jnp.where(row == 1, start, 0.0))
    seg_ref[...] = seg.astype(I32)


def _route(logits):
    n = logits.shape[0]
    tt = ROW_TILE
    lstrict = jnp.asarray(np.tril(np.ones((tt, tt), np.float32), -1), BF16)
    ustrict = jnp.asarray(np.triu(np.ones((LANES, LANES), np.float32), 1), BF16)
    row = lambda: pl.BlockSpec((tt, LANES), lambda i: (i, 0))
    return pl.pallas_call(
        _route_body,
        grid=(n // tt,),
        in_specs=[row(), _full((tt, tt)), _full((LANES, LANES))],
        out_specs=[row(), row(), pl.BlockSpec((SUBLANES, LANES), lambda i: (i, 0))],
        out_shape=[jax.ShapeDtypeStruct((n, LANES), F32), jax.ShapeDtypeStruct((n, LANES), F32),
                   jax.ShapeDtypeStruct((n // tt * SUBLANES, LANES), I32)],
        compiler_params=_params("arbitrary"),
        name="route_topk",
    )(logits, lstrict, ustrict)


def _segment_copies(i, len_ref, lst_ref, off_ref, make_copy, act):
    for e in range(N_EXPERTS):
        seg = i * N_EXPERTS + e
        length = len_ref[seg]
        local = lst_ref[seg]
        glob = off_ref[seg]
        for size in SEG_SIZES:
            take = length & size

            @pl.when(take != 0)
            def _(local=local, glob=glob, size=size):
                act(make_copy(pl.multiple_of(local, SUBLANES), pl.multiple_of(glob, SUBLANES), size))

            local = local + take
            glob = glob + take


def _local_onehot(pos_rows, base, rows, values=None):
    r = (lax.broadcasted_iota(I32, (rows, pos_rows[0].shape[1]), 0) + base).astype(F32)
    out = jnp.zeros(r.shape, F32)
    for k in range(TOP_K):
        out = jnp.where(r == pos_rows[k], 1.0 if values is None else values[k], out)
    return out


def _dispatch_body(len_ref, lst_ref, off_ref, fill_ref, nused_ref, xf_ref, pos_ref, xs_hbm,
                   xloc, zbuf, sem, zsem, *, tm, n_tiles):
    i = pl.program_id(0)
    tt = xf_ref.shape[0]
    fill_rows = zbuf.shape[0]

    @pl.when(i == 0)
    def _():
        zbuf[...] = jnp.zeros_like(zbuf)

        def fill(e):
            start = pl.multiple_of(fill_ref[e], SUBLANES)
            return pltpu.make_async_copy(zbuf, xs_hbm.at[pl.ds(start, fill_rows)], zsem)

        for e in range(N_EXPERTS):
            fill(e).start()
            fill(e).wait()

        def tail(j, carry):
            cp = pltpu.make_async_copy(zbuf.at[pl.ds(0, tm)], xs_hbm.at[pl.ds(pl.multiple_of(j * tm, tm), tm)], zsem)
            cp.start()
            cp.wait()
            return carry

        lax.fori_loop(nused_ref[0], n_tiles, tail, 0)

    pos_t = jnp.transpose(pos_ref[...])
    pos_rows = [pos_t[k:k + 1, :] for k in range(TOP_K)]
    xb = xf_ref[...].astype(BF16)

    slot = lax.rem(i, 2)

    def sort_rows(c, carry):
        r0 = pl.multiple_of(c * SORT_CHUNK, SORT_CHUNK)
        sel = _local_onehot(pos_rows, r0, SORT_CHUNK).astype(BF16)
        xloc[slot, pl.ds(r0, SORT_CHUNK), :] = jnp.dot(sel, xb, preferred_element_type=F32)
        return carry

    lax.fori_loop(0, LOCAL_ROWS // SORT_CHUNK, sort_rows, 0)

    def copies(step, which):
        def make_copy(local, glob, size):
            return pltpu.make_async_copy(xloc.at[which, pl.ds(local, size)], xs_hbm.at[pl.ds(glob, size)],
                                         sem.at[which])
        return functools.partial(_segment_copies, step, len_ref, lst_ref, off_ref, make_copy)

    copies(i, slot)(lambda cp: cp.start())

    @pl.when(i > 0)
    def _():
        copies(i - 1, 1 - slot)(lambda cp: cp.wait())

    @pl.when(i == pl.num_programs(0) - 1)
    def _():
        copies(i, slot)(lambda cp: cp.wait())


def _dispatch(seg_len, seg_local, seg_off, fill_start, n_used, xf, pos, n_tiles):
    n = xf.shape[0]
    tt, tm = ROW_TILE, EXPERT_TILE
    fill_rows = tm + SUBLANES
    grid_spec = pltpu.PrefetchScalarGridSpec(
        num_scalar_prefetch=5,
        grid=(n // tt,),
        in_specs=[pl.BlockSpec((tt, D_MODEL), lambda i, *_: (i, 0)),
                  pl.BlockSpec((tt, LANES), lambda i, *_: (i, 0))],
        out_specs=pl.BlockSpec(memory_space=pl.ANY),
        scratch_shapes=[pltpu.VMEM((2, LOCAL_ROWS, D_MODEL), F32), pltpu.VMEM((fill_rows, D_MODEL), F32),
                        pltpu.SemaphoreType.DMA((2,)), pltpu.SemaphoreType.DMA(())],
    )
    return pl.pallas_call(
        functools.partial(_dispatch_body, tm=tm, n_tiles=n_tiles + 2),
        grid_spec=grid_spec,
        out_shape=jax.ShapeDtypeStruct(((n_tiles + 2) * tm, D_MODEL), F32),
        compiler_params=_params("arbitrary"),
        name="expert_dispatch",
    )(seg_len, seg_local, seg_off, fill_start, n_used, xf, pos)


def _expert_body(te_ref, nused_ref, x_ref, wup_ref, wdn_ref, perm_ref, bg_ref, bl_ref, bd_ref,
                 y_ref, wg_s, wl_s, wd_s):
    i = pl.program_id(0)
    n_used = nused_ref[0]

    @pl.when(i >= n_used)
    def _():
        y_ref[...] = jnp.zeros_like(y_ref)

    @pl.when((i == 0) | (te_ref[i] != te_ref[jnp.maximum(i - 1, 0)]))
    def _():
        perm = perm_ref[...]
        for c in range(2 * D_FF // 256):
            blk = wup_ref[0, :, c * 256:(c + 1) * 256].astype(BF16)
            sep = jnp.dot(blk, perm, preferred_element_type=F32).astype(BF16)
            wg_s[:, c * 128:(c + 1) * 128] = sep[:, :128]
            wl_s[:, c * 128:(c + 1) * 128] = sep[:, 128:]
        wd_s[...] = wdn_ref[0].astype(BF16)

    @pl.when(i < n_used)
    def _():
        x = x_ref[...].astype(BF16)
        glu = jnp.minimum(jnp.dot(x, wg_s[...], preferred_element_type=F32) + bg_ref[0], SWIGLU_LIMIT)
        lin = jnp.clip(jnp.dot(x, wl_s[...], preferred_element_type=F32) + bl_ref[0], -SWIGLU_LIMIT, SWIGLU_LIMIT)
        act = glu * jax.nn.sigmoid(SWIGLU_ALPHA * glu) * (lin + 1.0)
        y_ref[...] = jnp.dot(act.astype(BF16), wd_s[...], preferred_element_type=F32) + bd_ref[0]


def _deinterleave_perm():
    p = np.zeros((256, 256), np.float32)
    j = np.arange(128)
    p[2 * j, j] = 1.0
    p[2 * j + 1, 128 + j] = 1.0
    return jnp.asarray(p, BF16)


def _experts(tile_expert, n_used, x_sorted, w_up, w_down, b_glu, b_lin, b_down):
    tm = EXPERT_TILE
    n_tiles = tile_expert.shape[0]
    wspec = lambda k, n: pl.BlockSpec((1, k, n), lambda i, te, nu: (te[i], 0, 0))
    grid_spec = pltpu.PrefetchScalarGridSpec(
        num_scalar_prefetch=2,
        grid=(n_tiles,),
        in_specs=[pl.BlockSpec((tm, D_MODEL), lambda i, te, nu: (jnp.minimum(i, nu[0] - 1), 0)),
                  wspec(D_MODEL, 2 * D_FF), wspec(D_FF, D_MODEL),
                  pl.BlockSpec((256, 256), lambda i, te, nu: (0, 0)),
                  wspec(1, D_FF), wspec(1, D_FF), wspec(1, D_MODEL)],
        out_specs=pl.BlockSpec((tm, D_MODEL), lambda i, te, nu: (i, 0)),
        scratch_shapes=[pltpu.VMEM((D_MODEL, D_FF), BF16), pltpu.VMEM((D_MODEL, D_FF), BF16),
                        pltpu.VMEM((D_FF, D_MODEL), BF16)],
    )
    return pl.pallas_call(
        _expert_body,
        grid_spec=grid_spec,
        out_shape=jax.ShapeDtypeStruct((n_tiles * tm, D_MODEL), F32),
        compiler_params=_params("arbitrary"),
        name="expert_ffn",
    )(tile_expert, n_used, x_sorted, w_up, w_down, _deinterleave_perm(), b_glu, b_lin, b_down)


def _combine_body(len_ref, lst_ref, off_ref, y_ref, gate_ref, pos_ref, rows_hbm, outp_ref, outs_ref,
                  yloc, acc_s, wide_s, sem, *, prompt_tiles):
    i = pl.program_id(0)
    slot = lax.rem(i, 2)
    tt = y_ref.shape[0]

    def copies(step, which):
        def make_copy(local, glob, size):
            return pltpu.make_async_copy(rows_hbm.at[pl.ds(glob, size)], yloc.at[which, pl.ds(local, size)],
                                         sem.at[which])
        return functools.partial(_segment_copies, step, len_ref, lst_ref, off_ref, make_copy)

    @pl.when(i == 0)
    def _():
        yloc[...] = jnp.zeros_like(yloc)
        copies(i, slot)(lambda cp: cp.start())

    @pl.when(i + 1 < pl.num_programs(0))
    def _():
        copies(i + 1, 1 - slot)(lambda cp: cp.start())

    pos = pos_ref[...]
    gate = gate_ref[...]
    for k in range(TOP_K):
        wide_s[k] = jnp.broadcast_to(pos[:, k:k + 1], (tt, LANES))
        wide_s[TOP_K + k] = jnp.broadcast_to(gate[:, k:k + 1], (tt, LANES))
    acc_s[...] = y_ref[...]
    copies(i, slot)(lambda cp: cp.wait())

    lane = lax.broadcasted_iota(I32, (tt, LANES), 1).astype(F32)

    def weigh(c, carry):
        r0 = pl.multiple_of(c * SORT_CHUNK, SORT_CHUNK)
        halves = []
        for half in range(SORT_CHUNK // LANES):
            r = lane + (r0 + half * LANES).astype(F32)
            w = jnp.zeros((tt, LANES), F32)
            for k in range(TOP_K):
                w = jnp.where(r == wide_s[k], wide_s[TOP_K + k], w)
            halves.append(w.astype(BF16))
        w = jnp.concatenate(halves, axis=1)
        acc_s[...] += jnp.dot(w, yloc[slot, pl.ds(r0, SORT_CHUNK), :].astype(BF16), preferred_element_type=F32)
        return carry

    lax.fori_loop(0, LOCAL_ROWS // SORT_CHUNK, weigh, 0)

    @pl.when(i < prompt_tiles)
    def _():
        outp_ref[...] = acc_s[...]

    @pl.when(i >= prompt_tiles)
    def _():
        outs_ref[...] = acc_s[...]


def _combine(seg_len, seg_local, seg_off, y, gates, pos, y_rows, n_prompt):
    n = y.shape[0]
    tt = ROW_TILE
    nt, pt = n // tt, n_prompt // tt
    grid_spec = pltpu.PrefetchScalarGridSpec(
        num_scalar_prefetch=3,
        grid=(nt,),
        in_specs=[pl.BlockSpec((tt, D_MODEL), lambda i, *_: (i, 0)),
                  pl.BlockSpec((tt, LANES), lambda i, *_: (i, 0)),
                  pl.BlockSpec((tt, LANES), lambda i, *_: (i, 0)),
                  pl.BlockSpec(memory_space=pl.ANY)],
        out_specs=[pl.BlockSpec((tt, D_MODEL), lambda i, *_: (jnp.minimum(i, pt - 1), 0)),
                   pl.BlockSpec((tt, D_MODEL), lambda i, *_: (jnp.maximum(i - pt, 0), 0))],
        scratch_shapes=[pltpu.VMEM((2, LOCAL_ROWS, D_MODEL), F32), pltpu.VMEM((tt, D_MODEL), F32),
                        pltpu.VMEM((2 * TOP_K, tt, LANES), F32), pltpu.SemaphoreType.DMA((2,))],
    )
    return pl.pallas_call(
        functools.partial(_combine_body, prompt_tiles=pt),
        grid_spec=grid_spec,
        out_shape=[jax.ShapeDtypeStruct((n_prompt, D_MODEL), F32),
                   jax.ShapeDtypeStruct((n - n_prompt, D_MODEL), F32)],
        compiler_params=_params("arbitrary"),
        name="expert_combine",
    )(seg_len, seg_local, seg_off, y, gates, pos, y_rows)


def _moe(y, xf, logits, ffn_w, n_prompt):
    n = y.shape[0]
    tm, tt = EXPERT_TILE, ROW_TILE
    n_tt = n // tt
    gates, pos, seg = _route(logits)
    seg = seg.reshape(n_tt, SUBLANES, LANES)
    seg_len, seg_local = seg[:, 0, :N_EXPERTS], seg[:, 1, :N_EXPERTS]
    rows = jnp.sum(seg_len, axis=0)
    padded = (rows + tm - 1) // tm * tm
    pad_end = jnp.cumsum(padded)
    pad_start = pad_end - padded
    seg_off = pad_start[None, :] + jnp.cumsum(seg_len, axis=0) - seg_len
    n_tiles = -(-(n * TOP_K + n_tt * N_EXPERTS * (SUBLANES - 1) + N_EXPERTS * (tm - 1)) // tm)
    tile_expert = jnp.minimum(jnp.sum(pad_end[None, :] <= (jnp.arange(n_tiles) * tm)[:, None], axis=1),
                              N_EXPERTS - 1).astype(I32)
    n_used = (pad_end[-1:] // tm).astype(I32)
    fill_start = (pad_start + rows).astype(I32)
    flat = lambda a: a.astype(I32).reshape(n_tt * N_EXPERTS)

    x_sorted = _dispatch(flat(seg_len), flat(seg_local), flat(seg_off), fill_start, n_used, xf, pos, n_tiles)
    y_rows = _experts(tile_expert, n_used, x_sorted, *ffn_w)
    return _combine(flat(seg_len), flat(seg_local), flat(seg_off), y, gates, pos, y_rows, n_prompt)


def _mixer(x, lw, cache, state):
    b, t, _ = x.shape
    n = b * t
    q, k, v, mq, mk, mv, mo, gates = _in_proj(x.reshape(n, D_MODEL), lw["g_mix"], lw["w_main"], lw["w_gate"],
                                              lw["gq"], lw["gk"], lw["gmat"])
    heads = lambda a, rows: a.reshape(b, rows, N_HEADS, HEAD_DIM)
    if cache is None:
        tiles = t // PAST_BAND
        att = _attention(q, k, k, v, v, lw["bias_prompt"], lw["hmask_prompt"], batch=b, tiles=tiles, cq=CHUNK,
                         nq=PAST_BAND // CHUNK,
                         prev_index=lambda bi, i: (bi * tiles + jnp.maximum(i - 1, 0), 0), mask_first=True)
        keep = min(PAST_BAND, t)
        k_new = heads(k.reshape(b, t, GROUP_W)[:, t - keep:], keep)
        v_new = heads(v.reshape(b, t, GROUP_W)[:, t - keep:], keep)
    else:
        ck, cv = cache
        att = _attention(q, ck.reshape(b * PAST_BAND, GROUP_W), k, cv.reshape(b * PAST_BAND, GROUP_W), v,
                         lw["bias_sample"], lw["hmask_sample"], batch=b, tiles=1, cq=t, nq=1,
                         prev_index=lambda bi, i: (bi, 0), mask_first=False)
        k_new, v_new = heads(k, t), heads(v, t)

    tp = -(-t // CHUNK) * CHUNK
    valid = t if t < CHUNK else CHUNK

    def streams(a):
        a = a.reshape(b, t, -1)
        return a if tp == t else jnp.pad(a, ((0, 0), (0, tp - t), (0, 0)))

    if state is None:
        c0t = jnp.zeros((b, HEAD_DIM, GROUP_W), F32)
        n0 = jnp.zeros((b, 1, GROUP_W), F32)
        m0 = jnp.zeros((b, 1, GROUP_W), F32)
    else:
        c_in, n_in, m_in = state
        c0t = c_in.astype(F32).transpose(0, 3, 1, 2).reshape(b, HEAD_DIM, GROUP_W)
        n0 = n_in.astype(F32).reshape(b, 1, GROUP_W)
        m0 = jnp.repeat(m_in.astype(F32), HEAD_DIM, axis=-1).reshape(b, 1, GROUP_W)
    hm, ct, n_out, m_out = _mlstm(streams(mq), streams(mk), streams(mv), streams(mo), streams(gates),
                                  c0t, n0, m0, lw["mlstm_consts"], valid=valid)
    hm = hm[:, :t].reshape(n, GROUP_W)
    c_new = ct.reshape(b, HEAD_DIM, N_HEADS, HEAD_DIM).transpose(0, 2, 3, 1)
    n_new = n_out.reshape(b, N_HEADS, HEAD_DIM)
    m_new = m_out.reshape(b, N_HEADS, HEAD_DIM)[:, :, 0]
    return att, hm, (k_new, v_new, c_new, n_new, m_new)


def kernel(x_prompt, x_sample, cache_k, cache_v, state_C, state_n, state_m, g_mix, w_in, g_q, g_k, rel_bias,
           b_igate, b_fgate, g_mlstm, w_out, g_ffn, w_router, b_router, w_up, b_up, w_down, b_down):
    depth = w_in.shape[0]
    yp, ys = x_prompt, x_sample
    bs, ts = x_sample.shape[0], x_sample.shape[1]
    n_prompt = x_prompt.shape[0] * x_prompt.shape[1]
    st_prompt, st_sample = [], []
    n_main = N_PROJ * GROUP_W
    gmat = jnp.asarray(_head_block_diag() / HEAD_DIM, BF16)
    for l in range(depth):
        lw = dict(
            g_mix=g_mix[l].astype(F32)[None, :],
            w_main=w_in[l][:, :n_main].astype(BF16),
            w_gate=jnp.pad(w_in[l][:, n_main:], ((0, 0), (0, LANES - 2 * N_HEADS))).astype(BF16),
            gq=jnp.tile(g_q[l].astype(F32), N_HEADS)[None, :],
            gk=jnp.tile(g_k[l].astype(F32), N_HEADS)[None, :],
            gmat=gmat,
            bias_prompt=_rel_table(rel_bias[l], CHUNK),
            hmask_prompt=_head_row_mask(CHUNK),
            bias_sample=_rel_table(rel_bias[l], ts),
            hmask_sample=_head_row_mask(ts),
            mlstm_consts=_mlstm_consts(b_igate[l], b_fgate[l], g_mlstm[l]),
        )
        ffn_w = (w_up[l].astype(F32), w_down[l].astype(F32),
                 b_up[l][:, None, 0::2].astype(F32), b_up[l][:, None, 1::2].astype(F32),
                 b_down[l][:, None, :].astype(F32))
        att_p, hm_p, sp = _mixer(yp, lw, None, None)
        cache = (cache_k[l].reshape(bs, PAST_BAND, GROUP_W), cache_v[l].reshape(bs, PAST_BAND, GROUP_W))
        att_s, hm_s, ss = _mixer(ys, lw, cache, (state_C[l], state_n[l], state_m[l]))
        y, xf, logits = _out_proj(
            yp.reshape(-1, D_MODEL), ys.reshape(-1, D_MODEL), att_p, att_s, hm_p, hm_s,
            w_out[l][:GROUP_W].astype(BF16), w_out[l][GROUP_W:].astype(BF16), g_ffn[l].astype(F32)[None, :],
            _stack_hi_lo(jnp.pad(w_router[l].astype(F32), ((0, 0), (0, LANES - N_EXPERTS)))),
            jnp.pad(b_router[l].astype(F32), (0, LANES - N_EXPERTS))[None, :])
        out_p, out_s = _moe(y, xf, logits, ffn_w, n_prompt)
        yp, ys = out_p.reshape(x_prompt.shape), out_s.reshape(x_sample.shape)
        st_prompt.append(sp)
        st_sample.append(ss)
    k_p, v_p, c_p, n_p, m_p = [jnp.stack(a) for a in zip(*st_prompt)]
    k_s, v_s, c_s, n_s, m_s = [jnp.stack(a) for a in zip(*st_sample)]
    return (yp, ys, k_p, v_p, c_p, n_p, m_p, k_s, v_s, c_s, n_s, m_s)
```

```python
import functools

import numpy as np
import jax
import jax.numpy as jnp
from jax import lax
from jax.experimental import pallas as pl
from jax.experimental.pallas import tpu as pltpu

F32 = jnp.float32
BF16 = jnp.bfloat16
I32 = jnp.int32
HIGHEST = lax.Precision.HIGHEST

D_MODEL = 1024
N_HEADS = 8
HEAD_DIM = 64
GROUP_W = N_HEADS * HEAD_DIM
N_PROJ = 7
LANES = 128
CHUNK = 64
PAST_BAND = 512
KEY_WIN = 640
REL_CLIP = 256
N_EXPERTS = 32
TOP_K = 4
D_FF = 1024
SWIGLU_ALPHA = 1.702
SWIGLU_LIMIT = 7.0
RMS_EPS = 1e-6
NEG_BIG = -1e30
ROW_TILE = 512
EXPERT_TILE = 512
SUBLANES = 8
SORT_CHUNK = 256
LOCAL_ROWS = -(-(ROW_TILE * TOP_K + N_EXPERTS * (SUBLANES - 1)) // SORT_CHUNK) * SORT_CHUNK
SEG_SIZES = (512, 256, 128, 64, 32, 16, 8)
MLSTM_STREAMS = 4
VMEM_LIMIT_BYTES = 56 * 1024 * 1024


def _params(*sem):
    return pltpu.CompilerParams(dimension_semantics=sem, vmem_limit_bytes=VMEM_LIMIT_BYTES)


def _head_block_diag():
    h = np.arange(GROUP_W) // HEAD_DIM
    return (h[:, None] == h[None, :]).astype(np.float32)


def _full(shape):
    return pl.BlockSpec(shape, lambda *_: (0,) * len(shape))


def _split3(x):
    hi = x.astype(BF16)
    r = x - hi.astype(F32)
    mid = r.astype(BF16)
    lo = (r - mid.astype(F32)).astype(BF16)
    return hi, mid, lo


def _dot_f32ish(x, w_stack):
    hi = x.astype(BF16)
    lo = (x - hi.astype(F32)).astype(BF16)
    return jnp.dot(jnp.concatenate([hi, lo, hi], axis=1), w_stack, preferred_element_type=F32)


def _stack_hi_lo(w):
    hi = w.astype(BF16)
    lo = (w - hi.astype(F32)).astype(BF16)
    return jnp.concatenate([hi, hi, lo], axis=0)


def _in_proj_body(x_ref, gmix_ref, w_ref, wg_ref, gq_ref, gk_ref, gmat_ref,
                  q_ref, k_ref, v_ref, mq_ref, mk_ref, mv_ref, mo_ref, gate_ref):
    x = x_ref[...]
    xn = x * lax.rsqrt(jnp.mean(x * x, axis=-1, keepdims=True) + RMS_EPS) * gmix_ref[...]
    xb = xn.astype(BF16)

    def proj(j):
        return jnp.dot(xb, w_ref[:, j * GROUP_W:(j + 1) * GROUP_W], preferred_element_type=F32)

    def head_norm(a, g_ref):
        msq = jnp.dot((a * a).astype(BF16), gmat_ref[...], preferred_element_type=F32)
        return a * lax.rsqrt(msq + RMS_EPS) * g_ref[...]

    q_ref[...] = head_norm(proj(0), gq_ref)
    k_ref[...] = head_norm(proj(1), gk_ref)
    v_ref[...] = proj(2)
    mq_ref[...] = proj(3)
    mk_ref[...] = proj(4)
    mv_ref[...] = proj(5)
    mo_ref[...] = proj(6)
    gate_ref[...] = jnp.dot(xb, wg_ref[...], preferred_element_type=F32)


def _in_proj(x2d, g_mix, w_main, w_gate, gq_row, gk_row, gmat):
    n = x2d.shape[0]
    tm = ROW_TILE
    row = lambda w: pl.BlockSpec((tm, w), lambda i: (i, 0))
    outs = [jax.ShapeDtypeStruct((n, GROUP_W), F32)] * N_PROJ + [jax.ShapeDtypeStruct((n, LANES), F32)]
    return pl.pallas_call(
        _in_proj_body,
        grid=(n // tm,),
        in_specs=[row(D_MODEL), _full((1, D_MODEL)), _full((D_MODEL, N_PROJ * GROUP_W)),
                  _full((D_MODEL, LANES)), _full((1, GROUP_W)), _full((1, GROUP_W)),
                  _full((GROUP_W, GROUP_W))],
        out_specs=[row(GROUP_W)] * N_PROJ + [row(LANES)],
        out_shape=outs,
        compiler_params=_params("arbitrary"),
        name="in_proj",
    )(x2d, g_mix, w_main, w_gate, gq_row, gk_row, gmat)


def _attn_body(q_ref, kp_ref, kc_ref, vp_ref, vc_ref, base_ref, hmask_ref, o_ref, kwin, vwin, bias_s,
               *, cq, nq, mask_first):
    tc = cq * nq
    i = pl.program_id(1)
    kwin[0:PAST_BAND, :] = kp_ref[...].astype(BF16)
    kwin[PAST_BAND:PAST_BAND + tc, :] = kc_ref[...].astype(BF16)
    vwin[0:PAST_BAND, :] = vp_ref[...].astype(BF16)
    vwin[PAST_BAND:PAST_BAND + tc, :] = vc_ref[...].astype(BF16)
    pad_rows = kwin.shape[0] - PAST_BAND - tc
    kwin[PAST_BAND + tc:, :] = jnp.zeros((pad_rows, GROUP_W), BF16)
    vwin[PAST_BAND + tc:, :] = jnp.zeros((pad_rows, GROUP_W), BF16)

    hm = hmask_ref[...]
    kk = lax.broadcasted_iota(I32, (1, KEY_WIN), 1)

    @pl.when((pl.program_id(0) == 0) & (i == 0))
    def _():
        for h in range(N_HEADS):
            rows = jnp.broadcast_to(base_ref[h:h + 1, :], (cq, KEY_WIN))
            rows = pltpu.roll(rows, 0, 1, stride=1, stride_axis=0)
            bias_s[h * cq:(h + 1) * cq, :] = jnp.where(kk < PAST_BAND + cq, rows, NEG_BIG)

    bias = bias_s[...]

    def chunk(j, carry):
        r0 = pl.multiple_of(j * cq, cq)
        q = q_ref[pl.ds(r0, cq), :] * (HEAD_DIM ** -0.5)
        qm = (jnp.concatenate([q] * N_HEADS, axis=0) * hm).astype(BF16)
        kw = kwin[pl.ds(r0, KEY_WIN), :]
        s = lax.dot_general(qm, kw, (((1,), (1,)), ((), ())), preferred_element_type=F32) + bias
        if mask_first:
            first_valid = jnp.where(i == 0, PAST_BAND - r0, 0)
            s = jnp.where(kk >= first_valid, s, NEG_BIG)
        m = jnp.max(s, axis=-1, keepdims=True)
        p = jnp.exp(s - m)
        l = jnp.sum(p, axis=-1, keepdims=True)
        vw = vwin[pl.ds(r0, KEY_WIN), :]
        o_all = jnp.dot(p.astype(BF16), vw, preferred_element_type=F32) / l * hm
        o = o_all[0:cq]
        for h in range(1, N_HEADS):
            o = o + o_all[h * cq:(h + 1) * cq]
        o_ref[pl.ds(r0, cq), :] = o
        return carry

    lax.fori_loop(0, nq, chunk, 0)


def _attention(q, k_prev_src, k_cur_src, v_prev_src, v_cur_src, bias, hmask, *, batch, tiles, cq, nq,
               prev_index, mask_first):
    tc = cq * nq
    cur = pl.BlockSpec((tc, GROUP_W), lambda b, i: (b * tiles + i, 0))
    prev = pl.BlockSpec((PAST_BAND, GROUP_W), prev_index)
    win_rows = (nq - 1) * cq + KEY_WIN
    return pl.pallas_call(
        functools.partial(_attn_body, cq=cq, nq=nq, mask_first=mask_first),
        grid=(batch, tiles),
        in_specs=[cur, prev, cur, prev, cur, _full((N_HEADS, KEY_WIN)), _full((N_HEADS * cq, GROUP_W))],
        out_specs=cur,
        out_shape=jax.ShapeDtypeStruct(q.shape, F32),
        scratch_shapes=[pltpu.VMEM((win_rows, GROUP_W), BF16), pltpu.VMEM((win_rows, GROUP_W), BF16),
                        pltpu.VMEM((N_HEADS * cq, KEY_WIN), F32)],
        compiler_params=_params("arbitrary", "arbitrary"),
        name="band_attention",
    )(q, k_prev_src, k_cur_src, v_prev_src, v_cur_src, bias, hmask)


def _rel_base(rel_bias_l, cq):
    nk = PAST_BAND + cq
    dist = np.concatenate([PAST_BAND - np.arange(nk), np.zeros(KEY_WIN - nk - (cq - 1), np.int64),
                           PAST_BAND + np.arange(cq - 1, 0, -1)])
    return rel_bias_l[:, np.clip(dist, -REL_CLIP, REL_CLIP) + REL_CLIP].astype(F32)


def _head_row_mask(cq):
    h_row = np.repeat(np.arange(N_HEADS), cq)
    h_col = np.arange(GROUP_W) // HEAD_DIM
    return jnp.asarray((h_row[:, None] == h_col[None, :]).astype(np.float32))


def _log_sigmoid(x):
    return jnp.minimum(x, 0.0) - jnp.log(1.0 + jnp.exp(-jnp.abs(x)))


def _mlstm_body(q_ref, k_ref, v_ref, o_ref, g_ref, c0_ref, n0_ref, m0_ref,
                expand_ref, gbias_ref, bd_ref, bdb_ref, gmat_ref, ltri_ref, eye_ref, causal_ref, gml_ref,
                h_ref, ct_ref, n_ref, m_ref, ct_s, n_s, m_s, *, valid, nb):
    c = pl.program_id(1)
    last = c == pl.num_programs(1) - 1
    bd = bd_ref[...]
    bdb = bdb_ref[...]
    gmat = gmat_ref[...]
    eye = eye_ref[...] > 0.5
    causal = causal_ref[...] > 0.5
    gate_lane = lax.broadcasted_iota(I32, (CHUNK, LANES), 1)

    @pl.when(c == 0)
    def _():
        for b in range(nb):
            ct_s[b] = jnp.concatenate([c0_ref[b]] * N_HEADS, axis=0) * bd
            n_s[b] = n0_ref[b]
            m_s[b] = m0_ref[b]

    for b in range(nb):
        gates = g_ref[b] + gbias_ref[...]
        gates = jnp.where(gate_lane < N_HEADS, gates, _log_sigmoid(gates))
        gp = jnp.dot(jnp.concatenate(_split3(gates), axis=1), expand_ref[...], preferred_element_type=F32)
        log_i = gp[:, :GROUP_W]
        log_f = gp[:, GROUP_W:]
        if valid < CHUNK:
            live = lax.broadcasted_iota(I32, (CHUNK, GROUP_W), 0) < valid
            log_i = jnp.where(live, log_i, -jnp.inf)
            log_f = jnp.where(live, log_f, 0.0)
        cum_f = jnp.dot(ltri_ref[...], jnp.concatenate(_split3(log_f), axis=0), preferred_element_type=F32)

        b_row = jnp.sum(jnp.where(eye, log_i - cum_f, 0.0), axis=0, keepdims=True)
        m_prev = m_s[b]
        log_inter = cum_f + m_prev
        log_d = jnp.where(causal, cum_f + b_row, -jnp.inf)
        max_d = jnp.concatenate(
            [jnp.broadcast_to(jnp.max(log_d[:, h * HEAD_DIM:(h + 1) * HEAD_DIM], axis=-1, keepdims=True),
                              (CHUNK, HEAD_DIM)) for h in range(N_HEADS)], axis=1)
        m_t = jnp.maximum(log_inter, max_d)
        w_intra = jnp.exp(log_d - m_t)
        w_inter = jnp.exp(log_inter - m_t)

        q = q_ref[b] * (HEAD_DIM ** -0.5)
        k = k_ref[b]
        v = v_ref[b]
        qb = q.astype(BF16)
        vb = v.astype(BF16)
        kbd = jnp.concatenate([k.astype(BF16)] * N_HEADS, axis=0) * bdb
        vbd = jnp.concatenate([vb] * N_HEADS, axis=0) * bdb
        s = lax.dot_general(qb, kbd, (((1,), (1,)), ((), ())), preferred_element_type=F32) * w_intra
        ct = ct_s[b]
        n_prev = n_s[b]
        num = (jnp.dot(s.astype(BF16), vbd, preferred_element_type=F32)
               + w_inter * lax.dot_general(qb, ct.astype(BF16), (((1,), (1,)), ((), ())),
                                           preferred_element_type=F32))
        den_terms = (s + w_inter * q * n_prev) * float(HEAD_DIM)
        den_hi = den_terms.astype(BF16)
        den_lo = (den_terms - den_hi.astype(F32)).astype(BF16)
        den = (jnp.dot(den_hi, gmat, preferred_element_type=F32)
               + jnp.dot(den_lo, gmat, preferred_element_type=F32))
        hb = num / jnp.maximum(jnp.abs(den), jnp.exp(-m_t))

        m_new = m_t[CHUNK - 1:CHUNK, :]
        cum_last = cum_f[CHUNK - 1:CHUNK, :]
        w_state = jnp.exp(cum_last - cum_f + log_i - m_new)
        decay = jnp.exp(cum_last + m_prev - m_new)
        kw = k * w_state
        upd = lax.dot_general(vb, kw.astype(BF16), (((0,), (0,)), ((), ())),
                              preferred_element_type=F32)
        ct_new = decay * ct + upd * bd
        n_new = decay * n_prev + jnp.sum(kw, axis=0, keepdims=True)
        ct_s[b] = ct_new
        n_s[b] = n_new
        m_s[b] = m_new

        msq = jnp.dot((hb * hb).astype(BF16), gmat, preferred_element_type=F32)
        h_ref[b] = jax.nn.sigmoid(o_ref[b]) * (hb * lax.rsqrt(msq + RMS_EPS) * gml_ref[...])

    @pl.when(last)
    def _():
        for b in range(nb):
            ct_new = ct_s[b]
            acc = ct_new[0:HEAD_DIM]
            for h in range(1, N_HEADS):
                acc = acc + ct_new[h * HEAD_DIM:(h + 1) * HEAD_DIM]
            ct_ref[b] = acc
            n_ref[b] = n_s[b]
            m_ref[b] = m_s[b]


def _mlstm(mq, mk, mv, mo, gates, c0t, n0, m0, consts, *, valid):
    batch, t, _ = mq.shape
    chunks = t // CHUNK
    nb = MLSTM_STREAMS
    row = lambda w: pl.BlockSpec((nb, CHUNK, w), lambda g, c: (g, c, 0))
    per_b = lambda r: pl.BlockSpec((nb, r, GROUP_W), lambda g, c: (g, 0, 0))
    expand, gbias, bd, bdb, gmat, ltri, eye, causal, gml = consts
    return pl.pallas_call(
        functools.partial(_mlstm_body, valid=valid, nb=nb),
        grid=(batch // nb, chunks),
        in_specs=[row(GROUP_W)] * 4 + [row(LANES), per_b(HEAD_DIM), per_b(1), per_b(1),
                  _full((3 * LANES, 2 * GROUP_W)), _full((1, LANES)), _full((GROUP_W, GROUP_W)),
                  _full((GROUP_W, GROUP_W)), _full((GROUP_W, GROUP_W)), _full((CHUNK, 3 * CHUNK)),
                  _full((CHUNK, GROUP_W)), _full((CHUNK, GROUP_W)), _full((1, GROUP_W))],
        out_specs=[row(GROUP_W), per_b(HEAD_DIM), per_b(1), per_b(1)],
        out_shape=[jax.ShapeDtypeStruct(mq.shape, F32),
                   jax.ShapeDtypeStruct((batch, HEAD_DIM, GROUP_W), F32),
                   jax.ShapeDtypeStruct((batch, 1, GROUP_W), F32),
                   jax.ShapeDtypeStruct((batch, 1, GROUP_W), F32)],
        scratch_shapes=[pltpu.VMEM((nb, GROUP_W, GROUP_W), F32), pltpu.VMEM((nb, 1, GROUP_W), F32),
                        pltpu.VMEM((nb, 1, GROUP_W), F32)],
        compiler_params=_params("arbitrary", "arbitrary"),
        name="mlstm",
    )(mq, mk, mv, mo, gates, c0t, n0, m0, expand, gbias, bd, bdb, gmat, ltri, eye, causal, gml)


def _mlstm_consts(b_igate_l, b_fgate_l, g_mlstm_l):
    expand = np.zeros((LANES, 2 * GROUP_W), np.float32)
    for h in range(N_HEADS):
        expand[h, h * HEAD_DIM:(h + 1) * HEAD_DIM] = 1.0
        expand[N_HEADS + h, GROUP_W + h * HEAD_DIM:GROUP_W + (h + 1) * HEAD_DIM] = 1.0
    gbias = jnp.concatenate([b_igate_l.astype(F32), b_fgate_l.astype(F32),
                             jnp.zeros((LANES - 2 * N_HEADS,), F32)])[None, :]
    bd = _head_block_diag()
    ltri = np.tril(np.ones((CHUNK, CHUNK), np.float32))
    s_of_lane = np.arange(GROUP_W) % HEAD_DIM
    t = np.arange(CHUNK)
    eye = (t[:, None] == s_of_lane[None, :]).astype(np.float32)
    causal = (s_of_lane[None, :] <= t[:, None]).astype(np.float32)
    return (jnp.asarray(np.concatenate([expand] * 3, axis=0), BF16), gbias, jnp.asarray(bd), jnp.asarray(bd, BF16),
            jnp.asarray(bd / HEAD_DIM, BF16), jnp.asarray(np.concatenate([ltri] * 3, axis=1), BF16), jnp.asarray(eye),
            jnp.asarray(causal), g_mlstm_l.astype(F32).reshape(1, GROUP_W))


def _out_proj_body(xp_ref, xs_ref, ap_ref, as_ref, hp_ref, hs_ref, wa_ref, wm_ref, gffn_ref, wr_ref, br_ref,
                   y_ref, xf_ref, logit_ref, *, prompt_tiles):
    is_prompt = pl.program_id(0) < prompt_tiles
    x = jnp.where(is_prompt, xp_ref[...], xs_ref[...])
    att = jnp.where(is_prompt, ap_ref[...], as_ref[...])
    hm = jnp.where(is_prompt, hp_ref[...], hs_ref[...])
    y = (x + jnp.dot(att.astype(BF16), wa_ref[...], preferred_element_type=F32)
         + jnp.dot(hm.astype(BF16), wm_ref[...], preferred_element_type=F32))
    y_ref[...] = y
    xf = y * lax.rsqrt(jnp.mean(y * y, axis=-1, keepdims=True) + RMS_EPS) * gffn_ref[...]
    xf_ref[...] = xf
    logit_ref[...] = _dot_f32ish(xf, wr_ref[...]) + br_ref[...]


def _out_proj(xp, xs, att_p, att_s, hm_p, hm_s, wa, wm, g_ffn, w_router, b_router):
    tm = ROW_TILE
    pt, st = xp.shape[0] // tm, xs.shape[0] // tm
    n = xp.shape[0] + xs.shape[0]
    p_row = lambda w: pl.BlockSpec((tm, w), lambda i: (jnp.minimum(i, pt - 1), 0))
    s_row = lambda w: pl.BlockSpec((tm, w), lambda i: (jnp.maximum(i - pt, 0), 0))
    row = lambda w: pl.BlockSpec((tm, w), lambda i: (i, 0))
    return pl.pallas_call(
        functools.partial(_out_proj_body, prompt_tiles=pt),
        grid=(pt + st,),
        in_specs=[p_row(D_MODEL), s_row(D_MODEL), p_row(GROUP_W), s_row(GROUP_W), p_row(GROUP_W), s_row(GROUP_W),
                  _full((GROUP_W, D_MODEL)), _full((GROUP_W, D_MODEL)),
                  _full((1, D_MODEL)), _full((3 * D_MODEL, LANES)), _full((1, LANES))],
        out_specs=[row(D_MODEL), row(D_MODEL), row(LANES)],
        out_shape=[jax.ShapeDtypeStruct((n, D_MODEL), F32), jax.ShapeDtypeStruct((n, D_MODEL), F32),
                   jax.ShapeDtypeStruct((n, LANES), F32)],
        compiler_params=_params("arbitrary"),
        name="out_proj_router",
    )(xp, xs, att_p, att_s, hm_p, hm_s, wa, wm, g_ffn, w_router, b_router)


def _route_body(logit_ref, lstrict_ref, ustrict_ref, gate_ref, pos_ref, seg_ref):
    tt = logit_ref.shape[0]
    lane = lax.broadcasted_iota(I32, (tt, LANES), 1)
    work = jnp.where(lane < N_EXPERTS, logit_ref[...], -jnp.inf)
    vals, idxs = [], []
    for _ in range(TOP_K):
        m = jnp.max(work, axis=-1, keepdims=True)
        idx = jnp.min(jnp.where(work == m, lane, LANES), axis=-1, keepdims=True)
        vals.append(m)
        idxs.append(idx)
        work = jnp.where(lane == idx, -jnp.inf, work)
    exps = [jnp.exp(v - vals[0]) for v in vals]
    total = exps[0] + exps[1] + exps[2] + exps[3]

    chosen = jnp.zeros((tt, LANES), F32)
    for idx in idxs:
        chosen = chosen + (lane == idx).astype(F32)
    before = jnp.dot(lstrict_ref[...], chosen.astype(BF16), preferred_element_type=F32)
    count = jnp.sum(chosen, axis=0, keepdims=True)
    groups = jnp.floor((count + (SUBLANES - 1)) * (1.0 / SUBLANES))
    groups8 = jnp.broadcast_to(groups, (SUBLANES, LANES)).astype(BF16)
    start = jnp.dot(groups8, ustrict_ref[...], preferred_element_type=F32) * float(SUBLANES)
    local = before + start[0:1, :]

    gate_out = jnp.zeros((tt, LANES), F32)
    pos_out = jnp.zeros((tt, LANES), F32)
    for k in range(TOP_K):
        pos = jnp.sum(jnp.where(lane == idxs[k], local, 0.0), axis=-1, keepdims=True)
        gate_out = jnp.where(lane == k, exps[k] / total, gate_out)
        pos_out = jnp.where(lane == k, pos, pos_out)
    gate_ref[...] = gate_out
    pos_ref[...] = pos_out
    row = lax.broadcasted_iota(I32, (SUBLANES, LANES), 0)
    seg = jnp.where(row == 0, groups * float(SUBLANES), jnp.where(row == 1, start, 0.0))
    seg_ref[...] = seg.astype(I32)


def _route(logits):
    n = logits.shape[0]
    tt = ROW_TILE
    lstrict = jnp.asarray(np.tril(np.ones((tt, tt), np.float32), -1), BF16)
    ustrict = jnp.asarray(np.triu(np.ones((LANES, LANES), np.float32), 1), BF16)
    row = lambda: pl.BlockSpec((tt, LANES), lambda i: (i, 0))
    return pl.pallas_call(
        _route_body,
        grid=(n // tt,),
        in_specs=[row(), _full((tt, tt)), _full((LANES, LANES))],
        out_specs=[row(), row(), pl.BlockSpec((SUBLANES, LANES), lambda i: (i, 0))],
        out_shape=[jax.ShapeDtypeStruct((n, LANES), F32), jax.ShapeDtypeStruct((n, LANES), F32),
                   jax.ShapeDtypeStruct((n // tt * SUBLANES, LANES), I32)],
        compiler_params=_params("arbitrary"),
        name="route_topk",
    )(logits, lstrict, ustrict)


def _segment_copies(i, len_ref, lst_ref, off_ref, make_copy, act):
    for e in range(N_EXPERTS):
        seg = i * N_EXPERTS + e
        length = len_ref[seg]
        local = lst_ref[seg]
        glob = off_ref[seg]
        for size in SEG_SIZES:
            take = length & size

            @pl.when(take != 0)
            def _(local=local, glob=glob, size=size):
                act(make_copy(pl.multiple_of(local, SUBLANES), pl.multiple_of(glob, SUBLANES), size))

            local = local + take
            glob = glob + take


def _local_onehot(pos_rows, base, rows, values=None):
    r = (lax.broadcasted_iota(I32, (rows, pos_rows[0].shape[1]), 0) + base).astype(F32)
    out = jnp.zeros(r.shape, F32)
    for k in range(TOP_K):
        out = jnp.where(r == pos_rows[k], 1.0 if values is None else values[k], out)
    return out


def _dispatch_body(len_ref, lst_ref, off_ref, fill_ref, nused_ref, xf_ref, pos_ref, xs_hbm,
                   xloc, zbuf, sem, zsem, *, tm, n_tiles):
    i = pl.program_id(0)
    tt = xf_ref.shape[0]
    fill_rows = zbuf.shape[0]

    @pl.when(i == 0)
    def _():
        zbuf[...] = jnp.zeros_like(zbuf)

        def fill(e):
            start = pl.multiple_of(fill_ref[e], SUBLANES)
            return pltpu.make_async_copy(zbuf, xs_hbm.at[pl.ds(start, fill_rows)], zsem)

        for e in range(N_EXPERTS):
            fill(e).start()
            fill(e).wait()

        def tail(j, carry):
            cp = pltpu.make_async_copy(zbuf.at[pl.ds(0, tm)], xs_hbm.at[pl.ds(pl.multiple_of(j * tm, tm), tm)], zsem)
            cp.start()
            cp.wait()
            return carry

        lax.fori_loop(nused_ref[0], n_tiles, tail, 0)

    pos_t = jnp.transpose(pos_ref[...])
    pos_rows = [pos_t[k:k + 1, :] for k in range(TOP_K)]
    xb = xf_ref[...].astype(BF16)

    slot = lax.rem(i, 2)

    def sort_rows(c, carry):
        r0 = pl.multiple_of(c * SORT_CHUNK, SORT_CHUNK)
        sel = _local_onehot(pos_rows, r0, SORT_CHUNK).astype(BF16)
        xloc[slot, pl.ds(r0, SORT_CHUNK), :] = jnp.dot(sel, xb, preferred_element_type=F32)
        return carry

    lax.fori_loop(0, LOCAL_ROWS // SORT_CHUNK, sort_rows, 0)

    def copies(step, which):
        def make_copy(local, glob, size):
            return pltpu.make_async_copy(xloc.at[which, pl.ds(local, size)], xs_hbm.at[pl.ds(glob, size)],
                                         sem.at[which])
        return functools.partial(_segment_copies, step, len_ref, lst_ref, off_ref, make_copy)

    copies(i, slot)(lambda cp: cp.start())

    @pl.when(i > 0)
    def _():
        copies(i - 1, 1 - slot)(lambda cp: cp.wait())

    @pl.when(i == pl.num_programs(0) - 1)
    def _():
        copies(i, slot)(lambda cp: cp.wait())


def _dispatch(seg_len, seg_local, seg_off, fill_start, n_used, xf, pos, n_tiles):
    n = xf.shape[0]
    tt, tm = ROW_TILE, EXPERT_TILE
    fill_rows = tm + SUBLANES
    grid_spec = pltpu.PrefetchScalarGridSpec(
        num_scalar_prefetch=5,
        grid=(n // tt,),
        in_specs=[pl.BlockSpec((tt, D_MODEL), lambda i, *_: (i, 0)),
                  pl.BlockSpec((tt, LANES), lambda i, *_: (i, 0))],
        out_specs=pl.BlockSpec(memory_space=pl.ANY),
        scratch_shapes=[pltpu.VMEM((2, LOCAL_ROWS, D_MODEL), F32), pltpu.VMEM((fill_rows, D_MODEL), F32),
                        pltpu.SemaphoreType.DMA((2,)), pltpu.SemaphoreType.DMA(())],
    )
    return pl.pallas_call(
        functools.partial(_dispatch_body, tm=tm, n_tiles=n_tiles + 2),
        grid_spec=grid_spec,
        out_shape=jax.ShapeDtypeStruct(((n_tiles + 2) * tm, D_MODEL), F32),
        compiler_params=_params("arbitrary"),
        name="expert_dispatch",
    )(seg_len, seg_local, seg_off, fill_start, n_used, xf, pos)


def _expert_body(te_ref, nused_ref, x_ref, wup_ref, wdn_ref, perm_ref, bg_ref, bl_ref, bd_ref,
                 y_ref, wg_s, wl_s, wd_s):
    i = pl.program_id(0)
    n_used = nused_ref[0]

    @pl.when(i >= n_used)
    def _():
        y_ref[...] = jnp.zeros_like(y_ref)

    @pl.when((i == 0) | (te_ref[i] != te_ref[jnp.maximum(i - 1, 0)]))
    def _():
        perm = perm_ref[...]
        for c in range(2 * D_FF // 256):
            blk = wup_ref[0, :, c * 256:(c + 1) * 256].astype(BF16)
            sep = jnp.dot(blk, perm, preferred_element_type=F32).astype(BF16)
            wg_s[:, c * 128:(c + 1) * 128] = sep[:, :128]
            wl_s[:, c * 128:(c + 1) * 128] = sep[:, 128:]
        wd_s[...] = wdn_ref[0].astype(BF16)

    @pl.when(i < n_used)
    def _():
        x = x_ref[...].astype(BF16)
        glu = jnp.minimum(jnp.dot(x, wg_s[...], preferred_element_type=F32) + bg_ref[0], SWIGLU_LIMIT)
        lin = jnp.clip(jnp.dot(x, wl_s[...], preferred_element_type=F32) + bl_ref[0], -SWIGLU_LIMIT, SWIGLU_LIMIT)
        act = glu * jax.nn.sigmoid(SWIGLU_ALPHA * glu) * (lin + 1.0)
        y_ref[...] = jnp.dot(act.astype(BF16), wd_s[...], preferred_element_type=F32) + bd_ref[0]


def _deinterleave_perm():
    p = np.zeros((256, 256), np.float32)
    j = np.arange(128)
    p[2 * j, j] = 1.0
    p[2 * j + 1, 128 + j] = 1.0
    return jnp.asarray(p, BF16)


def _experts(tile_expert, n_used, x_sorted, w_up, w_down, b_glu, b_lin, b_down):
    tm = EXPERT_TILE
    n_tiles = tile_expert.shape[0]
    wspec = lambda k, n: pl.BlockSpec((1, k, n), lambda i, te, nu: (te[i], 0, 0))
    grid_spec = pltpu.PrefetchScalarGridSpec(
        num_scalar_prefetch=2,
        grid=(n_tiles,),
        in_specs=[pl.BlockSpec((tm, D_MODEL), lambda i, te, nu: (jnp.minimum(i, nu[0] - 1), 0)),
                  wspec(D_MODEL, 2 * D_FF), wspec(D_FF, D_MODEL),
                  pl.BlockSpec((256, 256), lambda i, te, nu: (0, 0)),
                  wspec(1, D_FF), wspec(1, D_FF), wspec(1, D_MODEL)],
        out_specs=pl.BlockSpec((tm, D_MODEL), lambda i, te, nu: (i, 0)),
        scratch_shapes=[pltpu.VMEM((D_MODEL, D_FF), BF16), pltpu.VMEM((D_MODEL, D_FF), BF16),
                        pltpu.VMEM((D_FF, D_MODEL), BF16)],
    )
    return pl.pallas_call(
        _expert_body,
        grid_spec=grid_spec,
        out_shape=jax.ShapeDtypeStruct((n_tiles * tm, D_MODEL), F32),
        compiler_params=_params("arbitrary"),
        name="expert_ffn",
    )(tile_expert, n_used, x_sorted, w_up, w_down, _deinterleave_perm(), b_glu, b_lin, b_down)


def _combine_body(len_ref, lst_ref, off_ref, y_ref, gate_ref, pos_ref, rows_hbm, outp_ref, outs_ref,
                  yloc, acc_s, wide_s, sem, *, prompt_tiles):
    i = pl.program_id(0)
    slot = lax.rem(i, 2)
    tt = y_ref.shape[0]

    def copies(step, which):
        def make_copy(local, glob, size):
            return pltpu.make_async_copy(rows_hbm.at[pl.ds(glob, size)], yloc.at[which, pl.ds(local, size)],
                                         sem.at[which])
        return functools.partial(_segment_copies, step, len_ref, lst_ref, off_ref, make_copy)

    @pl.when(i == 0)
    def _():
        yloc[...] = jnp.zeros_like(yloc)
        copies(i, slot)(lambda cp: cp.start())

    @pl.when(i + 1 < pl.num_programs(0))
    def _():
        copies(i + 1, 1 - slot)(lambda cp: cp.start())

    pos = pos_ref[...]
    gate = gate_ref[...]
    for k in range(TOP_K):
        wide_s[k] = jnp.broadcast_to(pos[:, k:k + 1], (tt, LANES))
        wide_s[TOP_K + k] = jnp.broadcast_to(gate[:, k:k + 1], (tt, LANES))
    acc_s[...] = y_ref[...]
    copies(i, slot)(lambda cp: cp.wait())

    lane = lax.broadcasted_iota(I32, (tt, LANES), 1).astype(F32)

    def weigh(c, carry):
        r0 = pl.multiple_of(c * SORT_CHUNK, SORT_CHUNK)
        halves = []
        for half in range(SORT_CHUNK // LANES):
            r = lane + (r0 + half * LANES).astype(F32)
            w = jnp.zeros((tt, LANES), F32)
            for k in range(TOP_K):
                w = jnp.where(r == wide_s[k], wide_s[TOP_K + k], w)
            halves.append(w.astype(BF16))
        w = jnp.concatenate(halves, axis=1)
        acc_s[...] += jnp.dot(w, yloc[slot, pl.ds(r0, SORT_CHUNK), :].astype(BF16), preferred_element_type=F32)
        return carry

    lax.fori_loop(0, LOCAL_ROWS // SORT_CHUNK, weigh, 0)

    @pl.when(i < prompt_tiles)
    def _():
        outp_ref[...] = acc_s[...]

    @pl.when(i >= prompt_tiles)
    def _():
        outs_ref[...] = acc_s[...]


def _combine(seg_len, seg_local, seg_off, y, gates, pos, y_rows, n_prompt):
    n = y.shape[0]
    tt = ROW_TILE
    nt, pt = n // tt, n_prompt // tt
    grid_spec = pltpu.PrefetchScalarGridSpec(
        num_scalar_prefetch=3,
        grid=(nt,),
        in_specs=[pl.BlockSpec((tt, D_MODEL), lambda i, *_: (i, 0)),
                  pl.BlockSpec((tt, LANES), lambda i, *_: (i, 0)),
                  pl.BlockSpec((tt, LANES), lambda i, *_: (i, 0)),
                  pl.BlockSpec(memory_space=pl.ANY)],
        out_specs=[pl.BlockSpec((tt, D_MODEL), lambda i, *_: (jnp.minimum(i, pt - 1), 0)),
                   pl.BlockSpec((tt, D_MODEL), lambda i, *_: (jnp.maximum(i - pt, 0), 0))],
        scratch_shapes=[pltpu.VMEM((2, LOCAL_ROWS, D_MODEL), F32), pltpu.VMEM((tt, D_MODEL), F32),
                        pltpu.VMEM((2 * TOP_K, tt, LANES), F32), pltpu.SemaphoreType.DMA((2,))],
    )
    return pl.pallas_call(
        functools.partial(_combine_body, prompt_tiles=pt),
        grid_spec=grid_spec,
        out_shape=[jax.ShapeDtypeStruct((n_prompt, D_MODEL), F32),
                   jax.ShapeDtypeStruct((n - n_prompt, D_MODEL), F32)],
        compiler_params=_params("arbitrary"),
        name="expert_combine",
    )(seg_len, seg_local, seg_off, y, gates, pos, y_rows)


def _moe(y, xf, logits, ffn_w, n_prompt):
    n = y.shape[0]
    tm, tt = EXPERT_TILE, ROW_TILE
    n_tt = n // tt
    gates, pos, seg = _route(logits)
    seg = seg.reshape(n_tt, SUBLANES, LANES)
    seg_len, seg_local = seg[:, 0, :N_EXPERTS], seg[:, 1, :N_EXPERTS]
    rows = jnp.sum(seg_len, axis=0)
    padded = (rows + tm - 1) // tm * tm
    pad_end = jnp.cumsum(padded)
    pad_start = pad_end - padded
    seg_off = pad_start[None, :] + jnp.cumsum(seg_len, axis=0) - seg_len
    n_tiles = -(-(n * TOP_K + n_tt * N_EXPERTS * (SUBLANES - 1) + N_EXPERTS * (tm - 1)) // tm)
    tile_expert = jnp.minimum(jnp.sum(pad_end[None, :] <= (jnp.arange(n_tiles) * tm)[:, None], axis=1),
                              N_EXPERTS - 1).astype(I32)
    n_used = (pad_end[-1:] // tm).astype(I32)
    fill_start = (pad_start + rows).astype(I32)
    flat = lambda a: a.astype(I32).reshape(n_tt * N_EXPERTS)

    x_sorted = _dispatch(flat(seg_len), flat(seg_local), flat(seg_off), fill_start, n_used, xf, pos, n_tiles)
    y_rows = _experts(tile_expert, n_used, x_sorted, *ffn_w)
    return _combine(flat(seg_len), flat(seg_local), flat(seg_off), y, gates, pos, y_rows, n_prompt)


def _mixer(x, lw, cache, state):
    b, t, _ = x.shape
    n = b * t
    q, k, v, mq, mk, mv, mo, gates = _in_proj(x.reshape(n, D_MODEL), lw["g_mix"], lw["w_main"], lw["w_gate"],
                                              lw["gq"], lw["gk"], lw["gmat"])
    heads = lambda a, rows: a.reshape(b, rows, N_HEADS, HEAD_DIM)
    if cache is None:
        tiles = t // PAST_BAND
        att = _attention(q, k, k, v, v, lw["bias_prompt"], lw["hmask_prompt"], batch=b, tiles=tiles, cq=CHUNK,
                         nq=PAST_BAND // CHUNK,
                         prev_index=lambda bi, i: (bi * tiles + jnp.maximum(i - 1, 0), 0), mask_first=True)
        keep = min(PAST_BAND, t)
        k_new = heads(k.reshape(b, t, GROUP_W)[:, t - keep:], keep)
        v_new = heads(v.reshape(b, t, GROUP_W)[:, t - keep:], keep)
    else:
        ck, cv = cache
        att = _attention(q, ck.reshape(b * PAST_BAND, GROUP_W), k, cv.reshape(b * PAST_BAND, GROUP_W), v,
                         lw["bias_sample"], lw["hmask_sample"], batch=b, tiles=1, cq=t, nq=1,
                         prev_index=lambda bi, i: (bi, 0), mask_first=False)
        k_new, v_new = heads(k, t), heads(v, t)

    tp = -(-t // CHUNK) * CHUNK
    valid = t if t < CHUNK else CHUNK

    def streams(a):
        a = a.reshape(b, t, -1)
        return a if tp == t else jnp.pad(a, ((0, 0), (0, tp - t), (0, 0)))

    if state is None:
        c0t = jnp.zeros((b, HEAD_DIM, GROUP_W), F32)
        n0 = jnp.zeros((b, 1, GROUP_W), F32)
        m0 = jnp.zeros((b, 1, GROUP_W), F32)
    else:
        c_in, n_in, m_in = state
        c0t = c_in.astype(F32).transpose(0, 3, 1, 2).reshape(b, HEAD_DIM, GROUP_W)
        n0 = n_in.astype(F32).reshape(b, 1, GROUP_W)
        m0 = jnp.repeat(m_in.astype(F32), HEAD_DIM, axis=-1).reshape(b, 1, GROUP_W)
    hm, ct, n_out, m_out = _mlstm(streams(mq), streams(mk), streams(mv), streams(mo), streams(gates),
                                  c0t, n0, m0, lw["mlstm_consts"], valid=valid)
    hm = hm[:, :t].reshape(n, GROUP_W)
    c_new = ct.reshape(b, HEAD_DIM, N_HEADS, HEAD_DIM).transpose(0, 2, 3, 1)
    n_new = n_out.reshape(b, N_HEADS, HEAD_DIM)
    m_new = m_out.reshape(b, N_HEADS, HEAD_DIM)[:, :, 0]
    return att, hm, (k_new, v_new, c_new, n_new, m_new)


def kernel(x_prompt, x_sample, cache_k, cache_v, state_C, state_n, state_m, g_mix, w_in, g_q, g_k, rel_bias,
           b_igate, b_fgate, g_mlstm, w_out, g_ffn, w_router, b_router, w_up, b_up, w_down, b_down):
    depth = w_in.shape[0]
    yp, ys = x_prompt, x_sample
    bs, ts = x_sample.shape[0], x_sample.shape[1]
    n_prompt = x_prompt.shape[0] * x_prompt.shape[1]
    st_prompt, st_sample = [], []
    n_main = N_PROJ * GROUP_W
    gmat = jnp.asarray(_head_block_diag() / HEAD_DIM, BF16)
    for l in range(depth):
        lw = dict(
            g_mix=g_mix[l].astype(F32)[None, :],
            w_main=w_in[l][:, :n_main].astype(BF16),
            w_gate=jnp.pad(w_in[l][:, n_main:], ((0, 0), (0, LANES - 2 * N_HEADS))).astype(BF16),
            gq=jnp.tile(g_q[l].astype(F32), N_HEADS)[None, :],
            gk=jnp.tile(g_k[l].astype(F32), N_HEADS)[None, :],
            gmat=gmat,
            bias_prompt=_rel_base(rel_bias[l], CHUNK),
            hmask_prompt=_head_row_mask(CHUNK),
            bias_sample=_rel_base(rel_bias[l], ts),
            hmask_sample=_head_row_mask(ts),
            mlstm_consts=_mlstm_consts(b_igate[l], b_fgate[l], g_mlstm[l]),
        )
        ffn_w = (w_up[l].astype(F32), w_down[l].astype(F32),
                 b_up[l][:, None, 0::2].astype(F32), b_up[l][:, None, 1::2].astype(F32),
                 b_down[l][:, None, :].astype(F32))
        att_p, hm_p, sp = _mixer(yp, lw, None, None)
        cache = (cache_k[l].reshape(bs, PAST_BAND, GROUP_W), cache_v[l].reshape(bs, PAST_BAND, GROUP_W))
        att_s, hm_s, ss = _mixer(ys, lw, cache, (state_C[l], state_n[l], state_m[l]))
        y, xf, logits = _out_proj(
            yp.reshape(-1, D_MODEL), ys.reshape(-1, D_MODEL), att_p, att_s, hm_p, hm_s,
            w_out[l][:GROUP_W].astype(BF16), w_out[l][GROUP_W:].astype(BF16), g_ffn[l].astype(F32)[None, :],
            _stack_hi_lo(jnp.pad(w_router[l].astype(F32), ((0, 0), (0, LANES - N_EXPERTS)))),
            jnp.pad(b_router[l].astype(F32), (0, LANES - N_EXPERTS))[None, :])
        out_p, out_s = _moe(y, xf, logits, ffn_w, n_prompt)
        yp, ys = out_p.reshape(x_prompt.shape), out_s.reshape(x_sample.shape)
        st_prompt.append(sp)
        st_sample.append(ss)
    k_p, v_p, c_p, n_p, m_p = [jnp.stack(a) for a in zip(*st_prompt)]
    k_s, v_s, c_s, n_s, m_s = [jnp.stack(a) for a in zip(*st_sample)]
    return (yp, ys, k_p, v_p, c_p, n_p, m_p, k_s, v_s, c_s, n_s, m_s)
```

```python
import functools

import numpy as np
import jax
import jax.numpy as jnp
from jax import lax
from jax.experimental import pallas as pl
from jax.experimental.pallas import tpu as pltpu

F32 = jnp.float32
BF16 = jnp.bfloat16
I32 = jnp.int32
HIGHEST = lax.Precision.HIGHEST

D_MODEL = 1024
N_HEADS = 8
HEAD_DIM = 64
GROUP_W = N_HEADS * HEAD_DIM
N_PROJ = 7
LANES = 128
CHUNK = 64
PAST_BAND = 512
KEY_WIN = 640
REL_CLIP = 256
N_EXPERTS = 32
TOP_K = 4
D_FF = 1024
SWIGLU_ALPHA = 1.702
SWIGLU_LIMIT = 7.0
RMS_EPS = 1e-6
NEG_BIG = -1e30
ROW_TILE = 512
EXPERT_TILE = 512
SUBLANES = 8
SORT_CHUNK = 256
LOCAL_ROWS = -(-(ROW_TILE * TOP_K + N_EXPERTS * (SUBLANES - 1)) // SORT_CHUNK) * SORT_CHUNK
SEG_SIZES = (512, 256, 128, 64, 32, 16, 8)
MLSTM_STREAMS = 4
VMEM_LIMIT_BYTES = 56 * 1024 * 1024


def _params(*sem):
    return pltpu.CompilerParams(dimension_semantics=sem, vmem_limit_bytes=VMEM_LIMIT_BYTES)


def _head_block_diag():
    h = np.arange(GROUP_W) // HEAD_DIM
    return (h[:, None] == h[None, :]).astype(np.float32)


def _full(shape):
    return pl.BlockSpec(shape, lambda *_: (0,) * len(shape))


def _split3(x):
    hi = x.astype(BF16)
    r = x - hi.astype(F32)
    mid = r.astype(BF16)
    lo = (r - mid.astype(F32)).astype(BF16)
    return hi, mid, lo


def _dot_f32ish(x, w_stack):
    hi = x.astype(BF16)
    lo = (x - hi.astype(F32)).astype(BF16)
    return jnp.dot(jnp.concatenate([hi, lo, hi], axis=1), w_stack, preferred_element_type=F32)


def _stack_hi_lo(w):
    hi = w.astype(BF16)
    lo = (w - hi.astype(F32)).astype(BF16)
    return jnp.concatenate([hi, hi, lo], axis=0)


def _in_proj_body(x_ref, gmix_ref, w_ref, wg_ref, gq_ref, gk_ref, gmat_ref,
                  q_ref, k_ref, v_ref, mq_ref, mk_ref, mv_ref, mo_ref, gate_ref):
    x = x_ref[...]
    xn = x * lax.rsqrt(jnp.mean(x * x, axis=-1, keepdims=True) + RMS_EPS) * gmix_ref[...]
    xb = xn.astype(BF16)

    def proj(j):
        return jnp.dot(xb, w_ref[:, j * GROUP_W:(j + 1) * GROUP_W], preferred_element_type=F32)

    def head_norm(a, g_ref):
        msq = jnp.dot((a * a).astype(BF16), gmat_ref[...], preferred_element_type=F32)
        return a * lax.rsqrt(msq + RMS_EPS) * g_ref[...]

    q_ref[...] = head_norm(proj(0), gq_ref)
    k_ref[...] = head_norm(proj(1), gk_ref)
    v_ref[...] = proj(2)
    mq_ref[...] = proj(3)
    mk_ref[...] = proj(4)
    mv_ref[...] = proj(5)
    mo_ref[...] = proj(6)
    gate_ref[...] = jnp.dot(xb, wg_ref[...], preferred_element_type=F32)


def _in_proj(x2d, g_mix, w_main, w_gate, gq_row, gk_row, gmat):
    n = x2d.shape[0]
    tm = ROW_TILE
    row = lambda w: pl.BlockSpec((tm, w), lambda i: (i, 0))
    outs = [jax.ShapeDtypeStruct((n, GROUP_W), F32)] * N_PROJ + [jax.ShapeDtypeStruct((n, LANES), F32)]
    return pl.pallas_call(
        _in_proj_body,
        grid=(n // tm,),
        in_specs=[row(D_MODEL), _full((1, D_MODEL)), _full((D_MODEL, N_PROJ * GROUP_W)),
                  _full((D_MODEL, LANES)), _full((1, GROUP_W)), _full((1, GROUP_W)),
                  _full((GROUP_W, GROUP_W))],
        out_specs=[row(GROUP_W)] * N_PROJ + [row(LANES)],
        out_shape=outs,
        compiler_params=_params("arbitrary"),
        name="in_proj",
    )(x2d, g_mix, w_main, w_gate, gq_row, gk_row, gmat)


def _attn_body(q_ref, kp_ref, kc_ref, vp_ref, vc_ref, base_ref, hmask_ref, o_ref, kwin, vwin, bias_s,
               *, cq, nq, mask_first):
    tc = cq * nq
    i = pl.program_id(1)
    kwin[0:PAST_BAND, :] = kp_ref[...].astype(BF16)
    kwin[PAST_BAND:PAST_BAND + tc, :] = kc_ref[...].astype(BF16)
    vwin[0:PAST_BAND, :] = vp_ref[...].astype(BF16)
    vwin[PAST_BAND:PAST_BAND + tc, :] = vc_ref[...].astype(BF16)
    pad_rows = kwin.shape[0] - PAST_BAND - tc
    kwin[PAST_BAND + tc:, :] = jnp.zeros((pad_rows, GROUP_W), BF16)
    vwin[PAST_BAND + tc:, :] = jnp.zeros((pad_rows, GROUP_W), BF16)

    hm = hmask_ref[...]
    kk = lax.broadcasted_iota(I32, (1, KEY_WIN), 1)

    @pl.when((pl.program_id(0) == 0) & (i == 0))
    def _():
        for h in range(N_HEADS):
            rows = jnp.broadcast_to(base_ref[h:h + 1, :], (cq, KEY_WIN))
            rows = pltpu.roll(rows, 0, 1, stride=1, stride_axis=0)
            bias_s[h * cq:(h + 1) * cq, :] = jnp.where(kk < PAST_BAND + cq, rows, NEG_BIG)

    bias = bias_s[...]

    def chunk(j, carry):
        r0 = pl.multiple_of(j * cq, cq)
        q = q_ref[pl.ds(r0, cq), :] * (HEAD_DIM ** -0.5)
        qm = (jnp.concatenate([q] * N_HEADS, axis=0) * hm).astype(BF16)
        kw = kwin[pl.ds(r0, KEY_WIN), :]
        s = lax.dot_general(qm, kw, (((1,), (1,)), ((), ())), preferred_element_type=F32) + bias
        if mask_first:
            first_valid = jnp.where(i == 0, PAST_BAND - r0, 0)
            s = jnp.where(kk >= first_valid, s, NEG_BIG)
        m = jnp.max(s, axis=-1, keepdims=True)
        p = jnp.exp(s - m)
        l = jnp.sum(p, axis=-1, keepdims=True)
        vw = vwin[pl.ds(r0, KEY_WIN), :]
        o_all = jnp.dot(p.astype(BF16), vw, preferred_element_type=F32) / l * hm
        o = o_all[0:cq]
        for h in range(1, N_HEADS):
            o = o + o_all[h * cq:(h + 1) * cq]
        o_ref[pl.ds(r0, cq), :] = o
        return carry

    lax.fori_loop(0, nq, chunk, 0)


def _attention(q, k_prev_src, k_cur_src, v_prev_src, v_cur_src, bias, hmask, *, batch, tiles, cq, nq,
               prev_index, mask_first):
    tc = cq * nq
    cur = pl.BlockSpec((tc, GROUP_W), lambda b, i: (b * tiles + i, 0))
    prev = pl.BlockSpec((PAST_BAND, GROUP_W), prev_index)
    win_rows = (nq - 1) * cq + KEY_WIN
    return pl.pallas_call(
        functools.partial(_attn_body, cq=cq, nq=nq, mask_first=mask_first),
        grid=(batch, tiles),
        in_specs=[cur, prev, cur, prev, cur, _full((N_HEADS, KEY_WIN)), _full((N_HEADS * cq, GROUP_W))],
        out_specs=cur,
        out_shape=jax.ShapeDtypeStruct(q.shape, F32),
        scratch_shapes=[pltpu.VMEM((win_rows, GROUP_W), BF16), pltpu.VMEM((win_rows, GROUP_W), BF16),
                        pltpu.VMEM((N_HEADS * cq, KEY_WIN), F32)],
        compiler_params=_params("arbitrary", "arbitrary"),
        name="band_attention",
    )(q, k_prev_src, k_cur_src, v_prev_src, v_cur_src, bias, hmask)


def _rel_base(rel_bias_l, cq):
    nk = PAST_BAND + cq
    dist = np.concatenate([PAST_BAND - np.arange(nk), np.zeros(KEY_WIN - nk - (cq - 1), np.int64),
                           PAST_BAND + np.arange(cq - 1, 0, -1)])
    return rel_bias_l[:, np.clip(dist, -REL_CLIP, REL_CLIP) + REL_CLIP].astype(F32)


def _head_row_mask(cq):
    h_row = np.repeat(np.arange(N_HEADS), cq)
    h_col = np.arange(GROUP_W) // HEAD_DIM
    return jnp.asarray((h_row[:, None] == h_col[None, :]).astype(np.float32))


def _log_sigmoid(x):
    return jnp.minimum(x, 0.0) - jnp.log(1.0 + jnp.exp(-jnp.abs(x)))


def _mlstm_body(q_ref, k_ref, v_ref, o_ref, g_ref, c0_ref, n0_ref, m0_ref,
                expand_ref, gbias_ref, bd_ref, bdb_ref, gmat_ref, ltri_ref, eye_ref, causal_ref, gml_ref,
                h_ref, ct_ref, n_ref, m_ref, ct_s, n_s, m_s, *, valid, nb):
    c = pl.program_id(1)
    last = c == pl.num_programs(1) - 1
    bd = bd_ref[...]
    bdb = bdb_ref[...]
    gmat = gmat_ref[...]
    eye = eye_ref[...] > 0.5
    causal = causal_ref[...] > 0.5
    gate_lane = lax.broadcasted_iota(I32, (nb * CHUNK, LANES), 1)

    @pl.when(c == 0)
    def _():
        for b in range(nb):
            ct_s[b] = jnp.concatenate([c0_ref[b]] * N_HEADS, axis=0) * bd
            n_s[b] = n0_ref[b]
            m_s[b] = m0_ref[b]

    rows_of = lambda a, b: a[b * CHUNK:(b + 1) * CHUNK]

    gates = jnp.concatenate([g_ref[b] for b in range(nb)], axis=0) + gbias_ref[...]
    gates = jnp.where(gate_lane < N_HEADS, gates, _log_sigmoid(gates))
    gp_all = jnp.dot(jnp.concatenate(_split3(gates), axis=1), expand_ref[...], preferred_element_type=F32)

    nums, den_parts, m_ts = [], [], []
    for b in range(nb):
        gp = rows_of(gp_all, b)
        log_i = gp[:, :GROUP_W]
        log_f = gp[:, GROUP_W:]
        if valid < CHUNK:
            live = lax.broadcasted_iota(I32, (CHUNK, GROUP_W), 0) < valid
            log_i = jnp.where(live, log_i, -jnp.inf)
            log_f = jnp.where(live, log_f, 0.0)
        cum_f = jnp.dot(ltri_ref[...], jnp.concatenate(_split3(log_f), axis=0), preferred_element_type=F32)

        b_row = jnp.sum(jnp.where(eye, log_i - cum_f, 0.0), axis=0, keepdims=True)
        m_prev = m_s[b]
        log_inter = cum_f + m_prev
        log_d = jnp.where(causal, cum_f + b_row, -jnp.inf)
        max_d = jnp.concatenate(
            [jnp.broadcast_to(jnp.max(log_d[:, h * HEAD_DIM:(h + 1) * HEAD_DIM], axis=-1, keepdims=True),
                              (CHUNK, HEAD_DIM)) for h in range(N_HEADS)], axis=1)
        m_t = jnp.maximum(log_inter, max_d)
        w_intra = jnp.exp(log_d - m_t)
        w_inter = jnp.exp(log_inter - m_t)

        q = q_ref[b] * (HEAD_DIM ** -0.5)
        k = k_ref[b]
        v = v_ref[b]
        qb = q.astype(BF16)
        vb = v.astype(BF16)
        kbd = jnp.concatenate([k.astype(BF16)] * N_HEADS, axis=0) * bdb
        vbd = jnp.concatenate([vb] * N_HEADS, axis=0) * bdb
        s = lax.dot_general(qb, kbd, (((1,), (1,)), ((), ())), preferred_element_type=F32) * w_intra
        ct = ct_s[b]
        n_prev = n_s[b]
        num = (jnp.dot(s.astype(BF16), vbd, preferred_element_type=F32)
               + w_inter * lax.dot_general(qb, ct.astype(BF16), (((1,), (1,)), ((), ())),
                                           preferred_element_type=F32))
        nums.append(num)
        den_parts.append((s + w_inter * q * n_prev) * float(HEAD_DIM))
        m_ts.append(m_t)

        m_new = m_t[CHUNK - 1:CHUNK, :]
        cum_last = cum_f[CHUNK - 1:CHUNK, :]
        w_state = jnp.exp(cum_last - cum_f + log_i - m_new)
        decay = jnp.exp(cum_last + m_prev - m_new)
        kw = k * w_state
        upd = lax.dot_general(vb, kw.astype(BF16), (((0,), (0,)), ((), ())),
                              preferred_element_type=F32)
        ct_new = decay * ct + upd * bd
        n_new = decay * n_prev + jnp.sum(kw, axis=0, keepdims=True)
        ct_s[b] = ct_new
        n_s[b] = n_new
        m_s[b] = m_new

    den_terms = jnp.concatenate(den_parts, axis=0)
    den_hi = den_terms.astype(BF16)
    den_lo = (den_terms - den_hi.astype(F32)).astype(BF16)
    den = (jnp.dot(den_hi, gmat, preferred_element_type=F32)
           + jnp.dot(den_lo, gmat, preferred_element_type=F32))
    hb = jnp.concatenate(nums, axis=0) / jnp.maximum(jnp.abs(den), jnp.exp(-jnp.concatenate(m_ts, axis=0)))
    msq = jnp.dot((hb * hb).astype(BF16), gmat, preferred_element_type=F32)
    hn = hb * lax.rsqrt(msq + RMS_EPS) * gml_ref[...]
    for b in range(nb):
        h_ref[b] = jax.nn.sigmoid(o_ref[b]) * rows_of(hn, b)

    @pl.when(last)
    def _():
        for b in range(nb):
            ct_new = ct_s[b]
            acc = ct_new[0:HEAD_DIM]
            for h in range(1, N_HEADS):
                acc = acc + ct_new[h * HEAD_DIM:(h + 1) * HEAD_DIM]
            ct_ref[b] = acc
            n_ref[b] = n_s[b]
            m_ref[b] = m_s[b]


def _mlstm(mq, mk, mv, mo, gates, c0t, n0, m0, consts, *, valid):
    batch, t, _ = mq.shape
    chunks = t // CHUNK
    nb = MLSTM_STREAMS
    row = lambda w: pl.BlockSpec((nb, CHUNK, w), lambda g, c: (g, c, 0))
    per_b = lambda r: pl.BlockSpec((nb, r, GROUP_W), lambda g, c: (g, 0, 0))
    expand, gbias, bd, bdb, gmat, ltri, eye, causal, gml = consts
    return pl.pallas_call(
        functools.partial(_mlstm_body, valid=valid, nb=nb),
        grid=(batch // nb, chunks),
        in_specs=[row(GROUP_W)] * 4 + [row(LANES), per_b(HEAD_DIM), per_b(1), per_b(1),
                  _full((3 * LANES, 2 * GROUP_W)), _full((1, LANES)), _full((GROUP_W, GROUP_W)),
                  _full((GROUP_W, GROUP_W)), _full((GROUP_W, GROUP_W)), _full((CHUNK, 3 * CHUNK)),
                  _full((CHUNK, GROUP_W)), _full((CHUNK, GROUP_W)), _full((1, GROUP_W))],
        out_specs=[row(GROUP_W), per_b(HEAD_DIM), per_b(1), per_b(1)],
        out_shape=[jax.ShapeDtypeStruct(mq.shape, F32),
                   jax.ShapeDtypeStruct((batch, HEAD_DIM, GROUP_W), F32),
                   jax.ShapeDtypeStruct((batch, 1, GROUP_W), F32),
                   jax.ShapeDtypeStruct((batch, 1, GROUP_W), F32)],
        scratch_shapes=[pltpu.VMEM((nb, GROUP_W, GROUP_W), F32), pltpu.VMEM((nb, 1, GROUP_W), F32),
                        pltpu.VMEM((nb, 1, GROUP_W), F32)],
        compiler_params=_params("arbitrary", "arbitrary"),
        name="mlstm",
    )(mq, mk, mv, mo, gates, c0t, n0, m0, expand, gbias, bd, bdb, gmat, ltri, eye, causal, gml)


def _mlstm_consts(b_igate_l, b_fgate_l, g_mlstm_l):
    expand = np.zeros((LANES, 2 * GROUP_W), np.float32)
    for h in range(N_HEADS):
        expand[h, h * HEAD_DIM:(h + 1) * HEAD_DIM] = 1.0
        expand[N_HEADS + h, GROUP_W + h * HEAD_DIM:GROUP_W + (h + 1) * HEAD_DIM] = 1.0
    gbias = jnp.concatenate([b_igate_l.astype(F32), b_fgate_l.astype(F32),
                             jnp.zeros((LANES - 2 * N_HEADS,), F32)])[None, :]
    bd = _head_block_diag()
    ltri = np.tril(np.ones((CHUNK, CHUNK), np.float32))
    s_of_lane = np.arange(GROUP_W) % HEAD_DIM
    t = np.arange(CHUNK)
    eye = (t[:, None] == s_of_lane[None, :]).astype(np.float32)
    causal = (s_of_lane[None, :] <= t[:, None]).astype(np.float32)
    return (jnp.asarray(np.concatenate([expand] * 3, axis=0), BF16), gbias, jnp.asarray(bd), jnp.asarray(bd, BF16),
            jnp.asarray(bd / HEAD_DIM, BF16), jnp.asarray(np.concatenate([ltri] * 3, axis=1), BF16), jnp.asarray(eye),
            jnp.asarray(causal), g_mlstm_l.astype(F32).reshape(1, GROUP_W))


def _out_proj_body(xp_ref, xs_ref, ap_ref, as_ref, hp_ref, hs_ref, wa_ref, wm_ref, gffn_ref, wr_ref, br_ref,
                   y_ref, xf_ref, logit_ref, *, prompt_tiles):
    is_prompt = pl.program_id(0) < prompt_tiles
    x = jnp.where(is_prompt, xp_ref[...], xs_ref[...])
    att = jnp.where(is_prompt, ap_ref[...], as_ref[...])
    hm = jnp.where(is_prompt, hp_ref[...], hs_ref[...])
    y = (x + jnp.dot(att.astype(BF16), wa_ref[...], preferred_element_type=F32)
         + jnp.dot(hm.astype(BF16), wm_ref[...], preferred_element_type=F32))
    y_ref[...] = y
    xf = y * lax.rsqrt(jnp.mean(y * y, axis=-1, keepdims=True) + RMS_EPS) * gffn_ref[...]
    xf_ref[...] = xf
    logit_ref[...] = _dot_f32ish(xf, wr_ref[...]) + br_ref[...]


def _out_proj(xp, xs, att_p, att_s, hm_p, hm_s, wa, wm, g_ffn, w_router, b_router):
    tm = ROW_TILE
    pt, st = xp.shape[0] // tm, xs.shape[0] // tm
    n = xp.shape[0] + xs.shape[0]
    p_row = lambda w: pl.BlockSpec((tm, w), lambda i: (jnp.minimum(i, pt - 1), 0))
    s_row = lambda w: pl.BlockSpec((tm, w), lambda i: (jnp.maximum(i - pt, 0), 0))
    row = lambda w: pl.BlockSpec((tm, w), lambda i: (i, 0))
    return pl.pallas_call(
        functools.partial(_out_proj_body, prompt_tiles=pt),
        grid=(pt + st,),
        in_specs=[p_row(D_MODEL), s_row(D_MODEL), p_row(GROUP_W), s_row(GROUP_W), p_row(GROUP_W), s_row(GROUP_W),
                  _full((GROUP_W, D_MODEL)), _full((GROUP_W, D_MODEL)),
                  _full((1, D_MODEL)), _full((3 * D_MODEL, LANES)), _full((1, LANES))],
        out_specs=[row(D_MODEL), row(D_MODEL), row(LANES)],
        out_shape=[jax.ShapeDtypeStruct((n, D_MODEL), F32), jax.ShapeDtypeStruct((n, D_MODEL), F32),
                   jax.ShapeDtypeStruct((n, LANES), F32)],
        compiler_params=_params("arbitrary"),
        name="out_proj_router",
    )(xp, xs, att_p, att_s, hm_p, hm_s, wa, wm, g_ffn, w_router, b_router)


def _route_body(logit_ref, lstrict_ref, ustrict_ref, gate_ref, pos_ref, seg_ref):
    tt = logit_ref.shape[0]
    lane = lax.broadcasted_iota(I32, (tt, LANES), 1)
    work = jnp.where(lane < N_EXPERTS, logit_ref[...], -jnp.inf)
    vals, idxs = [], []
    for _ in range(TOP_K):
        m = jnp.max(work, axis=-1, keepdims=True)
        idx = jnp.min(jnp.where(work == m, lane, LANES), axis=-1, keepdims=True)
        vals.append(m)
        idxs.append(idx)
        work = jnp.where(lane == idx, -jnp.inf, work)
    exps = [jnp.exp(v - vals[0]) for v in vals]
    total = exps[0] + exps[1] + exps[2] + exps[3]

    chosen = jnp.zeros((tt, LANES), F32)
    for idx in idxs:
        chosen = chosen + (lane == idx).astype(F32)
    before = jnp.dot(lstrict_ref[...], chosen.astype(BF16), preferred_element_type=F32)
    count = jnp.sum(chosen, axis=0, keepdims=True)
    groups = jnp.floor((count + (SUBLANES - 1)) * (1.0 / SUBLANES))
    groups8 = jnp.broadcast_to(groups, (SUBLANES, LANES)).astype(BF16)
    start = jnp.dot(groups8, ustrict_ref[...], preferred_element_type=F32) * float(SUBLANES)
    local = before + start[0:1, :]

    gate_out = jnp.zeros((tt, LANES), F32)
    pos_out = jnp.zeros((tt, LANES), F32)
    for k in range(TOP_K):
        pos = jnp.sum(jnp.where(lane == idxs[k], local, 0.0), axis=-1, keepdims=True)
        gate_out = jnp.where(lane == k, exps[k] / total, gate_out)
        pos_out = jnp.where(lane == k, pos, pos_out)
    gate_ref[...] = gate_out
    pos_ref[...] = pos_out
    row = lax.broadcasted_iota(I32, (SUBLANES, LANES), 0)
    seg = jnp.where(row == 0, groups * float(SUBLANES), jnp.where(row == 1, start, 0.0))
    seg_ref[...] = seg.astype(I32)


def _route(logits):
    n = logits.shape[0]
    tt = ROW_TILE
    lstrict = jnp.asarray(np.tril(np.ones((tt, tt), np.float32), -1), BF16)
    ustrict = jnp.asarray(np.triu(np.ones((LANES, LANES), np.float32), 1), BF16)
    row = lambda: pl.BlockSpec((tt, LANES), lambda i: (i, 0))
    return pl.pallas_call(
        _route_body,
        grid=(n // tt,),
        in_specs=[row(), _full((tt, tt)), _full((LANES, LANES))],
        out_specs=[row(), row(), pl.BlockSpec((SUBLANES, LANES), lambda i: (i, 0))],
        out_shape=[jax.ShapeDtypeStruct((n, LANES), F32), jax.ShapeDtypeStruct((n, LANES), F32),
                   jax.ShapeDtypeStruct((n // tt * SUBLANES, LANES), I32)],
        compiler_params=_params("arbitrary"),
        name="route_topk",
    )(logits, lstrict, ustrict)


def _segment_copies(i, len_ref, lst_ref, off_ref, make_copy, act):
    for e in range(N_EXPERTS):
        seg = i * N_EXPERTS + e
        length = len_ref[seg]
        local = lst_ref[seg]
        glob = off_ref[seg]
        for size in SEG_SIZES:
            take = length & size

            @pl.when(take != 0)
            def _(local=local, glob=glob, size=size):
                act(make_copy(pl.multiple_of(local, SUBLANES), pl.multiple_of(glob, SUBLANES), size))

            local = local + take
            glob = glob + take


def _local_onehot(pos_rows, base, rows, values=None):
    r = (lax.broadcasted_iota(I32, (rows, pos_rows[0].shape[1]), 0) + base).astype(F32)
    out = jnp.zeros(r.shape, F32)
    for k in range(TOP_K):
        out = jnp.where(r == pos_rows[k], 1.0 if values is None else values[k], out)
    return out


def _dispatch_body(len_ref, lst_ref, off_ref, fill_ref, nused_ref, xf_ref, pos_ref, xs_hbm,
                   xloc, zbuf, sem, zsem, *, tm, n_tiles):
    i = pl.program_id(0)
    tt = xf_ref.shape[0]
    fill_rows = zbuf.shape[0]

    @pl.when(i == 0)
    def _():
        zbuf[...] = jnp.zeros_like(zbuf)

        def fill(e):
            start = pl.multiple_of(fill_ref[e], SUBLANES)
            return pltpu.make_async_copy(zbuf, xs_hbm.at[pl.ds(start, fill_rows)], zsem)

        for e in range(N_EXPERTS):
            fill(e).start()
            fill(e).wait()

        def tail(j, carry):
            cp = pltpu.make_async_copy(zbuf.at[pl.ds(0, tm)], xs_hbm.at[pl.ds(pl.multiple_of(j * tm, tm), tm)], zsem)
            cp.start()
            cp.wait()
            return carry

        lax.fori_loop(nused_ref[0], n_tiles, tail, 0)

    pos_t = jnp.transpose(pos_ref[...])
    pos_rows = [pos_t[k:k + 1, :] for k in range(TOP_K)]
    xb = xf_ref[...].astype(BF16)

    slot = lax.rem(i, 2)

    def sort_rows(c, carry):
        r0 = pl.multiple_of(c * SORT_CHUNK, SORT_CHUNK)
        sel = _local_onehot(pos_rows, r0, SORT_CHUNK).astype(BF16)
        xloc[slot, pl.ds(r0, SORT_CHUNK), :] = jnp.dot(sel, xb, preferred_element_type=F32)
        return carry

    lax.fori_loop(0, LOCAL_ROWS // SORT_CHUNK, sort_rows, 0)

    def copies(step, which):
        def make_copy(local, glob, size):
            return pltpu.make_async_copy(xloc.at[which, pl.ds(local, size)], xs_hbm.at[pl.ds(glob, size)],
                                         sem.at[which])
        return functools.partial(_segment_copies, step, len_ref, lst_ref, off_ref, make_copy)

    copies(i, slot)(lambda cp: cp.start())

    @pl.when(i > 0)
    def _():
        copies(i - 1, 1 - slot)(lambda cp: cp.wait())

    @pl.when(i == pl.num_programs(0) - 1)
    def _():
        copies(i, slot)(lambda cp: cp.wait())


def _dispatch(seg_len, seg_local, seg_off, fill_start, n_used, xf, pos, n_tiles):
    n = xf.shape[0]
    tt, tm = ROW_TILE, EXPERT_TILE
    fill_rows = tm + SUBLANES
    grid_spec = pltpu.PrefetchScalarGridSpec(
        num_scalar_prefetch=5,
        grid=(n // tt,),
        in_specs=[pl.BlockSpec((tt, D_MODEL), lambda i, *_: (i, 0)),
                  pl.BlockSpec((tt, LANES), lambda i, *_: (i, 0))],
        out_specs=pl.BlockSpec(memory_space=pl.ANY),
        scratch_shapes=[pltpu.VMEM((2, LOCAL_ROWS, D_MODEL), F32), pltpu.VMEM((fill_rows, D_MODEL), F32),
                        pltpu.SemaphoreType.DMA((2,)), pltpu.SemaphoreType.DMA(())],
    )
    return pl.pallas_call(
        functools.partial(_dispatch_body, tm=tm, n_tiles=n_tiles + 2),
        grid_spec=grid_spec,
        out_shape=jax.ShapeDtypeStruct(((n_tiles + 2) * tm, D_MODEL), F32),
        compiler_params=_params("arbitrary"),
        name="expert_dispatch",
    )(seg_len, seg_local, seg_off, fill_start, n_used, xf, pos)


def _expert_body(te_ref, nused_ref, x_ref, wup_ref, wdn_ref, perm_ref, bg_ref, bl_ref, bd_ref,
                 y_ref, wg_s, wl_s, wd_s):
    i = pl.program_id(0)
    n_used = nused_ref[0]

    @pl.when(i >= n_used)
    def _():
        y_ref[...] = jnp.zeros_like(y_ref)

    @pl.when((i == 0) | (te_ref[i] != te_ref[jnp.maximum(i - 1, 0)]))
    def _():
        perm = perm_ref[...]
        for c in range(2 * D_FF // 256):
            blk = wup_ref[0, :, c * 256:(c + 1) * 256].astype(BF16)
            sep = jnp.dot(blk, perm, preferred_element_type=F32).astype(BF16)
            wg_s[:, c * 128:(c + 1) * 128] = sep[:, :128]
            wl_s[:, c * 128:(c + 1) * 128] = sep[:, 128:]
        wd_s[...] = wdn_ref[0].astype(BF16)

    @pl.when(i < n_used)
    def _():
        x = x_ref[...].astype(BF16)
        glu = jnp.minimum(jnp.dot(x, wg_s[...], preferred_element_type=F32) + bg_ref[0], SWIGLU_LIMIT)
        lin = jnp.clip(jnp.dot(x, wl_s[...], preferred_element_type=F32) + bl_ref[0], -SWIGLU_LIMIT, SWIGLU_LIMIT)
        act = glu * jax.nn.sigmoid(SWIGLU_ALPHA * glu) * (lin + 1.0)
        y_ref[...] = jnp.dot(act.astype(BF16), wd_s[...], preferred_element_type=F32) + bd_ref[0]


def _deinterleave_perm():
    p = np.zeros((256, 256), np.float32)
    j = np.arange(128)
    p[2 * j, j] = 1.0
    p[2 * j + 1, 128 + j] = 1.0
    return jnp.asarray(p, BF16)


def _experts(tile_expert, n_used, x_sorted, w_up, w_down, b_glu, b_lin, b_down):
    tm = EXPERT_TILE
    n_tiles = tile_expert.shape[0]
    wspec = lambda k, n: pl.BlockSpec((1, k, n), lambda i, te, nu: (te[i], 0, 0))
    grid_spec = pltpu.PrefetchScalarGridSpec(
        num_scalar_prefetch=2,
        grid=(n_tiles,),
        in_specs=[pl.BlockSpec((tm, D_MODEL), lambda i, te, nu: (jnp.minimum(i, nu[0] - 1), 0)),
                  wspec(D_MODEL, 2 * D_FF), wspec(D_FF, D_MODEL),
                  pl.BlockSpec((256, 256), lambda i, te, nu: (0, 0)),
                  wspec(1, D_FF), wspec(1, D_FF), wspec(1, D_MODEL)],
        out_specs=pl.BlockSpec((tm, D_MODEL), lambda i, te, nu: (i, 0)),
        scratch_shapes=[pltpu.VMEM((D_MODEL, D_FF), BF16), pltpu.VMEM((D_MODEL, D_FF), BF16),
                        pltpu.VMEM((D_FF, D_MODEL), BF16)],
    )
    return pl.pallas_call(
        _expert_body,
        grid_spec=grid_spec,
        out_shape=jax.ShapeDtypeStruct((n_tiles * tm, D_MODEL), F32),
        compiler_params=_params("arbitrary"),
        name="expert_ffn",
    )(tile_expert, n_used, x_sorted, w_up, w_down, _deinterleave_perm(), b_glu, b_lin, b_down)


def _combine_body(len_ref, lst_ref, off_ref, y_ref, gate_ref, pos_ref, rows_hbm, outp_ref, outs_ref,
                  yloc, acc_s, wide_s, sem, *, prompt_tiles):
    i = pl.program_id(0)
    slot = lax.rem(i, 2)
    tt = y_ref.shape[0]

    def copies(step, which):
        def make_copy(local, glob, size):
            return pltpu.make_async_copy(rows_hbm.at[pl.ds(glob, size)], yloc.at[which, pl.ds(local, size)],
                                         sem.at[which])
        return functools.partial(_segment_copies, step, len_ref, lst_ref, off_ref, make_copy)

    @pl.when(i == 0)
    def _():
        yloc[...] = jnp.zeros_like(yloc)
        copies(i, slot)(lambda cp: cp.start())

    @pl.when(i + 1 < pl.num_programs(0))
    def _():
        copies(i + 1, 1 - slot)(lambda cp: cp.start())

    pos = pos_ref[...]
    gate = gate_ref[...]
    for k in range(TOP_K):
        wide_s[k] = jnp.broadcast_to(pos[:, k:k + 1], (tt, LANES))
        wide_s[TOP_K + k] = jnp.broadcast_to(gate[:, k:k + 1], (tt, LANES))
    acc_s[...] = y_ref[...]
    copies(i, slot)(lambda cp: cp.wait())

    lane = lax.broadcasted_iota(I32, (tt, LANES), 1).astype(F32)

    def weigh(c, carry):
        r0 = pl.multiple_of(c * SORT_CHUNK, SORT_CHUNK)
        halves = []
        for half in range(SORT_CHUNK // LANES):
            r = lane + (r0 + half * LANES).astype(F32)
            w = jnp.zeros((tt, LANES), F32)
            for k in range(TOP_K):
                w = jnp.where(r == wide_s[k], wide_s[TOP_K + k], w)
            halves.append(w.astype(BF16))
        w = jnp.concatenate(halves, axis=1)
        acc_s[...] += jnp.dot(w, yloc[slot, pl.ds(r0, SORT_CHUNK), :].astype(BF16), preferred_element_type=F32)
        return carry

    lax.fori_loop(0, LOCAL_ROWS // SORT_CHUNK, weigh, 0)

    @pl.when(i < prompt_tiles)
    def _():
        outp_ref[...] = acc_s[...]

    @pl.when(i >= prompt_tiles)
    def _():
        outs_ref[...] = acc_s[...]


def _combine(seg_len, seg_local, seg_off, y, gates, pos, y_rows, n_prompt):
    n = y.shape[0]
    tt = ROW_TILE
    nt, pt = n // tt, n_prompt // tt
    grid_spec = pltpu.PrefetchScalarGridSpec(
        num_scalar_prefetch=3,
        grid=(nt,),
        in_specs=[pl.BlockSpec((tt, D_MODEL), lambda i, *_: (i, 0)),
                  pl.BlockSpec((tt, LANES), lambda i, *_: (i, 0)),
                  pl.BlockSpec((tt, LANES), lambda i, *_: (i, 0)),
                  pl.BlockSpec(memory_space=pl.ANY)],
        out_specs=[pl.BlockSpec((tt, D_MODEL), lambda i, *_: (jnp.minimum(i, pt - 1), 0)),
                   pl.BlockSpec((tt, D_MODEL), lambda i, *_: (jnp.maximum(i - pt, 0), 0))],
        scratch_shapes=[pltpu.VMEM((2, LOCAL_ROWS, D_MODEL), F32), pltpu.VMEM((tt, D_MODEL), F32),
                        pltpu.VMEM((2 * TOP_K, tt, LANES), F32), pltpu.SemaphoreType.DMA((2,))],
    )
    return pl.pallas_call(
        functools.partial(_combine_body, prompt_tiles=pt),
        grid_spec=grid_spec,
        out_shape=[jax.ShapeDtypeStruct((n_prompt, D_MODEL), F32),
                   jax.ShapeDtypeStruct((n - n_prompt, D_MODEL), F32)],
        compiler_params=_params("arbitrary"),
        name="expert_combine",
    )(seg_len, seg_local, seg_off, y, gates, pos, y_rows)


def _moe(y, xf, logits, ffn_w, n_prompt):
    n = y.shape[0]
    tm, tt = EXPERT_TILE, ROW_TILE
    n_tt = n // tt
    gates, pos, seg = _route(logits)
    seg = seg.reshape(n_tt, SUBLANES, LANES)
    seg_len, seg_local = seg[:, 0, :N_EXPERTS], seg[:, 1, :N_EXPERTS]
    rows = jnp.sum(seg_len, axis=0)
    padded = (rows + tm - 1) // tm * tm
    pad_end = jnp.cumsum(padded)
    pad_start = pad_end - padded
    seg_off = pad_start[None, :] + jnp.cumsum(seg_len, axis=0) - seg_len
    n_tiles = -(-(n * TOP_K + n_tt * N_EXPERTS * (SUBLANES - 1) + N_EXPERTS * (tm - 1)) // tm)
    tile_expert = jnp.minimum(jnp.sum(pad_end[None, :] <= (jnp.arange(n_tiles) * tm)[:, None], axis=1),
                              N_EXPERTS - 1).astype(I32)
    n_used = (pad_end[-1:] // tm).astype(I32)
    fill_start = (pad_start + rows).astype(I32)
    flat = lambda a: a.astype(I32).reshape(n_tt * N_EXPERTS)

    x_sorted = _dispatch(flat(seg_len), flat(seg_local), flat(seg_off), fill_start, n_used, xf, pos, n_tiles)
    y_rows = _experts(tile_expert, n_used, x_sorted, *ffn_w)
    return _combine(flat(seg_len), flat(seg_local), flat(seg_off), y, gates, pos, y_rows, n_prompt)


def _mixer(x, lw, cache, state):
    b, t, _ = x.shape
    n = b * t
    q, k, v, mq, mk, mv, mo, gates = _in_proj(x.reshape(n, D_MODEL), lw["g_mix"], lw["w_main"], lw["w_gate"],
                                              lw["gq"], lw["gk"], lw["gmat"])
    heads = lambda a, rows: a.reshape(b, rows, N_HEADS, HEAD_DIM)
    if cache is None:
        tiles = t // PAST_BAND
        att = _attention(q, k, k, v, v, lw["bias_prompt"], lw["hmask_prompt"], batch=b, tiles=tiles, cq=CHUNK,
                         nq=PAST_BAND // CHUNK,
                         prev_index=lambda bi, i: (bi * tiles + jnp.maximum(i - 1, 0), 0), mask_first=True)
        keep = min(PAST_BAND, t)
        k_new = heads(k.reshape(b, t, GROUP_W)[:, t - keep:], keep)
        v_new = heads(v.reshape(b, t, GROUP_W)[:, t - keep:], keep)
    else:
        ck, cv = cache
        att = _attention(q, ck.reshape(b * PAST_BAND, GROUP_W), k, cv.reshape(b * PAST_BAND, GROUP_W), v,
                         lw["bias_sample"], lw["hmask_sample"], batch=b, tiles=1, cq=t, nq=1,
                         prev_index=lambda bi, i: (bi, 0), mask_first=False)
        k_new, v_new = heads(k, t), heads(v, t)

    tp = -(-t // CHUNK) * CHUNK
    valid = t if t < CHUNK else CHUNK

    def streams(a):
        a = a.reshape(b, t, -1)
        return a if tp == t else jnp.pad(a, ((0, 0), (0, tp - t), (0, 0)))

    if state is None:
        c0t = jnp.zeros((b, HEAD_DIM, GROUP_W), F32)
        n0 = jnp.zeros((b, 1, GROUP_W), F32)
        m0 = jnp.zeros((b, 1, GROUP_W), F32)
    else:
        c_in, n_in, m_in = state
        c0t = c_in.astype(F32).transpose(0, 3, 1, 2).reshape(b, HEAD_DIM, GROUP_W)
        n0 = n_in.astype(F32).reshape(b, 1, GROUP_W)
        m0 = jnp.repeat(m_in.astype(F32), HEAD_DIM, axis=-1).reshape(b, 1, GROUP_W)
    hm, ct, n_out, m_out = _mlstm(streams(mq), streams(mk), streams(mv), streams(mo), streams(gates),
                                  c0t, n0, m0, lw["mlstm_consts"], valid=valid)
    hm = hm[:, :t].reshape(n, GROUP_W)
    c_new = ct.reshape(b, HEAD_DIM, N_HEADS, HEAD_DIM).transpose(0, 2, 3, 1)
    n_new = n_out.reshape(b, N_HEADS, HEAD_DIM)
    m_new = m_out.reshape(b, N_HEADS, HEAD_DIM)[:, :, 0]
    return att, hm, (k_new, v_new, c_new, n_new, m_new)


def kernel(x_prompt, x_sample, cache_k, cache_v, state_C, state_n, state_m, g_mix, w_in, g_q, g_k, rel_bias,
           b_igate, b_fgate, g_mlstm, w_out, g_ffn, w_router, b_router, w_up, b_up, w_down, b_down):
    depth = w_in.shape[0]
    yp, ys = x_prompt, x_sample
    bs, ts = x_sample.shape[0], x_sample.shape[1]
    n_prompt = x_prompt.shape[0] * x_prompt.shape[1]
    st_prompt, st_sample = [], []
    n_main = N_PROJ * GROUP_W
    gmat = jnp.asarray(_head_block_diag() / HEAD_DIM, BF16)
    for l in range(depth):
        lw = dict(
            g_mix=g_mix[l].astype(F32)[None, :],
            w_main=w_in[l][:, :n_main].astype(BF16),
            w_gate=jnp.pad(w_in[l][:, n_main:], ((0, 0), (0, LANES - 2 * N_HEADS))).astype(BF16),
            gq=jnp.tile(g_q[l].astype(F32), N_HEADS)[None, :],
            gk=jnp.tile(g_k[l].astype(F32), N_HEADS)[None, :],
            gmat=gmat,
            bias_prompt=_rel_base(rel_bias[l], CHUNK),
            hmask_prompt=_head_row_mask(CHUNK),
            bias_sample=_rel_base(rel_bias[l], ts),
            hmask_sample=_head_row_mask(ts),
            mlstm_consts=_mlstm_consts(b_igate[l], b_fgate[l], g_mlstm[l]),
        )
        ffn_w = (w_up[l].astype(F32), w_down[l].astype(F32),
                 b_up[l][:, None, 0::2].astype(F32), b_up[l][:, None, 1::2].astype(F32),
                 b_down[l][:, None, :].astype(F32))
        att_p, hm_p, sp = _mixer(yp, lw, None, None)
        cache = (cache_k[l].reshape(bs, PAST_BAND, GROUP_W), cache_v[l].reshape(bs, PAST_BAND, GROUP_W))
        att_s, hm_s, ss = _mixer(ys, lw, cache, (state_C[l], state_n[l], state_m[l]))
        y, xf, logits = _out_proj(
            yp.reshape(-1, D_MODEL), ys.reshape(-1, D_MODEL), att_p, att_s, hm_p, hm_s,
            w_out[l][:GROUP_W].astype(BF16), w_out[l][GROUP_W:].astype(BF16), g_ffn[l].astype(F32)[None, :],
            _stack_hi_lo(jnp.pad(w_router[l].astype(F32), ((0, 0), (0, LANES - N_EXPERTS)))),
            jnp.pad(b_router[l].astype(F32), (0, LANES - N_EXPERTS))[None, :])
        out_p, out_s = _moe(y, xf, logits, ffn_w, n_prompt)
        yp, ys = out_p.reshape(x_prompt.shape), out_s.reshape(x_sample.shape)
        st_prompt.append(sp)
        st_sample.append(ss)
    k_p, v_p, c_p, n_p, m_p = [jnp.stack(a) for a in zip(*st_prompt)]
    k_s, v_s, c_s, n_s, m_s = [jnp.stack(a) for a in zip(*st_sample)]
    return (yp, ys, k_p, v_p, c_p, n_p, m_p, k_s, v_s, c_s, n_s, m_s)
```

```python
import functools

import numpy as np
import jax
import jax.numpy as jnp
from jax import lax
from jax.experimental import pallas as pl
from jax.experimental.pallas import tpu as pltpu

F32 = jnp.float32
BF16 = jnp.bfloat16
I32 = jnp.int32

D_MODEL = 1024
N_HEADS = 8
HEAD_DIM = 64
GROUP_W = N_HEADS * HEAD_DIM
HALF_W = GROUP_W // 2
N_PROJ = 7
LANES = 128
CHUNK = 64
PAST_BAND = 512
KEY_WIN = 640
REL_CLIP = 256
N_EXPERTS = 32
TOP_K = 4
D_FF = 1024
SWIGLU_ALPHA = 1.702
SWIGLU_LIMIT = 7.0
RMS_EPS = 1e-6
NEG_BIG = -1e30
ROW_TILE = 512
EXPERT_TILE = 512
SUBLANES = 8
SORT_CHUNK = 256
LOCAL_ROWS = -(-(ROW_TILE * TOP_K + N_EXPERTS * (SUBLANES - 1)) // SORT_CHUNK) * SORT_CHUNK
SEG_SIZES = (512, 256, 128, 64, 32, 16, 8)
MLSTM_STREAMS = 4
VMEM_LIMIT_BYTES = 56 * 1024 * 1024


def _params(*sem):
    return pltpu.CompilerParams(dimension_semantics=sem, vmem_limit_bytes=VMEM_LIMIT_BYTES)


def _head_block_diag():
    h = np.arange(GROUP_W) // HEAD_DIM
    return (h[:, None] == h[None, :]).astype(np.float32)


def _full(shape):
    return pl.BlockSpec(shape, lambda *_: (0,) * len(shape))


def _halves(a):
    return a[:, :HALF_W], a[:, HALF_W:]


def _group_mean(x, gmat_half):
    return jnp.concatenate([jnp.dot(h.astype(BF16), gmat_half, preferred_element_type=F32) for h in _halves(x)],
                           axis=1)


def _split3(x):
    hi = x.astype(BF16)
    r = x - hi.astype(F32)
    mid = r.astype(BF16)
    lo = (r - mid.astype(F32)).astype(BF16)
    return hi, mid, lo


def _dot_f32ish(x, w_stack):
    hi = x.astype(BF16)
    lo = (x - hi.astype(F32)).astype(BF16)
    return jnp.dot(jnp.concatenate([hi, lo, hi], axis=1), w_stack, preferred_element_type=F32)


def _stack_hi_lo(w):
    hi = w.astype(BF16)
    lo = (w - hi.astype(F32)).astype(BF16)
    return jnp.concatenate([hi, hi, lo], axis=0)


def _in_proj_body(x_ref, gmix_ref, w_ref, wg_ref, gq_ref, gk_ref, gmat_ref,
                  q_ref, k_ref, v_ref, mq_ref, mk_ref, mv_ref, mo_ref, gate_ref):
    x = x_ref[...]
    xn = x * lax.rsqrt(jnp.mean(x * x, axis=-1, keepdims=True) + RMS_EPS) * gmix_ref[...]
    xb = xn.astype(BF16)

    def proj(j):
        return jnp.dot(xb, w_ref[:, j * GROUP_W:(j + 1) * GROUP_W], preferred_element_type=F32)

    def head_norm(a, g_ref):
        msq = _group_mean(a * a, gmat_ref[...])
        return a * lax.rsqrt(msq + RMS_EPS) * g_ref[...]

    q_ref[...] = head_norm(proj(0), gq_ref)
    k_ref[...] = head_norm(proj(1), gk_ref)
    v_ref[...] = proj(2)
    mq_ref[...] = proj(3)
    mk_ref[...] = proj(4)
    mv_ref[...] = proj(5)
    mo_ref[...] = proj(6)
    gate_ref[...] = jnp.dot(xb, wg_ref[...], preferred_element_type=F32)


def _in_proj(x2d, g_mix, w_main, w_gate, gq_row, gk_row, gmat):
    n = x2d.shape[0]
    tm = ROW_TILE
    row = lambda w: pl.BlockSpec((tm, w), lambda i: (i, 0))
    outs = [jax.ShapeDtypeStruct((n, GROUP_W), F32)] * N_PROJ + [jax.ShapeDtypeStruct((n, LANES), F32)]
    return pl.pallas_call(
        _in_proj_body,
        grid=(n // tm,),
        in_specs=[row(D_MODEL), _full((1, D_MODEL)), _full((D_MODEL, N_PROJ * GROUP_W)),
                  _full((D_MODEL, LANES)), _full((1, GROUP_W)), _full((1, GROUP_W)),
                  _full((HALF_W, HALF_W))],
        out_specs=[row(GROUP_W)] * N_PROJ + [row(LANES)],
        out_shape=outs,
        compiler_params=_params("arbitrary"),
        name="in_proj",
    )(x2d, g_mix, w_main, w_gate, gq_row, gk_row, gmat)


def _attn_body(q_ref, kp_ref, kc_ref, vp_ref, vc_ref, base_ref, hmask_ref, o_ref, kwin, vwin, bias_s,
               *, cq, nq, mask_first):
    tc = cq * nq
    i = pl.program_id(1)
    kwin[0:PAST_BAND, :] = kp_ref[...].astype(BF16)
    kwin[PAST_BAND:PAST_BAND + tc, :] = kc_ref[...].astype(BF16)
    vwin[0:PAST_BAND, :] = vp_ref[...].astype(BF16)
    vwin[PAST_BAND:PAST_BAND + tc, :] = vc_ref[...].astype(BF16)
    pad_rows = kwin.shape[0] - PAST_BAND - tc
    kwin[PAST_BAND + tc:, :] = jnp.zeros((pad_rows, GROUP_W), BF16)
    vwin[PAST_BAND + tc:, :] = jnp.zeros((pad_rows, GROUP_W), BF16)

    hm = hmask_ref[...]
    kk = lax.broadcasted_iota(I32, (1, KEY_WIN), 1)

    @pl.when((pl.program_id(0) == 0) & (i == 0))
    def _():
        for h in range(N_HEADS):
            rows = jnp.broadcast_to(base_ref[h:h + 1, :], (cq, KEY_WIN))
            rows = pltpu.roll(rows, 0, 1, stride=1, stride_axis=0)
            bias_s[h * cq:(h + 1) * cq, :] = jnp.where(kk < PAST_BAND + cq, rows, NEG_BIG)

    bias = bias_s[...]

    heads_per_half = N_HEADS // 2
    rows_half = heads_per_half * cq

    def chunk(j, carry):
        r0 = pl.multiple_of(j * cq, cq)
        q = q_ref[pl.ds(r0, cq), :] * (HEAD_DIM ** -0.5)
        s_parts = []
        for hh in range(2):
            half = slice(hh * HALF_W, (hh + 1) * HALF_W)
            qm = (jnp.concatenate([q[:, half]] * heads_per_half, axis=0) * hm).astype(BF16)
            kw = kwin[pl.ds(r0, KEY_WIN), half]
            s_parts.append(lax.dot_general(qm, kw, (((1,), (1,)), ((), ())), preferred_element_type=F32))
        s = jnp.concatenate(s_parts, axis=0) + bias
        if mask_first:
            first_valid = jnp.where(i == 0, PAST_BAND - r0, 0)
            s = jnp.where(kk >= first_valid, s, NEG_BIG)
        m = jnp.max(s, axis=-1, keepdims=True)
        p = jnp.exp(s - m)
        l = jnp.sum(p, axis=-1, keepdims=True)
        pb = p.astype(BF16)
        o_halves = []
        for hh in range(2):
            half = slice(hh * HALF_W, (hh + 1) * HALF_W)
            rows = slice(hh * rows_half, (hh + 1) * rows_half)
            vw = vwin[pl.ds(r0, KEY_WIN), half]
            o_all = jnp.dot(pb[rows], vw, preferred_element_type=F32) / l[rows] * hm
            o = o_all[0:cq]
            for h in range(1, heads_per_half):
                o = o + o_all[h * cq:(h + 1) * cq]
            o_halves.append(o)
        o_ref[pl.ds(r0, cq), :] = jnp.concatenate(o_halves, axis=1)
        return carry

    lax.fori_loop(0, nq, chunk, 0)


def _attention(q, k_prev_src, k_cur_src, v_prev_src, v_cur_src, bias, hmask, *, batch, tiles, cq, nq,
               prev_index, mask_first):
    tc = cq * nq
    cur = pl.BlockSpec((tc, GROUP_W), lambda b, i: (b * tiles + i, 0))
    prev = pl.BlockSpec((PAST_BAND, GROUP_W), prev_index)
    win_rows = (nq - 1) * cq + KEY_WIN
    return pl.pallas_call(
        functools.partial(_attn_body, cq=cq, nq=nq, mask_first=mask_first),
        grid=(batch, tiles),
        in_specs=[cur, prev, cur, prev, cur, _full((N_HEADS, KEY_WIN)), _full((N_HEADS // 2 * cq, HALF_W))],
        out_specs=cur,
        out_shape=jax.ShapeDtypeStruct(q.shape, F32),
        scratch_shapes=[pltpu.VMEM((win_rows, GROUP_W), BF16), pltpu.VMEM((win_rows, GROUP_W), BF16),
                        pltpu.VMEM((N_HEADS * cq, KEY_WIN), F32)],
        compiler_params=_params("arbitrary", "arbitrary"),
        name="band_attention",
    )(q, k_prev_src, k_cur_src, v_prev_src, v_cur_src, bias, hmask)


def _rel_base(rel_bias_l, cq):
    nk = PAST_BAND + cq
    dist = np.concatenate([PAST_BAND - np.arange(nk), np.zeros(KEY_WIN - nk - (cq - 1), np.int64),
                           PAST_BAND + np.arange(cq - 1, 0, -1)])
    return rel_bias_l[:, np.clip(dist, -REL_CLIP, REL_CLIP) + REL_CLIP].astype(F32)


def _head_row_mask(cq):
    h_row = np.repeat(np.arange(N_HEADS // 2), cq)
    h_col = np.arange(HALF_W) // HEAD_DIM
    return jnp.asarray((h_row[:, None] == h_col[None, :]).astype(np.float32))


def _log_sigmoid(x):
    return jnp.minimum(x, 0.0) - jnp.log(1.0 + jnp.exp(-jnp.abs(x)))


def _mlstm_body(q_ref, k_ref, v_ref, o_ref, g_ref, c0_ref, n0_ref, m0_ref,
                expand_ref, gbias_ref, bd_ref, bdb_ref, gmat_ref, ltri_ref, eye_ref, causal_ref, gml_ref,
                h_ref, ct_ref, n_ref, m_ref, ct_s, n_s, m_s, *, valid, nb):
    c = pl.program_id(1)
    last = c == pl.num_programs(1) - 1
    bd = bd_ref[...]
    bdb = bdb_ref[...]
    gmat = gmat_ref[...]
    eye = eye_ref[...] > 0.5
    causal = causal_ref[...] > 0.5
    gate_lane = lax.broadcasted_iota(I32, (nb * CHUNK, LANES), 1)
    heads_per_half = N_HEADS // 2

    def block_diag(a_half, mask):
        return jnp.concatenate([a_half] * heads_per_half, axis=0) * mask

    @pl.when(c == 0)
    def _():
        for b in range(nb):
            for hh, c0_half in enumerate(_halves(c0_ref[b])):
                ct_s[b, hh] = block_diag(c0_half, bd)
            n_s[b] = n0_ref[b]
            m_s[b] = m0_ref[b]

    rows_of = lambda a, b: a[b * CHUNK:(b + 1) * CHUNK]

    gates = jnp.concatenate([g_ref[b] for b in range(nb)], axis=0) + gbias_ref[...]
    gates = jnp.where(gate_lane < N_HEADS, gates, _log_sigmoid(gates))
    gp_all = jnp.dot(jnp.concatenate(_split3(gates), axis=1), expand_ref[...], preferred_element_type=F32)

    nums, den_parts, m_ts = [], [], []
    for b in range(nb):
        gp = rows_of(gp_all, b)
        log_i = gp[:, :GROUP_W]
        log_f = gp[:, GROUP_W:]
        if valid < CHUNK:
            live = lax.broadcasted_iota(I32, (CHUNK, GROUP_W), 0) < valid
            log_i = jnp.where(live, log_i, -jnp.inf)
            log_f = jnp.where(live, log_f, 0.0)
        cum_f = jnp.dot(ltri_ref[...], jnp.concatenate(_split3(log_f), axis=0), preferred_element_type=F32)

        b_row = jnp.sum(jnp.where(eye, log_i - cum_f, 0.0), axis=0, keepdims=True)
        m_prev = m_s[b]
        log_inter = cum_f + m_prev
        log_d = jnp.where(causal, cum_f + b_row, -jnp.inf)
        max_d = jnp.concatenate(
            [jnp.broadcast_to(jnp.max(log_d[:, h * HEAD_DIM:(h + 1) * HEAD_DIM], axis=-1, keepdims=True),
                              (CHUNK, HEAD_DIM)) for h in range(N_HEADS)], axis=1)
        m_t = jnp.maximum(log_inter, max_d)
        w_intra = jnp.exp(log_d - m_t)
        w_inter = jnp.exp(log_inter - m_t)

        q = q_ref[b] * (HEAD_DIM ** -0.5)
        k = k_ref[b]
        qb = q.astype(BF16)
        kb = k.astype(BF16)
        vb = v_ref[b].astype(BF16)
        n_prev = n_s[b]
        m_new = m_t[CHUNK - 1:CHUNK, :]
        cum_last = cum_f[CHUNK - 1:CHUNK, :]
        w_state = jnp.exp(cum_last - cum_f + log_i - m_new)
        decay = jnp.exp(cum_last + m_prev - m_new)
        kw = k * w_state
        kwb = kw.astype(BF16)

        s_halves, num_halves = [], []
        for hh in range(2):
            half = slice(hh * HALF_W, (hh + 1) * HALF_W)
            kbd = block_diag(kb[:, half], bdb)
            vbd = block_diag(vb[:, half], bdb)
            s_h = lax.dot_general(qb[:, half], kbd, (((1,), (1,)), ((), ())),
                                  preferred_element_type=F32) * w_intra[:, half]
            ct = ct_s[b, hh]
            num_halves.append(
                jnp.dot(s_h.astype(BF16), vbd, preferred_element_type=F32)
                + w_inter[:, half] * lax.dot_general(qb[:, half], ct.astype(BF16), (((1,), (1,)), ((), ())),
                                                     preferred_element_type=F32))
            s_halves.append(s_h)
            upd = lax.dot_general(vb[:, half], kwb[:, half], (((0,), (0,)), ((), ())),
                                  preferred_element_type=F32)
            ct_s[b, hh] = decay[:, half] * ct + upd * bd
        s = jnp.concatenate(s_halves, axis=1)
        nums.append(jnp.concatenate(num_halves, axis=1))
        den_parts.append((s + w_inter * q * n_prev) * float(HEAD_DIM))
        m_ts.append(m_t)
        n_s[b] = decay * n_prev + jnp.sum(kw, axis=0, keepdims=True)
        m_s[b] = m_new

    den_terms = jnp.concatenate(den_parts, axis=0)
    den_hi = den_terms.astype(BF16)
    den_lo = den_terms - den_hi.astype(F32)
    den = _group_mean(den_hi.astype(F32), gmat) + _group_mean(den_lo, gmat)
    hb = jnp.concatenate(nums, axis=0) / jnp.maximum(jnp.abs(den), jnp.exp(-jnp.concatenate(m_ts, axis=0)))
    hn = hb * lax.rsqrt(_group_mean(hb * hb, gmat) + RMS_EPS) * gml_ref[...]
    for b in range(nb):
        h_ref[b] = jax.nn.sigmoid(o_ref[b]) * rows_of(hn, b)

    @pl.when(last)
    def _():
        for b in range(nb):
            out_halves = []
            for hh in range(2):
                ct_half = ct_s[b, hh]
                acc = ct_half[0:HEAD_DIM]
                for h in range(1, heads_per_half):
                    acc = acc + ct_half[h * HEAD_DIM:(h + 1) * HEAD_DIM]
                out_halves.append(acc)
            ct_ref[b] = jnp.concatenate(out_halves, axis=1)
            n_ref[b] = n_s[b]
            m_ref[b] = m_s[b]


def _mlstm(mq, mk, mv, mo, gates, c0t, n0, m0, consts, *, valid):
    batch, t, _ = mq.shape
    chunks = t // CHUNK
    nb = MLSTM_STREAMS
    row = lambda w: pl.BlockSpec((nb, CHUNK, w), lambda g, c: (g, c, 0))
    per_b = lambda r: pl.BlockSpec((nb, r, GROUP_W), lambda g, c: (g, 0, 0))
    expand, gbias, bd, bdb, gmat, ltri, eye, causal, gml = consts
    return pl.pallas_call(
        functools.partial(_mlstm_body, valid=valid, nb=nb),
        grid=(batch // nb, chunks),
        in_specs=[row(GROUP_W)] * 4 + [row(LANES), per_b(HEAD_DIM), per_b(1), per_b(1),
                  _full((3 * LANES, 2 * GROUP_W)), _full((1, LANES)), _full((HALF_W, HALF_W)),
                  _full((HALF_W, HALF_W)), _full((HALF_W, HALF_W)), _full((CHUNK, 3 * CHUNK)),
                  _full((CHUNK, GROUP_W)), _full((CHUNK, GROUP_W)), _full((1, GROUP_W))],
        out_specs=[row(GROUP_W), per_b(HEAD_DIM), per_b(1), per_b(1)],
        out_shape=[jax.ShapeDtypeStruct(mq.shape, F32),
                   jax.ShapeDtypeStruct((batch, HEAD_DIM, GROUP_W), F32),
                   jax.ShapeDtypeStruct((batch, 1, GROUP_W), F32),
                   jax.ShapeDtypeStruct((batch, 1, GROUP_W), F32)],
        scratch_shapes=[pltpu.VMEM((nb, 2, HALF_W, HALF_W), F32), pltpu.VMEM((nb, 1, GROUP_W), F32),
                        pltpu.VMEM((nb, 1, GROUP_W), F32)],
        compiler_params=_params("arbitrary", "arbitrary"),
        name="mlstm",
    )(mq, mk, mv, mo, gates, c0t, n0, m0, expand, gbias, bd, bdb, gmat, ltri, eye, causal, gml)


def _mlstm_consts(b_igate_l, b_fgate_l, g_mlstm_l):
    expand = np.zeros((LANES, 2 * GROUP_W), np.float32)
    for h in range(N_HEADS):
        expand[h, h * HEAD_DIM:(h + 1) * HEAD_DIM] = 1.0
        expand[N_HEADS + h, GROUP_W + h * HEAD_DIM:GROUP_W + (h + 1) * HEAD_DIM] = 1.0
    gbias = jnp.concatenate([b_igate_l.astype(F32), b_fgate_l.astype(F32),
                             jnp.zeros((LANES - 2 * N_HEADS,), F32)])[None, :]
    bd = _head_block_diag()[:HALF_W, :HALF_W]
    ltri = np.tril(np.ones((CHUNK, CHUNK), np.float32))
    s_of_lane = np.arange(GROUP_W) % HEAD_DIM
    t = np.arange(CHUNK)
    eye = (t[:, None] == s_of_lane[None, :]).astype(np.float32)
    causal = (s_of_lane[None, :] <= t[:, None]).astype(np.float32)
    return (jnp.asarray(np.concatenate([expand] * 3, axis=0), BF16), gbias, jnp.asarray(bd), jnp.asarray(bd, BF16),
            jnp.asarray(bd / HEAD_DIM, BF16), jnp.asarray(np.concatenate([ltri] * 3, axis=1), BF16), jnp.asarray(eye),
            jnp.asarray(causal), g_mlstm_l.astype(F32).reshape(1, GROUP_W))


def _out_proj_body(xp_ref, xs_ref, ap_ref, as_ref, hp_ref, hs_ref, wa_ref, wm_ref, gffn_ref, wr_ref, br_ref,
                   y_ref, xf_ref, logit_ref, *, prompt_tiles):
    is_prompt = pl.program_id(0) < prompt_tiles
    x = jnp.where(is_prompt, xp_ref[...], xs_ref[...])
    att = jnp.where(is_prompt, ap_ref[...], as_ref[...])
    hm = jnp.where(is_prompt, hp_ref[...], hs_ref[...])
    y = (x + jnp.dot(att.astype(BF16), wa_ref[...], preferred_element_type=F32)
         + jnp.dot(hm.astype(BF16), wm_ref[...], preferred_element_type=F32))
    y_ref[...] = y
    xf = y * lax.rsqrt(jnp.mean(y * y, axis=-1, keepdims=True) + RMS_EPS) * gffn_ref[...]
    xf_ref[...] = xf
    logit_ref[...] = _dot_f32ish(xf, wr_ref[...]) + br_ref[...]


def _out_proj(xp, xs, att_p, att_s, hm_p, hm_s, wa, wm, g_ffn, w_router, b_router):
    tm = ROW_TILE
    pt, st = xp.shape[0] // tm, xs.shape[0] // tm
    n = xp.shape[0] + xs.shape[0]
    p_row = lambda w: pl.BlockSpec((tm, w), lambda i: (jnp.minimum(i, pt - 1), 0))
    s_row = lambda w: pl.BlockSpec((tm, w), lambda i: (jnp.maximum(i - pt, 0), 0))
    row = lambda w: pl.BlockSpec((tm, w), lambda i: (i, 0))
    return pl.pallas_call(
        functools.partial(_out_proj_body, prompt_tiles=pt),
        grid=(pt + st,),
        in_specs=[p_row(D_MODEL), s_row(D_MODEL), p_row(GROUP_W), s_row(GROUP_W), p_row(GROUP_W), s_row(GROUP_W),
                  _full((GROUP_W, D_MODEL)), _full((GROUP_W, D_MODEL)),
                  _full((1, D_MODEL)), _full((3 * D_MODEL, LANES)), _full((1, LANES))],
        out_specs=[row(D_MODEL), row(D_MODEL), row(LANES)],
        out_shape=[jax.ShapeDtypeStruct((n, D_MODEL), F32), jax.ShapeDtypeStruct((n, D_MODEL), F32),
                   jax.ShapeDtypeStruct((n, LANES), F32)],
        compiler_params=_params("arbitrary"),
        name="out_proj_router",
    )(xp, xs, att_p, att_s, hm_p, hm_s, wa, wm, g_ffn, w_router, b_router)


def _route_body(logit_ref, lstrict_ref, ustrict_ref, gate_ref, pos_ref, seg_ref):
    tt = logit_ref.shape[0]
    lane = lax.broadcasted_iota(I32, (tt, LANES), 1)
    work = jnp.where(lane < N_EXPERTS, logit_ref[...], -jnp.inf)
    vals, idxs = [], []
    for _ in range(TOP_K):
        m = jnp.max(work, axis=-1, keepdims=True)
        idx = jnp.min(jnp.where(work == m, lane, LANES), axis=-1, keepdims=True)
        vals.append(m)
        idxs.append(idx)
        work = jnp.where(lane == idx, -jnp.inf, work)
    exps = [jnp.exp(v - vals[0]) for v in vals]
    total = exps[0] + exps[1] + exps[2] + exps[3]

    chosen = jnp.zeros((tt, LANES), F32)
    for idx in idxs:
        chosen = chosen + (lane == idx).astype(F32)
    before = jnp.dot(lstrict_ref[...], chosen.astype(BF16), preferred_element_type=F32)
    count = jnp.sum(chosen, axis=0, keepdims=True)
    groups = jnp.floor((count + (SUBLANES - 1)) * (1.0 / SUBLANES))
    groups8 = jnp.broadcast_to(groups, (SUBLANES, LANES)).astype(BF16)
    start = jnp.dot(groups8, ustrict_ref[...], preferred_element_type=F32) * float(SUBLANES)
    local = before + start[0:1, :]

    gate_out = jnp.zeros((tt, LANES), F32)
    pos_out = jnp.zeros((tt, LANES), F32)
    for k in range(TOP_K):
        pos = jnp.sum(jnp.where(lane == idxs[k], local, 0.0), axis=-1, keepdims=True)
        gate_out = jnp.where(lane == k, exps[k] / total, gate_out)
        pos_out = jnp.where(lane == k, pos, pos_out)
    gate_ref[...] = gate_out
    pos_ref[...] = pos_out
    row = lax.broadcasted_iota(I32, (SUBLANES, LANES), 0)
    seg = jnp.where(row == 0, groups * float(SUBLANES), jnp.where(row == 1, start, 0.0))
    seg_ref[...] = seg.astype(I32)


def _route(logits):
    n = logits.shape[0]
    tt = ROW_TILE
    lstrict = jnp.asarray(np.tril(np.ones((tt, tt), np.float32), -1), BF16)
    ustrict = jnp.asarray(np.triu(np.ones((LANES, LANES), np.float32), 1), BF16)
    row = lambda: pl.BlockSpec((tt, LANES), lambda i: (i, 0))
    return pl.pallas_call(
        _route_body,
        grid=(n // tt,),
        in_specs=[row(), _full((tt, tt)), _full((LANES, LANES))],
        out_specs=[row(), row(), pl.BlockSpec((SUBLANES, LANES), lambda i: (i, 0))],
        out_shape=[jax.ShapeDtypeStruct((n, LANES), F32), jax.ShapeDtypeStruct((n, LANES), F32),
                   jax.ShapeDtypeStruct((n // tt * SUBLANES, LANES), I32)],
        compiler_params=_params("arbitrary"),
        name="route_topk",
    )(logits, lstrict, ustrict)


def _segment_copies(i, len_ref, lst_ref, off_ref, make_copy, act):
    for e in range(N_EXPERTS):
        seg = i * N_EXPERTS + e
        length = len_ref[seg]
        local = lst_ref[seg]
        glob = off_ref[seg]
        for size in SEG_SIZES:
            take = length & size

            @pl.when(take != 0)
            def _(local=local, glob=glob, size=size):
                act(make_copy(pl.multiple_of(local, SUBLANES), pl.multiple_of(glob, SUBLANES), size))

            local = local + take
            glob = glob + take


def _local_onehot(pos_rows, base, rows):
    r = (lax.broadcasted_iota(I32, (rows, pos_rows[0].shape[1]), 0) + base).astype(F32)
    out = jnp.zeros(r.shape, F32)
    for k in range(TOP_K):
        out = jnp.where(r == pos_rows[k], 1.0, out)
    return out


def _dispatch_body(len_ref, lst_ref, off_ref, fill_ref, nused_ref, xf_ref, pos_ref, xs_hbm,
                   xloc, zbuf, sem, zsem, *, tm, n_tiles):
    i = pl.program_id(0)
    tt = xf_ref.shape[0]
    fill_rows = zbuf.shape[0]

    @pl.when(i == 0)
    def _():
        zbuf[...] = jnp.zeros_like(zbuf)

        def fill(e):
            start = pl.multiple_of(fill_ref[e], SUBLANES)
            return pltpu.make_async_copy(zbuf, xs_hbm.at[pl.ds(start, fill_rows)], zsem)

        for e in range(N_EXPERTS):
            fill(e).start()
            fill(e).wait()

        def tail(j, carry):
            cp = pltpu.make_async_copy(zbuf.at[pl.ds(0, tm)], xs_hbm.at[pl.ds(pl.multiple_of(j * tm, tm), tm)], zsem)
            cp.start()
            cp.wait()
            return carry

        lax.fori_loop(nused_ref[0], n_tiles, tail, 0)

    pos_t = jnp.transpose(pos_ref[...])
    pos_rows = [pos_t[k:k + 1, :] for k in range(TOP_K)]
    xb = xf_ref[...].astype(BF16)

    slot = lax.rem(i, 2)

    def sort_rows(c, carry):
        r0 = pl.multiple_of(c * SORT_CHUNK, SORT_CHUNK)
        sel = _local_onehot(pos_rows, r0, SORT_CHUNK).astype(BF16)
        xloc[slot, pl.ds(r0, SORT_CHUNK), :] = jnp.dot(sel, xb, preferred_element_type=F32)
        return carry

    lax.fori_loop(0, LOCAL_ROWS // SORT_CHUNK, sort_rows, 0)

    def copies(step, which):
        def make_copy(local, glob, size):
            return pltpu.make_async_copy(xloc.at[which, pl.ds(local, size)], xs_hbm.at[pl.ds(glob, size)],
                                         sem.at[which])
        return functools.partial(_segment_copies, step, len_ref, lst_ref, off_ref, make_copy)

    copies(i, slot)(lambda cp: cp.start())

    @pl.when(i > 0)
    def _():
        copies(i - 1, 1 - slot)(lambda cp: cp.wait())

    @pl.when(i == pl.num_programs(0) - 1)
    def _():
        copies(i, slot)(lambda cp: cp.wait())


def _dispatch(seg_len, seg_local, seg_off, fill_start, n_used, xf, pos, n_tiles):
    n = xf.shape[0]
    tt, tm = ROW_TILE, EXPERT_TILE
    fill_rows = tm + SUBLANES
    grid_spec = pltpu.PrefetchScalarGridSpec(
        num_scalar_prefetch=5,
        grid=(n // tt,),
        in_specs=[pl.BlockSpec((tt, D_MODEL), lambda i, *_: (i, 0)),
                  pl.BlockSpec((tt, LANES), lambda i, *_: (i, 0))],
        out_specs=pl.BlockSpec(memory_space=pl.ANY),
        scratch_shapes=[pltpu.VMEM((2, LOCAL_ROWS, D_MODEL), F32), pltpu.VMEM((fill_rows, D_MODEL), F32),
                        pltpu.SemaphoreType.DMA((2,)), pltpu.SemaphoreType.DMA(())],
    )
    return pl.pallas_call(
        functools.partial(_dispatch_body, tm=tm, n_tiles=n_tiles + 2),
        grid_spec=grid_spec,
        out_shape=jax.ShapeDtypeStruct(((n_tiles + 2) * tm, D_MODEL), F32),
        compiler_params=_params("arbitrary"),
        name="expert_dispatch",
    )(seg_len, seg_local, seg_off, fill_start, n_used, xf, pos)


def _expert_body(te_ref, nused_ref, x_ref, wup_ref, wdn_ref, perm_ref, bg_ref, bl_ref, bd_ref,
                 y_ref, wg_s, wl_s, wd_s):
    i = pl.program_id(0)
    n_used = nused_ref[0]

    @pl.when(i >= n_used)
    def _():
        y_ref[...] = jnp.zeros_like(y_ref)

    @pl.when((i == 0) | (te_ref[i] != te_ref[jnp.maximum(i - 1, 0)]))
    def _():
        perm = perm_ref[...]
        for c in range(2 * D_FF // 256):
            blk = wup_ref[0, :, c * 256:(c + 1) * 256].astype(BF16)
            sep = jnp.dot(blk, perm, preferred_element_type=F32).astype(BF16)
            wg_s[:, c * 128:(c + 1) * 128] = sep[:, :128]
            wl_s[:, c * 128:(c + 1) * 128] = sep[:, 128:]
        wd_s[...] = wdn_ref[0].astype(BF16)

    @pl.when(i < n_used)
    def _():
        x = x_ref[...].astype(BF16)
        glu = jnp.minimum(jnp.dot(x, wg_s[...], preferred_element_type=F32) + bg_ref[0], SWIGLU_LIMIT)
        lin = jnp.clip(jnp.dot(x, wl_s[...], preferred_element_type=F32) + bl_ref[0], -SWIGLU_LIMIT, SWIGLU_LIMIT)
        act = glu * jax.nn.sigmoid(SWIGLU_ALPHA * glu) * (lin + 1.0)
        y_ref[...] = jnp.dot(act.astype(BF16), wd_s[...], preferred_element_type=F32) + bd_ref[0]


def _deinterleave_perm():
    p = np.zeros((256, 256), np.float32)
    j = np.arange(128)
    p[2 * j, j] = 1.0
    p[2 * j + 1, 128 + j] = 1.0
    return jnp.asarray(p, BF16)


def _experts(tile_expert, n_used, x_sorted, w_up, w_down, b_glu, b_lin, b_down):
    tm = EXPERT_TILE
    n_tiles = tile_expert.shape[0]
    wspec = lambda k, n: pl.BlockSpec((1, k, n), lambda i, te, nu: (te[i], 0, 0))
    grid_spec = pltpu.PrefetchScalarGridSpec(
        num_scalar_prefetch=2,
        grid=(n_tiles,),
        in_specs=[pl.BlockSpec((tm, D_MODEL), lambda i, te, nu: (jnp.minimum(i, nu[0] - 1), 0)),
                  wspec(D_MODEL, 2 * D_FF), wspec(D_FF, D_MODEL),
                  pl.BlockSpec((256, 256), lambda i, te, nu: (0, 0)),
                  wspec(1, D_FF), wspec(1, D_FF), wspec(1, D_MODEL)],
        out_specs=pl.BlockSpec((tm, D_MODEL), lambda i, te, nu: (i, 0)),
        scratch_shapes=[pltpu.VMEM((D_MODEL, D_FF), BF16), pltpu.VMEM((D_MODEL, D_FF), BF16),
                        pltpu.VMEM((D_FF, D_MODEL), BF16)],
    )
    return pl.pallas_call(
        _expert_body,
        grid_spec=grid_spec,
        out_shape=jax.ShapeDtypeStruct((n_tiles * tm, D_MODEL), F32),
        compiler_params=_params("arbitrary"),
        name="expert_ffn",
    )(tile_expert, n_used, x_sorted, w_up, w_down, _deinterleave_perm(), b_glu, b_lin, b_down)


def _combine_body(len_ref, lst_ref, off_ref, y_ref, gate_ref, pos_ref, rows_hbm, outp_ref, outs_ref,
                  yloc, acc_s, wide_s, sem, *, prompt_tiles):
    i = pl.program_id(0)
    slot = lax.rem(i, 2)
    tt = y_ref.shape[0]

    def copies(step, which):
        def make_copy(local, glob, size):
            return pltpu.make_async_copy(rows_hbm.at[pl.ds(glob, size)], yloc.at[which, pl.ds(local, size)],
                                         sem.at[which])
        return functools.partial(_segment_copies, step, len_ref, lst_ref, off_ref, make_copy)

    @pl.when(i == 0)
    def _():
        yloc[...] = jnp.zeros_like(yloc)
        copies(i, slot)(lambda cp: cp.start())

    @pl.when(i + 1 < pl.num_programs(0))
    def _():
        copies(i + 1, 1 - slot)(lambda cp: cp.start())

    pos = pos_ref[...]
    gate = gate_ref[...]
    for k in range(TOP_K):
        wide_s[k] = jnp.broadcast_to(pos[:, k:k + 1], (tt, LANES))
        wide_s[TOP_K + k] = jnp.broadcast_to(gate[:, k:k + 1], (tt, LANES))
    acc_s[...] = y_ref[...]
    copies(i, slot)(lambda cp: cp.wait())

    lane = lax.broadcasted_iota(I32, (tt, LANES), 1).astype(F32)

    def weigh(c, carry):
        r0 = pl.multiple_of(c * SORT_CHUNK, SORT_CHUNK)
        halves = []
        for half in range(SORT_CHUNK // LANES):
            r = lane + (r0 + half * LANES).astype(F32)
            w = jnp.zeros((tt, LANES), F32)
            for k in range(TOP_K):
                w = jnp.where(r == wide_s[k], wide_s[TOP_K + k], w)
            halves.append(w.astype(BF16))
        w = jnp.concatenate(halves, axis=1)
        acc_s[...] += jnp.dot(w, yloc[slot, pl.ds(r0, SORT_CHUNK), :].astype(BF16), preferred_element_type=F32)
        return carry

    lax.fori_loop(0, LOCAL_ROWS // SORT_CHUNK, weigh, 0)

    @pl.when(i < prompt_tiles)
    def _():
        outp_ref[...] = acc_s[...]

    @pl.when(i >= prompt_tiles)
    def _():
        outs_ref[...] = acc_s[...]


def _combine(seg_len, seg_local, seg_off, y, gates, pos, y_rows, n_prompt):
    n = y.shape[0]
    tt = ROW_TILE
    nt, pt = n // tt, n_prompt // tt
    grid_spec = pltpu.PrefetchScalarGridSpec(
        num_scalar_prefetch=3,
        grid=(nt,),
        in_specs=[pl.BlockSpec((tt, D_MODEL), lambda i, *_: (i, 0)),
                  pl.BlockSpec((tt, LANES), lambda i, *_: (i, 0)),
                  pl.BlockSpec((tt, LANES), lambda i, *_: (i, 0)),
                  pl.BlockSpec(memory_space=pl.ANY)],
        out_specs=[pl.BlockSpec((tt, D_MODEL), lambda i, *_: (jnp.minimum(i, pt - 1), 0)),
                   pl.BlockSpec((tt, D_MODEL), lambda i, *_: (jnp.maximum(i - pt, 0), 0))],
        scratch_shapes=[pltpu.VMEM((2, LOCAL_ROWS, D_MODEL), F32), pltpu.VMEM((tt, D_MODEL), F32),
                        pltpu.VMEM((2 * TOP_K, tt, LANES), F32), pltpu.SemaphoreType.DMA((2,))],
    )
    return pl.pallas_call(
        functools.partial(_combine_body, prompt_tiles=pt),
        grid_spec=grid_spec,
        out_shape=[jax.ShapeDtypeStruct((n_prompt, D_MODEL), F32),
                   jax.ShapeDtypeStruct((n - n_prompt, D_MODEL), F32)],
        compiler_params=_params("arbitrary"),
        name="expert_combine",
    )(seg_len, seg_local, seg_off, y, gates, pos, y_rows)


def _moe(y, xf, logits, ffn_w, n_prompt):
    n = y.shape[0]
    tm, tt = EXPERT_TILE, ROW_TILE
    n_tt = n // tt
    gates, pos, seg = _route(logits)
    seg = seg.reshape(n_tt, SUBLANES, LANES)
    seg_len, seg_local = seg[:, 0, :N_EXPERTS], seg[:, 1, :N_EXPERTS]
    rows = jnp.sum(seg_len, axis=0)
    padded = (rows + tm - 1) // tm * tm
    pad_end = jnp.cumsum(padded)
    pad_start = pad_end - padded
    seg_off = pad_start[None, :] + jnp.cumsum(seg_len, axis=0) - seg_len
    n_tiles = -(-(n * TOP_K + n_tt * N_EXPERTS * (SUBLANES - 1) + N_EXPERTS * (tm - 1)) // tm)
    tile_expert = jnp.minimum(jnp.sum(pad_end[None, :] <= (jnp.arange(n_tiles) * tm)[:, None], axis=1),
                              N_EXPERTS - 1).astype(I32)
    n_used = (pad_end[-1:] // tm).astype(I32)
    fill_start = (pad_start + rows).astype(I32)
    flat = lambda a: a.astype(I32).reshape(n_tt * N_EXPERTS)

    x_sorted = _dispatch(flat(seg_len), flat(seg_local), flat(seg_off), fill_start, n_used, xf, pos, n_tiles)
    y_rows = _experts(tile_expert, n_used, x_sorted, *ffn_w)
    return _combine(flat(seg_len), flat(seg_local), flat(seg_off), y, gates, pos, y_rows, n_prompt)


def _mixer(x, lw, cache, state):
    b, t, _ = x.shape
    n = b * t
    q, k, v, mq, mk, mv, mo, gates = _in_proj(x.reshape(n, D_MODEL), lw["g_mix"], lw["w_main"], lw["w_gate"],
                                              lw["gq"], lw["gk"], lw["gmat"])
    heads = lambda a, rows: a.reshape(b, rows, N_HEADS, HEAD_DIM)
    if cache is None:
        tiles = t // PAST_BAND
        att = _attention(q, k, k, v, v, lw["bias_prompt"], lw["hmask_prompt"], batch=b, tiles=tiles, cq=CHUNK,
                         nq=PAST_BAND // CHUNK,
                         prev_index=lambda bi, i: (bi * tiles + jnp.maximum(i - 1, 0), 0), mask_first=True)
        keep = min(PAST_BAND, t)
        k_new = heads(k.reshape(b, t, GROUP_W)[:, t - keep:], keep)
        v_new = heads(v.reshape(b, t, GROUP_W)[:, t - keep:], keep)
    else:
        ck, cv = cache
        att = _attention(q, ck.reshape(b * PAST_BAND, GROUP_W), k, cv.reshape(b * PAST_BAND, GROUP_W), v,
                         lw["bias_sample"], lw["hmask_sample"], batch=b, tiles=1, cq=t, nq=1,
                         prev_index=lambda bi, i: (bi, 0), mask_first=False)
        k_new, v_new = heads(k, t), heads(v, t)

    tp = -(-t // CHUNK) * CHUNK
    valid = t if t < CHUNK else CHUNK

    def streams(a):
        a = a.reshape(b, t, -1)
        return a if tp == t else jnp.pad(a, ((0, 0), (0, tp - t), (0, 0)))

    if state is None:
        c0t = jnp.zeros((b, HEAD_DIM, GROUP_W), F32)
        n0 = jnp.zeros((b, 1, GROUP_W), F32)
        m0 = jnp.zeros((b, 1, GROUP_W), F32)
    else:
        c_in, n_in, m_in = state
        c0t = c_in.astype(F32).transpose(0, 3, 1, 2).reshape(b, HEAD_DIM, GROUP_W)
        n0 = n_in.astype(F32).reshape(b, 1, GROUP_W)
        m0 = jnp.repeat(m_in.astype(F32), HEAD_DIM, axis=-1).reshape(b, 1, GROUP_W)
    hm, ct, n_out, m_out = _mlstm(streams(mq), streams(mk), streams(mv), streams(mo), streams(gates),
                                  c0t, n0, m0, lw["mlstm_consts"], valid=valid)
    hm = hm[:, :t].reshape(n, GROUP_W)
    c_new = ct.reshape(b, HEAD_DIM, N_HEADS, HEAD_DIM).transpose(0, 2, 3, 1)
    n_new = n_out.reshape(b, N_HEADS, HEAD_DIM)
    m_new = m_out.reshape(b, N_HEADS, HEAD_DIM)[:, :, 0]
    return att, hm, (k_new, v_new, c_new, n_new, m_new)


def kernel(x_prompt, x_sample, cache_k, cache_v, state_C, state_n, state_m, g_mix, w_in, g_q, g_k, rel_bias,
           b_igate, b_fgate, g_mlstm, w_out, g_ffn, w_router, b_router, w_up, b_up, w_down, b_down):
    depth = w_in.shape[0]
    yp, ys = x_prompt, x_sample
    bs, ts = x_sample.shape[0], x_sample.shape[1]
    n_prompt = x_prompt.shape[0] * x_prompt.shape[1]
    st_prompt, st_sample = [], []
    n_main = N_PROJ * GROUP_W
    gmat = jnp.asarray(_head_block_diag()[:HALF_W, :HALF_W] / HEAD_DIM, BF16)
    for l in range(depth):
        lw = dict(
            g_mix=g_mix[l].astype(F32)[None, :],
            w_main=w_in[l][:, :n_main].astype(BF16),
            w_gate=jnp.pad(w_in[l][:, n_main:], ((0, 0), (0, LANES - 2 * N_HEADS))).astype(BF16),
            gq=jnp.tile(g_q[l].astype(F32), N_HEADS)[None, :],
            gk=jnp.tile(g_k[l].astype(F32), N_HEADS)[None, :],
            gmat=gmat,
            bias_prompt=_rel_base(rel_bias[l], CHUNK),
            hmask_prompt=_head_row_mask(CHUNK),
            bias_sample=_rel_base(rel_bias[l], ts),
            hmask_sample=_head_row_mask(ts),
            mlstm_consts=_mlstm_consts(b_igate[l], b_fgate[l], g_mlstm[l]),
        )
        ffn_w = (w_up[l].astype(F32), w_down[l].astype(F32),
                 b_up[l][:, None, 0::2].astype(F32), b_up[l][:, None, 1::2].astype(F32),
                 b_down[l][:, None, :].astype(F32))
        att_p, hm_p, sp = _mixer(yp, lw, None, None)
        cache = (cache_k[l].reshape(bs, PAST_BAND, GROUP_W), cache_v[l].reshape(bs, PAST_BAND, GROUP_W))
        att_s, hm_s, ss = _mixer(ys, lw, cache, (state_C[l], state_n[l], state_m[l]))
        y, xf, logits = _out_proj(
            yp.reshape(-1, D_MODEL), ys.reshape(-1, D_MODEL), att_p, att_s, hm_p, hm_s,
            w_out[l][:GROUP_W].astype(BF16), w_out[l][GROUP_W:].astype(BF16), g_ffn[l].astype(F32)[None, :],
            _stack_hi_lo(jnp.pad(w_router[l].astype(F32), ((0, 0), (0, LANES - N_EXPERTS)))),
            jnp.pad(b_router[l].astype(F32), (0, LANES - N_EXPERTS))[None, :])
        out_p, out_s = _moe(y, xf, logits, ffn_w, n_prompt)
        yp, ys = out_p.reshape(x_prompt.shape), out_s.reshape(x_sample.shape)
        st_prompt.append(sp)
        st_sample.append(ss)
    k_p, v_p, c_p, n_p, m_p = [jnp.stack(a) for a in zip(*st_prompt)]
    k_s, v_s, c_s, n_s, m_s = [jnp.stack(a) for a in zip(*st_sample)]
    return (yp, ys, k_p, v_p, c_p, n_p, m_p, k_s, v_s, c_s, n_s, m_s)
```

```python
import functools

import numpy as np
import jax
import jax.numpy as jnp
from jax import lax
from jax.experimental import pallas as pl
from jax.experimental.pallas import tpu as pltpu

F32 = jnp.float32
BF16 = jnp.bfloat16
I32 = jnp.int32

D_MODEL = 1024
N_HEADS = 8
HEAD_DIM = 64
GROUP_W = N_HEADS * HEAD_DIM
HALF_W = GROUP_W // 2
N_PROJ = 7
LANES = 128
CHUNK = 64
PAST_BAND = 512
KEY_WIN = 640
REL_CLIP = 256
N_EXPERTS = 32
TOP_K = 4
D_FF = 1024
SWIGLU_ALPHA = 1.702
SWIGLU_LIMIT = 7.0
RMS_EPS = 1e-6
NEG_BIG = -1e30
ROW_TILE = 512
EXPERT_TILE = 512
SUBLANES = 8
SORT_CHUNK = 256
LOCAL_ROWS = -(-(ROW_TILE * TOP_K + N_EXPERTS * (SUBLANES - 1)) // SORT_CHUNK) * SORT_CHUNK
SEG_SIZES = (512, 256, 128, 64, 32, 16, 8)
MLSTM_STREAMS = 4
VMEM_LIMIT_BYTES = 56 * 1024 * 1024


def _params(*sem):
    return pltpu.CompilerParams(dimension_semantics=sem, vmem_limit_bytes=VMEM_LIMIT_BYTES)


def _head_block_diag():
    h = np.arange(GROUP_W) // HEAD_DIM
    return (h[:, None] == h[None, :]).astype(np.float32)


def _full(shape):
    return pl.BlockSpec(shape, lambda *_: (0,) * len(shape))


def _halves(a):
    return a[:, :HALF_W], a[:, HALF_W:]


def _group_mean(x, gmat_half):
    return jnp.concatenate([jnp.dot(h.astype(BF16), gmat_half, preferred_element_type=F32) for h in _halves(x)],
                           axis=1)


def _split3(x):
    hi = x.astype(BF16)
    r = x - hi.astype(F32)
    mid = r.astype(BF16)
    lo = (r - mid.astype(F32)).astype(BF16)
    return hi, mid, lo


def _dot_f32ish(x, w_stack):
    hi = x.astype(BF16)
    lo = (x - hi.astype(F32)).astype(BF16)
    return jnp.dot(jnp.concatenate([hi, lo, hi], axis=1), w_stack, preferred_element_type=F32)


def _stack_hi_lo(w):
    hi = w.astype(BF16)
    lo = (w - hi.astype(F32)).astype(BF16)
    return jnp.concatenate([hi, hi, lo], axis=0)


def _in_proj_body(x_ref, gmix_ref, w_ref, wg_ref, gq_ref, gk_ref, gmat_ref,
                  q_ref, k_ref, v_ref, mq_ref, mk_ref, mv_ref, mo_ref, gate_ref):
    x = x_ref[...]
    xn = x * lax.rsqrt(jnp.mean(x * x, axis=-1, keepdims=True) + RMS_EPS) * gmix_ref[...]
    xb = xn.astype(BF16)

    def proj(j):
        return jnp.dot(xb, w_ref[:, j * GROUP_W:(j + 1) * GROUP_W], preferred_element_type=F32)

    def head_norm(a, g_ref):
        msq = _group_mean(a * a, gmat_ref[...])
        return a * lax.rsqrt(msq + RMS_EPS) * g_ref[...]

    q_ref[...] = head_norm(proj(0), gq_ref)
    k_ref[...] = head_norm(proj(1), gk_ref)
    v_ref[...] = proj(2)
    mq_ref[...] = proj(3)
    mk_ref[...] = proj(4)
    mv_ref[...] = proj(5)
    mo_ref[...] = proj(6)
    gate_ref[...] = jnp.dot(xb, wg_ref[...], preferred_element_type=F32)


def _in_proj(x2d, g_mix, w_main, w_gate, gq_row, gk_row, gmat):
    n = x2d.shape[0]
    tm = ROW_TILE
    row = lambda w: pl.BlockSpec((tm, w), lambda i: (i, 0))
    outs = [jax.ShapeDtypeStruct((n, GROUP_W), F32)] * N_PROJ + [jax.ShapeDtypeStruct((n, LANES), F32)]
    return pl.pallas_call(
        _in_proj_body,
        grid=(n // tm,),
        in_specs=[row(D_MODEL), _full((1, D_MODEL)), _full((D_MODEL, N_PROJ * GROUP_W)),
                  _full((D_MODEL, LANES)), _full((1, GROUP_W)), _full((1, GROUP_W)),
                  _full((HALF_W, HALF_W))],
        out_specs=[row(GROUP_W)] * N_PROJ + [row(LANES)],
        out_shape=outs,
        compiler_params=_params("arbitrary"),
        name="in_proj",
    )(x2d, g_mix, w_main, w_gate, gq_row, gk_row, gmat)


def _attn_body(q_ref, kp_ref, kc_ref, vp_ref, vc_ref, base_ref, hmask_ref, o_ref, kwin, vwin, bias_s,
               *, cq, nq, mask_first):
    tc = cq * nq
    i = pl.program_id(1)
    kwin[0:PAST_BAND, :] = kp_ref[...].astype(BF16)
    kwin[PAST_BAND:PAST_BAND + tc, :] = kc_ref[...].astype(BF16)
    vwin[0:PAST_BAND, :] = vp_ref[...].astype(BF16)
    vwin[PAST_BAND:PAST_BAND + tc, :] = vc_ref[...].astype(BF16)
    pad_rows = kwin.shape[0] - PAST_BAND - tc
    kwin[PAST_BAND + tc:, :] = jnp.zeros((pad_rows, GROUP_W), BF16)
    vwin[PAST_BAND + tc:, :] = jnp.zeros((pad_rows, GROUP_W), BF16)

    hm = hmask_ref[...]
    kk = lax.broadcasted_iota(I32, (1, KEY_WIN), 1)

    @pl.when((pl.program_id(0) == 0) & (i == 0))
    def _():
        for h in range(N_HEADS):
            rows = jnp.broadcast_to(base_ref[h:h + 1, :], (cq, KEY_WIN))
            rows = pltpu.roll(rows, 0, 1, stride=1, stride_axis=0)
            bias_s[h * cq:(h + 1) * cq, :] = jnp.where(kk < PAST_BAND + cq, rows, NEG_BIG)

    bias = bias_s[...]

    heads_per_half = N_HEADS // 2
    rows_half = heads_per_half * cq

    def chunk(j, carry):
        r0 = pl.multiple_of(j * cq, cq)
        q = q_ref[pl.ds(r0, cq), :] * (HEAD_DIM ** -0.5)
        s_parts = []
        for hh in range(2):
            half = slice(hh * HALF_W, (hh + 1) * HALF_W)
            qm = (jnp.concatenate([q[:, half]] * heads_per_half, axis=0) * hm).astype(BF16)
            kw = kwin[pl.ds(r0, KEY_WIN), half]
            s_parts.append(lax.dot_general(qm, kw, (((1,), (1,)), ((), ())), preferred_element_type=F32))
        s = jnp.concatenate(s_parts, axis=0) + bias
        if mask_first:
            first_valid = jnp.where(i == 0, PAST_BAND - r0, 0)
            s = jnp.where(kk >= first_valid, s, NEG_BIG)
        m = jnp.max(s, axis=-1, keepdims=True)
        p = jnp.exp(s - m)
        l = jnp.sum(p, axis=-1, keepdims=True)
        pb = p.astype(BF16)
        o_halves = []
        for hh in range(2):
            half = slice(hh * HALF_W, (hh + 1) * HALF_W)
            rows = slice(hh * rows_half, (hh + 1) * rows_half)
            vw = vwin[pl.ds(r0, KEY_WIN), half]
            o_all = jnp.dot(pb[rows], vw, preferred_element_type=F32) / l[rows] * hm
            o = o_all[0:cq]
            for h in range(1, heads_per_half):
                o = o + o_all[h * cq:(h + 1) * cq]
            o_halves.append(o)
        o_ref[pl.ds(r0, cq), :] = jnp.concatenate(o_halves, axis=1)
        return carry

    lax.fori_loop(0, nq, chunk, 0)


def _attention(q, k_prev_src, k_cur_src, v_prev_src, v_cur_src, bias, hmask, *, batch, tiles, cq, nq,
               prev_index, mask_first):
    tc = cq * nq
    cur = pl.BlockSpec((tc, GROUP_W), lambda b, i: (b * tiles + i, 0))
    prev = pl.BlockSpec((PAST_BAND, GROUP_W), prev_index)
    win_rows = (nq - 1) * cq + KEY_WIN
    return pl.pallas_call(
        functools.partial(_attn_body, cq=cq, nq=nq, mask_first=mask_first),
        grid=(batch, tiles),
        in_specs=[cur, prev, cur, prev, cur, _full((N_HEADS, KEY_WIN)), _full((N_HEADS // 2 * cq, HALF_W))],
        out_specs=cur,
        out_shape=jax.ShapeDtypeStruct(q.shape, F32),
        scratch_shapes=[pltpu.VMEM((win_rows, GROUP_W), BF16), pltpu.VMEM((win_rows, GROUP_W), BF16),
                        pltpu.VMEM((N_HEADS * cq, KEY_WIN), F32)],
        compiler_params=_params("arbitrary", "arbitrary"),
        name="band_attention",
    )(q, k_prev_src, k_cur_src, v_prev_src, v_cur_src, bias, hmask)


def _rel_base(rel_bias_l, cq):
    nk = PAST_BAND + cq
    dist = np.concatenate([PAST_BAND - np.arange(nk), np.zeros(KEY_WIN - nk - (cq - 1), np.int64),
                           PAST_BAND + np.arange(cq - 1, 0, -1)])
    return rel_bias_l[:, np.clip(dist, -REL_CLIP, REL_CLIP) + REL_CLIP].astype(F32)


def _head_row_mask(cq):
    h_row = np.repeat(np.arange(N_HEADS // 2), cq)
    h_col = np.arange(HALF_W) // HEAD_DIM
    return jnp.asarray((h_row[:, None] == h_col[None, :]).astype(np.float32))


def _log_sigmoid(x):
    return jnp.minimum(x, 0.0) - jnp.log(1.0 + jnp.exp(-jnp.abs(x)))


def _mlstm_body(q_ref, k_ref, v_ref, o_ref, g_ref, c0_ref, n0_ref, m0_ref,
                expand_ref, gbias_ref, bd_ref, bdb_ref, gmat_ref, ltri_ref, eye_ref, causal_ref, gml_ref,
                h_ref, ct_ref, n_ref, m_ref, ct_s, n_s, m_s, *, valid, nb):
    c = pl.program_id(1)
    last = c == pl.num_programs(1) - 1
    bd = bd_ref[...]
    bdb = bdb_ref[...]
    gmat = gmat_ref[...]
    eye = eye_ref[...] > 0.5
    causal = causal_ref[...] > 0.5
    gate_lane = lax.broadcasted_iota(I32, (nb * CHUNK, LANES), 1)
    heads_per_half = N_HEADS // 2

    def block_diag(a_half, mask):
        return jnp.concatenate([a_half] * heads_per_half, axis=0) * mask

    @pl.when(c == 0)
    def _():
        for b in range(nb):
            for hh, c0_half in enumerate(_halves(c0_ref[b])):
                ct_s[b, hh] = block_diag(c0_half, bd)
            n_s[b] = n0_ref[b]
            m_s[b] = m0_ref[b]

    rows_of = lambda a, b: a[b * CHUNK:(b + 1) * CHUNK]

    gates = jnp.concatenate([g_ref[b] for b in range(nb)], axis=0) + gbias_ref[...]
    gates = jnp.where(gate_lane < N_HEADS, gates, _log_sigmoid(gates))
    gp_all = jnp.dot(jnp.concatenate(_split3(gates), axis=1), expand_ref[...], preferred_element_type=F32)

    nums, den_parts, m_ts = [], [], []
    for b in range(nb):
        gp = rows_of(gp_all, b)
        log_i = gp[:, :GROUP_W]
        log_f = gp[:, GROUP_W:]
        if valid < CHUNK:
            live = lax.broadcasted_iota(I32, (CHUNK, GROUP_W), 0) < valid
            log_i = jnp.where(live, log_i, -jnp.inf)
            log_f = jnp.where(live, log_f, 0.0)
        cum_f = jnp.dot(ltri_ref[...], jnp.concatenate(_split3(log_f), axis=0), preferred_element_type=F32)

        b_row = jnp.sum(jnp.where(eye, log_i - cum_f, 0.0), axis=0, keepdims=True)
        m_prev = m_s[b]
        log_inter = cum_f + m_prev
        log_d = jnp.where(causal, cum_f + b_row, -jnp.inf)
        max_d = jnp.concatenate(
            [jnp.broadcast_to(jnp.max(log_d[:, h * HEAD_DIM:(h + 1) * HEAD_DIM], axis=-1, keepdims=True),
                              (CHUNK, HEAD_DIM)) for h in range(N_HEADS)], axis=1)
        m_t = jnp.maximum(log_inter, max_d)
        w_intra = jnp.exp(log_d - m_t)
        w_inter = jnp.exp(log_inter - m_t)

        q = q_ref[b] * (HEAD_DIM ** -0.5)
        k = k_ref[b]
        qb = q.astype(BF16)
        kb = k.astype(BF16)
        vb = v_ref[b].astype(BF16)
        n_prev = n_s[b]
        m_new = m_t[CHUNK - 1:CHUNK, :]
        cum_last = cum_f[CHUNK - 1:CHUNK, :]
        w_state = jnp.exp(cum_last - cum_f + log_i - m_new)
        decay = jnp.exp(cum_last + m_prev - m_new)
        kw = k * w_state
        kwb = kw.astype(BF16)

        s_halves, num_halves = [], []
        for hh in range(2):
            half = slice(hh * HALF_W, (hh + 1) * HALF_W)
            kbd = block_diag(kb[:, half], bdb)
            vbd = block_diag(vb[:, half], bdb)
            s_h = lax.dot_general(qb[:, half], kbd, (((1,), (1,)), ((), ())),
                                  preferred_element_type=F32) * w_intra[:, half]
            ct = ct_s[b, hh]
            num_halves.append(
                jnp.dot(s_h.astype(BF16), vbd, preferred_element_type=F32)
                + w_inter[:, half] * lax.dot_general(qb[:, half], ct.astype(BF16), (((1,), (1,)), ((), ())),
                                                     preferred_element_type=F32))
            s_halves.append(s_h)
            upd = lax.dot_general(vb[:, half], kwb[:, half], (((0,), (0,)), ((), ())),
                                  preferred_element_type=F32)
            ct_s[b, hh] = decay[:, half] * ct + upd * bd
        s = jnp.concatenate(s_halves, axis=1)
        nums.append(jnp.concatenate(num_halves, axis=1))
        den_parts.append((s + w_inter * q * n_prev) * float(HEAD_DIM))
        m_ts.append(m_t)
        n_s[b] = decay * n_prev + jnp.sum(kw, axis=0, keepdims=True)
        m_s[b] = m_new

    den_terms = jnp.concatenate(den_parts, axis=0)
    den_hi = den_terms.astype(BF16)
    den_lo = den_terms - den_hi.astype(F32)
    den = _group_mean(den_hi.astype(F32), gmat) + _group_mean(den_lo, gmat)
    hb = jnp.concatenate(nums, axis=0) / jnp.maximum(jnp.abs(den), jnp.exp(-jnp.concatenate(m_ts, axis=0)))
    hn = hb * lax.rsqrt(_group_mean(hb * hb, gmat) + RMS_EPS) * gml_ref[...]
    for b in range(nb):
        h_ref[b] = jax.nn.sigmoid(o_ref[b]) * rows_of(hn, b)

    @pl.when(last)
    def _():
        for b in range(nb):
            out_halves = []
            for hh in range(2):
                ct_half = ct_s[b, hh]
                acc = ct_half[0:HEAD_DIM]
                for h in range(1, heads_per_half):
                    acc = acc + ct_half[h * HEAD_DIM:(h + 1) * HEAD_DIM]
                out_halves.append(acc)
            ct_ref[b] = jnp.concatenate(out_halves, axis=1)
            n_ref[b] = n_s[b]
            m_ref[b] = m_s[b]


def _mlstm(mq, mk, mv, mo, gates, c0t, n0, m0, consts, *, valid):
    batch, t, _ = mq.shape
    chunks = t // CHUNK
    nb = MLSTM_STREAMS
    row = lambda w: pl.BlockSpec((nb, CHUNK, w), lambda g, c: (g, c, 0))
    per_b = lambda r: pl.BlockSpec((nb, r, GROUP_W), lambda g, c: (g, 0, 0))
    expand, gbias, bd, bdb, gmat, ltri, eye, causal, gml = consts
    return pl.pallas_call(
        functools.partial(_mlstm_body, valid=valid, nb=nb),
        grid=(batch // nb, chunks),
        in_specs=[row(GROUP_W)] * 4 + [row(LANES), per_b(HEAD_DIM), per_b(1), per_b(1),
                  _full((3 * LANES, 2 * GROUP_W)), _full((1, LANES)), _full((HALF_W, HALF_W)),
                  _full((HALF_W, HALF_W)), _full((HALF_W, HALF_W)), _full((CHUNK, 3 * CHUNK)),
                  _full((CHUNK, GROUP_W)), _full((CHUNK, GROUP_W)), _full((1, GROUP_W))],
        out_specs=[row(GROUP_W), per_b(HEAD_DIM), per_b(1), per_b(1)],
        out_shape=[jax.ShapeDtypeStruct(mq.shape, F32),
                   jax.ShapeDtypeStruct((batch, HEAD_DIM, GROUP_W), F32),
                   jax.ShapeDtypeStruct((batch, 1, GROUP_W), F32),
                   jax.ShapeDtypeStruct((batch, 1, GROUP_W), F32)],
        scratch_shapes=[pltpu.VMEM((nb, 2, HALF_W, HALF_W), F32), pltpu.VMEM((nb, 1, GROUP_W), F32),
                        pltpu.VMEM((nb, 1, GROUP_W), F32)],
        compiler_params=_params("arbitrary", "arbitrary"),
        name="mlstm",
    )(mq, mk, mv, mo, gates, c0t, n0, m0, expand, gbias, bd, bdb, gmat, ltri, eye, causal, gml)


def _mlstm_consts(b_igate_l, b_fgate_l, g_mlstm_l):
    expand = np.zeros((LANES, 2 * GROUP_W), np.float32)
    for h in range(N_HEADS):
        expand[h, h * HEAD_DIM:(h + 1) * HEAD_DIM] = 1.0
        expand[N_HEADS + h, GROUP_W + h * HEAD_DIM:GROUP_W + (h + 1) * HEAD_DIM] = 1.0
    gbias = jnp.concatenate([b_igate_l.astype(F32), b_fgate_l.astype(F32),
                             jnp.zeros((LANES - 2 * N_HEADS,), F32)])[None, :]
    bd = _head_block_diag()[:HALF_W, :HALF_W]
    ltri = np.tril(np.ones((CHUNK, CHUNK), np.float32))
    s_of_lane = np.arange(GROUP_W) % HEAD_DIM
    t = np.arange(CHUNK)
    eye = (t[:, None] == s_of_lane[None, :]).astype(np.float32)
    causal = (s_of_lane[None, :] <= t[:, None]).astype(np.float32)
    return (jnp.asarray(np.concatenate([expand] * 3, axis=0), BF16), gbias, jnp.asarray(bd), jnp.asarray(bd, BF16),
            jnp.asarray(bd / HEAD_DIM, BF16), jnp.asarray(np.concatenate([ltri] * 3, axis=1), BF16), jnp.asarray(eye),
            jnp.asarray(causal), g_mlstm_l.astype(F32).reshape(1, GROUP_W))


def _out_proj_body(xp_ref, xs_ref, ap_ref, as_ref, hp_ref, hs_ref, wa_ref, wm_ref, gffn_ref, wr_ref, br_ref,
                   lstrict_ref, ustrict_ref, y_ref, xf_ref, gate_ref, pos_ref, seg_ref, *, prompt_tiles):
    is_prompt = pl.program_id(0) < prompt_tiles
    x = jnp.where(is_prompt, xp_ref[...], xs_ref[...])
    att = jnp.where(is_prompt, ap_ref[...], as_ref[...])
    hm = jnp.where(is_prompt, hp_ref[...], hs_ref[...])
    y = (x + jnp.dot(att.astype(BF16), wa_ref[...], preferred_element_type=F32)
         + jnp.dot(hm.astype(BF16), wm_ref[...], preferred_element_type=F32))
    y_ref[...] = y
    xf = y * lax.rsqrt(jnp.mean(y * y, axis=-1, keepdims=True) + RMS_EPS) * gffn_ref[...]
    xf_ref[...] = xf
    _route_tile(_dot_f32ish(xf, wr_ref[...]) + br_ref[...], lstrict_ref, ustrict_ref, gate_ref, pos_ref, seg_ref)


def _out_proj(xp, xs, att_p, att_s, hm_p, hm_s, wa, wm, g_ffn, w_router, b_router):
    tm = ROW_TILE
    pt, st = xp.shape[0] // tm, xs.shape[0] // tm
    n = xp.shape[0] + xs.shape[0]
    p_row = lambda w: pl.BlockSpec((tm, w), lambda i: (jnp.minimum(i, pt - 1), 0))
    s_row = lambda w: pl.BlockSpec((tm, w), lambda i: (jnp.maximum(i - pt, 0), 0))
    row = lambda w: pl.BlockSpec((tm, w), lambda i: (i, 0))
    lstrict = jnp.asarray(np.tril(np.ones((tm, tm), np.float32), -1), BF16)
    ustrict = jnp.asarray(np.triu(np.ones((LANES, LANES), np.float32), 1), BF16)
    return pl.pallas_call(
        functools.partial(_out_proj_body, prompt_tiles=pt),
        grid=(pt + st,),
        in_specs=[p_row(D_MODEL), s_row(D_MODEL), p_row(GROUP_W), s_row(GROUP_W), p_row(GROUP_W), s_row(GROUP_W),
                  _full((GROUP_W, D_MODEL)), _full((GROUP_W, D_MODEL)),
                  _full((1, D_MODEL)), _full((3 * D_MODEL, LANES)), _full((1, LANES)),
                  _full((tm, tm)), _full((LANES, LANES))],
        out_specs=[row(D_MODEL), row(D_MODEL), row(LANES), row(LANES),
                   pl.BlockSpec((SUBLANES, LANES), lambda i: (i, 0))],
        out_shape=[jax.ShapeDtypeStruct((n, D_MODEL), F32), jax.ShapeDtypeStruct((n, D_MODEL), F32),
                   jax.ShapeDtypeStruct((n, LANES), F32), jax.ShapeDtypeStruct((n, LANES), F32),
                   jax.ShapeDtypeStruct((n // tm * SUBLANES, LANES), I32)],
        compiler_params=_params("arbitrary"),
        name="out_proj_router",
    )(xp, xs, att_p, att_s, hm_p, hm_s, wa, wm, g_ffn, w_router, b_router, lstrict, ustrict)


def _route_tile(logits, lstrict_ref, ustrict_ref, gate_ref, pos_ref, seg_ref):
    tt = logits.shape[0]
    lane = lax.broadcasted_iota(I32, (tt, LANES), 1)
    work = jnp.where(lane < N_EXPERTS, logits, -jnp.inf)
    vals, idxs = [], []
    for _ in range(TOP_K):
        m = jnp.max(work, axis=-1, keepdims=True)
        idx = jnp.min(jnp.where(work == m, lane, LANES), axis=-1, keepdims=True)
        vals.append(m)
        idxs.append(idx)
        work = jnp.where(lane == idx, -jnp.inf, work)
    exps = [jnp.exp(v - vals[0]) for v in vals]
    total = exps[0] + exps[1] + exps[2] + exps[3]

    chosen = jnp.zeros((tt, LANES), F32)
    for idx in idxs:
        chosen = chosen + (lane == idx).astype(F32)
    before = jnp.dot(lstrict_ref[...], chosen.astype(BF16), preferred_element_type=F32)
    count = jnp.sum(chosen, axis=0, keepdims=True)
    groups = jnp.floor((count + (SUBLANES - 1)) * (1.0 / SUBLANES))
    groups8 = jnp.broadcast_to(groups, (SUBLANES, LANES)).astype(BF16)
    start = jnp.dot(groups8, ustrict_ref[...], preferred_element_type=F32) * float(SUBLANES)
    local = before + start[0:1, :]

    gate_out = jnp.zeros((tt, LANES), F32)
    pos_out = jnp.zeros((tt, LANES), F32)
    for k in range(TOP_K):
        pos = jnp.sum(jnp.where(lane == idxs[k], local, 0.0), axis=-1, keepdims=True)
        gate_out = jnp.where(lane == k, exps[k] / total, gate_out)
        pos_out = jnp.where(lane == k, pos, pos_out)
    gate_ref[...] = gate_out
    pos_ref[...] = pos_out
    row = lax.broadcasted_iota(I32, (SUBLANES, LANES), 0)
    seg = jnp.where(row == 0, groups * float(SUBLANES), jnp.where(row == 1, start, 0.0))
    seg_ref[...] = seg.astype(I32)


def _segment_copies(i, len_ref, lst_ref, off_ref, make_copy, act):
    for e in range(N_EXPERTS):
        seg = i * N_EXPERTS + e
        length = len_ref[seg]
        local = lst_ref[seg]
        glob = off_ref[seg]
        for size in SEG_SIZES:
            take = length & size

            @pl.when(take != 0)
            def _(local=local, glob=glob, size=size):
                act(make_copy(pl.multiple_of(local, SUBLANES), pl.multiple_of(glob, SUBLANES), size))

            local = local + take
            glob = glob + take


def _local_onehot(pos_rows, base, rows):
    r = (lax.broadcasted_iota(I32, (rows, pos_rows[0].shape[1]), 0) + base).astype(F32)
    out = jnp.zeros(r.shape, F32)
    for k in range(TOP_K):
        out = jnp.where(r == pos_rows[k], 1.0, out)
    return out


def _dispatch_body(len_ref, lst_ref, off_ref, fill_ref, nused_ref, xf_ref, pos_ref, xs_hbm,
                   xloc, zbuf, sem, zsem, *, tm, n_tiles):
    i = pl.program_id(0)
    tt = xf_ref.shape[0]
    fill_rows = zbuf.shape[0]

    @pl.when(i == 0)
    def _():
        zbuf[...] = jnp.zeros_like(zbuf)

        def fill(e):
            start = pl.multiple_of(fill_ref[e], SUBLANES)
            return pltpu.make_async_copy(zbuf, xs_hbm.at[pl.ds(start, fill_rows)], zsem)

        for e in range(N_EXPERTS):
            fill(e).start()
            fill(e).wait()

        def tail(j, carry):
            cp = pltpu.make_async_copy(zbuf.at[pl.ds(0, tm)], xs_hbm.at[pl.ds(pl.multiple_of(j * tm, tm), tm)], zsem)
            cp.start()
            cp.wait()
            return carry

        lax.fori_loop(nused_ref[0], n_tiles, tail, 0)

    pos_t = jnp.transpose(pos_ref[...])
    pos_rows = [pos_t[k:k + 1, :] for k in range(TOP_K)]
    xb = xf_ref[...].astype(BF16)

    slot = lax.rem(i, 2)

    def sort_rows(c, carry):
        r0 = pl.multiple_of(c * SORT_CHUNK, SORT_CHUNK)
        sel = _local_onehot(pos_rows, r0, SORT_CHUNK).astype(BF16)
        xloc[slot, pl.ds(r0, SORT_CHUNK), :] = jnp.dot(sel, xb, preferred_element_type=F32)
        return carry

    lax.fori_loop(0, LOCAL_ROWS // SORT_CHUNK, sort_rows, 0)

    def copies(step, which):
        def make_copy(local, glob, size):
            return pltpu.make_async_copy(xloc.at[which, pl.ds(local, size)], xs_hbm.at[pl.ds(glob, size)],
                                         sem.at[which])
        return functools.partial(_segment_copies, step, len_ref, lst_ref, off_ref, make_copy)

    copies(i, slot)(lambda cp: cp.start())

    @pl.when(i > 0)
    def _():
        copies(i - 1, 1 - slot)(lambda cp: cp.wait())

    @pl.when(i == pl.num_programs(0) - 1)
    def _():
        copies(i, slot)(lambda cp: cp.wait())


def _dispatch(seg_len, seg_local, seg_off, fill_start, n_used, xf, pos, n_tiles):
    n = xf.shape[0]
    tt, tm = ROW_TILE, EXPERT_TILE
    fill_rows = tm + SUBLANES
    grid_spec = pltpu.PrefetchScalarGridSpec(
        num_scalar_prefetch=5,
        grid=(n // tt,),
        in_specs=[pl.BlockSpec((tt, D_MODEL), lambda i, *_: (i, 0)),
                  pl.BlockSpec((tt, LANES), lambda i, *_: (i, 0))],
        out_specs=pl.BlockSpec(memory_space=pl.ANY),
        scratch_shapes=[pltpu.VMEM((2, LOCAL_ROWS, D_MODEL), F32), pltpu.VMEM((fill_rows, D_MODEL), F32),
                        pltpu.SemaphoreType.DMA((2,)), pltpu.SemaphoreType.DMA(())],
    )
    return pl.pallas_call(
        functools.partial(_dispatch_body, tm=tm, n_tiles=n_tiles + 2),
        grid_spec=grid_spec,
        out_shape=jax.ShapeDtypeStruct(((n_tiles + 2) * tm, D_MODEL), F32),
        compiler_params=_params("arbitrary"),
        name="expert_dispatch",
    )(seg_len, seg_local, seg_off, fill_start, n_used, xf, pos)


def _expert_body(te_ref, nused_ref, x_ref, wup_ref, wdn_ref, perm_ref, bg_ref, bl_ref, bd_ref,
                 y_ref, wg_s, wl_s, wd_s):
    i = pl.program_id(0)
    n_used = nused_ref[0]

    @pl.when(i >= n_used)
    def _():
        y_ref[...] = jnp.zeros_like(y_ref)

    @pl.when((i == 0) | (te_ref[i] != te_ref[jnp.maximum(i - 1, 0)]))
    def _():
        perm = perm_ref[...]
        for c in range(2 * D_FF // 256):
            blk = wup_ref[0, :, c * 256:(c + 1) * 256].astype(BF16)
            sep = jnp.dot(blk, perm, preferred_element_type=F32).astype(BF16)
            wg_s[:, c * 128:(c + 1) * 128] = sep[:, :128]
            wl_s[:, c * 128:(c + 1) * 128] = sep[:, 128:]
        wd_s[...] = wdn_ref[0].astype(BF16)

    @pl.when(i < n_used)
    def _():
        x = x_ref[...].astype(BF16)
        glu = jnp.minimum(jnp.dot(x, wg_s[...], preferred_element_type=F32) + bg_ref[0], SWIGLU_LIMIT)
        lin = jnp.clip(jnp.dot(x, wl_s[...], preferred_element_type=F32) + bl_ref[0], -SWIGLU_LIMIT, SWIGLU_LIMIT)
        act = glu * jax.nn.sigmoid(SWIGLU_ALPHA * glu) * (lin + 1.0)
        y_ref[...] = jnp.dot(act.astype(BF16), wd_s[...], preferred_element_type=F32) + bd_ref[0]


def _deinterleave_perm():
    p = np.zeros((256, 256), np.float32)
    j = np.arange(128)
    p[2 * j, j] = 1.0
    p[2 * j + 1, 128 + j] = 1.0
    return jnp.asarray(p, BF16)


def _experts(tile_expert, n_used, x_sorted, w_up, w_down, b_glu, b_lin, b_down):
    tm = EXPERT_TILE
    n_tiles = tile_expert.shape[0]
    wspec = lambda k, n: pl.BlockSpec((1, k, n), lambda i, te, nu: (te[i], 0, 0))
    grid_spec = pltpu.PrefetchScalarGridSpec(
        num_scalar_prefetch=2,
        grid=(n_tiles,),
        in_specs=[pl.BlockSpec((tm, D_MODEL), lambda i, te, nu: (jnp.minimum(i, nu[0] - 1), 0)),
                  wspec(D_MODEL, 2 * D_FF), wspec(D_FF, D_MODEL),
                  pl.BlockSpec((256, 256), lambda i, te, nu: (0, 0)),
                  wspec(1, D_FF), wspec(1, D_FF), wspec(1, D_MODEL)],
        out_specs=pl.BlockSpec((tm, D_MODEL), lambda i, te, nu: (i, 0)),
        scratch_shapes=[pltpu.VMEM((D_MODEL, D_FF), BF16), pltpu.VMEM((D_MODEL, D_FF), BF16),
                        pltpu.VMEM((D_FF, D_MODEL), BF16)],
    )
    return pl.pallas_call(
        _expert_body,
        grid_spec=grid_spec,
        out_shape=jax.ShapeDtypeStruct((n_tiles * tm, D_MODEL), F32),
        compiler_params=_params("arbitrary"),
        name="expert_ffn",
    )(tile_expert, n_used, x_sorted, w_up, w_down, _deinterleave_perm(), b_glu, b_lin, b_down)


def _combine_body(len_ref, lst_ref, off_ref, y_ref, gate_ref, pos_ref, rows_hbm, outp_ref, outs_ref,
                  yloc, acc_s, wide_s, sem, *, prompt_tiles):
    i = pl.program_id(0)
    slot = lax.rem(i, 2)
    tt = y_ref.shape[0]

    def copies(step, which):
        def make_copy(local, glob, size):
            return pltpu.make_async_copy(rows_hbm.at[pl.ds(glob, size)], yloc.at[which, pl.ds(local, size)],
                                         sem.at[which])
        return functools.partial(_segment_copies, step, len_ref, lst_ref, off_ref, make_copy)

    @pl.when(i == 0)
    def _():
        yloc[...] = jnp.zeros_like(yloc)
        copies(i, slot)(lambda cp: cp.start())

    @pl.when(i + 1 < pl.num_programs(0))
    def _():
        copies(i + 1, 1 - slot)(lambda cp: cp.start())

    pos = pos_ref[...]
    gate = gate_ref[...]
    for k in range(TOP_K):
        wide_s[k] = jnp.broadcast_to(pos[:, k:k + 1], (tt, LANES))
        wide_s[TOP_K + k] = jnp.broadcast_to(gate[:, k:k + 1], (tt, LANES))
    acc_s[...] = y_ref[...]
    copies(i, slot)(lambda cp: cp.wait())

    lane = lax.broadcasted_iota(I32, (tt, LANES), 1).astype(F32)

    def weigh(c, carry):
        r0 = pl.multiple_of(c * SORT_CHUNK, SORT_CHUNK)
        halves = []
        for half in range(SORT_CHUNK // LANES):
            r = lane + (r0 + half * LANES).astype(F32)
            w = jnp.zeros((tt, LANES), F32)
            for k in range(TOP_K):
                w = jnp.where(r == wide_s[k], wide_s[TOP_K + k], w)
            halves.append(w.astype(BF16))
        w = jnp.concatenate(halves, axis=1)
        acc_s[...] += jnp.dot(w, yloc[slot, pl.ds(r0, SORT_CHUNK), :].astype(BF16), preferred_element_type=F32)
        return carry

    lax.fori_loop(0, LOCAL_ROWS // SORT_CHUNK, weigh, 0)

    @pl.when(i < prompt_tiles)
    def _():
        outp_ref[...] = acc_s[...]

    @pl.when(i >= prompt_tiles)
    def _():
        outs_ref[...] = acc_s[...]


def _combine(seg_len, seg_local, seg_off, y, gates, pos, y_rows, n_prompt):
    n = y.shape[0]
    tt = ROW_TILE
    nt, pt = n // tt, n_prompt // tt
    grid_spec = pltpu.PrefetchScalarGridSpec(
        num_scalar_prefetch=3,
        grid=(nt,),
        in_specs=[pl.BlockSpec((tt, D_MODEL), lambda i, *_: (i, 0)),
                  pl.BlockSpec((tt, LANES), lambda i, *_: (i, 0)),
                  pl.BlockSpec((tt, LANES), lambda i, *_: (i, 0)),
                  pl.BlockSpec(memory_space=pl.ANY)],
        out_specs=[pl.BlockSpec((tt, D_MODEL), lambda i, *_: (jnp.minimum(i, pt - 1), 0)),
                   pl.BlockSpec((tt, D_MODEL), lambda i, *_: (jnp.maximum(i - pt, 0), 0))],
        scratch_shapes=[pltpu.VMEM((2, LOCAL_ROWS, D_MODEL), F32), pltpu.VMEM((tt, D_MODEL), F32),
                        pltpu.VMEM((2 * TOP_K, tt, LANES), F32), pltpu.SemaphoreType.DMA((2,))],
    )
    return pl.pallas_call(
        functools.partial(_combine_body, prompt_tiles=pt),
        grid_spec=grid_spec,
        out_shape=[jax.ShapeDtypeStruct((n_prompt, D_MODEL), F32),
                   jax.ShapeDtypeStruct((n - n_prompt, D_MODEL), F32)],
        compiler_params=_params("arbitrary"),
        name="expert_combine",
    )(seg_len, seg_local, seg_off, y, gates, pos, y_rows)


def _moe(y, xf, gates, pos, seg, ffn_w, n_prompt):
    n = y.shape[0]
    tm, tt = EXPERT_TILE, ROW_TILE
    n_tt = n // tt
    seg = seg.reshape(n_tt, SUBLANES, LANES)
    seg_len, seg_local = seg[:, 0, :N_EXPERTS], seg[:, 1, :N_EXPERTS]
    rows = jnp.sum(seg_len, axis=0)
    padded = (rows + tm - 1) // tm * tm
    pad_end = jnp.cumsum(padded)
    pad_start = pad_end - padded
    seg_off = pad_start[None, :] + jnp.cumsum(seg_len, axis=0) - seg_len
    n_tiles = -(-(n * TOP_K + n_tt * N_EXPERTS * (SUBLANES - 1) + N_EXPERTS * (tm - 1)) // tm)
    tile_expert = jnp.minimum(jnp.sum(pad_end[None, :] <= (jnp.arange(n_tiles) * tm)[:, None], axis=1),
                              N_EXPERTS - 1).astype(I32)
    n_used = (pad_end[-1:] // tm).astype(I32)
    fill_start = (pad_start + rows).astype(I32)
    flat = lambda a: a.astype(I32).reshape(n_tt * N_EXPERTS)

    x_sorted = _dispatch(flat(seg_len), flat(seg_local), flat(seg_off), fill_start, n_used, xf, pos, n_tiles)
    y_rows = _experts(tile_expert, n_used, x_sorted, *ffn_w)
    return _combine(flat(seg_len), flat(seg_local), flat(seg_off), y, gates, pos, y_rows, n_prompt)


def _mixer(x, lw, cache, state):
    b, t, _ = x.shape
    n = b * t
    q, k, v, mq, mk, mv, mo, gates = _in_proj(x.reshape(n, D_MODEL), lw["g_mix"], lw["w_main"], lw["w_gate"],
                                              lw["gq"], lw["gk"], lw["gmat"])
    heads = lambda a, rows: a.reshape(b, rows, N_HEADS, HEAD_DIM)
    if cache is None:
        tiles = t // PAST_BAND
        att = _attention(q, k, k, v, v, lw["bias_prompt"], lw["hmask_prompt"], batch=b, tiles=tiles, cq=CHUNK,
                         nq=PAST_BAND // CHUNK,
                         prev_index=lambda bi, i: (bi * tiles + jnp.maximum(i - 1, 0), 0), mask_first=True)
        keep = min(PAST_BAND, t)
        k_new = heads(k.reshape(b, t, GROUP_W)[:, t - keep:], keep)
        v_new = heads(v.reshape(b, t, GROUP_W)[:, t - keep:], keep)
    else:
        ck, cv = cache
        att = _attention(q, ck.reshape(b * PAST_BAND, GROUP_W), k, cv.reshape(b * PAST_BAND, GROUP_W), v,
                         lw["bias_sample"], lw["hmask_sample"], batch=b, tiles=1, cq=t, nq=1,
                         prev_index=lambda bi, i: (bi, 0), mask_first=False)
        k_new, v_new = heads(k, t), heads(v, t)

    tp = -(-t // CHUNK) * CHUNK
    valid = t if t < CHUNK else CHUNK

    def streams(a):
        a = a.reshape(b, t, -1)
        return a if tp == t else jnp.pad(a, ((0, 0), (0, tp - t), (0, 0)))

    if state is None:
        c0t = jnp.zeros((b, HEAD_DIM, GROUP_W), F32)
        n0 = jnp.zeros((b, 1, GROUP_W), F32)
        m0 = jnp.zeros((b, 1, GROUP_W), F32)
    else:
        c_in, n_in, m_in = state
        c0t = c_in.astype(F32).transpose(0, 3, 1, 2).reshape(b, HEAD_DIM, GROUP_W)
        n0 = n_in.astype(F32).reshape(b, 1, GROUP_W)
        m0 = jnp.repeat(m_in.astype(F32), HEAD_DIM, axis=-1).reshape(b, 1, GROUP_W)
    hm, ct, n_out, m_out = _mlstm(streams(mq), streams(mk), streams(mv), streams(mo), streams(gates),
                                  c0t, n0, m0, lw["mlstm_consts"], valid=valid)
    hm = hm[:, :t].reshape(n, GROUP_W)
    c_new = ct.reshape(b, HEAD_DIM, N_HEADS, HEAD_DIM).transpose(0, 2, 3, 1)
    n_new = n_out.reshape(b, N_HEADS, HEAD_DIM)
    m_new = m_out.reshape(b, N_HEADS, HEAD_DIM)[:, :, 0]
    return att, hm, (k_new, v_new, c_new, n_new, m_new)


def kernel(x_prompt, x_sample, cache_k, cache_v, state_C, state_n, state_m, g_mix, w_in, g_q, g_k, rel_bias,
           b_igate, b_fgate, g_mlstm, w_out, g_ffn, w_router, b_router, w_up, b_up, w_down, b_down):
    depth = w_in.shape[0]
    yp, ys = x_prompt, x_sample
    bs, ts = x_sample.shape[0], x_sample.shape[1]
    n_prompt = x_prompt.shape[0] * x_prompt.shape[1]
    st_prompt, st_sample = [], []
    n_main = N_PROJ * GROUP_W
    gmat = jnp.asarray(_head_block_diag()[:HALF_W, :HALF_W] / HEAD_DIM, BF16)
    for l in range(depth):
        lw = dict(
            g_mix=g_mix[l].astype(F32)[None, :],
            w_main=w_in[l][:, :n_main].astype(BF16),
            w_gate=jnp.pad(w_in[l][:, n_main:], ((0, 0), (0, LANES - 2 * N_HEADS))).astype(BF16),
            gq=jnp.tile(g_q[l].astype(F32), N_HEADS)[None, :],
            gk=jnp.tile(g_k[l].astype(F32), N_HEADS)[None, :],
            gmat=gmat,
            bias_prompt=_rel_base(rel_bias[l], CHUNK),
            hmask_prompt=_head_row_mask(CHUNK),
            bias_sample=_rel_base(rel_bias[l], ts),
            hmask_sample=_head_row_mask(ts),
            mlstm_consts=_mlstm_consts(b_igate[l], b_fgate[l], g_mlstm[l]),
        )
        ffn_w = (w_up[l].astype(F32), w_down[l].astype(F32),
                 b_up[l][:, None, 0::2].astype(F32), b_up[l][:, None, 1::2].astype(F32),
                 b_down[l][:, None, :].astype(F32))
        att_p, hm_p, sp = _mixer(yp, lw, None, None)
        cache = (cache_k[l].reshape(bs, PAST_BAND, GROUP_W), cache_v[l].reshape(bs, PAST_BAND, GROUP_W))
        att_s, hm_s, ss = _mixer(ys, lw, cache, (state_C[l], state_n[l], state_m[l]))
        y, xf, gates, pos, seg = _out_proj(
            yp.reshape(-1, D_MODEL), ys.reshape(-1, D_MODEL), att_p, att_s, hm_p, hm_s,
            w_out[l][:GROUP_W].astype(BF16), w_out[l][GROUP_W:].astype(BF16), g_ffn[l].astype(F32)[None, :],
            _stack_hi_lo(jnp.pad(w_router[l].astype(F32), ((0, 0), (0, LANES - N_EXPERTS)))),
            jnp.pad(b_router[l].astype(F32), (0, LANES - N_EXPERTS))[None, :])
        out_p, out_s = _moe(y, xf, gates, pos, seg, ffn_w, n_prompt)
        yp, ys = out_p.reshape(x_prompt.shape), out_s.reshape(x_sample.shape)
        st_prompt.append(sp)
        st_sample.append(ss)
    k_p, v_p, c_p, n_p, m_p = [jnp.stack(a) for a in zip(*st_prompt)]
    k_s, v_s, c_s, n_s, m_s = [jnp.stack(a) for a in zip(*st_sample)]
    return (yp, ys, k_p, v_p, c_p, n_p, m_p, k_s, v_s, c_s, n_s, m_s)
```

```python
import functools

import numpy as np
import jax
import jax.numpy as jnp
from jax import lax
from jax.experimental import pallas as pl
from jax.experimental.pallas import tpu as pltpu

F32 = jnp.float32
BF16 = jnp.bfloat16
I32 = jnp.int32

D_MODEL = 1024
N_HEADS = 8
HEAD_DIM = 64
GROUP_W = N_HEADS * HEAD_DIM
HALF_W = GROUP_W // 2
N_PROJ = 7
LANES = 128
CHUNK = 64
PAST_BAND = 512
KEY_WIN = 640
REL_CLIP = 256
N_EXPERTS = 32
TOP_K = 4
D_FF = 1024
SWIGLU_ALPHA = 1.702
SWIGLU_LIMIT = 7.0
RMS_EPS = 1e-6
NEG_BIG = -1e30
ROW_TILE = 512
EXPERT_TILE = 512
SUBLANES = 8
SORT_CHUNK = 256
LOCAL_ROWS = -(-(ROW_TILE * TOP_K + N_EXPERTS * (SUBLANES - 1)) // SORT_CHUNK) * SORT_CHUNK
SEG_SIZES = (512, 256, 128, 64, 32, 16, 8)
MLSTM_STREAMS = 4
VMEM_LIMIT_BYTES = 56 * 1024 * 1024


def _params(*sem):
    return pltpu.CompilerParams(dimension_semantics=sem, vmem_limit_bytes=VMEM_LIMIT_BYTES)


def _head_block_diag():
    h = np.arange(GROUP_W) // HEAD_DIM
    return (h[:, None] == h[None, :]).astype(np.float32)


def _full(shape):
    return pl.BlockSpec(shape, lambda *_: (0,) * len(shape))


def _halves(a):
    return a[:, :HALF_W], a[:, HALF_W:]


def _group_mean(x, gmat_half):
    return jnp.concatenate([jnp.dot(h.astype(BF16), gmat_half, preferred_element_type=F32) for h in _halves(x)],
                           axis=1)


def _split3(x):
    hi = x.astype(BF16)
    r = x - hi.astype(F32)
    mid = r.astype(BF16)
    lo = (r - mid.astype(F32)).astype(BF16)
    return hi, mid, lo


def _dot_f32ish(x, w_stack):
    hi = x.astype(BF16)
    lo = (x - hi.astype(F32)).astype(BF16)
    return jnp.dot(jnp.concatenate([hi, lo, hi], axis=1), w_stack, preferred_element_type=F32)


def _stack_hi_lo(w):
    hi = w.astype(BF16)
    lo = (w - hi.astype(F32)).astype(BF16)
    return jnp.concatenate([hi, hi, lo], axis=0)


def _in_proj_body(x_ref, gmix_ref, w_ref, wg_ref, gq_ref, gk_ref, gmat_ref,
                  q_ref, k_ref, v_ref, mq_ref, mk_ref, mv_ref, mo_ref, gate_ref):
    x = x_ref[...]
    xn = x * lax.rsqrt(jnp.mean(x * x, axis=-1, keepdims=True) + RMS_EPS) * gmix_ref[...]
    xb = xn.astype(BF16)

    def proj(j):
        return jnp.dot(xb, w_ref[:, j * GROUP_W:(j + 1) * GROUP_W], preferred_element_type=F32)

    def head_norm(a, g_ref):
        msq = _group_mean(a * a, gmat_ref[...])
        return a * lax.rsqrt(msq + RMS_EPS) * g_ref[...]

    q_ref[...] = head_norm(proj(0), gq_ref).astype(q_ref.dtype)
    k_ref[...] = head_norm(proj(1), gk_ref)
    v_ref[...] = proj(2)
    mq_ref[...] = proj(3)
    mk_ref[...] = proj(4)
    mv_ref[...] = proj(5).astype(mv_ref.dtype)
    mo_ref[...] = proj(6)
    gate_ref[...] = jnp.dot(xb, wg_ref[...], preferred_element_type=F32)


def _in_proj(x2d, g_mix, w_main, w_gate, gq_row, gk_row, gmat):
    n = x2d.shape[0]
    tm = ROW_TILE
    row = lambda w: pl.BlockSpec((tm, w), lambda i: (i, 0))
    proj_dtypes = (BF16, F32, F32, F32, F32, BF16, F32)
    outs = [jax.ShapeDtypeStruct((n, GROUP_W), dt) for dt in proj_dtypes] + [jax.ShapeDtypeStruct((n, LANES), F32)]
    return pl.pallas_call(
        _in_proj_body,
        grid=(n // tm,),
        in_specs=[row(D_MODEL), _full((1, D_MODEL)), _full((D_MODEL, N_PROJ * GROUP_W)),
                  _full((D_MODEL, LANES)), _full((1, GROUP_W)), _full((1, GROUP_W)),
                  _full((HALF_W, HALF_W))],
        out_specs=[row(GROUP_W)] * N_PROJ + [row(LANES)],
        out_shape=outs,
        compiler_params=_params("arbitrary"),
        name="in_proj",
    )(x2d, g_mix, w_main, w_gate, gq_row, gk_row, gmat)


def _attn_body(q_ref, kp_ref, kc_ref, vp_ref, vc_ref, base_ref, hmask_ref, o_ref, kwin, vwin, bias_s,
               *, cq, nq, mask_first):
    tc = cq * nq
    i = pl.program_id(1)
    kwin[0:PAST_BAND, :] = kp_ref[...].astype(BF16)
    kwin[PAST_BAND:PAST_BAND + tc, :] = kc_ref[...].astype(BF16)
    vwin[0:PAST_BAND, :] = vp_ref[...].astype(BF16)
    vwin[PAST_BAND:PAST_BAND + tc, :] = vc_ref[...].astype(BF16)
    pad_rows = kwin.shape[0] - PAST_BAND - tc
    kwin[PAST_BAND + tc:, :] = jnp.zeros((pad_rows, GROUP_W), BF16)
    vwin[PAST_BAND + tc:, :] = jnp.zeros((pad_rows, GROUP_W), BF16)

    hm = hmask_ref[...]
    kk = lax.broadcasted_iota(I32, (1, KEY_WIN), 1)

    @pl.when((pl.program_id(0) == 0) & (i == 0))
    def _():
        for h in range(N_HEADS):
            rows = jnp.broadcast_to(base_ref[h:h + 1, :], (cq, KEY_WIN))
            rows = pltpu.roll(rows, 0, 1, stride=1, stride_axis=0)
            bias_s[h * cq:(h + 1) * cq, :] = jnp.where(kk < PAST_BAND + cq, rows, NEG_BIG)

    bias = bias_s[...]

    heads_per_half = N_HEADS // 2
    rows_half = heads_per_half * cq

    def chunk(j, carry):
        r0 = pl.multiple_of(j * cq, cq)
        q = q_ref[pl.ds(r0, cq), :] * (HEAD_DIM ** -0.5)
        s_parts = []
        for hh in range(2):
            half = slice(hh * HALF_W, (hh + 1) * HALF_W)
            qm = (jnp.concatenate([q[:, half]] * heads_per_half, axis=0) * hm).astype(BF16)
            kw = kwin[pl.ds(r0, KEY_WIN), half]
            s_parts.append(lax.dot_general(qm, kw, (((1,), (1,)), ((), ())), preferred_element_type=F32))
        s = jnp.concatenate(s_parts, axis=0) + bias
        if mask_first:
            first_valid = jnp.where(i == 0, PAST_BAND - r0, 0)
            s = jnp.where(kk >= first_valid, s, NEG_BIG)
        m = jnp.max(s, axis=-1, keepdims=True)
        p = jnp.exp(s - m)
        l = jnp.sum(p, axis=-1, keepdims=True)
        pb = p.astype(BF16)
        o_halves = []
        for hh in range(2):
            half = slice(hh * HALF_W, (hh + 1) * HALF_W)
            rows = slice(hh * rows_half, (hh + 1) * rows_half)
            vw = vwin[pl.ds(r0, KEY_WIN), half]
            o_all = jnp.dot(pb[rows], vw, preferred_element_type=F32) / l[rows] * hm
            o = o_all[0:cq]
            for h in range(1, heads_per_half):
                o = o + o_all[h * cq:(h + 1) * cq]
            o_halves.append(o)
        o_ref[pl.ds(r0, cq), :] = jnp.concatenate(o_halves, axis=1).astype(o_ref.dtype)
        return carry

    lax.fori_loop(0, nq, chunk, 0)


def _attention(q, k_prev_src, k_cur_src, v_prev_src, v_cur_src, bias, hmask, *, batch, tiles, cq, nq,
               prev_index, mask_first):
    tc = cq * nq
    cur = pl.BlockSpec((tc, GROUP_W), lambda b, i: (b * tiles + i, 0))
    prev = pl.BlockSpec((PAST_BAND, GROUP_W), prev_index)
    win_rows = (nq - 1) * cq + KEY_WIN
    return pl.pallas_call(
        functools.partial(_attn_body, cq=cq, nq=nq, mask_first=mask_first),
        grid=(batch, tiles),
        in_specs=[cur, prev, cur, prev, cur, _full((N_HEADS, KEY_WIN)), _full((N_HEADS // 2 * cq, HALF_W))],
        out_specs=cur,
        out_shape=jax.ShapeDtypeStruct(q.shape, BF16),
        scratch_shapes=[pltpu.VMEM((win_rows, GROUP_W), BF16), pltpu.VMEM((win_rows, GROUP_W), BF16),
                        pltpu.VMEM((N_HEADS * cq, KEY_WIN), F32)],
        compiler_params=_params("arbitrary", "arbitrary"),
        name="band_attention",
    )(q, k_prev_src, k_cur_src, v_prev_src, v_cur_src, bias, hmask)


def _rel_base(rel_bias_l, cq):
    nk = PAST_BAND + cq
    dist = np.concatenate([PAST_BAND - np.arange(nk), np.zeros(KEY_WIN - nk - (cq - 1), np.int64),
                           PAST_BAND + np.arange(cq - 1, 0, -1)])
    return rel_bias_l[:, np.clip(dist, -REL_CLIP, REL_CLIP) + REL_CLIP].astype(F32)


def _head_row_mask(cq):
    h_row = np.repeat(np.arange(N_HEADS // 2), cq)
    h_col = np.arange(HALF_W) // HEAD_DIM
    return jnp.asarray((h_row[:, None] == h_col[None, :]).astype(np.float32))


def _log_sigmoid(x):
    return jnp.minimum(x, 0.0) - jnp.log(1.0 + jnp.exp(-jnp.abs(x)))


def _mlstm_body(q_ref, k_ref, v_ref, o_ref, g_ref, c0_ref, n0_ref, m0_ref,
                expand_ref, gbias_ref, bd_ref, bdb_ref, gmat_ref, ltri_ref, eye_ref, causal_ref, gml_ref,
                h_ref, ct_ref, n_ref, m_ref, ct_s, n_s, m_s, *, valid, nb):
    c = pl.program_id(1)
    last = c == pl.num_programs(1) - 1
    bd = bd_ref[...]
    bdb = bdb_ref[...]
    gmat = gmat_ref[...]
    eye = eye_ref[...] > 0.5
    causal = causal_ref[...] > 0.5
    gate_lane = lax.broadcasted_iota(I32, (nb * CHUNK, LANES), 1)
    heads_per_half = N_HEADS // 2

    def block_diag(a_half, mask):
        return jnp.concatenate([a_half] * heads_per_half, axis=0) * mask

    @pl.when(c == 0)
    def _():
        for b in range(nb):
            for hh, c0_half in enumerate(_halves(c0_ref[b])):
                ct_s[b, hh] = block_diag(c0_half, bd)
            n_s[b] = n0_ref[b]
            m_s[b] = m0_ref[b]

    rows_of = lambda a, b: a[b * CHUNK:(b + 1) * CHUNK]

    gates = jnp.concatenate([g_ref[b] for b in range(nb)], axis=0) + gbias_ref[...]
    gates = jnp.where(gate_lane < N_HEADS, gates, _log_sigmoid(gates))
    gp_all = jnp.dot(jnp.concatenate(_split3(gates), axis=1), expand_ref[...], preferred_element_type=F32)

    nums, den_parts, m_ts = [], [], []
    for b in range(nb):
        gp = rows_of(gp_all, b)
        log_i = gp[:, :GROUP_W]
        log_f = gp[:, GROUP_W:]
        if valid < CHUNK:
            live = lax.broadcasted_iota(I32, (CHUNK, GROUP_W), 0) < valid
            log_i = jnp.where(live, log_i, -jnp.inf)
            log_f = jnp.where(live, log_f, 0.0)
        cum_f = jnp.dot(ltri_ref[...], jnp.concatenate(_split3(log_f), axis=0), preferred_element_type=F32)

        b_row = jnp.sum(jnp.where(eye, log_i - cum_f, 0.0), axis=0, keepdims=True)
        m_prev = m_s[b]
        log_inter = cum_f + m_prev
        log_d = jnp.where(causal, cum_f + b_row, -jnp.inf)
        max_d = jnp.concatenate(
            [jnp.broadcast_to(jnp.max(log_d[:, h * HEAD_DIM:(h + 1) * HEAD_DIM], axis=-1, keepdims=True),
                              (CHUNK, HEAD_DIM)) for h in range(N_HEADS)], axis=1)
        m_t = jnp.maximum(log_inter, max_d)
        w_intra = jnp.exp(log_d - m_t)
        w_inter = jnp.exp(log_inter - m_t)

        q = q_ref[b] * (HEAD_DIM ** -0.5)
        k = k_ref[b]
        qb = q.astype(BF16)
        kb = k.astype(BF16)
        vb = v_ref[b].astype(BF16)
        n_prev = n_s[b]
        m_new = m_t[CHUNK - 1:CHUNK, :]
        cum_last = cum_f[CHUNK - 1:CHUNK, :]
        w_state = jnp.exp(cum_last - cum_f + log_i - m_new)
        decay = jnp.exp(cum_last + m_prev - m_new)
        kw = k * w_state
        kwb = kw.astype(BF16)

        s_halves, num_halves = [], []
        for hh in range(2):
            half = slice(hh * HALF_W, (hh + 1) * HALF_W)
            kbd = block_diag(kb[:, half], bdb)
            vbd = block_diag(vb[:, half], bdb)
            s_h = lax.dot_general(qb[:, half], kbd, (((1,), (1,)), ((), ())),
                                  preferred_element_type=F32) * w_intra[:, half]
            ct = ct_s[b, hh]
            num_halves.append(
                jnp.dot(s_h.astype(BF16), vbd, preferred_element_type=F32)
                + w_inter[:, half] * lax.dot_general(qb[:, half], ct.astype(BF16), (((1,), (1,)), ((), ())),
                                                     preferred_element_type=F32))
            s_halves.append(s_h)
            upd = lax.dot_general(vb[:, half], kwb[:, half], (((0,), (0,)), ((), ())),
                                  preferred_element_type=F32)
            ct_s[b, hh] = decay[:, half] * ct + upd * bd
        s = jnp.concatenate(s_halves, axis=1)
        nums.append(jnp.concatenate(num_halves, axis=1))
        den_parts.append((s + w_inter * q * n_prev) * float(HEAD_DIM))
        m_ts.append(m_t)
        n_s[b] = decay * n_prev + jnp.sum(kw, axis=0, keepdims=True)
        m_s[b] = m_new

    den_terms = jnp.concatenate(den_parts, axis=0)
    den_hi = den_terms.astype(BF16)
    den_lo = den_terms - den_hi.astype(F32)
    den = _group_mean(den_hi.astype(F32), gmat) + _group_mean(den_lo, gmat)
    hb = jnp.concatenate(nums, axis=0) / jnp.maximum(jnp.abs(den), jnp.exp(-jnp.concatenate(m_ts, axis=0)))
    hn = hb * lax.rsqrt(_group_mean(hb * hb, gmat) + RMS_EPS) * gml_ref[...]
    for b in range(nb):
        h_ref[b] = (jax.nn.sigmoid(o_ref[b]) * rows_of(hn, b)).astype(h_ref.dtype)

    @pl.when(last)
    def _():
        for b in range(nb):
            out_halves = []
            for hh in range(2):
                ct_half = ct_s[b, hh]
                acc = ct_half[0:HEAD_DIM]
                for h in range(1, heads_per_half):
                    acc = acc + ct_half[h * HEAD_DIM:(h + 1) * HEAD_DIM]
                out_halves.append(acc)
            ct_ref[b] = jnp.concatenate(out_halves, axis=1)
            n_ref[b] = n_s[b]
            m_ref[b] = m_s[b]


def _mlstm(mq, mk, mv, mo, gates, c0t, n0, m0, consts, *, valid):
    batch, t, _ = mq.shape
    chunks = t // CHUNK
    nb = MLSTM_STREAMS
    row = lambda w: pl.BlockSpec((nb, CHUNK, w), lambda g, c: (g, c, 0))
    per_b = lambda r: pl.BlockSpec((nb, r, GROUP_W), lambda g, c: (g, 0, 0))
    expand, gbias, bd, bdb, gmat, ltri, eye, causal, gml = consts
    return pl.pallas_call(
        functools.partial(_mlstm_body, valid=valid, nb=nb),
        grid=(batch // nb, chunks),
        in_specs=[row(GROUP_W)] * 4 + [row(LANES), per_b(HEAD_DIM), per_b(1), per_b(1),
                  _full((3 * LANES, 2 * GROUP_W)), _full((1, LANES)), _full((HALF_W, HALF_W)),
                  _full((HALF_W, HALF_W)), _full((HALF_W, HALF_W)), _full((CHUNK, 3 * CHUNK)),
                  _full((CHUNK, GROUP_W)), _full((CHUNK, GROUP_W)), _full((1, GROUP_W))],
        out_specs=[row(GROUP_W), per_b(HEAD_DIM), per_b(1), per_b(1)],
        out_shape=[jax.ShapeDtypeStruct(mq.shape, BF16),
                   jax.ShapeDtypeStruct((batch, HEAD_DIM, GROUP_W), F32),
                   jax.ShapeDtypeStruct((batch, 1, GROUP_W), F32),
                   jax.ShapeDtypeStruct((batch, 1, GROUP_W), F32)],
        scratch_shapes=[pltpu.VMEM((nb, 2, HALF_W, HALF_W), F32), pltpu.VMEM((nb, 1, GROUP_W), F32),
                        pltpu.VMEM((nb, 1, GROUP_W), F32)],
        compiler_params=_params("arbitrary", "arbitrary"),
        name="mlstm",
    )(mq, mk, mv, mo, gates, c0t, n0, m0, expand, gbias, bd, bdb, gmat, ltri, eye, causal, gml)


def _mlstm_consts(b_igate_l, b_fgate_l, g_mlstm_l):
    expand = np.zeros((LANES, 2 * GROUP_W), np.float32)
    for h in range(N_HEADS):
        expand[h, h * HEAD_DIM:(h + 1) * HEAD_DIM] = 1.0
        expand[N_HEADS + h, GROUP_W + h * HEAD_DIM:GROUP_W + (h + 1) * HEAD_DIM] = 1.0
    gbias = jnp.concatenate([b_igate_l.astype(F32), b_fgate_l.astype(F32),
                             jnp.zeros((LANES - 2 * N_HEADS,), F32)])[None, :]
    bd = _head_block_diag()[:HALF_W, :HALF_W]
    ltri = np.tril(np.ones((CHUNK, CHUNK), np.float32))
    s_of_lane = np.arange(GROUP_W) % HEAD_DIM
    t = np.arange(CHUNK)
    eye = (t[:, None] == s_of_lane[None, :]).astype(np.float32)
    causal = (s_of_lane[None, :] <= t[:, None]).astype(np.float32)
    return (jnp.asarray(np.concatenate([expand] * 3, axis=0), BF16), gbias, jnp.asarray(bd), jnp.asarray(bd, BF16),
            jnp.asarray(bd / HEAD_DIM, BF16), jnp.asarray(np.concatenate([ltri] * 3, axis=1), BF16), jnp.asarray(eye),
            jnp.asarray(causal), g_mlstm_l.astype(F32).reshape(1, GROUP_W))


def _out_proj_body(xp_ref, xs_ref, ap_ref, as_ref, hp_ref, hs_ref, wa_ref, wm_ref, gffn_ref, wr_ref, br_ref,
                   lstrict_ref, ustrict_ref, y_ref, xf_ref, gate_ref, pos_ref, seg_ref, *, prompt_tiles):
    is_prompt = pl.program_id(0) < prompt_tiles
    x = jnp.where(is_prompt, xp_ref[...], xs_ref[...])
    att = jnp.where(is_prompt, ap_ref[...], as_ref[...])
    hm = jnp.where(is_prompt, hp_ref[...], hs_ref[...])
    y = (x + jnp.dot(att.astype(BF16), wa_ref[...], preferred_element_type=F32)
         + jnp.dot(hm.astype(BF16), wm_ref[...], preferred_element_type=F32))
    y_ref[...] = y
    xf = y * lax.rsqrt(jnp.mean(y * y, axis=-1, keepdims=True) + RMS_EPS) * gffn_ref[...]
    xf_ref[...] = xf
    _route_tile(_dot_f32ish(xf, wr_ref[...]) + br_ref[...], lstrict_ref, ustrict_ref, gate_ref, pos_ref, seg_ref)


def _out_proj(xp, xs, att_p, att_s, hm_p, hm_s, wa, wm, g_ffn, w_router, b_router):
    tm = ROW_TILE
    pt, st = xp.shape[0] // tm, xs.shape[0] // tm
    n = xp.shape[0] + xs.shape[0]
    p_row = lambda w: pl.BlockSpec((tm, w), lambda i: (jnp.minimum(i, pt - 1), 0))
    s_row = lambda w: pl.BlockSpec((tm, w), lambda i: (jnp.maximum(i - pt, 0), 0))
    row = lambda w: pl.BlockSpec((tm, w), lambda i: (i, 0))
    lstrict = jnp.asarray(np.tril(np.ones((tm, tm), np.float32), -1), BF16)
    ustrict = jnp.asarray(np.triu(np.ones((LANES, LANES), np.float32), 1), BF16)
    return pl.pallas_call(
        functools.partial(_out_proj_body, prompt_tiles=pt),
        grid=(pt + st,),
        in_specs=[p_row(D_MODEL), s_row(D_MODEL), p_row(GROUP_W), s_row(GROUP_W), p_row(GROUP_W), s_row(GROUP_W),
                  _full((GROUP_W, D_MODEL)), _full((GROUP_W, D_MODEL)),
                  _full((1, D_MODEL)), _full((3 * D_MODEL, LANES)), _full((1, LANES)),
                  _full((tm, tm)), _full((LANES, LANES))],
        out_specs=[row(D_MODEL), row(D_MODEL), row(LANES), row(LANES),
                   pl.BlockSpec((SUBLANES, LANES), lambda i: (i, 0))],
        out_shape=[jax.ShapeDtypeStruct((n, D_MODEL), F32), jax.ShapeDtypeStruct((n, D_MODEL), F32),
                   jax.ShapeDtypeStruct((n, LANES), F32), jax.ShapeDtypeStruct((n, LANES), F32),
                   jax.ShapeDtypeStruct((n // tm * SUBLANES, LANES), I32)],
        compiler_params=_params("arbitrary"),
        name="out_proj_router",
    )(xp, xs, att_p, att_s, hm_p, hm_s, wa, wm, g_ffn, w_router, b_router, lstrict, ustrict)


def _route_tile(logits, lstrict_ref, ustrict_ref, gate_ref, pos_ref, seg_ref):
    tt = logits.shape[0]
    lane = lax.broadcasted_iota(I32, (tt, LANES), 1)
    work = jnp.where(lane < N_EXPERTS, logits, -jnp.inf)
    vals, idxs = [], []
    for _ in range(TOP_K):
        m = jnp.max(work, axis=-1, keepdims=True)
        idx = jnp.min(jnp.where(work == m, lane, LANES), axis=-1, keepdims=True)
        vals.append(m)
        idxs.append(idx)
        work = jnp.where(lane == idx, -jnp.inf, work)
    exps = [jnp.exp(v - vals[0]) for v in vals]
    total = exps[0] + exps[1] + exps[2] + exps[3]

    chosen = jnp.zeros((tt, LANES), F32)
    for idx in idxs:
        chosen = chosen + (lane == idx).astype(F32)
    before = jnp.dot(lstrict_ref[...], chosen.astype(BF16), preferred_element_type=F32)
    count = jnp.sum(chosen, axis=0, keepdims=True)
    groups = jnp.floor((count + (SUBLANES - 1)) * (1.0 / SUBLANES))
    groups8 = jnp.broadcast_to(groups, (SUBLANES, LANES)).astype(BF16)
    start = jnp.dot(groups8, ustrict_ref[...], preferred_element_type=F32) * float(SUBLANES)
    local = before + start[0:1, :]

    gate_out = jnp.zeros((tt, LANES), F32)
    pos_out = jnp.zeros((tt, LANES), F32)
    for k in range(TOP_K):
        pos = jnp.sum(jnp.where(lane == idxs[k], local, 0.0), axis=-1, keepdims=True)
        gate_out = jnp.where(lane == k, exps[k] / total, gate_out)
        pos_out = jnp.where(lane == k, pos, pos_out)
    gate_ref[...] = gate_out
    pos_ref[...] = pos_out
    row = lax.broadcasted_iota(I32, (SUBLANES, LANES), 0)
    seg = jnp.where(row == 0, groups * float(SUBLANES), jnp.where(row == 1, start, 0.0))
    seg_ref[...] = seg.astype(I32)


def _segment_copies(i, len_ref, lst_ref, off_ref, make_copy, act):
    for e in range(N_EXPERTS):
        seg = i * N_EXPERTS + e
        length = len_ref[seg]
        local = lst_ref[seg]
        glob = off_ref[seg]
        for size in SEG_SIZES:
            take = length & size

            @pl.when(take != 0)
            def _(local=local, glob=glob, size=size):
                act(make_copy(pl.multiple_of(local, SUBLANES), pl.multiple_of(glob, SUBLANES), size))

            local = local + take
            glob = glob + take


def _local_onehot(pos_rows, base, rows):
    r = (lax.broadcasted_iota(I32, (rows, pos_rows[0].shape[1]), 0) + base).astype(F32)
    out = jnp.zeros(r.shape, F32)
    for k in range(TOP_K):
        out = jnp.where(r == pos_rows[k], 1.0, out)
    return out


def _dispatch_body(len_ref, lst_ref, off_ref, fill_ref, nused_ref, xf_ref, pos_ref, xs_hbm,
                   xloc, zbuf, sem, zsem, *, tm, n_tiles):
    i = pl.program_id(0)
    tt = xf_ref.shape[0]
    fill_rows = zbuf.shape[0]

    @pl.when(i == 0)
    def _():
        zbuf[...] = jnp.zeros_like(zbuf)

        def fill(e):
            start = pl.multiple_of(fill_ref[e], SUBLANES)
            return pltpu.make_async_copy(zbuf, xs_hbm.at[pl.ds(start, fill_rows)], zsem)

        for e in range(N_EXPERTS):
            fill(e).start()
            fill(e).wait()

        def tail(j, carry):
            cp = pltpu.make_async_copy(zbuf.at[pl.ds(0, tm)], xs_hbm.at[pl.ds(pl.multiple_of(j * tm, tm), tm)], zsem)
            cp.start()
            cp.wait()
            return carry

        lax.fori_loop(nused_ref[0], n_tiles, tail, 0)

    pos_t = jnp.transpose(pos_ref[...])
    pos_rows = [pos_t[k:k + 1, :] for k in range(TOP_K)]
    xb = xf_ref[...].astype(BF16)

    slot = lax.rem(i, 2)

    def sort_rows(c, carry):
        r0 = pl.multiple_of(c * SORT_CHUNK, SORT_CHUNK)
        sel = _local_onehot(pos_rows, r0, SORT_CHUNK).astype(BF16)
        xloc[slot, pl.ds(r0, SORT_CHUNK), :] = jnp.dot(sel, xb, preferred_element_type=F32)
        return carry

    lax.fori_loop(0, LOCAL_ROWS // SORT_CHUNK, sort_rows, 0)

    def copies(step, which):
        def make_copy(local, glob, size):
            return pltpu.make_async_copy(xloc.at[which, pl.ds(local, size)], xs_hbm.at[pl.ds(glob, size)],
                                         sem.at[which])
        return functools.partial(_segment_copies, step, len_ref, lst_ref, off_ref, make_copy)

    copies(i, slot)(lambda cp: cp.start())

    @pl.when(i > 0)
    def _():
        copies(i - 1, 1 - slot)(lambda cp: cp.wait())

    @pl.when(i == pl.num_programs(0) - 1)
    def _():
        copies(i, slot)(lambda cp: cp.wait())


def _dispatch(seg_len, seg_local, seg_off, fill_start, n_used, xf, pos, n_tiles):
    n = xf.shape[0]
    tt, tm = ROW_TILE, EXPERT_TILE
    fill_rows = tm + SUBLANES
    grid_spec = pltpu.PrefetchScalarGridSpec(
        num_scalar_prefetch=5,
        grid=(n // tt,),
        in_specs=[pl.BlockSpec((tt, D_MODEL), lambda i, *_: (i, 0)),
                  pl.BlockSpec((tt, LANES), lambda i, *_: (i, 0))],
        out_specs=pl.BlockSpec(memory_space=pl.ANY),
        scratch_shapes=[pltpu.VMEM((2, LOCAL_ROWS, D_MODEL), F32), pltpu.VMEM((fill_rows, D_MODEL), F32),
                        pltpu.SemaphoreType.DMA((2,)), pltpu.SemaphoreType.DMA(())],
    )
    return pl.pallas_call(
        functools.partial(_dispatch_body, tm=tm, n_tiles=n_tiles + 2),
        grid_spec=grid_spec,
        out_shape=jax.ShapeDtypeStruct(((n_tiles + 2) * tm, D_MODEL), F32),
        compiler_params=_params("arbitrary"),
        name="expert_dispatch",
    )(seg_len, seg_local, seg_off, fill_start, n_used, xf, pos)


def _expert_body(te_ref, nused_ref, x_ref, wup_ref, wdn_ref, perm_ref, bg_ref, bl_ref, bd_ref,
                 y_ref, wg_s, wl_s, wd_s):
    i = pl.program_id(0)
    n_used = nused_ref[0]

    @pl.when(i >= n_used)
    def _():
        y_ref[...] = jnp.zeros_like(y_ref)

    @pl.when((i == 0) | (te_ref[i] != te_ref[jnp.maximum(i - 1, 0)]))
    def _():
        perm = perm_ref[...]
        for c in range(2 * D_FF // 256):
            blk = wup_ref[0, :, c * 256:(c + 1) * 256].astype(BF16)
            sep = jnp.dot(blk, perm, preferred_element_type=F32).astype(BF16)
            wg_s[:, c * 128:(c + 1) * 128] = sep[:, :128]
            wl_s[:, c * 128:(c + 1) * 128] = sep[:, 128:]
        wd_s[...] = wdn_ref[0].astype(BF16)

    @pl.when(i < n_used)
    def _():
        x = x_ref[...].astype(BF16)
        glu = jnp.minimum(jnp.dot(x, wg_s[...], preferred_element_type=F32) + bg_ref[0], SWIGLU_LIMIT)
        lin = jnp.clip(jnp.dot(x, wl_s[...], preferred_element_type=F32) + bl_ref[0], -SWIGLU_LIMIT, SWIGLU_LIMIT)
        act = glu * jax.nn.sigmoid(SWIGLU_ALPHA * glu) * (lin + 1.0)
        y_ref[...] = jnp.dot(act.astype(BF16), wd_s[...], preferred_element_type=F32) + bd_ref[0]


def _deinterleave_perm():
    p = np.zeros((256, 256), np.float32)
    j = np.arange(128)
    p[2 * j, j] = 1.0
    p[2 * j + 1, 128 + j] = 1.0
    return jnp.asarray(p, BF16)


def _experts(tile_expert, n_used, x_sorted, w_up, w_down, b_glu, b_lin, b_down):
    tm = EXPERT_TILE
    n_tiles = tile_expert.shape[0]
    wspec = lambda k, n: pl.BlockSpec((1, k, n), lambda i, te, nu: (te[i], 0, 0))
    grid_spec = pltpu.PrefetchScalarGridSpec(
        num_scalar_prefetch=2,
        grid=(n_tiles,),
        in_specs=[pl.BlockSpec((tm, D_MODEL), lambda i, te, nu: (jnp.minimum(i, nu[0] - 1), 0)),
                  wspec(D_MODEL, 2 * D_FF), wspec(D_FF, D_MODEL),
                  pl.BlockSpec((256, 256), lambda i, te, nu: (0, 0)),
                  wspec(1, D_FF), wspec(1, D_FF), wspec(1, D_MODEL)],
        out_specs=pl.BlockSpec((tm, D_MODEL), lambda i, te, nu: (i, 0)),
        scratch_shapes=[pltpu.VMEM((D_MODEL, D_FF), BF16), pltpu.VMEM((D_MODEL, D_FF), BF16),
                        pltpu.VMEM((D_FF, D_MODEL), BF16)],
    )
    return pl.pallas_call(
        _expert_body,
        grid_spec=grid_spec,
        out_shape=jax.ShapeDtypeStruct((n_tiles * tm, D_MODEL), F32),
        compiler_params=_params("arbitrary"),
        name="expert_ffn",
    )(tile_expert, n_used, x_sorted, w_up, w_down, _deinterleave_perm(), b_glu, b_lin, b_down)


def _combine_body(len_ref, lst_ref, off_ref, y_ref, gate_ref, pos_ref, rows_hbm, outp_ref, outs_ref,
                  yloc, acc_s, wide_s, sem, *, prompt_tiles):
    i = pl.program_id(0)
    slot = lax.rem(i, 2)
    tt = y_ref.shape[0]

    def copies(step, which):
        def make_copy(local, glob, size):
            return pltpu.make_async_copy(rows_hbm.at[pl.ds(glob, size)], yloc.at[which, pl.ds(local, size)],
                                         sem.at[which])
        return functools.partial(_segment_copies, step, len_ref, lst_ref, off_ref, make_copy)

    @pl.when(i == 0)
    def _():
        yloc[...] = jnp.zeros_like(yloc)
        copies(i, slot)(lambda cp: cp.start())

    @pl.when(i + 1 < pl.num_programs(0))
    def _():
        copies(i + 1, 1 - slot)(lambda cp: cp.start())

    pos = pos_ref[...]
    gate = gate_ref[...]
    for k in range(TOP_K):
        wide_s[k] = jnp.broadcast_to(pos[:, k:k + 1], (tt, LANES))
        wide_s[TOP_K + k] = jnp.broadcast_to(gate[:, k:k + 1], (tt, LANES))
    acc_s[...] = y_ref[...]
    copies(i, slot)(lambda cp: cp.wait())

    lane = lax.broadcasted_iota(I32, (tt, LANES), 1).astype(F32)

    def weigh(c, carry):
        r0 = pl.multiple_of(c * SORT_CHUNK, SORT_CHUNK)
        halves = []
        for half in range(SORT_CHUNK // LANES):
            r = lane + (r0 + half * LANES).astype(F32)
            w = jnp.zeros((tt, LANES), F32)
            for k in range(TOP_K):
                w = jnp.where(r == wide_s[k], wide_s[TOP_K + k], w)
            halves.append(w.astype(BF16))
        w = jnp.concatenate(halves, axis=1)
        acc_s[...] += jnp.dot(w, yloc[slot, pl.ds(r0, SORT_CHUNK), :].astype(BF16), preferred_element_type=F32)
        return carry

    lax.fori_loop(0, LOCAL_ROWS // SORT_CHUNK, weigh, 0)

    @pl.when(i < prompt_tiles)
    def _():
        outp_ref[...] = acc_s[...]

    @pl.when(i >= prompt_tiles)
    def _():
        outs_ref[...] = acc_s[...]


def _combine(seg_len, seg_local, seg_off, y, gates, pos, y_rows, n_prompt):
    n = y.shape[0]
    tt = ROW_TILE
    nt, pt = n // tt, n_prompt // tt
    grid_spec = pltpu.PrefetchScalarGridSpec(
        num_scalar_prefetch=3,
        grid=(nt,),
        in_specs=[pl.BlockSpec((tt, D_MODEL), lambda i, *_: (i, 0)),
                  pl.BlockSpec((tt, LANES), lambda i, *_: (i, 0)),
                  pl.BlockSpec((tt, LANES), lambda i, *_: (i, 0)),
                  pl.BlockSpec(memory_space=pl.ANY)],
        out_specs=[pl.BlockSpec((tt, D_MODEL), lambda i, *_: (jnp.minimum(i, pt - 1), 0)),
                   pl.BlockSpec((tt, D_MODEL), lambda i, *_: (jnp.maximum(i - pt, 0), 0))],
        scratch_shapes=[pltpu.VMEM((2, LOCAL_ROWS, D_MODEL), F32), pltpu.VMEM((tt, D_MODEL), F32),
                        pltpu.VMEM((2 * TOP_K, tt, LANES), F32), pltpu.SemaphoreType.DMA((2,))],
    )
    return pl.pallas_call(
        functools.partial(_combine_body, prompt_tiles=pt),
        grid_spec=grid_spec,
        out_shape=[jax.ShapeDtypeStruct((n_prompt, D_MODEL), F32),
                   jax.ShapeDtypeStruct((n - n_prompt, D_MODEL), F32)],
        compiler_params=_params("arbitrary"),
        name="expert_combine",
    )(seg_len, seg_local, seg_off, y, gates, pos, y_rows)


def _moe(y, xf, gates, pos, seg, ffn_w, n_prompt):
    n = y.shape[0]
    tm, tt = EXPERT_TILE, ROW_TILE
    n_tt = n // tt
    seg = seg.reshape(n_tt, SUBLANES, LANES)
    seg_len, seg_local = seg[:, 0, :N_EXPERTS], seg[:, 1, :N_EXPERTS]
    rows = jnp.sum(seg_len, axis=0)
    padded = (rows + tm - 1) // tm * tm
    pad_end = jnp.cumsum(padded)
    pad_start = pad_end - padded
    seg_off = pad_start[None, :] + jnp.cumsum(seg_len, axis=0) - seg_len
    n_tiles = -(-(n * TOP_K + n_tt * N_EXPERTS * (SUBLANES - 1) + N_EXPERTS * (tm - 1)) // tm)
    tile_expert = jnp.minimum(jnp.sum(pad_end[None, :] <= (jnp.arange(n_tiles) * tm)[:, None], axis=1),
                              N_EXPERTS - 1).astype(I32)
    n_used = (pad_end[-1:] // tm).astype(I32)
    fill_start = (pad_start + rows).astype(I32)
    flat = lambda a: a.astype(I32).reshape(n_tt * N_EXPERTS)

    x_sorted = _dispatch(flat(seg_len), flat(seg_local), flat(seg_off), fill_start, n_used, xf, pos, n_tiles)
    y_rows = _experts(tile_expert, n_used, x_sorted, *ffn_w)
    return _combine(flat(seg_len), flat(seg_local), flat(seg_off), y, gates, pos, y_rows, n_prompt)


def _mixer(x, lw, cache, state):
    b, t, _ = x.shape
    n = b * t
    q, k, v, mq, mk, mv, mo, gates = _in_proj(x.reshape(n, D_MODEL), lw["g_mix"], lw["w_main"], lw["w_gate"],
                                              lw["gq"], lw["gk"], lw["gmat"])
    heads = lambda a, rows: a.reshape(b, rows, N_HEADS, HEAD_DIM)
    if cache is None:
        tiles = t // PAST_BAND
        att = _attention(q, k, k, v, v, lw["bias_prompt"], lw["hmask_prompt"], batch=b, tiles=tiles, cq=CHUNK,
                         nq=PAST_BAND // CHUNK,
                         prev_index=lambda bi, i: (bi * tiles + jnp.maximum(i - 1, 0), 0), mask_first=True)
        keep = min(PAST_BAND, t)
        k_new = heads(k.reshape(b, t, GROUP_W)[:, t - keep:], keep)
        v_new = heads(v.reshape(b, t, GROUP_W)[:, t - keep:], keep)
    else:
        ck, cv = cache
        att = _attention(q, ck.reshape(b * PAST_BAND, GROUP_W), k, cv.reshape(b * PAST_BAND, GROUP_W), v,
                         lw["bias_sample"], lw["hmask_sample"], batch=b, tiles=1, cq=t, nq=1,
                         prev_index=lambda bi, i: (bi, 0), mask_first=False)
        k_new, v_new = heads(k, t), heads(v, t)

    tp = -(-t // CHUNK) * CHUNK
    valid = t if t < CHUNK else CHUNK

    def streams(a):
        a = a.reshape(b, t, -1)
        return a if tp == t else jnp.pad(a, ((0, 0), (0, tp - t), (0, 0)))

    if state is None:
        c0t = jnp.zeros((b, HEAD_DIM, GROUP_W), F32)
        n0 = jnp.zeros((b, 1, GROUP_W), F32)
        m0 = jnp.zeros((b, 1, GROUP_W), F32)
    else:
        c_in, n_in, m_in = state
        c0t = c_in.astype(F32).transpose(0, 3, 1, 2).reshape(b, HEAD_DIM, GROUP_W)
        n0 = n_in.astype(F32).reshape(b, 1, GROUP_W)
        m0 = jnp.repeat(m_in.astype(F32), HEAD_DIM, axis=-1).reshape(b, 1, GROUP_W)
    hm, ct, n_out, m_out = _mlstm(streams(mq), streams(mk), streams(mv), streams(mo), streams(gates),
                                  c0t, n0, m0, lw["mlstm_consts"], valid=valid)
    hm = hm[:, :t].reshape(n, GROUP_W)
    c_new = ct.reshape(b, HEAD_DIM, N_HEADS, HEAD_DIM).transpose(0, 2, 3, 1)
    n_new = n_out.reshape(b, N_HEADS, HEAD_DIM)
    m_new = m_out.reshape(b, N_HEADS, HEAD_DIM)[:, :, 0]
    return att, hm, (k_new, v_new, c_new, n_new, m_new)


def kernel(x_prompt, x_sample, cache_k, cache_v, state_C, state_n, state_m, g_mix, w_in, g_q, g_k, rel_bias,
           b_igate, b_fgate, g_mlstm, w_out, g_ffn, w_router, b_router, w_up, b_up, w_down, b_down):
    depth = w_in.shape[0]
    yp, ys = x_prompt, x_sample
    bs, ts = x_sample.shape[0], x_sample.shape[1]
    n_prompt = x_prompt.shape[0] * x_prompt.shape[1]
    st_prompt, st_sample = [], []
    n_main = N_PROJ * GROUP_W
    gmat = jnp.asarray(_head_block_diag()[:HALF_W, :HALF_W] / HEAD_DIM, BF16)
    for l in range(depth):
        lw = dict(
            g_mix=g_mix[l].astype(F32)[None, :],
            w_main=w_in[l][:, :n_main].astype(BF16),
            w_gate=jnp.pad(w_in[l][:, n_main:], ((0, 0), (0, LANES - 2 * N_HEADS))).astype(BF16),
            gq=jnp.tile(g_q[l].astype(F32), N_HEADS)[None, :],
            gk=jnp.tile(g_k[l].astype(F32), N_HEADS)[None, :],
            gmat=gmat,
            bias_prompt=_rel_base(rel_bias[l], CHUNK),
            hmask_prompt=_head_row_mask(CHUNK),
            bias_sample=_rel_base(rel_bias[l], ts),
            hmask_sample=_head_row_mask(ts),
            mlstm_consts=_mlstm_consts(b_igate[l], b_fgate[l], g_mlstm[l]),
        )
        ffn_w = (w_up[l].astype(F32), w_down[l].astype(F32),
                 b_up[l][:, None, 0::2].astype(F32), b_up[l][:, None, 1::2].astype(F32),
                 b_down[l][:, None, :].astype(F32))
        att_p, hm_p, sp = _mixer(yp, lw, None, None)
        cache = (cache_k[l].reshape(bs, PAST_BAND, GROUP_W), cache_v[l].reshape(bs, PAST_BAND, GROUP_W))
        att_s, hm_s, ss = _mixer(ys, lw, cache, (state_C[l], state_n[l], state_m[l]))
        y, xf, gates, pos, seg = _out_proj(
            yp.reshape(-1, D_MODEL), ys.reshape(-1, D_MODEL), att_p, att_s, hm_p, hm_s,
            w_out[l][:GROUP_W].astype(BF16), w_out[l][GROUP_W:].astype(BF16), g_ffn[l].astype(F32)[None, :],
            _stack_hi_lo(jnp.pad(w_router[l].astype(F32), ((0, 0), (0, LANES - N_EXPERTS)))),
            jnp.pad(b_router[l].astype(F32), (0, LANES - N_EXPERTS))[None, :])
        out_p, out_s = _moe(y, xf, gates, pos, seg, ffn_w, n_prompt)
        yp, ys = out_p.reshape(x_prompt.shape), out_s.reshape(x_sample.shape)
        st_prompt.append(sp)
        st_sample.append(ss)
    k_p, v_p, c_p, n_p, m_p = [jnp.stack(a) for a in zip(*st_prompt)]
    k_s, v_s, c_s, n_s, m_s = [jnp.stack(a) for a in zip(*st_sample)]
    return (yp, ys, k_p, v_p, c_p, n_p, m_p, k_s, v_s, c_s, n_s, m_s)
```

```python
import functools

import numpy as np
import jax
import jax.numpy as jnp
from jax import lax
from jax.experimental import pallas as pl
from jax.experimental.pallas import tpu as pltpu

F32 = jnp.float32
BF16 = jnp.bfloat16
I32 = jnp.int32

D_MODEL = 1024
N_HEADS = 8
HEAD_DIM = 64
GROUP_W = N_HEADS * HEAD_DIM
HALF_W = GROUP_W // 2
N_PROJ = 7
LANES = 128
CHUNK = 64
PAST_BAND = 512
KEY_WIN = 640
REL_CLIP = 256
N_EXPERTS = 32
TOP_K = 4
D_FF = 1024
SWIGLU_ALPHA = 1.702
SWIGLU_LIMIT = 7.0
RMS_EPS = 1e-6
NEG_BIG = -1e30
ROW_TILE = 512
EXPERT_TILE = 512
SUBLANES = 8
SORT_CHUNK = 256
LOCAL_ROWS = -(-(ROW_TILE * TOP_K + N_EXPERTS * (SUBLANES - 1)) // SORT_CHUNK) * SORT_CHUNK
SEG_SIZES = (512, 256, 128, 64, 32, 16, 8)
SEG_RARE = 128
MLSTM_STREAMS = 4
VMEM_LIMIT_BYTES = 56 * 1024 * 1024


def _params(*sem):
    return pltpu.CompilerParams(dimension_semantics=sem, vmem_limit_bytes=VMEM_LIMIT_BYTES)


def _head_block_diag():
    h = np.arange(GROUP_W) // HEAD_DIM
    return (h[:, None] == h[None, :]).astype(np.float32)


def _full(shape):
    return pl.BlockSpec(shape, lambda *_: (0,) * len(shape))


def _halves(a):
    return a[:, :HALF_W], a[:, HALF_W:]


def _group_mean(x, gmat_half):
    return jnp.concatenate([jnp.dot(h.astype(BF16), gmat_half, preferred_element_type=F32) for h in _halves(x)],
                           axis=1)


def _split3(x):
    hi = x.astype(BF16)
    r = x - hi.astype(F32)
    mid = r.astype(BF16)
    lo = (r - mid.astype(F32)).astype(BF16)
    return hi, mid, lo


def _dot_f32ish(x, w_stack):
    hi = x.astype(BF16)
    lo = (x - hi.astype(F32)).astype(BF16)
    return jnp.dot(jnp.concatenate([hi, lo, hi], axis=1), w_stack, preferred_element_type=F32)


def _stack_hi_lo(w):
    hi = w.astype(BF16)
    lo = (w - hi.astype(F32)).astype(BF16)
    return jnp.concatenate([hi, hi, lo], axis=0)


def _in_proj_body(x_ref, gmix_ref, w_ref, wg_ref, gq_ref, gk_ref, gmat_ref,
                  q_ref, k_ref, v_ref, mq_ref, mk_ref, mv_ref, mo_ref, gate_ref):
    x = x_ref[...]
    xn = x * lax.rsqrt(jnp.mean(x * x, axis=-1, keepdims=True) + RMS_EPS) * gmix_ref[...]
    xb = xn.astype(BF16)

    def proj(j):
        return jnp.dot(xb, w_ref[:, j * GROUP_W:(j + 1) * GROUP_W], preferred_element_type=F32)

    def head_norm(a, g_ref):
        msq = _group_mean(a * a, gmat_ref[...])
        return a * lax.rsqrt(msq + RMS_EPS) * g_ref[...]

    q_ref[...] = head_norm(proj(0), gq_ref)
    k_ref[...] = head_norm(proj(1), gk_ref)
    v_ref[...] = proj(2)
    mq_ref[...] = proj(3)
    mk_ref[...] = proj(4)
    mv_ref[...] = proj(5)
    mo_ref[...] = proj(6)
    gate_ref[...] = jnp.dot(xb, wg_ref[...], preferred_element_type=F32)


def _in_proj(x2d, g_mix, w_main, w_gate, gq_row, gk_row, gmat):
    n = x2d.shape[0]
    tm = ROW_TILE
    row = lambda w: pl.BlockSpec((tm, w), lambda i: (i, 0))
    outs = [jax.ShapeDtypeStruct((n, GROUP_W), F32)] * N_PROJ + [jax.ShapeDtypeStruct((n, LANES), F32)]
    return pl.pallas_call(
        _in_proj_body,
        grid=(n // tm,),
        in_specs=[row(D_MODEL), _full((1, D_MODEL)), _full((D_MODEL, N_PROJ * GROUP_W)),
                  _full((D_MODEL, LANES)), _full((1, GROUP_W)), _full((1, GROUP_W)),
                  _full((HALF_W, HALF_W))],
        out_specs=[row(GROUP_W)] * N_PROJ + [row(LANES)],
        out_shape=outs,
        compiler_params=_params("arbitrary"),
        name="in_proj",
    )(x2d, g_mix, w_main, w_gate, gq_row, gk_row, gmat)


def _attn_body(q_ref, kp_ref, kc_ref, vp_ref, vc_ref, base_ref, hmask_ref, o_ref, kwin, vwin, bias_s,
               *, cq, nq, mask_first):
    tc = cq * nq
    i = pl.program_id(1)
    kwin[0:PAST_BAND, :] = kp_ref[...].astype(BF16)
    kwin[PAST_BAND:PAST_BAND + tc, :] = kc_ref[...].astype(BF16)
    vwin[0:PAST_BAND, :] = vp_ref[...].astype(BF16)
    vwin[PAST_BAND:PAST_BAND + tc, :] = vc_ref[...].astype(BF16)
    pad_rows = kwin.shape[0] - PAST_BAND - tc
    kwin[PAST_BAND + tc:, :] = jnp.zeros((pad_rows, GROUP_W), BF16)
    vwin[PAST_BAND + tc:, :] = jnp.zeros((pad_rows, GROUP_W), BF16)

    hm = hmask_ref[...]
    kk = lax.broadcasted_iota(I32, (1, KEY_WIN), 1)

    @pl.when((pl.program_id(0) == 0) & (i == 0))
    def _():
        for h in range(N_HEADS):
            rows = jnp.broadcast_to(base_ref[h:h + 1, :], (cq, KEY_WIN))
            rows = pltpu.roll(rows, 0, 1, stride=1, stride_axis=0)
            bias_s[h * cq:(h + 1) * cq, :] = jnp.where(kk < PAST_BAND + cq, rows, NEG_BIG)

    bias = bias_s[...]

    heads_per_half = N_HEADS // 2
    rows_half = heads_per_half * cq

    def chunk(j, carry):
        r0 = pl.multiple_of(j * cq, cq)
        q = q_ref[pl.ds(r0, cq), :] * (HEAD_DIM ** -0.5)
        s_parts = []
        for hh in range(2):
            half = slice(hh * HALF_W, (hh + 1) * HALF_W)
            qm = (jnp.concatenate([q[:, half]] * heads_per_half, axis=0) * hm).astype(BF16)
            kw = kwin[pl.ds(r0, KEY_WIN), half]
            s_parts.append(lax.dot_general(qm, kw, (((1,), (1,)), ((), ())), preferred_element_type=F32))
        s = jnp.concatenate(s_parts, axis=0) + bias
        if mask_first:
            first_valid = jnp.where(i == 0, PAST_BAND - r0, 0)
            s = jnp.where(kk >= first_valid, s, NEG_BIG)
        m = jnp.max(s, axis=-1, keepdims=True)
        p = jnp.exp(s - m)
        l = jnp.sum(p, axis=-1, keepdims=True)
        pb = p.astype(BF16)
        o_halves = []
        for hh in range(2):
            half = slice(hh * HALF_W, (hh + 1) * HALF_W)
            rows = slice(hh * rows_half, (hh + 1) * rows_half)
            vw = vwin[pl.ds(r0, KEY_WIN), half]
            o_all = jnp.dot(pb[rows], vw, preferred_element_type=F32) / l[rows] * hm
            o = o_all[0:cq]
            for h in range(1, heads_per_half):
                o = o + o_all[h * cq:(h + 1) * cq]
            o_halves.append(o)
        o_ref[pl.ds(r0, cq), :] = jnp.concatenate(o_halves, axis=1)
        return carry

    lax.fori_loop(0, nq, chunk, 0)


def _attention(q, k_prev_src, k_cur_src, v_prev_src, v_cur_src, bias, hmask, *, batch, tiles, cq, nq,
               prev_index, mask_first):
    tc = cq * nq
    cur = pl.BlockSpec((tc, GROUP_W), lambda b, i: (b * tiles + i, 0))
    prev = pl.BlockSpec((PAST_BAND, GROUP_W), prev_index)
    win_rows = (nq - 1) * cq + KEY_WIN
    return pl.pallas_call(
        functools.partial(_attn_body, cq=cq, nq=nq, mask_first=mask_first),
        grid=(batch, tiles),
        in_specs=[cur, prev, cur, prev, cur, _full((N_HEADS, KEY_WIN)), _full((N_HEADS // 2 * cq, HALF_W))],
        out_specs=cur,
        out_shape=jax.ShapeDtypeStruct(q.shape, F32),
        scratch_shapes=[pltpu.VMEM((win_rows, GROUP_W), BF16), pltpu.VMEM((win_rows, GROUP_W), BF16),
                        pltpu.VMEM((N_HEADS * cq, KEY_WIN), F32)],
        compiler_params=_params("arbitrary", "arbitrary"),
        name="band_attention",
    )(q, k_prev_src, k_cur_src, v_prev_src, v_cur_src, bias, hmask)


def _rel_base(rel_bias_l, cq):
    nk = PAST_BAND + cq
    dist = np.concatenate([PAST_BAND - np.arange(nk), np.zeros(KEY_WIN - nk - (cq - 1), np.int64),
                           PAST_BAND + np.arange(cq - 1, 0, -1)])
    return rel_bias_l[:, np.clip(dist, -REL_CLIP, REL_CLIP) + REL_CLIP].astype(F32)


def _head_row_mask(cq):
    h_row = np.repeat(np.arange(N_HEADS // 2), cq)
    h_col = np.arange(HALF_W) // HEAD_DIM
    return jnp.asarray((h_row[:, None] == h_col[None, :]).astype(np.float32))


def _log_sigmoid(x):
    return jnp.minimum(x, 0.0) - jnp.log(1.0 + jnp.exp(-jnp.abs(x)))


def _mlstm_body(q_ref, k_ref, v_ref, o_ref, g_ref, c0_ref, n0_ref, m0_ref,
                expand_ref, gbias_ref, bd_ref, bdb_ref, gmat_ref, ltri_ref, eye_ref, causal_ref, gml_ref,
                h_ref, ct_ref, n_ref, m_ref, ct_s, n_s, m_s, *, valid, nb):
    c = pl.program_id(1)
    last = c == pl.num_programs(1) - 1
    bd = bd_ref[...]
    bdb = bdb_ref[...]
    gmat = gmat_ref[...]
    eye = eye_ref[...] > 0.5
    causal = causal_ref[...] > 0.5
    gate_lane = lax.broadcasted_iota(I32, (nb * CHUNK, LANES), 1)
    heads_per_half = N_HEADS // 2

    def block_diag(a_half, mask):
        return jnp.concatenate([a_half] * heads_per_half, axis=0) * mask

    @pl.when(c == 0)
    def _():
        for b in range(nb):
            for hh, c0_half in enumerate(_halves(c0_ref[b])):
                ct_s[b, hh] = block_diag(c0_half, bd)
            n_s[b] = n0_ref[b]
            m_s[b] = m0_ref[b]

    rows_of = lambda a, b: a[b * CHUNK:(b + 1) * CHUNK]

    gates = jnp.concatenate([g_ref[b] for b in range(nb)], axis=0) + gbias_ref[...]
    gates = jnp.where(gate_lane < N_HEADS, gates, _log_sigmoid(gates))
    gp_all = jnp.dot(jnp.concatenate(_split3(gates), axis=1), expand_ref[...], preferred_element_type=F32)

    nums, den_parts, m_ts = [], [], []
    for b in range(nb):
        gp = rows_of(gp_all, b)
        log_i = gp[:, :GROUP_W]
        log_f = gp[:, GROUP_W:]
        if valid < CHUNK:
            live = lax.broadcasted_iota(I32, (CHUNK, GROUP_W), 0) < valid
            log_i = jnp.where(live, log_i, -jnp.inf)
            log_f = jnp.where(live, log_f, 0.0)
        cum_f = jnp.dot(ltri_ref[...], jnp.concatenate(_split3(log_f), axis=0), preferred_element_type=F32)

        b_row = jnp.sum(jnp.where(eye, log_i - cum_f, 0.0), axis=0, keepdims=True)
        m_prev = m_s[b]
        log_inter = cum_f + m_prev
        log_d = jnp.where(causal, cum_f + b_row, -jnp.inf)
        max_d = jnp.concatenate(
            [jnp.broadcast_to(jnp.max(log_d[:, h * HEAD_DIM:(h + 1) * HEAD_DIM], axis=-1, keepdims=True),
                              (CHUNK, HEAD_DIM)) for h in range(N_HEADS)], axis=1)
        m_t = jnp.maximum(log_inter, max_d)
        w_intra = jnp.exp(log_d - m_t)
        w_inter = jnp.exp(log_inter - m_t)

        q = q_ref[b] * (HEAD_DIM ** -0.5)
        k = k_ref[b]
        qb = q.astype(BF16)
        kb = k.astype(BF16)
        vb = v_ref[b].astype(BF16)
        n_prev = n_s[b]
        m_new = m_t[CHUNK - 1:CHUNK, :]
        cum_last = cum_f[CHUNK - 1:CHUNK, :]
        w_state = jnp.exp(cum_last - cum_f + log_i - m_new)
        decay = jnp.exp(cum_last + m_prev - m_new)
        kw = k * w_state
        kwb = kw.astype(BF16)

        s_halves, num_halves = [], []
        for hh in range(2):
            half = slice(hh * HALF_W, (hh + 1) * HALF_W)
            kbd = block_diag(kb[:, half], bdb)
            vbd = block_diag(vb[:, half], bdb)
            s_h = lax.dot_general(qb[:, half], kbd, (((1,), (1,)), ((), ())),
                                  preferred_element_type=F32) * w_intra[:, half]
            ct = ct_s[b, hh]
            num_halves.append(
                jnp.dot(s_h.astype(BF16), vbd, preferred_element_type=F32)
                + w_inter[:, half] * lax.dot_general(qb[:, half], ct.astype(BF16), (((1,), (1,)), ((), ())),
                                                     preferred_element_type=F32))
            s_halves.append(s_h)
            upd = lax.dot_general(vb[:, half], kwb[:, half], (((0,), (0,)), ((), ())),
                                  preferred_element_type=F32)
            ct_s[b, hh] = decay[:, half] * ct + upd * bd
        s = jnp.concatenate(s_halves, axis=1)
        nums.append(jnp.concatenate(num_halves, axis=1))
        den_parts.append((s + w_inter * q * n_prev) * float(HEAD_DIM))
        m_ts.append(m_t)
        n_s[b] = decay * n_prev + jnp.sum(kw, axis=0, keepdims=True)
        m_s[b] = m_new

    den_terms = jnp.concatenate(den_parts, axis=0)
    den_hi = den_terms.astype(BF16)
    den_lo = den_terms - den_hi.astype(F32)
    den = _group_mean(den_hi.astype(F32), gmat) + _group_mean(den_lo, gmat)
    hb = jnp.concatenate(nums, axis=0) / jnp.maximum(jnp.abs(den), jnp.exp(-jnp.concatenate(m_ts, axis=0)))
    hn = hb * lax.rsqrt(_group_mean(hb * hb, gmat) + RMS_EPS) * gml_ref[...]
    for b in range(nb):
        h_ref[b] = jax.nn.sigmoid(o_ref[b]) * rows_of(hn, b)

    @pl.when(last)
    def _():
        for b in range(nb):
            out_halves = []
            for hh in range(2):
                ct_half = ct_s[b, hh]
                acc = ct_half[0:HEAD_DIM]
                for h in range(1, heads_per_half):
                    acc = acc + ct_half[h * HEAD_DIM:(h + 1) * HEAD_DIM]
                out_halves.append(acc)
            ct_ref[b] = jnp.concatenate(out_halves, axis=1)
            n_ref[b] = n_s[b]
            m_ref[b] = m_s[b]


def _mlstm(mq, mk, mv, mo, gates, c0t, n0, m0, consts, *, valid):
    batch, t, _ = mq.shape
    chunks = t // CHUNK
    nb = MLSTM_STREAMS
    row = lambda w: pl.BlockSpec((nb, CHUNK, w), lambda g, c: (g, c, 0))
    per_b = lambda r: pl.BlockSpec((nb, r, GROUP_W), lambda g, c: (g, 0, 0))
    expand, gbias, bd, bdb, gmat, ltri, eye, causal, gml = consts
    return pl.pallas_call(
        functools.partial(_mlstm_body, valid=valid, nb=nb),
        grid=(batch // nb, chunks),
        in_specs=[row(GROUP_W)] * 4 + [row(LANES), per_b(HEAD_DIM), per_b(1), per_b(1),
                  _full((3 * LANES, 2 * GROUP_W)), _full((1, LANES)), _full((HALF_W, HALF_W)),
                  _full((HALF_W, HALF_W)), _full((HALF_W, HALF_W)), _full((CHUNK, 3 * CHUNK)),
                  _full((CHUNK, GROUP_W)), _full((CHUNK, GROUP_W)), _full((1, GROUP_W))],
        out_specs=[row(GROUP_W), per_b(HEAD_DIM), per_b(1), per_b(1)],
        out_shape=[jax.ShapeDtypeStruct(mq.shape, F32),
                   jax.ShapeDtypeStruct((batch, HEAD_DIM, GROUP_W), F32),
                   jax.ShapeDtypeStruct((batch, 1, GROUP_W), F32),
                   jax.ShapeDtypeStruct((batch, 1, GROUP_W), F32)],
        scratch_shapes=[pltpu.VMEM((nb, 2, HALF_W, HALF_W), F32), pltpu.VMEM((nb, 1, GROUP_W), F32),
                        pltpu.VMEM((nb, 1, GROUP_W), F32)],
        compiler_params=_params("arbitrary", "arbitrary"),
        name="mlstm",
    )(mq, mk, mv, mo, gates, c0t, n0, m0, expand, gbias, bd, bdb, gmat, ltri, eye, causal, gml)


def _mlstm_consts(b_igate_l, b_fgate_l, g_mlstm_l):
    expand = np.zeros((LANES, 2 * GROUP_W), np.float32)
    for h in range(N_HEADS):
        expand[h, h * HEAD_DIM:(h + 1) * HEAD_DIM] = 1.0
        expand[N_HEADS + h, GROUP_W + h * HEAD_DIM:GROUP_W + (h + 1) * HEAD_DIM] = 1.0
    gbias = jnp.concatenate([b_igate_l.astype(F32), b_fgate_l.astype(F32),
                             jnp.zeros((LANES - 2 * N_HEADS,), F32)])[None, :]
    bd = _head_block_diag()[:HALF_W, :HALF_W]
    ltri = np.tril(np.ones((CHUNK, CHUNK), np.float32))
    s_of_lane = np.arange(GROUP_W) % HEAD_DIM
    t = np.arange(CHUNK)
    eye = (t[:, None] == s_of_lane[None, :]).astype(np.float32)
    causal = (s_of_lane[None, :] <= t[:, None]).astype(np.float32)
    return (jnp.asarray(np.concatenate([expand] * 3, axis=0), BF16), gbias, jnp.asarray(bd), jnp.asarray(bd, BF16),
            jnp.asarray(bd / HEAD_DIM, BF16), jnp.asarray(np.concatenate([ltri] * 3, axis=1), BF16), jnp.asarray(eye),
            jnp.asarray(causal), g_mlstm_l.astype(F32).reshape(1, GROUP_W))


def _out_proj_body(xp_ref, xs_ref, ap_ref, as_ref, hp_ref, hs_ref, wa_ref, wm_ref, gffn_ref, wr_ref, br_ref,
                   lstrict_ref, ustrict_ref, y_ref, xf_ref, gate_ref, pos_ref, seg_ref, *, prompt_tiles):
    is_prompt = pl.program_id(0) < prompt_tiles
    x = jnp.where(is_prompt, xp_ref[...], xs_ref[...])
    att = jnp.where(is_prompt, ap_ref[...], as_ref[...])
    hm = jnp.where(is_prompt, hp_ref[...], hs_ref[...])
    y = (x + jnp.dot(att.astype(BF16), wa_ref[...], preferred_element_type=F32)
         + jnp.dot(hm.astype(BF16), wm_ref[...], preferred_element_type=F32))
    y_ref[...] = y
    xf = y * lax.rsqrt(jnp.mean(y * y, axis=-1, keepdims=True) + RMS_EPS) * gffn_ref[...]
    xf_ref[...] = xf
    _route_tile(_dot_f32ish(xf, wr_ref[...]) + br_ref[...], lstrict_ref, ustrict_ref, gate_ref, pos_ref, seg_ref)


def _out_proj(xp, xs, att_p, att_s, hm_p, hm_s, wa, wm, g_ffn, w_router, b_router):
    tm = ROW_TILE
    pt, st = xp.shape[0] // tm, xs.shape[0] // tm
    n = xp.shape[0] + xs.shape[0]
    p_row = lambda w: pl.BlockSpec((tm, w), lambda i: (jnp.minimum(i, pt - 1), 0))
    s_row = lambda w: pl.BlockSpec((tm, w), lambda i: (jnp.maximum(i - pt, 0), 0))
    row = lambda w: pl.BlockSpec((tm, w), lambda i: (i, 0))
    lstrict = jnp.asarray(np.tril(np.ones((tm, tm), np.float32), -1), BF16)
    ustrict = jnp.asarray(np.triu(np.ones((LANES, LANES), np.float32), 1), BF16)
    return pl.pallas_call(
        functools.partial(_out_proj_body, prompt_tiles=pt),
        grid=(pt + st,),
        in_specs=[p_row(D_MODEL), s_row(D_MODEL), p_row(GROUP_W), s_row(GROUP_W), p_row(GROUP_W), s_row(GROUP_W),
                  _full((GROUP_W, D_MODEL)), _full((GROUP_W, D_MODEL)),
                  _full((1, D_MODEL)), _full((3 * D_MODEL, LANES)), _full((1, LANES)),
                  _full((tm, tm)), _full((LANES, LANES))],
        out_specs=[row(D_MODEL), row(D_MODEL), row(LANES), row(LANES),
                   pl.BlockSpec((SUBLANES, LANES), lambda i: (i, 0))],
        out_shape=[jax.ShapeDtypeStruct((n, D_MODEL), F32), jax.ShapeDtypeStruct((n, D_MODEL), F32),
                   jax.ShapeDtypeStruct((n, LANES), F32), jax.ShapeDtypeStruct((n, LANES), F32),
                   jax.ShapeDtypeStruct((n // tm * SUBLANES, LANES), I32)],
        compiler_params=_params("arbitrary"),
        name="out_proj_router",
    )(xp, xs, att_p, att_s, hm_p, hm_s, wa, wm, g_ffn, w_router, b_router, lstrict, ustrict)


def _route_tile(logits, lstrict_ref, ustrict_ref, gate_ref, pos_ref, seg_ref):
    tt = logits.shape[0]
    lane = lax.broadcasted_iota(I32, (tt, LANES), 1)
    work = jnp.where(lane < N_EXPERTS, logits, -jnp.inf)
    vals, idxs = [], []
    for _ in range(TOP_K):
        m = jnp.max(work, axis=-1, keepdims=True)
        idx = jnp.min(jnp.where(work == m, lane, LANES), axis=-1, keepdims=True)
        vals.append(m)
        idxs.append(idx)
        work = jnp.where(lane == idx, -jnp.inf, work)
    exps = [jnp.exp(v - vals[0]) for v in vals]
    total = exps[0] + exps[1] + exps[2] + exps[3]

    chosen = jnp.zeros((tt, LANES), F32)
    for idx in idxs:
        chosen = chosen + (lane == idx).astype(F32)
    before = jnp.dot(lstrict_ref[...], chosen.astype(BF16), preferred_element_type=F32)
    count = jnp.sum(chosen, axis=0, keepdims=True)
    groups = jnp.floor((count + (SUBLANES - 1)) * (1.0 / SUBLANES))
    groups8 = jnp.broadcast_to(groups, (SUBLANES, LANES)).astype(BF16)
    start = jnp.dot(groups8, ustrict_ref[...], preferred_element_type=F32) * float(SUBLANES)
    local = before + start[0:1, :]

    gate_out = jnp.zeros((tt, LANES), F32)
    pos_out = jnp.zeros((tt, LANES), F32)
    for k in range(TOP_K):
        pos = jnp.sum(jnp.where(lane == idxs[k], local, 0.0), axis=-1, keepdims=True)
        gate_out = jnp.where(lane == k, exps[k] / total, gate_out)
        pos_out = jnp.where(lane == k, pos, pos_out)
    gate_ref[...] = gate_out
    pos_ref[...] = pos_out
    row = lax.broadcasted_iota(I32, (SUBLANES, LANES), 0)
    seg = jnp.where(row == 0, groups * float(SUBLANES), jnp.where(row == 1, start, 0.0))
    seg_ref[...] = seg.astype(I32)


def _segment_copies(i, len_ref, lst_ref, off_ref, make_copy, act):
    for e in range(N_EXPERTS):
        seg = i * N_EXPERTS + e
        length = len_ref[seg]
        local = lst_ref[seg]
        glob = off_ref[seg]
        def pieces(sizes, local, glob):
            for size in sizes:
                take = length & size

                @pl.when(take != 0)
                def _(local=local, glob=glob, size=size):
                    act(make_copy(pl.multiple_of(local, SUBLANES), pl.multiple_of(glob, SUBLANES), size))

                local = local + take
                glob = glob + take

        big = length & ~(SEG_RARE - 1)

        @pl.when(big != 0)
        def _(local=local, glob=glob):
            pieces([s for s in SEG_SIZES if s >= SEG_RARE], local, glob)

        pieces([s for s in SEG_SIZES if s < SEG_RARE], local + big, glob + big)


def _local_onehot(pos_rows, base, rows):
    r = (lax.broadcasted_iota(I32, (rows, pos_rows[0].shape[1]), 0) + base).astype(F32)
    out = jnp.zeros(r.shape, F32)
    for k in range(TOP_K):
        out = jnp.where(r == pos_rows[k], 1.0, out)
    return out


def _dispatch_body(len_ref, lst_ref, off_ref, fill_ref, nused_ref, xf_ref, pos_ref, xs_hbm,
                   xloc, zbuf, sem, zsem, *, tm, n_tiles):
    i = pl.program_id(0)
    tt = xf_ref.shape[0]
    fill_rows = zbuf.shape[0]

    @pl.when(i == 0)
    def _():
        zbuf[...] = jnp.zeros_like(zbuf)

        def fill(e):
            start = pl.multiple_of(fill_ref[e], SUBLANES)
            return pltpu.make_async_copy(zbuf, xs_hbm.at[pl.ds(start, fill_rows)], zsem)

        for e in range(N_EXPERTS):
            fill(e).start()
            fill(e).wait()

        def tail(j, carry):
            cp = pltpu.make_async_copy(zbuf.at[pl.ds(0, tm)], xs_hbm.at[pl.ds(pl.multiple_of(j * tm, tm), tm)], zsem)
            cp.start()
            cp.wait()
            return carry

        lax.fori_loop(nused_ref[0], n_tiles, tail, 0)

    pos_t = jnp.transpose(pos_ref[...])
    pos_rows = [pos_t[k:k + 1, :] for k in range(TOP_K)]
    xb = xf_ref[...].astype(BF16)

    slot = lax.rem(i, 2)

    def sort_rows(c, carry):
        r0 = pl.multiple_of(c * SORT_CHUNK, SORT_CHUNK)
        sel = _local_onehot(pos_rows, r0, SORT_CHUNK).astype(BF16)
        xloc[slot, pl.ds(r0, SORT_CHUNK), :] = jnp.dot(sel, xb, preferred_element_type=F32)
        return carry

    lax.fori_loop(0, LOCAL_ROWS // SORT_CHUNK, sort_rows, 0)

    def copies(step, which):
        def make_copy(local, glob, size):
            return pltpu.make_async_copy(xloc.at[which, pl.ds(local, size)], xs_hbm.at[pl.ds(glob, size)],
                                         sem.at[which])
        return functools.partial(_segment_copies, step, len_ref, lst_ref, off_ref, make_copy)

    copies(i, slot)(lambda cp: cp.start())

    @pl.when(i > 0)
    def _():
        copies(i - 1, 1 - slot)(lambda cp: cp.wait())

    @pl.when(i == pl.num_programs(0) - 1)
    def _():
        copies(i, slot)(lambda cp: cp.wait())


def _dispatch(seg_len, seg_local, seg_off, fill_start, n_used, xf, pos, n_tiles):
    n = xf.shape[0]
    tt, tm = ROW_TILE, EXPERT_TILE
    fill_rows = tm + SUBLANES
    grid_spec = pltpu.PrefetchScalarGridSpec(
        num_scalar_prefetch=5,
        grid=(n // tt,),
        in_specs=[pl.BlockSpec((tt, D_MODEL), lambda i, *_: (i, 0)),
                  pl.BlockSpec((tt, LANES), lambda i, *_: (i, 0))],
        out_specs=pl.BlockSpec(memory_space=pl.ANY),
        scratch_shapes=[pltpu.VMEM((2, LOCAL_ROWS, D_MODEL), F32), pltpu.VMEM((fill_rows, D_MODEL), F32),
                        pltpu.SemaphoreType.DMA((2,)), pltpu.SemaphoreType.DMA(())],
    )
    return pl.pallas_call(
        functools.partial(_dispatch_body, tm=tm, n_tiles=n_tiles + 2),
        grid_spec=grid_spec,
        out_shape=jax.ShapeDtypeStruct(((n_tiles + 2) * tm, D_MODEL), F32),
        compiler_params=_params("arbitrary"),
        name="expert_dispatch",
    )(seg_len, seg_local, seg_off, fill_start, n_used, xf, pos)


def _expert_body(te_ref, nused_ref, x_ref, wup_ref, wdn_ref, perm_ref, bg_ref, bl_ref, bd_ref,
                 y_ref, wg_s, wl_s, wd_s):
    i = pl.program_id(0)
    n_used = nused_ref[0]

    @pl.when(i >= n_used)
    def _():
        y_ref[...] = jnp.zeros_like(y_ref)

    @pl.when((i == 0) | (te_ref[i] != te_ref[jnp.maximum(i - 1, 0)]))
    def _():
        perm = perm_ref[...]
        for c in range(2 * D_FF // 256):
            blk = wup_ref[0, :, c * 256:(c + 1) * 256].astype(BF16)
            sep = jnp.dot(blk, perm, preferred_element_type=F32).astype(BF16)
            wg_s[:, c * 128:(c + 1) * 128] = sep[:, :128]
            wl_s[:, c * 128:(c + 1) * 128] = sep[:, 128:]
        wd_s[...] = wdn_ref[0].astype(BF16)

    @pl.when(i < n_used)
    def _():
        x = x_ref[...].astype(BF16)
        glu = jnp.minimum(jnp.dot(x, wg_s[...], preferred_element_type=F32) + bg_ref[0], SWIGLU_LIMIT)
        lin = jnp.clip(jnp.dot(x, wl_s[...], preferred_element_type=F32) + bl_ref[0], -SWIGLU_LIMIT, SWIGLU_LIMIT)
        act = glu * jax.nn.sigmoid(SWIGLU_ALPHA * glu) * (lin + 1.0)
        y_ref[...] = jnp.dot(act.astype(BF16), wd_s[...], preferred_element_type=F32) + bd_ref[0]


def _deinterleave_perm():
    p = np.zeros((256, 256), np.float32)
    j = np.arange(128)
    p[2 * j, j] = 1.0
    p[2 * j + 1, 128 + j] = 1.0
    return jnp.asarray(p, BF16)


def _experts(tile_expert, n_used, x_sorted, w_up, w_down, b_glu, b_lin, b_down):
    tm = EXPERT_TILE
    n_tiles = tile_expert.shape[0]
    wspec = lambda k, n: pl.BlockSpec((1, k, n), lambda i, te, nu: (te[i], 0, 0))
    grid_spec = pltpu.PrefetchScalarGridSpec(
        num_scalar_prefetch=2,
        grid=(n_tiles,),
        in_specs=[pl.BlockSpec((tm, D_MODEL), lambda i, te, nu: (jnp.minimum(i, nu[0] - 1), 0)),
                  wspec(D_MODEL, 2 * D_FF), wspec(D_FF, D_MODEL),
                  pl.BlockSpec((256, 256), lambda i, te, nu: (0, 0)),
                  wspec(1, D_FF), wspec(1, D_FF), wspec(1, D_MODEL)],
        out_specs=pl.BlockSpec((tm, D_MODEL), lambda i, te, nu: (i, 0)),
        scratch_shapes=[pltpu.VMEM((D_MODEL, D_FF), BF16), pltpu.VMEM((D_MODEL, D_FF), BF16),
                        pltpu.VMEM((D_FF, D_MODEL), BF16)],
    )
    return pl.pallas_call(
        _expert_body,
        grid_spec=grid_spec,
        out_shape=jax.ShapeDtypeStruct((n_tiles * tm, D_MODEL), F32),
        compiler_params=_params("arbitrary"),
        name="expert_ffn",
    )(tile_expert, n_used, x_sorted, w_up, w_down, _deinterleave_perm(), b_glu, b_lin, b_down)


def _combine_body(len_ref, lst_ref, off_ref, y_ref, gate_ref, pos_ref, rows_hbm, outp_ref, outs_ref,
                  yloc, acc_s, wide_s, sem, *, prompt_tiles):
    i = pl.program_id(0)
    slot = lax.rem(i, 2)
    tt = y_ref.shape[0]

    def copies(step, which):
        def make_copy(local, glob, size):
            return pltpu.make_async_copy(rows_hbm.at[pl.ds(glob, size)], yloc.at[which, pl.ds(local, size)],
                                         sem.at[which])
        return functools.partial(_segment_copies, step, len_ref, lst_ref, off_ref, make_copy)

    @pl.when(i == 0)
    def _():
        yloc[...] = jnp.zeros_like(yloc)
        copies(i, slot)(lambda cp: cp.start())

    @pl.when(i + 1 < pl.num_programs(0))
    def _():
        copies(i + 1, 1 - slot)(lambda cp: cp.start())

    pos = pos_ref[...]
    gate = gate_ref[...]
    for k in range(TOP_K):
        wide_s[k] = jnp.broadcast_to(pos[:, k:k + 1], (tt, LANES))
        wide_s[TOP_K + k] = jnp.broadcast_to(gate[:, k:k + 1], (tt, LANES))
    acc_s[...] = y_ref[...]
    copies(i, slot)(lambda cp: cp.wait())

    lane = lax.broadcasted_iota(I32, (tt, LANES), 1).astype(F32)

    def weigh(c, carry):
        r0 = pl.multiple_of(c * SORT_CHUNK, SORT_CHUNK)
        halves = []
        for half in range(SORT_CHUNK // LANES):
            r = lane + (r0 + half * LANES).astype(F32)
            w = jnp.zeros((tt, LANES), F32)
            for k in range(TOP_K):
                w = jnp.where(r == wide_s[k], wide_s[TOP_K + k], w)
            halves.append(w.astype(BF16))
        w = jnp.concatenate(halves, axis=1)
        acc_s[...] += jnp.dot(w, yloc[slot, pl.ds(r0, SORT_CHUNK), :].astype(BF16), preferred_element_type=F32)
        return carry

    lax.fori_loop(0, LOCAL_ROWS // SORT_CHUNK, weigh, 0)

    @pl.when(i < prompt_tiles)
    def _():
        outp_ref[...] = acc_s[...]

    @pl.when(i >= prompt_tiles)
    def _():
        outs_ref[...] = acc_s[...]


def _combine(seg_len, seg_local, seg_off, y, gates, pos, y_rows, n_prompt):
    n = y.shape[0]
    tt = ROW_TILE
    nt, pt = n // tt, n_prompt // tt
    grid_spec = pltpu.PrefetchScalarGridSpec(
        num_scalar_prefetch=3,
        grid=(nt,),
        in_specs=[pl.BlockSpec((tt, D_MODEL), lambda i, *_: (i, 0)),
                  pl.BlockSpec((tt, LANES), lambda i, *_: (i, 0)),
                  pl.BlockSpec((tt, LANES), lambda i, *_: (i, 0)),
                  pl.BlockSpec(memory_space=pl.ANY)],
        out_specs=[pl.BlockSpec((tt, D_MODEL), lambda i, *_: (jnp.minimum(i, pt - 1), 0)),
                   pl.BlockSpec((tt, D_MODEL), lambda i, *_: (jnp.maximum(i - pt, 0), 0))],
        scratch_shapes=[pltpu.VMEM((2, LOCAL_ROWS, D_MODEL), F32), pltpu.VMEM((tt, D_MODEL), F32),
                        pltpu.VMEM((2 * TOP_K, tt, LANES), F32), pltpu.SemaphoreType.DMA((2,))],
    )
    return pl.pallas_call(
        functools.partial(_combine_body, prompt_tiles=pt),
        grid_spec=grid_spec,
        out_shape=[jax.ShapeDtypeStruct((n_prompt, D_MODEL), F32),
                   jax.ShapeDtypeStruct((n - n_prompt, D_MODEL), F32)],
        compiler_params=_params("arbitrary"),
        name="expert_combine",
    )(seg_len, seg_local, seg_off, y, gates, pos, y_rows)


def _moe(y, xf, gates, pos, seg, ffn_w, n_prompt):
    n = y.shape[0]
    tm, tt = EXPERT_TILE, ROW_TILE
    n_tt = n // tt
    seg = seg.reshape(n_tt, SUBLANES, LANES)
    seg_len, seg_local = seg[:, 0, :N_EXPERTS], seg[:, 1, :N_EXPERTS]
    rows = jnp.sum(seg_len, axis=0)
    padded = (rows + tm - 1) // tm * tm
    pad_end = jnp.cumsum(padded)
    pad_start = pad_end - padded
    seg_off = pad_start[None, :] + jnp.cumsum(seg_len, axis=0) - seg_len
    n_tiles = -(-(n * TOP_K + n_tt * N_EXPERTS * (SUBLANES - 1) + N_EXPERTS * (tm - 1)) // tm)
    tile_expert = jnp.minimum(jnp.sum(pad_end[None, :] <= (jnp.arange(n_tiles) * tm)[:, None], axis=1),
                              N_EXPERTS - 1).astype(I32)
    n_used = (pad_end[-1:] // tm).astype(I32)
    fill_start = (pad_start + rows).astype(I32)
    flat = lambda a: a.astype(I32).reshape(n_tt * N_EXPERTS)

    x_sorted = _dispatch(flat(seg_len), flat(seg_local), flat(seg_off), fill_start, n_used, xf, pos, n_tiles)
    y_rows = _experts(tile_expert, n_used, x_sorted, *ffn_w)
    return _combine(flat(seg_len), flat(seg_local), flat(seg_off), y, gates, pos, y_rows, n_prompt)


def _mixer(x, lw, cache, state):
    b, t, _ = x.shape
    n = b * t
    q, k, v, mq, mk, mv, mo, gates = _in_proj(x.reshape(n, D_MODEL), lw["g_mix"], lw["w_main"], lw["w_gate"],
                                              lw["gq"], lw["gk"], lw["gmat"])
    heads = lambda a, rows: a.reshape(b, rows, N_HEADS, HEAD_DIM)
    if cache is None:
        tiles = t // PAST_BAND
        att = _attention(q, k, k, v, v, lw["bias_prompt"], lw["hmask_prompt"], batch=b, tiles=tiles, cq=CHUNK,
                         nq=PAST_BAND // CHUNK,
                         prev_index=lambda bi, i: (bi * tiles + jnp.maximum(i - 1, 0), 0), mask_first=True)
        keep = min(PAST_BAND, t)
        k_new = heads(k.reshape(b, t, GROUP_W)[:, t - keep:], keep)
        v_new = heads(v.reshape(b, t, GROUP_W)[:, t - keep:], keep)
    else:
        ck, cv = cache
        att = _attention(q, ck.reshape(b * PAST_BAND, GROUP_W), k, cv.reshape(b * PAST_BAND, GROUP_W), v,
                         lw["bias_sample"], lw["hmask_sample"], batch=b, tiles=1, cq=t, nq=1,
                         prev_index=lambda bi, i: (bi, 0), mask_first=False)
        k_new, v_new = heads(k, t), heads(v, t)

    tp = -(-t // CHUNK) * CHUNK
    valid = t if t < CHUNK else CHUNK

    def streams(a):
        a = a.reshape(b, t, -1)
        return a if tp == t else jnp.pad(a, ((0, 0), (0, tp - t), (0, 0)))

    if state is None:
        c0t = jnp.zeros((b, HEAD_DIM, GROUP_W), F32)
        n0 = jnp.zeros((b, 1, GROUP_W), F32)
        m0 = jnp.zeros((b, 1, GROUP_W), F32)
    else:
        c_in, n_in, m_in = state
        c0t = c_in.astype(F32).transpose(0, 3, 1, 2).reshape(b, HEAD_DIM, GROUP_W)
        n0 = n_in.astype(F32).reshape(b, 1, GROUP_W)
        m0 = jnp.repeat(m_in.astype(F32), HEAD_DIM, axis=-1).reshape(b, 1, GROUP_W)
    hm, ct, n_out, m_out = _mlstm(streams(mq), streams(mk), streams(mv), streams(mo), streams(gates),
                                  c0t, n0, m0, lw["mlstm_consts"], valid=valid)
    hm = hm[:, :t].reshape(n, GROUP_W)
    c_new = ct.reshape(b, HEAD_DIM, N_HEADS, HEAD_DIM).transpose(0, 2, 3, 1)
    n_new = n_out.reshape(b, N_HEADS, HEAD_DIM)
    m_new = m_out.reshape(b, N_HEADS, HEAD_DIM)[:, :, 0]
    return att, hm, (k_new, v_new, c_new, n_new, m_new)


def kernel(x_prompt, x_sample, cache_k, cache_v, state_C, state_n, state_m, g_mix, w_in, g_q, g_k, rel_bias,
           b_igate, b_fgate, g_mlstm, w_out, g_ffn, w_router, b_router, w_up, b_up, w_down, b_down):
    depth = w_in.shape[0]
    yp, ys = x_prompt, x_sample
    bs, ts = x_sample.shape[0], x_sample.shape[1]
    n_prompt = x_prompt.shape[0] * x_prompt.shape[1]
    st_prompt, st_sample = [], []
    n_main = N_PROJ * GROUP_W
    gmat = jnp.asarray(_head_block_diag()[:HALF_W, :HALF_W] / HEAD_DIM, BF16)
    for l in range(depth):
        lw = dict(
            g_mix=g_mix[l].astype(F32)[None, :],
            w_main=w_in[l][:, :n_main].astype(BF16),
            w_gate=jnp.pad(w_in[l][:, n_main:], ((0, 0), (0, LANES - 2 * N_HEADS))).astype(BF16),
            gq=jnp.tile(g_q[l].astype(F32), N_HEADS)[None, :],
            gk=jnp.tile(g_k[l].astype(F32), N_HEADS)[None, :],
            gmat=gmat,
            bias_prompt=_rel_base(rel_bias[l], CHUNK),
            hmask_prompt=_head_row_mask(CHUNK),
            bias_sample=_rel_base(rel_bias[l], ts),
            hmask_sample=_head_row_mask(ts),
            mlstm_consts=_mlstm_consts(b_igate[l], b_fgate[l], g_mlstm[l]),
        )
        ffn_w = (w_up[l].astype(F32), w_down[l].astype(F32),
                 b_up[l][:, None, 0::2].astype(F32), b_up[l][:, None, 1::2].astype(F32),
                 b_down[l][:, None, :].astype(F32))
        att_p, hm_p, sp = _mixer(yp, lw, None, None)
        cache = (cache_k[l].reshape(bs, PAST_BAND, GROUP_W), cache_v[l].reshape(bs, PAST_BAND, GROUP_W))
        att_s, hm_s, ss = _mixer(ys, lw, cache, (state_C[l], state_n[l], state_m[l]))
        y, xf, gates, pos, seg = _out_proj(
            yp.reshape(-1, D_MODEL), ys.reshape(-1, D_MODEL), att_p, att_s, hm_p, hm_s,
            w_out[l][:GROUP_W].astype(BF16), w_out[l][GROUP_W:].astype(BF16), g_ffn[l].astype(F32)[None, :],
            _stack_hi_lo(jnp.pad(w_router[l].astype(F32), ((0, 0), (0, LANES - N_EXPERTS)))),
            jnp.pad(b_router[l].astype(F32), (0, LANES - N_EXPERTS))[None, :])
        out_p, out_s = _moe(y, xf, gates, pos, seg, ffn_w, n_prompt)
        yp, ys = out_p.reshape(x_prompt.shape), out_s.reshape(x_sample.shape)
        st_prompt.append(sp)
        st_sample.append(ss)
    k_p, v_p, c_p, n_p, m_p = [jnp.stack(a) for a in zip(*st_prompt)]
    k_s, v_s, c_s, n_s, m_s = [jnp.stack(a) for a in zip(*st_sample)]
    return (yp, ys, k_p, v_p, c_p, n_p, m_p, k_s, v_s, c_s, n_s, m_s)
```
